```python
import math
import jax, jax.numpy as jnp
from jax import lax
import numpy as np

D_MODEL = 1024
BATCH = 4
SEQ = 8192
DEPTH = 2

CONV_DIM = 512
CONV_WIDTH = 3
ATTN_HEADS = 8
ATTN_HEAD_DIM = 64
ATTN_DIM = ATTN_HEADS * ATTN_HEAD_DIM
MOBA_BLOCK = 256
MOBA_TOPK = 3
Q_CHUNK = 128
REL_BUCKETS = 32
REL_MAX_DISTANCE = 128
GLA_HEADS = 4
GLA_KEY_DIM = 64
GLA_VALUE_DIM = 128
GLA_QK_DIM = GLA_HEADS * GLA_KEY_DIM
GLA_V_DIM = GLA_HEADS * GLA_VALUE_DIM
GLA_GATE_RANK = 16
GLA_GATE_TEMP = 16.0
GLA_CHUNK = 64
N_BRANCHES = 3
BRANCH_DIM = 512
IN_SPLITS = (CONV_DIM, CONV_DIM, CONV_DIM, ATTN_DIM, ATTN_DIM, ATTN_DIM,
             GLA_QK_DIM, GLA_QK_DIM, GLA_V_DIM, GLA_V_DIM, GLA_GATE_RANK)
D_IN = 3 * CONV_DIM + 3 * ATTN_DIM + 2 * GLA_QK_DIM + 2 * GLA_V_DIM + GLA_GATE_RANK
N_GROUPS = 4
EXPERTS_PER_GROUP = 8
N_EXPERTS = N_GROUPS * EXPERTS_PER_GROUP
TOP_K = 2
EXPERT_FF = 512
EXPERT_BLOCK = 256
RMS_EPS = 1e-6

kernel_name = 'hybrid_conv_moba_gla_hmoe'


def rms_norm(x, g):
    xf = x.astype(jnp.float32)
    y = xf * lax.rsqrt(jnp.mean(xf * xf, axis=-1, keepdims=True) + RMS_EPS)
    return (y * g.astype(jnp.float32)).astype(x.dtype)


def short_conv(b_gate, c_gate, x_in, conv_w, conv_b):
    u = c_gate * x_in
    sn = u.shape[1]
    up = jnp.pad(u, ((0, 0), (CONV_WIDTH - 1, 0), (0, 0)))
    y = conv_b
    for j in range(CONV_WIDTH):
        y = y + conv_w[j] * up[:, j:j + sn]
    return b_gate * y


def t5_bucket(rel):
    n = jnp.maximum(rel, 0)
    max_exact = REL_BUCKETS // 2
    scaled = jnp.log(jnp.maximum(n, max_exact).astype(jnp.float32) / max_exact) / math.log(REL_MAX_DISTANCE / max_exact)
    large = jnp.minimum(max_exact + (scaled * (REL_BUCKETS - max_exact)).astype(jnp.int32), REL_BUCKETS - 1)
    return jnp.where(n < max_exact, n, large)


def moba_attention(q, k, v, rel_bias_table):
    bn, sn, nh, dh = q.shape
    f32 = jnp.float32
    n_blocks = max(-(-sn // MOBA_BLOCK), MOBA_TOPK)
    pad = n_blocks * MOBA_BLOCK - sn
    n_chunks = sn // Q_CHUNK
    scale = dh ** -0.5
    qh = q.transpose(0, 2, 1, 3)

    def to_blocks(t):
        t = jnp.pad(t.transpose(0, 2, 1, 3), ((0, 0), (0, 0), (0, pad), (0, 0)))
        return t.reshape(bn, nh, n_blocks, MOBA_BLOCK, dh)

    kblk, vblk = to_blocks(k), to_blocks(v)
    kmean = kblk.mean(axis=3)
    q_block = jnp.arange(sn) // MOBA_BLOCK
    past = jnp.arange(n_blocks)[None, :] < q_block[:, None]
    gate = jnp.einsum('bhsd,bhnd->bhsn', qh, kmean).astype(f32)
    gate = jnp.where(past, gate, -jnp.inf)
    _, sel = lax.top_k(gate, MOBA_TOPK)

    q_items = qh.reshape(bn, nh, n_chunks, Q_CHUNK, dh).transpose(0, 2, 1, 3, 4).reshape(bn * n_chunks, nh, Q_CHUNK, dh)
    sel_items = sel.reshape(bn, nh, n_chunks, Q_CHUNK, MOBA_TOPK).transpose(0, 2, 1, 3, 4).reshape(bn * n_chunks, nh, Q_CHUNK, MOBA_TOPK)
    table = rel_bias_table.astype(f32).T
    head3 = jnp.arange(nh)[:, None, None]
    head4 = jnp.arange(nh)[:, None, None, None]
    n_sel = MOBA_TOPK * MOBA_BLOCK

    def attend(args):
        qc, selc, item = args
        b = item // n_chunks
        c = item % n_chunks
        kb, vb = kblk[b], vblk[b]
        q_pos = c * Q_CHUNK + jnp.arange(Q_CHUNK)
        own = (c * Q_CHUNK) // MOBA_BLOCK
        k_sel = kb[head3, selc]
        v_sel = vb[head3, selc]
        kpos_sel = selc[..., None] * MOBA_BLOCK + jnp.arange(MOBA_BLOCK)
        s_sel = (jnp.einsum('hqd,hqktd->hqkt', qc, k_sel).astype(f32) * scale
                 + table[head4, t5_bucket(q_pos[None, :, None, None] - kpos_sel)])
        valid = jnp.arange(MOBA_TOPK) < own
        s_sel = jnp.where(valid[None, None, :, None], s_sel, -jnp.inf)
        k_own = lax.dynamic_index_in_dim(kb, own, axis=1, keepdims=False)
        v_own = lax.dynamic_index_in_dim(vb, own, axis=1, keepdims=False)
        rel_own = q_pos[:, None] - (own * MOBA_BLOCK + jnp.arange(MOBA_BLOCK))[None, :]
        s_own = jnp.einsum('hqd,htd->hqt', qc, k_own).astype(f32) * scale + table[:, t5_bucket(rel_own)]
        s_own = jnp.where(rel_own >= 0, s_own, -jnp.inf)
        logits = jnp.concatenate([s_sel.reshape(nh, Q_CHUNK, n_sel), s_own], axis=-1)
        p = jax.nn.softmax(logits, axis=-1).astype(qc.dtype)
        p_sel = p[..., :n_sel].reshape(nh, Q_CHUNK, MOBA_TOPK, MOBA_BLOCK)
        p_own = p[..., n_sel:]
        return (jnp.einsum('hqkt,hqktd->hqd', p_sel, v_sel)
                + jnp.einsum('hqt,htd->hqd', p_own, v_own))

    out = lax.map(attend, (q_items, sel_items, jnp.arange(bn * n_chunks)))
    return out.reshape(bn, n_chunks, nh, Q_CHUNK, dh).transpose(0, 1, 3, 2, 4).reshape(bn, sn, nh * dh)


def gla(q, k, v, r, z, w_alpha, b_alpha, out_norm):
    bn, sn, _ = q.shape
    f32 = jnp.float32
    nc = sn // GLA_CHUNK
    qf = q.astype(f32).reshape(bn, sn, GLA_HEADS, GLA_KEY_DIM) * GLA_KEY_DIM ** -0.5
    kf = k.astype(f32).reshape(bn, sn, GLA_HEADS, GLA_KEY_DIM)
    vf = v.astype(f32).reshape(bn, sn, GLA_HEADS, GLA_VALUE_DIM)
    log_a = jax.nn.log_sigmoid((z @ w_alpha + b_alpha).astype(f32)) / GLA_GATE_TEMP
    log_a = log_a.reshape(bn, sn, GLA_HEADS, GLA_KEY_DIM)

    def chunked(t):
        return t.reshape(bn, nc, GLA_CHUNK, GLA_HEADS, -1).transpose(0, 3, 1, 2, 4)

    qc, kc, vc, ac = chunked(qf), chunked(kf), chunked(vf), chunked(log_a)
    bcum = jnp.cumsum(ac, axis=3)
    btot = bcum[:, :, :, -1:, :]
    qe = qc * jnp.exp(bcum)
    ke = kc * jnp.exp(-bcum)
    kd = kc * jnp.exp(btot - bcum)
    causal = jnp.tril(jnp.ones((GLA_CHUNK, GLA_CHUNK), dtype=bool))
    att = jnp.where(causal, jnp.einsum('bhnid,bhnjd->bhnij', qe, ke), 0.0)
    o = jnp.einsum('bhnij,bhnjv->bhniv', att, vc)
    kv = jnp.einsum('bhnjd,bhnjv->bhndv', kd, vc)
    decay = jnp.exp(btot[:, :, :, 0, :])

    def step(state, inp):
        kv_n, dec_n = inp
        return dec_n[..., None] * state + kv_n, state

    s0 = jnp.zeros((bn, GLA_HEADS, GLA_KEY_DIM, GLA_VALUE_DIM), f32)
    _, s_prev = lax.scan(step, s0, (kv.transpose(2, 0, 1, 3, 4), decay.transpose(2, 0, 1, 3)))
    s_prev = s_prev.transpose(1, 2, 0, 3, 4)
    o = o + jnp.einsum('bhnid,bhndv->bhniv', qe, s_prev)
    o = o.transpose(0, 2, 3, 1, 4).reshape(bn, sn, GLA_HEADS, GLA_VALUE_DIM)
    o = rms_norm(o, out_norm).reshape(bn, sn, GLA_V_DIM)
    return (o * jax.nn.silu(r.astype(f32))).astype(r.dtype)


def hier_moe(h, w_rg, b_rg, w_re, b_re, w_g, w_u, w_d):
    bn, sn, d = h.shape
    f32 = jnp.float32
    t = bn * sn
    xt = h.reshape(t, d)
    p_group = jax.nn.softmax((xt @ w_rg + b_rg).astype(f32), axis=-1)
    group = jnp.argmax(p_group, axis=-1)
    p_group_top = jnp.take_along_axis(p_group, group[:, None], axis=-1)
    le = (xt @ w_re + b_re).astype(f32).reshape(t, N_GROUPS, EXPERTS_PER_GROUP)
    le_sel = jnp.take_along_axis(le, group[:, None, None], axis=1)[:, 0]
    top_p, top_i = lax.top_k(jax.nn.softmax(le_sel, axis=-1), TOP_K)
    weights = p_group_top * top_p / jnp.sum(top_p, axis=-1, keepdims=True)
    expert_id = group[:, None] * EXPERTS_PER_GROUP + top_i

    n_assign = t * TOP_K
    e_flat = expert_id.reshape(-1).astype(jnp.int32)
    w_flat = weights.reshape(-1).astype(h.dtype)
    tok_flat = jnp.repeat(jnp.arange(t, dtype=jnp.int32), TOP_K)
    counts = jnp.zeros((N_EXPERTS,), jnp.int32).at[e_flat].add(1)
    padded = ((counts + EXPERT_BLOCK - 1) // EXPERT_BLOCK) * EXPERT_BLOCK
    pad_end = jnp.cumsum(padded)
    pad_start = pad_end - padded
    start = jnp.cumsum(counts) - counts
    order = jnp.argsort(e_flat)
    e_sorted = e_flat[order]
    dest = pad_start[e_sorted] + (jnp.arange(n_assign, dtype=jnp.int32) - start[e_sorted])
    n_blocks = -(-n_assign // EXPERT_BLOCK) + N_EXPERTS
    n_rows = n_blocks * EXPERT_BLOCK
    row_tok = jnp.full((n_rows,), t, jnp.int32).at[dest].set(tok_flat[order])
    row_w = jnp.zeros((n_rows,), h.dtype).at[dest].set(w_flat[order])
    block_expert = jnp.minimum(
        jnp.searchsorted(pad_end, jnp.arange(n_blocks, dtype=jnp.int32) * EXPERT_BLOCK, side='right'),
        N_EXPERTS - 1)
    x_pad = jnp.concatenate([xt, jnp.zeros((1, d), xt.dtype)], axis=0)
    xs = x_pad[row_tok].reshape(n_blocks, EXPERT_BLOCK, d)

    def expert_block(args):
        xb, e = args
        return (jax.nn.silu(xb @ w_g[e]) * (xb @ w_u[e])) @ w_d[e]

    ys = lax.map(expert_block, (xs, block_expert)).reshape(n_rows, d)
    y = jax.ops.segment_sum(ys * row_w[:, None], row_tok, num_segments=t + 1)[:t]
    return y.reshape(bn, sn, d).astype(h.dtype)


def setup_inputs(seed: int = 0) -> dict:
    key = jax.random.key(seed)
    ks = jax.random.split(key, 22)
    f32 = jnp.float32

    def normal(k, shape, scale):
        return jax.random.normal(k, shape, f32) * scale

    def gain(k, shape):
        return 1.0 + 0.01 * jax.random.normal(k, shape, f32)

    return {
        'x': normal(ks[0], (BATCH, SEQ, D_MODEL), 1.0),
        'rel_bias_table': normal(ks[1], (REL_BUCKETS, ATTN_HEADS), 0.1),
        'norm_mix': gain(ks[2], (DEPTH, D_MODEL)),
        'w_in': normal(ks[3], (DEPTH, D_MODEL, D_IN), D_MODEL ** -0.5),
        'conv_w': normal(ks[4], (DEPTH, CONV_WIDTH, CONV_DIM), CONV_WIDTH ** -0.5),
        'conv_b': normal(ks[5], (DEPTH, CONV_DIM), 0.02),
        'q_norm': gain(ks[6], (DEPTH, ATTN_HEAD_DIM)),
        'k_norm': gain(ks[7], (DEPTH, ATTN_HEAD_DIM)),
        'w_gla_alpha': normal(ks[8], (DEPTH, GLA_GATE_RANK, GLA_QK_DIM), GLA_GATE_RANK ** -0.5),
        'b_gla_alpha': normal(ks[9], (DEPTH, GLA_QK_DIM), 0.1),
        'gla_out_norm': gain(ks[10], (DEPTH, GLA_VALUE_DIM)),
        'w_merge_gate': normal(ks[11], (DEPTH, N_BRANCHES, D_MODEL, D_MODEL), D_MODEL ** -0.5),
        'w_branch': normal(ks[12], (DEPTH, N_BRANCHES, BRANCH_DIM, D_MODEL), BRANCH_DIM ** -0.5),
        'w_out': normal(ks[13], (DEPTH, D_MODEL, D_MODEL), D_MODEL ** -0.5),
        'norm_ffn': gain(ks[14], (DEPTH, D_MODEL)),
        'w_router_group': normal(ks[15], (DEPTH, D_MODEL, N_GROUPS), D_MODEL ** -0.5),
        'b_router_group': normal(ks[16], (DEPTH, N_GROUPS), 0.01),
        'w_router_expert': normal(ks[17], (DEPTH, D_MODEL, N_EXPERTS), D_MODEL ** -0.5),
        'b_router_expert': normal(ks[18], (DEPTH, N_EXPERTS), 0.01),
        'w_expert_gate': normal(ks[19], (DEPTH, N_EXPERTS, D_MODEL, EXPERT_FF), D_MODEL ** -0.5),
        'w_expert_up': normal(ks[20], (DEPTH, N_EXPERTS, D_MODEL, EXPERT_FF), D_MODEL ** -0.5),
        'w_expert_down': normal(ks[21], (DEPTH, N_EXPERTS, EXPERT_FF, D_MODEL), EXPERT_FF ** -0.5),
    }


def reference(x, rel_bias_table, norm_mix, w_in, conv_w, conv_b, q_norm, k_norm,
              w_gla_alpha, b_gla_alpha, gla_out_norm, w_merge_gate, w_branch, w_out,
              norm_ffn, w_router_group, b_router_group, w_router_expert, b_router_expert,
              w_expert_gate, w_expert_up, w_expert_down):
    bn, sn, _ = x.shape
    split_points = np.cumsum(IN_SPLITS)[:-1].tolist()
    for l in range(DEPTH):
        h = rms_norm(x, norm_mix[l])
        (cb, cc, cx, aq, ak, av, gq, gk, gv, gr, gz) = jnp.split(h @ w_in[l], split_points, axis=-1)
        y_conv = short_conv(cb, cc, cx, conv_w[l], conv_b[l])
        q = rms_norm(aq.reshape(bn, sn, ATTN_HEADS, ATTN_HEAD_DIM), q_norm[l])
        k = rms_norm(ak.reshape(bn, sn, ATTN_HEADS, ATTN_HEAD_DIM), k_norm[l])
        v = av.reshape(bn, sn, ATTN_HEADS, ATTN_HEAD_DIM)
        y_attn = moba_attention(q, k, v, rel_bias_table)
        y_gla = gla(gq, gk, gv, gr, gz, w_gla_alpha[l], b_gla_alpha[l], gla_out_norm[l])
        merged = jnp.zeros_like(x)
        for n, y_b in enumerate((y_conv, y_attn, y_gla)):
            merged = merged + jax.nn.sigmoid(h @ w_merge_gate[l, n]) * (y_b @ w_branch[l, n])
        x = x + merged @ w_out[l]
        h2 = rms_norm(x, norm_ffn[l])
        x = x + hier_moe(h2, w_router_group[l], b_router_group[l], w_router_expert[l], b_router_expert[l],
                         w_expert_gate[l], w_expert_up[l], w_expert_down[l])
    return x
```

```python
import functools
import math

import jax
import jax.numpy as jnp
import numpy as np
from jax import lax
from jax.experimental import pallas as pl
from jax.experimental.pallas import tpu as pltpu

CONV_DIM = 512
CONV_WIDTH = 3
ATTN_HEADS = 8
ATTN_HEAD_DIM = 64
ATTN_DIM = ATTN_HEADS * ATTN_HEAD_DIM
MOBA_BLOCK = 256
MOBA_TOPK = 3
REL_BUCKETS = 32
REL_MAX_DISTANCE = 128
GLA_HEADS = 4
GLA_KEY_DIM = 64
GLA_VALUE_DIM = 128
GLA_QK_DIM = GLA_HEADS * GLA_KEY_DIM
GLA_V_DIM = GLA_HEADS * GLA_VALUE_DIM
GLA_GATE_RANK = 16
GLA_GATE_TEMP = 16.0
GLA_CHUNK = 64
N_GROUPS = 4
EXPERTS_PER_GROUP = 8
N_EXPERTS = N_GROUPS * EXPERTS_PER_GROUP
TOP_K = 2
EXPERT_FF = 512
EXPERT_BLOCK = 256
RMS_EPS = 1e-6

LANES = 128
VMEM_LIMIT_BYTES = 56 * 1024 * 1024

MASK_VALUE = -1e30
ROUTE_LANES = 128
ROUTE_OUT = 8
GZ_PAD = LANES

F32 = jnp.float32
BF16 = jnp.bfloat16


def _cparams(n_axes):
    return pltpu.CompilerParams(
        dimension_semantics=("arbitrary",) * n_axes,
        vmem_limit_bytes=VMEM_LIMIT_BYTES,
    )


def _rms(x, gain):
    return x * lax.rsqrt(jnp.mean(x * x, axis=-1, keepdims=True) + RMS_EPS) * gain


def _split_bf16(x):
    hi = x.astype(BF16)
    lo = (x - hi.astype(F32)).astype(BF16)
    return hi, lo


def _dot(a, b):
    return jnp.dot(a, b, preferred_element_type=F32)


def _dot_nt(a, b):
    return lax.dot_general(a, b, (((1,), (1,)), ((), ())), preferred_element_type=F32)


def _dot_tn(a, b):
    return lax.dot_general(a, b, (((0,), (0,)), ((), ())), preferred_element_type=F32)


def _inproj_kernel(x_ref, gmix_ref, wconv_ref, wattn_ref, wgla_ref, wgz_ref, convw_ref, convb_ref,
                   qn_ref, kn_ref, hsum_ref,
                   yconv_ref, q_ref, k_ref, v_ref, kmean_ref, gq_ref, gk_ref, gv_ref, gr_ref, gz_ref,
                   carry_ref):
    s_idx = pl.program_id(1)
    ts = x_ref.shape[0]
    h = _rms(x_ref[...], gmix_ref[...]).astype(BF16)

    c = _dot(h, wconv_ref[...])
    cb = c[:, :CONV_DIM]
    u = c[:, CONV_DIM:2 * CONV_DIM] * c[:, 2 * CONV_DIM:]

    @pl.when(s_idx == 0)
    def _():
        carry_ref[...] = jnp.zeros_like(carry_ref)

    prev = carry_ref[...]
    row = lax.broadcasted_iota(jnp.int32, u.shape, 0)
    u1 = pltpu.roll(u, 1, 0)
    u1 = jnp.where(row == 0, prev[7:8, :], u1)
    u2 = pltpu.roll(u, 2, 0)
    u2 = jnp.where(row == 0, prev[6:7, :], jnp.where(row == 1, prev[7:8, :], u2))
    carry_ref[...] = u[ts - 8:, :]
    y = convb_ref[...] + convw_ref[0:1, :] * u2
    y = y + convw_ref[1:2, :] * u1
    y = y + convw_ref[2:3, :] * u
    yconv_ref[...] = (cb * y).astype(BF16)

    a = _dot(h, wattn_ref[...])
    hsum = hsum_ref[...]

    def head_norm(t, gain):
        hi, lo = _split_bf16(t * t)
        ss = _dot(hi, hsum) + _dot(lo, hsum)
        return t * lax.rsqrt(ss * (1.0 / ATTN_HEAD_DIM) + RMS_EPS) * gain

    qn = head_norm(a[:, :ATTN_DIM], qn_ref[...])
    kn = head_norm(a[:, ATTN_DIM:2 * ATTN_DIM], kn_ref[...])
    q_ref[...] = (qn * (ATTN_HEAD_DIM ** -0.5)).astype(BF16)
    k_ref[...] = kn.astype(BF16)
    v_ref[...] = a[:, 2 * ATTN_DIM:].astype(BF16)
    kmean_ref[...] = jnp.mean(kn, axis=0, keepdims=True)

    g = _dot(h, wgla_ref[...])
    gq_ref[...] = g[:, :GLA_QK_DIM]
    gk_ref[...] = g[:, GLA_QK_DIM:2 * GLA_QK_DIM]
    gv_ref[...] = g[:, 2 * GLA_QK_DIM:2 * GLA_QK_DIM + GLA_V_DIM]
    gr_ref[...] = g[:, 2 * GLA_QK_DIM + GLA_V_DIM:]
    gz_ref[...] = _dot(h, wgz_ref[...])


def _inproj(x, gmix, wconv, wattn, wgla, wgz, convw, convb, qn, kn, hsum):
    b, s, d = x.shape
    ts = MOBA_BLOCK
    nb = s // ts
    tok = lambda w: pl.BlockSpec((None, ts, w), lambda bi, si: (bi, si, 0))
    full = lambda arr: pl.BlockSpec(arr.shape, lambda bi, si: (0,) * arr.ndim)
    out_shapes = (
        jax.ShapeDtypeStruct((b, s, CONV_DIM), BF16),
        jax.ShapeDtypeStruct((b, s, ATTN_DIM), BF16),
        jax.ShapeDtypeStruct((b, s, ATTN_DIM), BF16),
        jax.ShapeDtypeStruct((b, s, ATTN_DIM), BF16),
        jax.ShapeDtypeStruct((b, nb, 1, ATTN_DIM), F32),
        jax.ShapeDtypeStruct((b, s, GLA_QK_DIM), F32),
        jax.ShapeDtypeStruct((b, s, GLA_QK_DIM), F32),
        jax.ShapeDtypeStruct((b, s, GLA_V_DIM), F32),
        jax.ShapeDtypeStruct((b, s, GLA_V_DIM), F32),
        jax.ShapeDtypeStruct((b, s, GZ_PAD), F32),
    )
    out_specs = (
        tok(CONV_DIM), tok(ATTN_DIM), tok(ATTN_DIM), tok(ATTN_DIM),
        pl.BlockSpec((None, None, 1, ATTN_DIM), lambda bi, si: (bi, si, 0, 0)),
        tok(GLA_QK_DIM), tok(GLA_QK_DIM), tok(GLA_V_DIM), tok(GLA_V_DIM), tok(GZ_PAD),
    )
    ins = (x, gmix, wconv, wattn, wgla, wgz, convw, convb, qn, kn, hsum)
    in_specs = [tok(d)] + [full(a) for a in ins[1:]]
    return pl.pallas_call(
        _inproj_kernel,
        grid=(b, nb),
        in_specs=in_specs,
        out_specs=out_specs,
        out_shape=out_shapes,
        scratch_shapes=[pltpu.VMEM((8, CONV_DIM), F32)],
        compiler_params=_cparams(2),
        name="inproj",
    )(*ins)


def _select_kernel(q_ref, k_ref, kmean_ref, qp_ref, kp_ref):
    own = pl.program_id(1)
    tq = q_ref.shape[0]
    nb = kmean_ref.shape[0]
    half = ATTN_HEAD_DIM
    lane = lax.broadcasted_iota(jnp.int32, (tq, LANES), 1)
    blk = lax.broadcasted_iota(jnp.int32, (LANES, tq), 0).astype(F32)
    own_f = own.astype(F32)
    onehot = jnp.where(lane - half == own, 1.0, 0.0).astype(F32)
    kmean = kmean_ref[...].astype(BF16)
    for p in range(ATTN_HEADS // 2):
        qpair = q_ref[:, p * LANES:(p + 1) * LANES]
        kpair = k_ref[:, p * LANES:(p + 1) * LANES].astype(F32)
        kmpair = kmean[:, p * LANES:(p + 1) * LANES]
        qpair_f = qpair.astype(F32)
        for sub in range(2):
            h = 2 * p + sub
            lane_sel = (lane >= sub * half) & (lane < (sub + 1) * half)
            qh = jnp.where(lane_sel, qpair_f, 0.0).astype(BF16)
            gate_t = _dot_nt(kmpair, qh)
            gate_t = jnp.concatenate(
                [gate_t, jnp.full((LANES - nb, tq), -jnp.inf, F32)], axis=0)
            g = jnp.where(blk < own_f, gate_t, -jnp.inf)
            alive = jnp.where(blk < nb, 1.0, 0.0)
            sel = jnp.where(blk == own_f, 1.0, 0.0)
            for r in range(MOBA_TOPK):
                ga = jnp.where(alive > 0.0, g, -jnp.inf)
                mx = jnp.max(ga, axis=0, keepdims=True)
                cand = jnp.where((alive > 0.0) & (g == mx), blk, 2.0 * LANES)
                first = jnp.min(cand, axis=0, keepdims=True)
                hit = blk == first
                sel = jnp.where(hit, jnp.maximum(sel, jnp.where(own_f > r, 1.0, 0.0)), sel)
                alive = jnp.where(hit, 0.0, alive)
            m_t = jnp.where(sel > 0.0, 0.0, MASK_VALUE).astype(F32)
            m = jnp.transpose(m_t)
            m = pltpu.roll(m, half, 1)
            m = jnp.where((lane >= half) & (lane < half + nb), m, 0.0)
            qs = qpair_f if sub == 0 else pltpu.roll(qpair_f, half, 1)
            ks = kpair if sub == 0 else pltpu.roll(kpair, half, 1)
            qp_ref[h] = jnp.where(lane < half, qs, m).astype(BF16)
            kp_ref[h] = jnp.where(lane < half, ks, onehot).astype(BF16)


def _select(q, k, kmean):
    b, s, _ = q.shape
    tq = MOBA_BLOCK
    nb = s // tq
    assert nb <= LANES - ATTN_HEAD_DIM, "block one-hot must fit beside the head dim in one lane tile"
    tok = pl.BlockSpec((None, tq, ATTN_DIM), lambda bi, si: (bi, si, 0))
    slab = pl.BlockSpec((None, ATTN_HEADS, tq, LANES), lambda bi, si: (bi, 0, si, 0))
    shp = jax.ShapeDtypeStruct((b, ATTN_HEADS, s, LANES), BF16)
    return pl.pallas_call(
        _select_kernel,
        grid=(b, nb),
        in_specs=[tok, tok, pl.BlockSpec((None, nb, ATTN_DIM), lambda bi, si: (bi, 0, 0))],
        out_specs=(slab, slab),
        out_shape=(shp, shp),
        compiler_params=_cparams(2),
        name="moba_select",
    )(q, k, kmean)


def _attn_kernel(qp_ref, kp_ref, v_ref, bias0_ref, bias1_ref, o_ref, m_ref, l_ref, acc_ref):
    own = pl.program_id(2)
    tq = qp_ref.shape[1]
    lane = lax.broadcasted_iota(jnp.int32, (tq, LANES), 1)

    def step(hh, j, bias):
        start = pl.multiple_of(j * tq, tq)
        kj = kp_ref[hh, pl.ds(start, tq), :]
        vj = v_ref[pl.ds(start, tq), :]
        s = _dot_nt(qp_ref[hh], kj)
        if bias is not None:
            s = s + bias
        m_prev = m_ref[hh]
        m_new = jnp.maximum(m_prev, jnp.max(s, axis=1, keepdims=True))
        alpha = jnp.exp(m_prev - m_new)
        p = jnp.exp(s - m_new)
        l_ref[hh] = alpha * l_ref[hh] + jnp.sum(p, axis=1, keepdims=True)
        acc_ref[hh] = alpha * acc_ref[hh] + _dot(p.astype(BF16), vj)
        m_ref[hh] = m_new

    outs = []
    for hh in range(2):
        m_ref[hh] = jnp.full((tq, 1), MASK_VALUE, F32)
        l_ref[hh] = jnp.zeros((tq, 1), F32)
        acc_ref[hh] = jnp.zeros((tq, LANES), F32)
        step(hh, own, bias0_ref[hh])

        @pl.when(own >= 1)
        def _():
            step(hh, own - 1, bias1_ref[hh])

        def far(j, carry):
            step(hh, j, None)
            return carry

        lax.fori_loop(0, jnp.maximum(own - 1, 0), far, 0)
        outs.append(acc_ref[hh] / l_ref[hh])
    o_ref[...] = jnp.where(lane < ATTN_HEAD_DIM, outs[0], outs[1]).astype(BF16)


def _attention(qp, kp, v, bias0, bias1):
    b, nh, s, _ = qp.shape
    tq = MOBA_BLOCK
    nq = s // tq
    return pl.pallas_call(
        _attn_kernel,
        grid=(b, nh // 2, nq),
        in_specs=[
            pl.BlockSpec((None, 2, tq, LANES), lambda bi, pi, qi: (bi, pi, qi, 0)),
            pl.BlockSpec((None, 2, s, LANES), lambda bi, pi, qi: (bi, pi, 0, 0)),
            pl.BlockSpec((None, s, LANES), lambda bi, pi, qi: (bi, 0, pi)),
            pl.BlockSpec((2, tq, tq), lambda bi, pi, qi: (pi, 0, 0)),
            pl.BlockSpec((2, tq, tq), lambda bi, pi, qi: (pi, 0, 0)),
        ],
        out_specs=pl.BlockSpec((None, tq, LANES), lambda bi, pi, qi: (bi, qi, pi)),
        out_shape=jax.ShapeDtypeStruct((b, s, ATTN_DIM), BF16),
        scratch_shapes=[
            pltpu.VMEM((2, tq, 1), F32),
            pltpu.VMEM((2, tq, 1), F32),
            pltpu.VMEM((2, tq, LANES), F32),
        ],
        compiler_params=_cparams(3),
        name="moba_attention",
    )(qp, kp, v, bias0, bias1)


def _t5_bucket(rel):
    n = jnp.maximum(rel, 0)
    max_exact = REL_BUCKETS // 2
    scaled = (jnp.log(jnp.maximum(n, max_exact).astype(F32) / max_exact)
              / math.log(REL_MAX_DISTANCE / max_exact))
    large = jnp.minimum(max_exact + (scaled * (REL_BUCKETS - max_exact)).astype(jnp.int32), REL_BUCKETS - 1)
    return jnp.where(n < max_exact, n, large)


def _attn_biases(rel_bias_table):
    table = rel_bias_table.astype(F32).T
    pos = jnp.arange(MOBA_BLOCK)
    rel0 = pos[:, None] - pos[None, :]
    far = table[:, REL_BUCKETS - 1][:, None, None]
    bias0 = jnp.where(rel0[None] >= 0, table[:, _t5_bucket(rel0)] - far, MASK_VALUE)
    bias1 = table[:, _t5_bucket(rel0 + MOBA_BLOCK)] - far
    return bias0, bias1


def _gla_kernel(gq_ref, gk_ref, gv_ref, gr_ref, gz_ref, wa_ref, ba_ref, on_ref, y_ref, state_ref):
    s_idx = pl.program_id(1)
    tg = gq_ref.shape[0]
    dk, dv, ck = GLA_KEY_DIM, GLA_VALUE_DIM, GLA_CHUNK

    @pl.when(s_idx == 0)
    def _():
        state_ref[...] = jnp.zeros_like(state_ref)

    z = gz_ref[...].astype(BF16)
    log_a = jax.nn.log_sigmoid(_dot(z, wa_ref[...]) + ba_ref[...]) / GLA_GATE_TEMP
    row = lax.broadcasted_iota(jnp.int32, log_a.shape, 0) % ck
    bcum = log_a
    shift = 1
    while shift < ck:
        bcum = bcum + jnp.where(row >= shift, pltpu.roll(bcum, shift, 0), 0.0)
        shift *= 2
    q = gq_ref[...] * (dk ** -0.5)
    k = gk_ref[...]
    tri = (lax.broadcasted_iota(jnp.int32, (ck, ck), 0) >= lax.broadcasted_iota(jnp.int32, (ck, ck), 1))
    for c in range(tg // ck):
        rows = slice(c * ck, (c + 1) * ck)
        for h in range(GLA_HEADS):
            kcols = slice(h * dk, (h + 1) * dk)
            vcols = slice(h * dv, (h + 1) * dv)
            bc = bcum[rows, kcols]
            btot = bc[ck - 1:ck, :]
            qe = (q[rows, kcols] * jnp.exp(bc)).astype(BF16)
            ke = (k[rows, kcols] * jnp.exp(-bc)).astype(BF16)
            kd = (k[rows, kcols] * jnp.exp(btot - bc)).astype(BF16)
            vc = gv_ref[rows, vcols].astype(BF16)
            att = jnp.where(tri, _dot_nt(qe, ke), 0.0).astype(BF16)
            st = state_ref[h]
            o = _dot(att, vc) + _dot_nt(qe, st.astype(BF16))
            state_ref[h] = st * jnp.exp(btot) + _dot_tn(vc, kd)
            o = _rms(o, on_ref[...])
            r = gr_ref[rows, vcols]
            y_ref[rows, vcols] = (o * (r * jax.nn.sigmoid(r))).astype(BF16)


def _gla(gq, gk, gv, gr, gz, wa, ba, on):
    b, s, _ = gq.shape
    tg = 256
    tok = lambda w: pl.BlockSpec((None, tg, w), lambda bi, si: (bi, si, 0))
    full = lambda arr: pl.BlockSpec(arr.shape, lambda bi, si: (0,) * arr.ndim)
    return pl.pallas_call(
        _gla_kernel,
        grid=(b, s // tg),
        in_specs=[tok(GLA_QK_DIM), tok(GLA_QK_DIM), tok(GLA_V_DIM), tok(GLA_V_DIM), tok(GZ_PAD),
                  full(wa), full(ba), full(on)],
        out_specs=tok(GLA_V_DIM),
        out_shape=jax.ShapeDtypeStruct((b, s, GLA_V_DIM), BF16),
        scratch_shapes=[pltpu.VMEM((GLA_HEADS, GLA_VALUE_DIM, GLA_KEY_DIM), F32)],
        compiler_params=_cparams(2),
        name="gla",
    )(gq, gk, gv, gr, gz, wa, ba, on)


def _merge_route_kernel(x_ref, yc_ref, ya_ref, yg_ref, gmix_ref, wg_ref, wb_ref, wo_ref, gffn_ref,
                        wr_hi_ref, wr_lo_ref, br_ref, xo_ref, h2_ref, route_ref):
    x = x_ref[...]
    h = _rms(x, gmix_ref[...]).astype(BF16)
    merged = None
    for n, y_ref in enumerate((yc_ref, ya_ref, yg_ref)):
        term = jax.nn.sigmoid(_dot(h, wg_ref[n])) * _dot(y_ref[...], wb_ref[n])
        merged = term if merged is None else merged + term
    xo = x + _dot(merged.astype(BF16), wo_ref[...])
    xo_ref[...] = xo
    h2 = _rms(xo, gffn_ref[...])
    h2_ref[...] = h2

    h_hi, h_lo = _split_bf16(h2)
    logits = (_dot(h_hi, wr_hi_ref[...]) + _dot(h_lo, wr_hi_ref[...]) + _dot(h_hi, wr_lo_ref[...])
              + br_ref[...])
    lane = lax.broadcasted_iota(jnp.int32, logits.shape, 1).astype(F32)
    big = 4.0 * ROUTE_LANES
    lg = jnp.where(lane < N_GROUPS, logits, -jnp.inf)
    gmax = jnp.max(lg, axis=1, keepdims=True)
    gidx = jnp.min(jnp.where(lg == gmax, lane, big), axis=1, keepdims=True)
    p_group_top = 1.0 / jnp.sum(jnp.exp(lg - gmax), axis=1, keepdims=True)
    lo_lane = N_GROUPS + gidx * EXPERTS_PER_GROUP
    le = jnp.where((lane >= lo_lane) & (lane < lo_lane + EXPERTS_PER_GROUP), logits, -jnp.inf)
    emax = jnp.max(le, axis=1, keepdims=True)
    i1 = jnp.min(jnp.where(le == emax, lane, big), axis=1, keepdims=True)
    esum = jnp.sum(jnp.exp(le - emax), axis=1, keepdims=True)
    le2 = jnp.where(lane == i1, -jnp.inf, le)
    emax2 = jnp.max(le2, axis=1, keepdims=True)
    i2 = jnp.min(jnp.where(le2 == emax2, lane, big), axis=1, keepdims=True)
    p1 = 1.0 / esum
    p2 = jnp.exp(emax2 - emax) / esum
    psum = p1 + p2
    w1 = p_group_top * p1 / psum
    w2 = p_group_top * p2 / psum
    e1 = i1 - N_GROUPS
    e2 = i2 - N_GROUPS
    rl = lax.broadcasted_iota(jnp.int32, (x.shape[0], ROUTE_OUT), 1)
    route_ref[...] = jnp.where(rl == 0, e1, jnp.where(rl == 1, e2, jnp.where(rl == 2, w1, jnp.where(rl == 3, w2, 0.0))))


def _merge_route(x2d, yc, ya, yg, gmix, wg, wb, wo, gffn, wr_hi, wr_lo, br):
    t, d = x2d.shape
    tm = 256
    tok = lambda w: pl.BlockSpec((tm, w), lambda i: (i, 0))
    full = lambda arr: pl.BlockSpec(arr.shape, lambda i: (0,) * arr.ndim)
    ins = (x2d, yc, ya, yg, gmix, wg, wb, wo, gffn, wr_hi, wr_lo, br)
    return pl.pallas_call(
        _merge_route_kernel,
        grid=(t // tm,),
        in_specs=[tok(d), tok(CONV_DIM), tok(ATTN_DIM), tok(GLA_V_DIM)] + [full(a) for a in ins[4:]],
        out_specs=(tok(d), tok(d), tok(ROUTE_OUT)),
        out_shape=(jax.ShapeDtypeStruct((t, d), F32), jax.ShapeDtypeStruct((t, d), F32),
                   jax.ShapeDtypeStruct((t, ROUTE_OUT), F32)),
        compiler_params=_cparams(1),
        name="merge_route",
    )(*ins)


def _expert_kernel(blk_expert_ref, n_used_ref, row_tok_ref, h2_hbm, roww_ref, wg_ref, wu_ref, wd_ref,
                   ys_ref, xbuf, sem):
    i = pl.program_id(0)
    rows = xbuf.shape[0]

    @pl.when(i < n_used_ref[0])
    def _():
        def row_copy(r):
            tok = row_tok_ref[0, r]
            return pltpu.make_async_copy(h2_hbm.at[pl.ds(tok, 1)], xbuf.at[pl.ds(r, 1)], sem)

        def issue(r, carry):
            row_copy(r).start()
            return carry

        lax.fori_loop(0, rows, issue, 0)

        def drain(r, carry):
            row_copy(r).wait()
            return carry

        lax.fori_loop(0, rows, drain, 0)
        xb = xbuf[...].astype(BF16)
        gate = _dot(xb, wg_ref[...])
        up = _dot(xb, wu_ref[...])
        act = (gate * jax.nn.sigmoid(gate) * up).astype(BF16)
        ys_ref[...] = _dot(act, wd_ref[...]) * roww_ref[...]

    @pl.when(i >= n_used_ref[0])
    def _():
        ys_ref[...] = jnp.zeros_like(ys_ref)


def _experts(blk_expert, n_used, row_tok, h2, row_w, wg, wu, wd):
    t, d = h2.shape
    n_blocks = blk_expert.shape[0]
    rb = EXPERT_BLOCK
    grid_spec = pltpu.PrefetchScalarGridSpec(
        num_scalar_prefetch=2,
        grid=(n_blocks,),
        in_specs=[
            pl.BlockSpec((None, 1, rb), lambda i, be, nu: (i, 0, 0), memory_space=pltpu.SMEM),
            pl.BlockSpec(memory_space=pl.ANY),
            pl.BlockSpec((rb, 1), lambda i, be, nu: (i, 0)),
            pl.BlockSpec((None, d, EXPERT_FF), lambda i, be, nu: (be[i], 0, 0)),
            pl.BlockSpec((None, d, EXPERT_FF), lambda i, be, nu: (be[i], 0, 0)),
            pl.BlockSpec((None, EXPERT_FF, d), lambda i, be, nu: (be[i], 0, 0)),
        ],
        out_specs=pl.BlockSpec((rb, d), lambda i, be, nu: (i, 0)),
        scratch_shapes=[pltpu.VMEM((rb, d), F32), pltpu.SemaphoreType.DMA(())],
    )
    return pl.pallas_call(
        _expert_kernel,
        grid_spec=grid_spec,
        out_shape=jax.ShapeDtypeStruct((n_blocks * rb, d), F32),
        compiler_params=_cparams(1),
        name="moe_experts",
    )(blk_expert, n_used, row_tok.reshape(n_blocks, 1, rb), h2, row_w.reshape(n_blocks * rb, 1), wg, wu, wd)


def _combine_kernel(dest_ref, x_ref, ys_hbm, o_ref, buf, sem):
    tc = x_ref.shape[0]

    def row_copy(r, kk):
        src = dest_ref[0, kk * tc + r]
        return pltpu.make_async_copy(ys_hbm.at[pl.ds(src, 1)], buf.at[kk, pl.ds(r, 1)], sem)

    def issue(r, carry):
        row_copy(r, 0).start()
        row_copy(r, 1).start()
        return carry

    lax.fori_loop(0, tc, issue, 0)

    def drain(r, carry):
        row_copy(r, 0).wait()
        row_copy(r, 1).wait()
        return carry

    lax.fori_loop(0, tc, drain, 0)
    o_ref[...] = x_ref[...] + (buf[0] + buf[1])


def _combine(dest, x2d, ys):
    t, d = x2d.shape
    tc = 256
    nt = t // tc
    dest_tiles = dest.reshape(nt, tc, TOP_K).transpose(0, 2, 1).reshape(nt, 1, TOP_K * tc)
    return pl.pallas_call(
        _combine_kernel,
        grid=(nt,),
        in_specs=[
            pl.BlockSpec((None, 1, TOP_K * tc), lambda i: (i, 0, 0), memory_space=pltpu.SMEM),
            pl.BlockSpec((tc, d), lambda i: (i, 0)),
            pl.BlockSpec(memory_space=pl.ANY),
        ],
        out_specs=pl.BlockSpec((tc, d), lambda i: (i, 0)),
        out_shape=jax.ShapeDtypeStruct((t, d), F32),
        scratch_shapes=[pltpu.VMEM((TOP_K, tc, d), F32), pltpu.SemaphoreType.DMA(())],
        compiler_params=_cparams(1),
        name="moe_combine",
    )(dest_tiles, x2d, ys)


def _dispatch_plan(route, t):
    e_flat = route[:, :TOP_K].astype(jnp.int32).reshape(-1)
    w_flat = route[:, TOP_K:2 * TOP_K].reshape(-1)
    n_assign = t * TOP_K
    tok_flat = jnp.repeat(jnp.arange(t, dtype=jnp.int32), TOP_K)
    counts = jnp.zeros((N_EXPERTS,), jnp.int32).at[e_flat].add(1)
    padded = ((counts + EXPERT_BLOCK - 1) // EXPERT_BLOCK) * EXPERT_BLOCK
    pad_end = jnp.cumsum(padded)
    pad_start = pad_end - padded
    start = jnp.cumsum(counts) - counts
    order = jnp.argsort(e_flat)
    e_sorted = e_flat[order]
    dest_sorted = pad_start[e_sorted] + (jnp.arange(n_assign, dtype=jnp.int32) - start[e_sorted])
    n_blocks = -(-n_assign // EXPERT_BLOCK) + N_EXPERTS
    n_rows = n_blocks * EXPERT_BLOCK
    row_tok = jnp.zeros((n_rows,), jnp.int32).at[dest_sorted].set(tok_flat[order])
    row_w = jnp.zeros((n_rows,), F32).at[dest_sorted].set(w_flat[order])
    dest = jnp.zeros((n_assign,), jnp.int32).at[order].set(dest_sorted).reshape(t, TOP_K)
    blk_expert = jnp.minimum(
        jnp.searchsorted(pad_end, jnp.arange(n_blocks, dtype=jnp.int32) * EXPERT_BLOCK, side="right"),
        N_EXPERTS - 1).astype(jnp.int32)
    n_used = (pad_end[-1:] // EXPERT_BLOCK).astype(jnp.int32)
    return blk_expert, n_used, row_tok, row_w, dest


def kernel(x, rel_bias_table, norm_mix, w_in, conv_w, conv_b, q_norm, k_norm, w_gla_alpha, b_gla_alpha,
           gla_out_norm, w_merge_gate, w_branch, w_out, norm_ffn, w_router_group, b_router_group,
           w_router_expert, b_router_expert, w_expert_gate, w_expert_up, w_expert_down):
    b, s, d = x.shape
    t = b * s
    depth = w_in.shape[0]
    assert s % MOBA_BLOCK == 0 and t % EXPERT_BLOCK == 0
    bias0, bias1 = _attn_biases(rel_bias_table)
    head_id = jnp.arange(ATTN_DIM) // ATTN_HEAD_DIM
    hsum = (head_id[:, None] == head_id[None, :]).astype(BF16)
    c3 = 3 * CONV_DIM
    a3 = c3 + 3 * ATTN_DIM
    g3 = a3 + 2 * GLA_QK_DIM + 2 * GLA_V_DIM
    for l in range(depth):
        w_l = w_in[l].astype(BF16)
        w_gz = jnp.pad(w_l[:, g3:], ((0, 0), (0, GZ_PAD - GLA_GATE_RANK)))
        w_alpha = jnp.pad(w_gla_alpha[l].astype(BF16), ((0, GZ_PAD - GLA_GATE_RANK), (0, 0)))
        yconv, q, k, v, kmean, gq, gk, gv, gr, gz = _inproj(
            x, norm_mix[l][None], w_l[:, :c3], w_l[:, c3:a3], w_l[:, a3:g3], w_gz,
            conv_w[l], conv_b[l][None],
            jnp.tile(q_norm[l], ATTN_HEADS)[None], jnp.tile(k_norm[l], ATTN_HEADS)[None], hsum)
        qp, kp = _select(q, k, kmean.reshape(b, s // MOBA_BLOCK, ATTN_DIM))
        yattn = _attention(qp, kp, v, bias0, bias1)
        ygla = _gla(gq, gk, gv, gr, gz, w_alpha, b_gla_alpha[l][None],
                    gla_out_norm[l][None])
        w_r = jnp.concatenate([w_router_group[l], w_router_expert[l]], axis=1)
        w_r = jnp.pad(w_r, ((0, 0), (0, ROUTE_LANES - w_r.shape[1])))
        b_r = jnp.pad(jnp.concatenate([b_router_group[l], b_router_expert[l]]),
                      (0, ROUTE_LANES - N_GROUPS - N_EXPERTS))[None]
        wr_hi, wr_lo = _split_bf16(w_r)
        xo, h2, route = _merge_route(
            x.reshape(t, d), yconv.reshape(t, -1), yattn.reshape(t, -1), ygla.reshape(t, -1),
            norm_mix[l][None], w_merge_gate[l].astype(BF16), w_branch[l].astype(BF16),
            w_out[l].astype(BF16), norm_ffn[l][None], wr_hi, wr_lo, b_r)
        blk_expert, n_used, row_tok, row_w, dest = _dispatch_plan(route, t)
        ys = _experts(blk_expert, n_used, row_tok, h2, row_w, w_expert_gate[l].astype(BF16),
                      w_expert_up[l].astype(BF16), w_expert_down[l].astype(BF16))
        x = _combine(dest, xo, ys).reshape(b, s, d)
    return x
```

```python
import functools
import math

import jax
import jax.numpy as jnp
import numpy as np
from jax import lax
from jax.experimental import pallas as pl
from jax.experimental.pallas import tpu as pltpu

CONV_DIM = 512
CONV_WIDTH = 3
ATTN_HEADS = 8
ATTN_HEAD_DIM = 64
ATTN_DIM = ATTN_HEADS * ATTN_HEAD_DIM
MOBA_BLOCK = 256
MOBA_TOPK = 3
REL_BUCKETS = 32
REL_MAX_DISTANCE = 128
GLA_HEADS = 4
GLA_KEY_DIM = 64
GLA_VALUE_DIM = 128
GLA_QK_DIM = GLA_HEADS * GLA_KEY_DIM
GLA_V_DIM = GLA_HEADS * GLA_VALUE_DIM
GLA_GATE_RANK = 16
GLA_GATE_TEMP = 16.0
GLA_CHUNK = 64
N_GROUPS = 4
EXPERTS_PER_GROUP = 8
N_EXPERTS = N_GROUPS * EXPERTS_PER_GROUP
TOP_K = 2
EXPERT_FF = 512
EXPERT_BLOCK = 256
RMS_EPS = 1e-6

LANES = 128
VMEM_LIMIT_BYTES = 56 * 1024 * 1024

MASK_VALUE = -1e30
ROUTE_LANES = 128
ROUTE_OUT = 8
ATTN_FAR_CHUNK = 4
ATTN_TAIL = 5
GZ_PAD = LANES

F32 = jnp.float32
BF16 = jnp.bfloat16


def _cparams(n_axes):
    return pltpu.CompilerParams(
        dimension_semantics=("arbitrary",) * n_axes,
        vmem_limit_bytes=VMEM_LIMIT_BYTES,
    )


def _rms(x, gain):
    return x * lax.rsqrt(jnp.mean(x * x, axis=-1, keepdims=True) + RMS_EPS) * gain


def _split_bf16(x):
    hi = x.astype(BF16)
    lo = (x - hi.astype(F32)).astype(BF16)
    return hi, lo


def _dot(a, b):
    return jnp.dot(a, b, preferred_element_type=F32)


def _dot_nt(a, b):
    return lax.dot_general(a, b, (((1,), (1,)), ((), ())), preferred_element_type=F32)


def _dot_tn(a, b):
    return lax.dot_general(a, b, (((0,), (0,)), ((), ())), preferred_element_type=F32)


def _inproj_kernel(x_ref, gmix_ref, wconv_ref, wattn_ref, wgla_ref, wgz_ref, convw_ref, convb_ref,
                   qn_ref, kn_ref, hsum_ref,
                   yconv_ref, q_ref, k_ref, v_ref, kmean_ref, gq_ref, gk_ref, gv_ref, gr_ref, gz_ref,
                   carry_ref):
    s_idx = pl.program_id(1)
    ts = x_ref.shape[0]
    h = _rms(x_ref[...], gmix_ref[...]).astype(BF16)

    c = _dot(h, wconv_ref[...])
    cb = c[:, :CONV_DIM]
    u = c[:, CONV_DIM:2 * CONV_DIM] * c[:, 2 * CONV_DIM:]

    @pl.when(s_idx == 0)
    def _():
        carry_ref[...] = jnp.zeros_like(carry_ref)

    prev = carry_ref[...]
    row = lax.broadcasted_iota(jnp.int32, u.shape, 0)
    u1 = pltpu.roll(u, 1, 0)
    u1 = jnp.where(row == 0, prev[7:8, :], u1)
    u2 = pltpu.roll(u, 2, 0)
    u2 = jnp.where(row == 0, prev[6:7, :], jnp.where(row == 1, prev[7:8, :], u2))
    carry_ref[...] = u[ts - 8:, :]
    y = convb_ref[...] + convw_ref[0:1, :] * u2
    y = y + convw_ref[1:2, :] * u1
    y = y + convw_ref[2:3, :] * u
    yconv_ref[...] = (cb * y).astype(BF16)

    a = _dot(h, wattn_ref[...])
    hsum = hsum_ref[...]

    def head_norm(t, gain):
        hi, lo = _split_bf16(t * t)
        ss = _dot(hi, hsum) + _dot(lo, hsum)
        return t * lax.rsqrt(ss * (1.0 / ATTN_HEAD_DIM) + RMS_EPS) * gain

    qn = head_norm(a[:, :ATTN_DIM], qn_ref[...])
    kn = head_norm(a[:, ATTN_DIM:2 * ATTN_DIM], kn_ref[...])
    q_ref[...] = (qn * (ATTN_HEAD_DIM ** -0.5)).astype(BF16)
    k_ref[...] = kn.astype(BF16)
    v_ref[...] = a[:, 2 * ATTN_DIM:].astype(BF16)
    kmean_ref[...] = jnp.mean(kn, axis=0, keepdims=True)

    g = _dot(h, wgla_ref[...])
    gq_ref[...] = g[:, :GLA_QK_DIM]
    gk_ref[...] = g[:, GLA_QK_DIM:2 * GLA_QK_DIM]
    gv_ref[...] = g[:, 2 * GLA_QK_DIM:2 * GLA_QK_DIM + GLA_V_DIM]
    gr_ref[...] = g[:, 2 * GLA_QK_DIM + GLA_V_DIM:]
    gz_ref[...] = _dot(h, wgz_ref[...])


def _inproj(x, gmix, wconv, wattn, wgla, wgz, convw, convb, qn, kn, hsum):
    b, s, d = x.shape
    ts = MOBA_BLOCK
    nb = s // ts
    tok = lambda w: pl.BlockSpec((None, ts, w), lambda bi, si: (bi, si, 0))
    full = lambda arr: pl.BlockSpec(arr.shape, lambda bi, si: (0,) * arr.ndim)
    out_shapes = (
        jax.ShapeDtypeStruct((b, s, CONV_DIM), BF16),
        jax.ShapeDtypeStruct((b, s, ATTN_DIM), BF16),
        jax.ShapeDtypeStruct((b, s, ATTN_DIM), BF16),
        jax.ShapeDtypeStruct((b, s, ATTN_DIM), BF16),
        jax.ShapeDtypeStruct((b, nb, 1, ATTN_DIM), F32),
        jax.ShapeDtypeStruct((b, s, GLA_QK_DIM), F32),
        jax.ShapeDtypeStruct((b, s, GLA_QK_DIM), F32),
        jax.ShapeDtypeStruct((b, s, GLA_V_DIM), F32),
        jax.ShapeDtypeStruct((b, s, GLA_V_DIM), F32),
        jax.ShapeDtypeStruct((b, s, GZ_PAD), F32),
    )
    out_specs = (
        tok(CONV_DIM), tok(ATTN_DIM), tok(ATTN_DIM), tok(ATTN_DIM),
        pl.BlockSpec((None, None, 1, ATTN_DIM), lambda bi, si: (bi, si, 0, 0)),
        tok(GLA_QK_DIM), tok(GLA_QK_DIM), tok(GLA_V_DIM), tok(GLA_V_DIM), tok(GZ_PAD),
    )
    ins = (x, gmix, wconv, wattn, wgla, wgz, convw, convb, qn, kn, hsum)
    in_specs = [tok(d)] + [full(a) for a in ins[1:]]
    return pl.pallas_call(
        _inproj_kernel,
        grid=(b, nb),
        in_specs=in_specs,
        out_specs=out_specs,
        out_shape=out_shapes,
        scratch_shapes=[pltpu.VMEM((8, CONV_DIM), F32)],
        compiler_params=_cparams(2),
        name="inproj",
    )(*ins)


def _select_kernel(q_ref, k_ref, kmean_ref, qp_ref, kp_ref):
    own = pl.program_id(1)
    tq = q_ref.shape[0]
    nb = kmean_ref.shape[0]
    half = ATTN_HEAD_DIM
    lane = lax.broadcasted_iota(jnp.int32, (tq, LANES), 1)
    blk = lax.broadcasted_iota(jnp.int32, (LANES, tq), 0).astype(F32)
    own_f = own.astype(F32)
    onehot = jnp.where(lane - half == own, 1.0, 0.0).astype(F32)
    kmean = kmean_ref[...].astype(BF16)
    for p in range(ATTN_HEADS // 2):
        qpair = q_ref[:, p * LANES:(p + 1) * LANES]
        kpair = k_ref[:, p * LANES:(p + 1) * LANES].astype(F32)
        kmpair = kmean[:, p * LANES:(p + 1) * LANES]
        qpair_f = qpair.astype(F32)
        for sub in range(2):
            h = 2 * p + sub
            lane_sel = (lane >= sub * half) & (lane < (sub + 1) * half)
            qh = jnp.where(lane_sel, qpair_f, 0.0).astype(BF16)
            gate_t = _dot_nt(kmpair, qh)
            gate_t = jnp.concatenate(
                [gate_t, jnp.full((LANES - nb, tq), -jnp.inf, F32)], axis=0)
            g = jnp.where(blk < own_f, gate_t, -jnp.inf)
            alive = jnp.where(blk < nb, 1.0, 0.0)
            sel = jnp.where(blk == own_f, 1.0, 0.0)
            for r in range(MOBA_TOPK):
                ga = jnp.where(alive > 0.0, g, -jnp.inf)
                mx = jnp.max(ga, axis=0, keepdims=True)
                cand = jnp.where((alive > 0.0) & (g == mx), blk, 2.0 * LANES)
                first = jnp.min(cand, axis=0, keepdims=True)
                hit = blk == first
                sel = jnp.where(hit, jnp.maximum(sel, jnp.where(own_f > r, 1.0, 0.0)), sel)
                alive = jnp.where(hit, 0.0, alive)
            m_t = jnp.where(sel > 0.0, 0.0, MASK_VALUE).astype(F32)
            m = jnp.transpose(m_t)
            m = pltpu.roll(m, half, 1)
            m = jnp.where((lane >= half) & (lane < half + nb), m, 0.0)
            qs = qpair_f if sub == 0 else pltpu.roll(qpair_f, half, 1)
            ks = kpair if sub == 0 else pltpu.roll(kpair, half, 1)
            qp_ref[h] = jnp.where(lane < half, qs, m).astype(BF16)
            kp_ref[h] = jnp.where(lane < half, ks, onehot).astype(BF16)


def _select(q, k, kmean):
    b, s, _ = q.shape
    tq = MOBA_BLOCK
    nb = s // tq
    assert nb <= LANES - ATTN_HEAD_DIM, "block one-hot must fit beside the head dim in one lane tile"
    tok = pl.BlockSpec((None, tq, ATTN_DIM), lambda bi, si: (bi, si, 0))
    slab = pl.BlockSpec((None, ATTN_HEADS, tq, LANES), lambda bi, si: (bi, 0, si, 0))
    shp = jax.ShapeDtypeStruct((b, ATTN_HEADS, s, LANES), BF16)
    return pl.pallas_call(
        _select_kernel,
        grid=(b, nb),
        in_specs=[tok, tok, pl.BlockSpec((None, nb, ATTN_DIM), lambda bi, si: (bi, 0, 0))],
        out_specs=(slab, slab),
        out_shape=(shp, shp),
        compiler_params=_cparams(2),
        name="moba_select",
    )(q, k, kmean)


def _attn_kernel(qp_ref, kp_ref, v_ref, bias0_ref, bias1_ref, o_ref, m_ref, l_ref, acc_ref):
    own = pl.program_id(2)
    tq = qp_ref.shape[1]
    lane = lax.broadcasted_iota(jnp.int32, (tq, LANES), 1)
    n_far_chunks = jnp.maximum(own - 1, 0) // ATTN_FAR_CHUNK

    def update(hh, start, adds):
        n = len(adds)
        kc = kp_ref[hh, pl.ds(start, n * tq), :]
        vc = v_ref[pl.ds(start, n * tq), :]
        s = _dot_nt(qp_ref[hh], kc)
        pieces = []
        for w, add in enumerate(adds):
            sw = s[:, w * tq:(w + 1) * tq]
            if add is not None:
                sw = sw + add
            pieces += [sw[:, i * LANES:(i + 1) * LANES] for i in range(tq // LANES)]
        cm = functools.reduce(jnp.maximum, pieces)
        m_prev = m_ref[hh]
        m_new = jnp.maximum(m_prev, jnp.max(cm, axis=1, keepdims=True))
        alpha = jnp.exp(m_prev - m_new)
        ps = [jnp.exp(piece - m_new) for piece in pieces]
        l_ref[hh] = alpha * l_ref[hh] + functools.reduce(jnp.add, ps)
        p = jnp.concatenate([x.astype(BF16) for x in ps], axis=1)
        acc_ref[hh] = alpha * acc_ref[hh] + _dot(p, vc)
        m_ref[hh] = m_new

    for hh in range(2):
        m_ref[hh] = jnp.full((tq, LANES), MASK_VALUE, F32)
        l_ref[hh] = jnp.zeros((tq, LANES), F32)
        acc_ref[hh] = jnp.zeros((tq, LANES), F32)

    def far(c, carry):
        start = pl.multiple_of(c * (ATTN_FAR_CHUNK * tq), ATTN_FAR_CHUNK * tq)
        for hh in range(2):
            update(hh, start, [None] * ATTN_FAR_CHUNK)
        return carry

    lax.fori_loop(0, n_far_chunks, far, 0)

    for own_small in range(ATTN_TAIL - 1):
        @pl.when(own == own_small)
        def _():
            for hh in range(2):
                adds = [None] * max(own_small - 1, 0)
                if own_small >= 1:
                    adds.append(bias1_ref[hh])
                adds.append(bias0_ref[hh])
                update(hh, 0, adds)

    @pl.when(own >= ATTN_TAIL - 1)
    def _():
        first = own - (ATTN_TAIL - 1)
        start = pl.multiple_of(first * tq, tq)
        done = n_far_chunks * ATTN_FAR_CHUNK
        for hh in range(2):
            adds = [jnp.where(first + w < done, MASK_VALUE, 0.0).astype(F32) for w in range(ATTN_TAIL - 2)]
            update(hh, start, adds + [bias1_ref[hh], bias0_ref[hh]])

    outs = [acc_ref[hh] / jnp.sum(l_ref[hh], axis=1, keepdims=True) for hh in range(2)]
    o_ref[...] = jnp.where(lane < ATTN_HEAD_DIM, outs[0], outs[1]).astype(BF16)


def _attention(qp, kp, v, bias0, bias1):
    b, nh, s, _ = qp.shape
    tq = MOBA_BLOCK
    nq = s // tq
    return pl.pallas_call(
        _attn_kernel,
        grid=(b, nh // 2, nq),
        in_specs=[
            pl.BlockSpec((None, 2, tq, LANES), lambda bi, pi, qi: (bi, pi, qi, 0)),
            pl.BlockSpec((None, 2, s, LANES), lambda bi, pi, qi: (bi, pi, 0, 0)),
            pl.BlockSpec((None, s, LANES), lambda bi, pi, qi: (bi, 0, pi)),
            pl.BlockSpec((2, tq, tq), lambda bi, pi, qi: (pi, 0, 0)),
            pl.BlockSpec((2, tq, tq), lambda bi, pi, qi: (pi, 0, 0)),
        ],
        out_specs=pl.BlockSpec((None, tq, LANES), lambda bi, pi, qi: (bi, qi, pi)),
        out_shape=jax.ShapeDtypeStruct((b, s, ATTN_DIM), BF16),
        scratch_shapes=[
            pltpu.VMEM((2, tq, LANES), F32),
            pltpu.VMEM((2, tq, LANES), F32),
            pltpu.VMEM((2, tq, LANES), F32),
        ],
        compiler_params=_cparams(3),
        name="moba_attention",
    )(qp, kp, v, bias0, bias1)


def _t5_bucket(rel):
    n = jnp.maximum(rel, 0)
    max_exact = REL_BUCKETS // 2
    scaled = (jnp.log(jnp.maximum(n, max_exact).astype(F32) / max_exact)
              / math.log(REL_MAX_DISTANCE / max_exact))
    large = jnp.minimum(max_exact + (scaled * (REL_BUCKETS - max_exact)).astype(jnp.int32), REL_BUCKETS - 1)
    return jnp.where(n < max_exact, n, large)


def _attn_biases(rel_bias_table):
    table = rel_bias_table.astype(F32).T
    pos = jnp.arange(MOBA_BLOCK)
    rel0 = pos[:, None] - pos[None, :]
    far = table[:, REL_BUCKETS - 1][:, None, None]
    bias0 = jnp.where(rel0[None] >= 0, table[:, _t5_bucket(rel0)] - far, MASK_VALUE)
    bias1 = table[:, _t5_bucket(rel0 + MOBA_BLOCK)] - far
    return bias0, bias1


def _gla_kernel(gq_ref, gk_ref, gv_ref, gr_ref, gz_ref, wa_ref, ba_ref, on_ref, y_ref, state_ref):
    s_idx = pl.program_id(1)
    tg = gq_ref.shape[0]
    dk, dv, ck = GLA_KEY_DIM, GLA_VALUE_DIM, GLA_CHUNK

    @pl.when(s_idx == 0)
    def _():
        state_ref[...] = jnp.zeros_like(state_ref)

    z = gz_ref[...].astype(BF16)
    log_a = jax.nn.log_sigmoid(_dot(z, wa_ref[...]) + ba_ref[...]) / GLA_GATE_TEMP
    row = lax.broadcasted_iota(jnp.int32, log_a.shape, 0) % ck
    bcum = log_a
    shift = 1
    while shift < ck:
        bcum = bcum + jnp.where(row >= shift, pltpu.roll(bcum, shift, 0), 0.0)
        shift *= 2
    q = gq_ref[...] * (dk ** -0.5)
    k = gk_ref[...]
    tri = (lax.broadcasted_iota(jnp.int32, (ck, ck), 0) >= lax.broadcasted_iota(jnp.int32, (ck, ck), 1))
    for c in range(tg // ck):
        rows = slice(c * ck, (c + 1) * ck)
        for h in range(GLA_HEADS):
            kcols = slice(h * dk, (h + 1) * dk)
            vcols = slice(h * dv, (h + 1) * dv)
            bc = bcum[rows, kcols]
            btot = bc[ck - 1:ck, :]
            qe = (q[rows, kcols] * jnp.exp(bc)).astype(BF16)
            ke = (k[rows, kcols] * jnp.exp(-bc)).astype(BF16)
            kd = (k[rows, kcols] * jnp.exp(btot - bc)).astype(BF16)
            vc = gv_ref[rows, vcols].astype(BF16)
            att = jnp.where(tri, _dot_nt(qe, ke), 0.0).astype(BF16)
            st = state_ref[h]
            o = _dot(att, vc) + _dot_nt(qe, st.astype(BF16))
            state_ref[h] = st * jnp.exp(btot) + _dot_tn(vc, kd)
            o = _rms(o, on_ref[...])
            r = gr_ref[rows, vcols]
            y_ref[rows, vcols] = (o * (r * jax.nn.sigmoid(r))).astype(BF16)


def _gla(gq, gk, gv, gr, gz, wa, ba, on):
    b, s, _ = gq.shape
    tg = 256
    tok = lambda w: pl.BlockSpec((None, tg, w), lambda bi, si: (bi, si, 0))
    full = lambda arr: pl.BlockSpec(arr.shape, lambda bi, si: (0,) * arr.ndim)
    return pl.pallas_call(
        _gla_kernel,
        grid=(b, s // tg),
        in_specs=[tok(GLA_QK_DIM), tok(GLA_QK_DIM), tok(GLA_V_DIM), tok(GLA_V_DIM), tok(GZ_PAD),
                  full(wa), full(ba), full(on)],
        out_specs=tok(GLA_V_DIM),
        out_shape=jax.ShapeDtypeStruct((b, s, GLA_V_DIM), BF16),
        scratch_shapes=[pltpu.VMEM((GLA_HEADS, GLA_VALUE_DIM, GLA_KEY_DIM), F32)],
        compiler_params=_cparams(2),
        name="gla",
    )(gq, gk, gv, gr, gz, wa, ba, on)


def _merge_route_kernel(x_ref, yc_ref, ya_ref, yg_ref, gmix_ref, wg_ref, wb_ref, wo_ref, gffn_ref,
                        wr_hi_ref, wr_lo_ref, br_ref, xo_ref, h2_ref, route_ref, counts_ref, run_ref):
    x = x_ref[...]
    h = _rms(x, gmix_ref[...]).astype(BF16)
    merged = None
    for n, y_ref in enumerate((yc_ref, ya_ref, yg_ref)):
        term = jax.nn.sigmoid(_dot(h, wg_ref[n])) * _dot(y_ref[...], wb_ref[n])
        merged = term if merged is None else merged + term
    xo = x + _dot(merged.astype(BF16), wo_ref[...])
    xo_ref[...] = xo
    h2 = _rms(xo, gffn_ref[...])
    h2_ref[...] = h2

    h_hi, h_lo = _split_bf16(h2)
    logits = (_dot(h_hi, wr_hi_ref[...]) + _dot(h_lo, wr_hi_ref[...]) + _dot(h_hi, wr_lo_ref[...])
              + br_ref[...])
    lane = lax.broadcasted_iota(jnp.int32, logits.shape, 1).astype(F32)
    big = 4.0 * ROUTE_LANES
    lg = jnp.where(lane < N_GROUPS, logits, -jnp.inf)
    gmax = jnp.max(lg, axis=1, keepdims=True)
    gidx = jnp.min(jnp.where(lg == gmax, lane, big), axis=1, keepdims=True)
    p_group_top = 1.0 / jnp.sum(jnp.exp(lg - gmax), axis=1, keepdims=True)
    lo_lane = N_GROUPS + gidx * EXPERTS_PER_GROUP
    le = jnp.where((lane >= lo_lane) & (lane < lo_lane + EXPERTS_PER_GROUP), logits, -jnp.inf)
    emax = jnp.max(le, axis=1, keepdims=True)
    i1 = jnp.min(jnp.where(le == emax, lane, big), axis=1, keepdims=True)
    esum = jnp.sum(jnp.exp(le - emax), axis=1, keepdims=True)
    le2 = jnp.where(lane == i1, -jnp.inf, le)
    emax2 = jnp.max(le2, axis=1, keepdims=True)
    i2 = jnp.min(jnp.where(le2 == emax2, lane, big), axis=1, keepdims=True)
    p1 = 1.0 / esum
    p2 = jnp.exp(emax2 - emax) / esum
    psum = p1 + p2
    w1 = p_group_top * p1 / psum
    w2 = p_group_top * p2 / psum
    e1 = i1 - N_GROUPS
    e2 = i2 - N_GROUPS

    @pl.when(pl.program_id(0) == 0)
    def _():
        run_ref[...] = jnp.zeros_like(run_ref)

    tm = x.shape[0]
    oh1 = jnp.where(lane == i1, 1.0, 0.0)
    oh2 = jnp.where(lane == i2, 1.0, 0.0)
    ohs = oh1 + oh2
    lower = (lax.broadcasted_iota(jnp.int32, (tm, tm), 0) > lax.broadcasted_iota(jnp.int32, (tm, tm), 1))
    before = _dot(jnp.where(lower, 1.0, 0.0).astype(BF16), ohs.astype(BF16)) + run_ref[...]
    rank1 = jnp.sum(oh1 * before, axis=1, keepdims=True)
    rank2 = jnp.sum(oh2 * before, axis=1, keepdims=True)
    run_ref[...] = run_ref[...] + jnp.sum(ohs, axis=0, keepdims=True)
    counts_ref[...] = run_ref[...]

    rl = lax.broadcasted_iota(jnp.int32, (tm, ROUTE_OUT), 1)
    rec = jnp.zeros((tm, ROUTE_OUT), F32)
    for slot, val in enumerate((e1, e2, w1, w2, rank1, rank2)):
        rec = jnp.where(rl == slot, val, rec)
    route_ref[...] = rec


def _merge_route(x2d, yc, ya, yg, gmix, wg, wb, wo, gffn, wr_hi, wr_lo, br):
    t, d = x2d.shape
    tm = 256
    tok = lambda w: pl.BlockSpec((tm, w), lambda i: (i, 0))
    full = lambda arr: pl.BlockSpec(arr.shape, lambda i: (0,) * arr.ndim)
    ins = (x2d, yc, ya, yg, gmix, wg, wb, wo, gffn, wr_hi, wr_lo, br)
    return pl.pallas_call(
        _merge_route_kernel,
        grid=(t // tm,),
        in_specs=[tok(d), tok(CONV_DIM), tok(ATTN_DIM), tok(GLA_V_DIM)] + [full(a) for a in ins[4:]],
        out_specs=(tok(d), tok(d), tok(ROUTE_OUT), pl.BlockSpec((1, ROUTE_LANES), lambda i: (0, 0))),
        out_shape=(jax.ShapeDtypeStruct((t, d), F32), jax.ShapeDtypeStruct((t, d), F32),
                   jax.ShapeDtypeStruct((t, ROUTE_OUT), F32), jax.ShapeDtypeStruct((1, ROUTE_LANES), F32)),
        scratch_shapes=[pltpu.VMEM((1, ROUTE_LANES), F32)],
        compiler_params=_cparams(1),
        name="merge_route",
    )(*ins)


def _tile_slots(dest, tc):
    nt = dest.shape[0] // tc
    return dest.reshape(nt, tc, TOP_K).transpose(0, 2, 1).reshape(nt, 1, TOP_K * tc)


def _dispatch_kernel(dest_ref, h2_ref, xs_init_hbm, xs_hbm, sem):
    del xs_init_hbm
    tc = h2_ref.shape[0]

    def row_copy(r, kk):
        dst = dest_ref[0, kk * tc + r]
        return pltpu.make_async_copy(h2_ref.at[pl.ds(r, 1)], xs_hbm.at[pl.ds(dst, 1)], sem)

    def issue(r, carry):
        row_copy(r, 0).start(priority=0)
        row_copy(r, 1).start(priority=1)
        return carry

    lax.fori_loop(0, tc, issue, 0)

    def drain(r, carry):
        row_copy(r, 0).wait()
        row_copy(r, 1).wait()
        return carry

    lax.fori_loop(0, tc, drain, 0)


def _dispatch(dest, h2, n_rows):
    t, d = h2.shape
    tc = 256
    return pl.pallas_call(
        _dispatch_kernel,
        grid=(t // tc,),
        in_specs=[
            pl.BlockSpec((None, 1, TOP_K * tc), lambda i: (i, 0, 0), memory_space=pltpu.SMEM),
            pl.BlockSpec((tc, d), lambda i: (i, 0)),
            pl.BlockSpec(memory_space=pl.ANY),
        ],
        out_specs=pl.BlockSpec(memory_space=pl.ANY),
        out_shape=jax.ShapeDtypeStruct((n_rows, d), F32),
        scratch_shapes=[pltpu.SemaphoreType.DMA(())],
        input_output_aliases={2: 0},
        compiler_params=_cparams(1),
        name="moe_dispatch",
    )(_tile_slots(dest, tc), h2, jnp.zeros((n_rows, d), F32))


def _expert_kernel(blk_expert_ref, n_used_ref, xs_ref, wg_ref, wu_ref, wd_ref, ys_ref):
    i = pl.program_id(0)

    @pl.when(i < n_used_ref[0])
    def _():
        xb = xs_ref[...].astype(BF16)
        gate = _dot(xb, wg_ref[...])
        up = _dot(xb, wu_ref[...])
        act = (gate * jax.nn.sigmoid(gate) * up).astype(BF16)
        ys_ref[...] = _dot(act, wd_ref[...])

    @pl.when(i >= n_used_ref[0])
    def _():
        ys_ref[...] = jnp.zeros_like(ys_ref)


def _experts(blk_expert, n_used, xs, wg, wu, wd):
    n_rows, d = xs.shape
    n_blocks = blk_expert.shape[0]
    rb = EXPERT_BLOCK
    grid_spec = pltpu.PrefetchScalarGridSpec(
        num_scalar_prefetch=2,
        grid=(n_blocks,),
        in_specs=[
            pl.BlockSpec((rb, d), lambda i, be, nu: (jnp.minimum(i, nu[0] - 1), 0)),
            pl.BlockSpec((None, d, EXPERT_FF), lambda i, be, nu: (be[i], 0, 0)),
            pl.BlockSpec((None, d, EXPERT_FF), lambda i, be, nu: (be[i], 0, 0)),
            pl.BlockSpec((None, EXPERT_FF, d), lambda i, be, nu: (be[i], 0, 0)),
        ],
        out_specs=pl.BlockSpec((rb, d), lambda i, be, nu: (i, 0)),
    )
    return pl.pallas_call(
        _expert_kernel,
        grid_spec=grid_spec,
        out_shape=jax.ShapeDtypeStruct((n_rows, d), F32),
        compiler_params=_cparams(1),
        name="moe_experts",
    )(blk_expert, n_used, xs, wg, wu, wd)


def _combine_kernel(dest_ref, x_ref, route_ref, ys_hbm, o_ref, buf, sem):
    tc = x_ref.shape[0]

    def row_copy(r, kk):
        src = dest_ref[0, kk * tc + r]
        return pltpu.make_async_copy(ys_hbm.at[pl.ds(src, 1)], buf.at[kk, pl.ds(r, 1)], sem)

    def issue(r, carry):
        row_copy(r, 0).start(priority=0)
        row_copy(r, 1).start(priority=1)
        return carry

    lax.fori_loop(0, tc, issue, 0)

    def drain(r, carry):
        row_copy(r, 0).wait()
        row_copy(r, 1).wait()
        return carry

    lax.fori_loop(0, tc, drain, 0)
    route = route_ref[...]
    w1 = route[:, TOP_K:TOP_K + 1]
    w2 = route[:, TOP_K + 1:TOP_K + 2]
    o_ref[...] = x_ref[...] + (buf[0] * w1 + buf[1] * w2)


def _combine(dest, x2d, route, ys):
    t, d = x2d.shape
    tc = 256
    return pl.pallas_call(
        _combine_kernel,
        grid=(t // tc,),
        in_specs=[
            pl.BlockSpec((None, 1, TOP_K * tc), lambda i: (i, 0, 0), memory_space=pltpu.SMEM),
            pl.BlockSpec((tc, d), lambda i: (i, 0)),
            pl.BlockSpec((tc, ROUTE_OUT), lambda i: (i, 0)),
            pl.BlockSpec(memory_space=pl.ANY),
        ],
        out_specs=pl.BlockSpec((tc, d), lambda i: (i, 0)),
        out_shape=jax.ShapeDtypeStruct((t, d), F32),
        scratch_shapes=[pltpu.VMEM((TOP_K, tc, d), F32), pltpu.SemaphoreType.DMA(())],
        compiler_params=_cparams(1),
        name="moe_combine",
    )(_tile_slots(dest, tc), x2d, route, ys)


def _dispatch_plan(route, counts, t):
    e_ids = route[:, :TOP_K].astype(jnp.int32)
    rank = route[:, 2 * TOP_K:3 * TOP_K].astype(jnp.int32)
    counts = counts[0, N_GROUPS:N_GROUPS + N_EXPERTS].astype(jnp.int32)
    padded = ((counts + EXPERT_BLOCK - 1) // EXPERT_BLOCK) * EXPERT_BLOCK
    pad_end = jnp.cumsum(padded)
    pad_start = pad_end - padded
    onehot = e_ids[:, :, None] == jnp.arange(N_EXPERTS, dtype=jnp.int32)
    dest = rank + jnp.sum(jnp.where(onehot, pad_start, 0), axis=-1)
    n_blocks = -(-(t * TOP_K) // EXPERT_BLOCK) + N_EXPERTS
    blk_start = jnp.arange(n_blocks, dtype=jnp.int32) * EXPERT_BLOCK
    blk_expert = jnp.minimum(jnp.sum(blk_start[:, None] >= pad_end[None, :], axis=1), N_EXPERTS - 1)
    n_used = jnp.maximum(pad_end[-1:] // EXPERT_BLOCK, 1).astype(jnp.int32)
    return blk_expert.astype(jnp.int32), n_used, dest, n_blocks * EXPERT_BLOCK


def kernel(x, rel_bias_table, norm_mix, w_in, conv_w, conv_b, q_norm, k_norm, w_gla_alpha, b_gla_alpha,
           gla_out_norm, w_merge_gate, w_branch, w_out, norm_ffn, w_router_group, b_router_group,
           w_router_expert, b_router_expert, w_expert_gate, w_expert_up, w_expert_down):
    b, s, d = x.shape
    t = b * s
    depth = w_in.shape[0]
    assert s % MOBA_BLOCK == 0 and t % EXPERT_BLOCK == 0
    bias0, bias1 = _attn_biases(rel_bias_table)
    head_id = jnp.arange(ATTN_DIM) // ATTN_HEAD_DIM
    hsum = (head_id[:, None] == head_id[None, :]).astype(BF16)
    c3 = 3 * CONV_DIM
    a3 = c3 + 3 * ATTN_DIM
    g3 = a3 + 2 * GLA_QK_DIM + 2 * GLA_V_DIM
    for l in range(depth):
        w_l = w_in[l].astype(BF16)
        w_gz = jnp.pad(w_l[:, g3:], ((0, 0), (0, GZ_PAD - GLA_GATE_RANK)))
        w_alpha = jnp.pad(w_gla_alpha[l].astype(BF16), ((0, GZ_PAD - GLA_GATE_RANK), (0, 0)))
        yconv, q, k, v, kmean, gq, gk, gv, gr, gz = _inproj(
            x, norm_mix[l][None], w_l[:, :c3], w_l[:, c3:a3], w_l[:, a3:g3], w_gz,
            conv_w[l], conv_b[l][None],
            jnp.tile(q_norm[l], ATTN_HEADS)[None], jnp.tile(k_norm[l], ATTN_HEADS)[None], hsum)
        qp, kp = _select(q, k, kmean.reshape(b, s // MOBA_BLOCK, ATTN_DIM))
        yattn = _attention(qp, kp, v, bias0, bias1)
        ygla = _gla(gq, gk, gv, gr, gz, w_alpha, b_gla_alpha[l][None],
                    gla_out_norm[l][None])
        w_r = jnp.concatenate([w_router_group[l], w_router_expert[l]], axis=1)
        w_r = jnp.pad(w_r, ((0, 0), (0, ROUTE_LANES - w_r.shape[1])))
        b_r = jnp.pad(jnp.concatenate([b_router_group[l], b_router_expert[l]]),
                      (0, ROUTE_LANES - N_GROUPS - N_EXPERTS))[None]
        wr_hi, wr_lo = _split_bf16(w_r)
        xo, h2, route, counts = _merge_route(
            x.reshape(t, d), yconv.reshape(t, -1), yattn.reshape(t, -1), ygla.reshape(t, -1),
            norm_mix[l][None], w_merge_gate[l].astype(BF16), w_branch[l].astype(BF16),
            w_out[l].astype(BF16), norm_ffn[l][None], wr_hi, wr_lo, b_r)
        blk_expert, n_used, dest, n_rows = _dispatch_plan(route, counts, t)
        xs = _dispatch(dest, h2, n_rows)
        ys = _experts(blk_expert, n_used, xs, w_expert_gate[l].astype(BF16),
                      w_expert_up[l].astype(BF16), w_expert_down[l].astype(BF16))
        x = _combine(dest, xo, route, ys).reshape(b, s, d)
    return x
```

```python
import functools
import math

import jax
import jax.numpy as jnp
import numpy as np
from jax import lax
from jax.experimental import pallas as pl
from jax.experimental.pallas import tpu as pltpu

CONV_DIM = 512
CONV_WIDTH = 3
ATTN_HEADS = 8
ATTN_HEAD_DIM = 64
ATTN_DIM = ATTN_HEADS * ATTN_HEAD_DIM
MOBA_BLOCK = 256
MOBA_TOPK = 3
REL_BUCKETS = 32
REL_MAX_DISTANCE = 128
GLA_HEADS = 4
GLA_KEY_DIM = 64
GLA_VALUE_DIM = 128
GLA_QK_DIM = GLA_HEADS * GLA_KEY_DIM
GLA_V_DIM = GLA_HEADS * GLA_VALUE_DIM
GLA_GATE_RANK = 16
GLA_GATE_TEMP = 16.0
GLA_CHUNK = 64
N_GROUPS = 4
EXPERTS_PER_GROUP = 8
N_EXPERTS = N_GROUPS * EXPERTS_PER_GROUP
TOP_K = 2
EXPERT_FF = 512
EXPERT_BLOCK = 256
RMS_EPS = 1e-6

LANES = 128
VMEM_LIMIT_BYTES = 56 * 1024 * 1024

MASK_VALUE = -1e30
ROUTE_LANES = 128
ROUTE_OUT = 8
ATTN_FAR_CHUNK = 4
ATTN_TAIL = 5
ROW_DMA_UNROLL = 8
GZ_PAD = LANES

F32 = jnp.float32
BF16 = jnp.bfloat16


def _cparams(n_axes):
    return pltpu.CompilerParams(
        dimension_semantics=("arbitrary",) * n_axes,
        vmem_limit_bytes=VMEM_LIMIT_BYTES,
    )


def _rms(x, gain):
    return x * lax.rsqrt(jnp.mean(x * x, axis=-1, keepdims=True) + RMS_EPS) * gain


def _split_bf16(x):
    hi = x.astype(BF16)
    lo = (x - hi.astype(F32)).astype(BF16)
    return hi, lo


def _dot(a, b):
    return jnp.dot(a, b, preferred_element_type=F32)


def _dot_nt(a, b):
    return lax.dot_general(a, b, (((1,), (1,)), ((), ())), preferred_element_type=F32)


def _dot_tn(a, b):
    return lax.dot_general(a, b, (((0,), (0,)), ((), ())), preferred_element_type=F32)


def _inproj_kernel(x_ref, gmix_ref, wconv_ref, wattn_ref, wgla_ref, wgz_ref, convw_ref, convb_ref,
                   qn_ref, kn_ref, hsum_ref,
                   yconv_ref, q_ref, k_ref, v_ref, kmean_ref, gq_ref, gk_ref, gv_ref, gr_ref, gz_ref,
                   carry_ref):
    s_idx = pl.program_id(1)
    ts = x_ref.shape[0]
    h = _rms(x_ref[...], gmix_ref[...]).astype(BF16)

    c = _dot(h, wconv_ref[...])
    cb = c[:, :CONV_DIM]
    u = c[:, CONV_DIM:2 * CONV_DIM] * c[:, 2 * CONV_DIM:]

    @pl.when(s_idx == 0)
    def _():
        carry_ref[...] = jnp.zeros_like(carry_ref)

    prev = carry_ref[...]
    row = lax.broadcasted_iota(jnp.int32, u.shape, 0)
    u1 = pltpu.roll(u, 1, 0)
    u1 = jnp.where(row == 0, prev[7:8, :], u1)
    u2 = pltpu.roll(u, 2, 0)
    u2 = jnp.where(row == 0, prev[6:7, :], jnp.where(row == 1, prev[7:8, :], u2))
    carry_ref[...] = u[ts - 8:, :]
    y = convb_ref[...] + convw_ref[0:1, :] * u2
    y = y + convw_ref[1:2, :] * u1
    y = y + convw_ref[2:3, :] * u
    yconv_ref[...] = (cb * y).astype(BF16)

    a = _dot(h, wattn_ref[...])
    hsum = hsum_ref[...]

    def head_norm(t, gain):
        hi, lo = _split_bf16(t * t)
        ss = _dot(hi, hsum) + _dot(lo, hsum)
        return t * lax.rsqrt(ss * (1.0 / ATTN_HEAD_DIM) + RMS_EPS) * gain

    qn = head_norm(a[:, :ATTN_DIM], qn_ref[...])
    kn = head_norm(a[:, ATTN_DIM:2 * ATTN_DIM], kn_ref[...])
    q_ref[...] = (qn * (ATTN_HEAD_DIM ** -0.5)).astype(BF16)
    k_ref[...] = kn.astype(BF16)
    v_ref[...] = a[:, 2 * ATTN_DIM:].astype(BF16)
    kmean_ref[...] = jnp.mean(kn, axis=0, keepdims=True)

    g = _dot(h, wgla_ref[...])
    gq_ref[...] = g[:, :GLA_QK_DIM]
    gk_ref[...] = g[:, GLA_QK_DIM:2 * GLA_QK_DIM]
    gv_ref[...] = g[:, 2 * GLA_QK_DIM:2 * GLA_QK_DIM + GLA_V_DIM]
    gr_ref[...] = g[:, 2 * GLA_QK_DIM + GLA_V_DIM:]
    gz_ref[...] = _dot(h, wgz_ref[...])


def _inproj(x, gmix, wconv, wattn, wgla, wgz, convw, convb, qn, kn, hsum):
    b, s, d = x.shape
    ts = MOBA_BLOCK
    nb = s // ts
    tok = lambda w: pl.BlockSpec((None, ts, w), lambda bi, si: (bi, si, 0))
    full = lambda arr: pl.BlockSpec(arr.shape, lambda bi, si: (0,) * arr.ndim)
    out_shapes = (
        jax.ShapeDtypeStruct((b, s, CONV_DIM), BF16),
        jax.ShapeDtypeStruct((b, s, ATTN_DIM), BF16),
        jax.ShapeDtypeStruct((b, s, ATTN_DIM), BF16),
        jax.ShapeDtypeStruct((b, s, ATTN_DIM), BF16),
        jax.ShapeDtypeStruct((b, nb, 1, ATTN_DIM), F32),
        jax.ShapeDtypeStruct((b, s, GLA_QK_DIM), F32),
        jax.ShapeDtypeStruct((b, s, GLA_QK_DIM), F32),
        jax.ShapeDtypeStruct((b, s, GLA_V_DIM), F32),
        jax.ShapeDtypeStruct((b, s, GLA_V_DIM), F32),
        jax.ShapeDtypeStruct((b, s, GZ_PAD), F32),
    )
    out_specs = (
        tok(CONV_DIM), tok(ATTN_DIM), tok(ATTN_DIM), tok(ATTN_DIM),
        pl.BlockSpec((None, None, 1, ATTN_DIM), lambda bi, si: (bi, si, 0, 0)),
        tok(GLA_QK_DIM), tok(GLA_QK_DIM), tok(GLA_V_DIM), tok(GLA_V_DIM), tok(GZ_PAD),
    )
    ins = (x, gmix, wconv, wattn, wgla, wgz, convw, convb, qn, kn, hsum)
    in_specs = [tok(d)] + [full(a) for a in ins[1:]]
    return pl.pallas_call(
        _inproj_kernel,
        grid=(b, nb),
        in_specs=in_specs,
        out_specs=out_specs,
        out_shape=out_shapes,
        scratch_shapes=[pltpu.VMEM((8, CONV_DIM), F32)],
        compiler_params=_cparams(2),
        name="inproj",
    )(*ins)


def _select_kernel(q_ref, k_ref, kmean_ref, qp_ref, kp_ref):
    own = pl.program_id(1)
    tq = q_ref.shape[0]
    nb = kmean_ref.shape[0]
    half = ATTN_HEAD_DIM
    lane = lax.broadcasted_iota(jnp.int32, (tq, LANES), 1)
    blk = lax.broadcasted_iota(jnp.int32, (LANES, tq), 0).astype(F32)
    own_f = own.astype(F32)
    onehot = jnp.where(lane - half == own, 1.0, 0.0).astype(F32)
    kmean = kmean_ref[...].astype(BF16)
    for p in range(ATTN_HEADS // 2):
        qpair = q_ref[:, p * LANES:(p + 1) * LANES]
        kpair = k_ref[:, p * LANES:(p + 1) * LANES].astype(F32)
        kmpair = kmean[:, p * LANES:(p + 1) * LANES]
        qpair_f = qpair.astype(F32)
        for sub in range(2):
            h = 2 * p + sub
            lane_sel = (lane >= sub * half) & (lane < (sub + 1) * half)
            qh = jnp.where(lane_sel, qpair_f, 0.0).astype(BF16)
            gate_t = _dot_nt(kmpair, qh)
            gate_t = jnp.concatenate(
                [gate_t, jnp.full((LANES - nb, tq), -jnp.inf, F32)], axis=0)
            g = jnp.where(blk < own_f, gate_t, -jnp.inf)
            alive = jnp.where(blk < nb, 1.0, 0.0)
            sel = jnp.where(blk == own_f, 1.0, 0.0)
            for r in range(MOBA_TOPK):
                ga = jnp.where(alive > 0.0, g, -jnp.inf)
                mx = jnp.max(ga, axis=0, keepdims=True)
                cand = jnp.where((alive > 0.0) & (g == mx), blk, 2.0 * LANES)
                first = jnp.min(cand, axis=0, keepdims=True)
                hit = blk == first
                sel = jnp.where(hit, jnp.maximum(sel, jnp.where(own_f > r, 1.0, 0.0)), sel)
                alive = jnp.where(hit, 0.0, alive)
            m_t = jnp.where(sel > 0.0, 0.0, MASK_VALUE).astype(F32)
            m = jnp.transpose(m_t)
            m = pltpu.roll(m, half, 1)
            m = jnp.where((lane >= half) & (lane < half + nb), m, 0.0)
            qs = qpair_f if sub == 0 else pltpu.roll(qpair_f, half, 1)
            ks = kpair if sub == 0 else pltpu.roll(kpair, half, 1)
            qp_ref[h] = jnp.where(lane < half, qs, m).astype(BF16)
            kp_ref[h] = jnp.where(lane < half, ks, onehot).astype(BF16)


def _select(q, k, kmean):
    b, s, _ = q.shape
    tq = MOBA_BLOCK
    nb = s // tq
    assert nb <= LANES - ATTN_HEAD_DIM, "block one-hot must fit beside the head dim in one lane tile"
    tok = pl.BlockSpec((None, tq, ATTN_DIM), lambda bi, si: (bi, si, 0))
    slab = pl.BlockSpec((None, ATTN_HEADS, tq, LANES), lambda bi, si: (bi, 0, si, 0))
    shp = jax.ShapeDtypeStruct((b, ATTN_HEADS, s, LANES), BF16)
    return pl.pallas_call(
        _select_kernel,
        grid=(b, nb),
        in_specs=[tok, tok, pl.BlockSpec((None, nb, ATTN_DIM), lambda bi, si: (bi, 0, 0))],
        out_specs=(slab, slab),
        out_shape=(shp, shp),
        compiler_params=_cparams(2),
        name="moba_select",
    )(q, k, kmean)


def _attn_kernel(qp_ref, kp_ref, v_ref, bias0_ref, bias1_ref, o_ref, m_ref, l_ref, acc_ref,
                 sa_ref, sb_ref, st_ref):
    own = pl.program_id(2)
    tq = qp_ref.shape[1]
    far_rows = ATTN_FAR_CHUNK * tq
    lane = lax.broadcasted_iota(jnp.int32, (tq, LANES), 1)
    n_far = jnp.maximum(own - 1, 0) // ATTN_FAR_CHUNK

    def scores(hh, start, n):
        return _dot_nt(qp_ref[hh], kp_ref[hh, pl.ds(start, n * tq), :])

    def far_scores(dst_ref, chunk):
        start = pl.multiple_of(chunk * far_rows, far_rows)
        for hh in range(2):
            dst_ref[hh] = scores(hh, start, ATTN_FAR_CHUNK)

    def softmax_pv(hh, s_ref, start, n):
        pieces = [s_ref[hh, :, i * LANES:(i + 1) * LANES] for i in range(n * tq // LANES)]
        m_prev = m_ref[hh]
        m_new = jnp.maximum(m_prev, jnp.max(functools.reduce(jnp.maximum, pieces), axis=1, keepdims=True))
        alpha = jnp.exp(m_prev - m_new)
        ps = [jnp.exp(piece - m_new) for piece in pieces]
        l_ref[hh] = alpha * l_ref[hh] + functools.reduce(jnp.add, ps)
        p = jnp.concatenate([x.astype(BF16) for x in ps], axis=1)
        acc_ref[hh] = alpha * acc_ref[hh] + _dot(p, v_ref[pl.ds(start, n * tq), :])
        m_ref[hh] = m_new

    for hh in range(2):
        m_ref[hh] = jnp.full((tq, LANES), MASK_VALUE, F32)
        l_ref[hh] = jnp.zeros((tq, LANES), F32)
        acc_ref[hh] = jnp.zeros((tq, LANES), F32)

    for own_small in range(ATTN_TAIL - 1):
        @pl.when(own == own_small)
        def _():
            for hh in range(2):
                s = scores(hh, 0, own_small + 1)
                for w in range(own_small + 1):
                    sw = s[:, w * tq:(w + 1) * tq]
                    if w == own_small:
                        sw = sw + bias0_ref[hh]
                    elif w == own_small - 1:
                        sw = sw + bias1_ref[hh]
                    st_ref[hh, :, w * tq:(w + 1) * tq] = sw

    @pl.when(own >= ATTN_TAIL - 1)
    def _():
        first = own - (ATTN_TAIL - 1)
        done = n_far * ATTN_FAR_CHUNK
        for hh in range(2):
            s = scores(hh, pl.multiple_of(first * tq, tq), ATTN_TAIL)
            for w in range(ATTN_TAIL):
                sw = s[:, w * tq:(w + 1) * tq]
                if w == ATTN_TAIL - 1:
                    sw = sw + bias0_ref[hh]
                elif w == ATTN_TAIL - 2:
                    sw = sw + bias1_ref[hh]
                else:
                    sw = sw + jnp.where(first + w < done, MASK_VALUE, 0.0).astype(F32)
                st_ref[hh, :, w * tq:(w + 1) * tq] = sw

    @pl.when(n_far > 0)
    def _():
        far_scores(sa_ref, 0)

    def far_pair(j, carry):
        c0 = 2 * j
        far_scores(sb_ref, jnp.minimum(c0 + 1, n_far - 1))
        for hh in range(2):
            softmax_pv(hh, sa_ref, pl.multiple_of(c0 * far_rows, far_rows), ATTN_FAR_CHUNK)

        @pl.when(c0 + 1 < n_far)
        def _():
            far_scores(sa_ref, jnp.minimum(c0 + 2, n_far - 1))
            for hh in range(2):
                softmax_pv(hh, sb_ref, pl.multiple_of((c0 + 1) * far_rows, far_rows), ATTN_FAR_CHUNK)

        return carry

    lax.fori_loop(0, (n_far + 1) // 2, far_pair, 0)

    for own_small in range(ATTN_TAIL - 1):
        @pl.when(own == own_small)
        def _():
            for hh in range(2):
                softmax_pv(hh, st_ref, 0, own_small + 1)

    @pl.when(own >= ATTN_TAIL - 1)
    def _():
        start = pl.multiple_of((own - (ATTN_TAIL - 1)) * tq, tq)
        for hh in range(2):
            softmax_pv(hh, st_ref, start, ATTN_TAIL)

    outs = [acc_ref[hh] / jnp.sum(l_ref[hh], axis=1, keepdims=True) for hh in range(2)]
    o_ref[...] = jnp.where(lane < ATTN_HEAD_DIM, outs[0], outs[1]).astype(BF16)


def _attention(qp, kp, v, bias0, bias1):
    b, nh, s, _ = qp.shape
    tq = MOBA_BLOCK
    nq = s // tq
    return pl.pallas_call(
        _attn_kernel,
        grid=(b, nh // 2, nq),
        in_specs=[
            pl.BlockSpec((None, 2, tq, LANES), lambda bi, pi, qi: (bi, pi, qi, 0)),
            pl.BlockSpec((None, 2, s, LANES), lambda bi, pi, qi: (bi, pi, 0, 0)),
            pl.BlockSpec((None, s, LANES), lambda bi, pi, qi: (bi, 0, pi)),
            pl.BlockSpec((2, tq, tq), lambda bi, pi, qi: (pi, 0, 0)),
            pl.BlockSpec((2, tq, tq), lambda bi, pi, qi: (pi, 0, 0)),
        ],
        out_specs=pl.BlockSpec((None, tq, LANES), lambda bi, pi, qi: (bi, qi, pi)),
        out_shape=jax.ShapeDtypeStruct((b, s, ATTN_DIM), BF16),
        scratch_shapes=[
            pltpu.VMEM((2, tq, LANES), F32),
            pltpu.VMEM((2, tq, LANES), F32),
            pltpu.VMEM((2, tq, LANES), F32),
            pltpu.VMEM((2, tq, ATTN_FAR_CHUNK * tq), F32),
            pltpu.VMEM((2, tq, ATTN_FAR_CHUNK * tq), F32),
            pltpu.VMEM((2, tq, ATTN_TAIL * tq), F32),
        ],
        compiler_params=_cparams(3),
        name="moba_attention",
    )(qp, kp, v, bias0, bias1)


def _t5_bucket(rel):
    n = jnp.maximum(rel, 0)
    max_exact = REL_BUCKETS // 2
    scaled = (jnp.log(jnp.maximum(n, max_exact).astype(F32) / max_exact)
              / math.log(REL_MAX_DISTANCE / max_exact))
    large = jnp.minimum(max_exact + (scaled * (REL_BUCKETS - max_exact)).astype(jnp.int32), REL_BUCKETS - 1)
    return jnp.where(n < max_exact, n, large)


def _attn_biases(rel_bias_table):
    table = rel_bias_table.astype(F32).T
    pos = jnp.arange(MOBA_BLOCK)
    rel0 = pos[:, None] - pos[None, :]
    far = table[:, REL_BUCKETS - 1][:, None, None]

    def lookup(bucket):
        hit = bucket[None, :, :, None] == jnp.arange(REL_BUCKETS)
        return jnp.sum(jnp.where(hit, table[:, None, None, :], 0.0), axis=-1)

    bias0 = jnp.where(rel0[None] >= 0, lookup(_t5_bucket(rel0)) - far, MASK_VALUE)
    bias1 = lookup(_t5_bucket(rel0 + MOBA_BLOCK)) - far
    return bias0, bias1


def _gla_kernel(gq_ref, gk_ref, gv_ref, gr_ref, gz_ref, wa_ref, ba_ref, on_ref, y_ref, state_ref):
    s_idx = pl.program_id(1)
    tg = gq_ref.shape[0]
    dk, dv, ck = GLA_KEY_DIM, GLA_VALUE_DIM, GLA_CHUNK

    @pl.when(s_idx == 0)
    def _():
        state_ref[...] = jnp.zeros_like(state_ref)

    z = gz_ref[...].astype(BF16)
    log_a = jax.nn.log_sigmoid(_dot(z, wa_ref[...]) + ba_ref[...]) / GLA_GATE_TEMP
    row = lax.broadcasted_iota(jnp.int32, log_a.shape, 0) % ck
    bcum = log_a
    shift = 1
    while shift < ck:
        bcum = bcum + jnp.where(row >= shift, pltpu.roll(bcum, shift, 0), 0.0)
        shift *= 2
    q = gq_ref[...] * (dk ** -0.5)
    k = gk_ref[...]
    tri = (lax.broadcasted_iota(jnp.int32, (ck, ck), 0) >= lax.broadcasted_iota(jnp.int32, (ck, ck), 1))
    for c in range(tg // ck):
        rows = slice(c * ck, (c + 1) * ck)
        for h in range(GLA_HEADS):
            kcols = slice(h * dk, (h + 1) * dk)
            vcols = slice(h * dv, (h + 1) * dv)
            bc = bcum[rows, kcols]
            btot = bc[ck - 1:ck, :]
            qe = (q[rows, kcols] * jnp.exp(bc)).astype(BF16)
            ke = (k[rows, kcols] * jnp.exp(-bc)).astype(BF16)
            kd = (k[rows, kcols] * jnp.exp(btot - bc)).astype(BF16)
            vc = gv_ref[rows, vcols].astype(BF16)
            att = jnp.where(tri, _dot_nt(qe, ke), 0.0).astype(BF16)
            st = state_ref[h]
            o = _dot(att, vc) + _dot_nt(qe, st.astype(BF16))
            state_ref[h] = st * jnp.exp(btot) + _dot_tn(vc, kd)
            o = _rms(o, on_ref[...])
            r = gr_ref[rows, vcols]
            y_ref[rows, vcols] = (o * (r * jax.nn.sigmoid(r))).astype(BF16)


def _gla(gq, gk, gv, gr, gz, wa, ba, on):
    b, s, _ = gq.shape
    tg = 256
    tok = lambda w: pl.BlockSpec((None, tg, w), lambda bi, si: (bi, si, 0))
    full = lambda arr: pl.BlockSpec(arr.shape, lambda bi, si: (0,) * arr.ndim)
    return pl.pallas_call(
        _gla_kernel,
        grid=(b, s // tg),
        in_specs=[tok(GLA_QK_DIM), tok(GLA_QK_DIM), tok(GLA_V_DIM), tok(GLA_V_DIM), tok(GZ_PAD),
                  full(wa), full(ba), full(on)],
        out_specs=tok(GLA_V_DIM),
        out_shape=jax.ShapeDtypeStruct((b, s, GLA_V_DIM), BF16),
        scratch_shapes=[pltpu.VMEM((GLA_HEADS, GLA_VALUE_DIM, GLA_KEY_DIM), F32)],
        compiler_params=_cparams(2),
        name="gla",
    )(gq, gk, gv, gr, gz, wa, ba, on)


def _merge_route_kernel(x_ref, yc_ref, ya_ref, yg_ref, gmix_ref, wg_ref, wb_ref, wo_ref, gffn_ref,
                        wr_hi_ref, wr_lo_ref, br_ref, xo_ref, h2_ref, route_ref, counts_ref, run_ref):
    x = x_ref[...]
    h = _rms(x, gmix_ref[...]).astype(BF16)
    merged = None
    for n, y_ref in enumerate((yc_ref, ya_ref, yg_ref)):
        term = jax.nn.sigmoid(_dot(h, wg_ref[n])) * _dot(y_ref[...], wb_ref[n])
        merged = term if merged is None else merged + term
    xo = x + _dot(merged.astype(BF16), wo_ref[...])
    xo_ref[...] = xo
    h2 = _rms(xo, gffn_ref[...])
    h2_ref[...] = h2

    h_hi, h_lo = _split_bf16(h2)
    logits = (_dot(h_hi, wr_hi_ref[...]) + _dot(h_lo, wr_hi_ref[...]) + _dot(h_hi, wr_lo_ref[...])
              + br_ref[...])
    lane = lax.broadcasted_iota(jnp.int32, logits.shape, 1).astype(F32)
    big = 4.0 * ROUTE_LANES
    lg = jnp.where(lane < N_GROUPS, logits, -jnp.inf)
    gmax = jnp.max(lg, axis=1, keepdims=True)
    gidx = jnp.min(jnp.where(lg == gmax, lane, big), axis=1, keepdims=True)
    p_group_top = 1.0 / jnp.sum(jnp.exp(lg - gmax), axis=1, keepdims=True)
    lo_lane = N_GROUPS + gidx * EXPERTS_PER_GROUP
    le = jnp.where((lane >= lo_lane) & (lane < lo_lane + EXPERTS_PER_GROUP), logits, -jnp.inf)
    emax = jnp.max(le, axis=1, keepdims=True)
    i1 = jnp.min(jnp.where(le == emax, lane, big), axis=1, keepdims=True)
    esum = jnp.sum(jnp.exp(le - emax), axis=1, keepdims=True)
    le2 = jnp.where(lane == i1, -jnp.inf, le)
    emax2 = jnp.max(le2, axis=1, keepdims=True)
    i2 = jnp.min(jnp.where(le2 == emax2, lane, big), axis=1, keepdims=True)
    p1 = 1.0 / esum
    p2 = jnp.exp(emax2 - emax) / esum
    psum = p1 + p2
    w1 = p_group_top * p1 / psum
    w2 = p_group_top * p2 / psum
    e1 = i1 - N_GROUPS
    e2 = i2 - N_GROUPS

    @pl.when(pl.program_id(0) == 0)
    def _():
        run_ref[...] = jnp.zeros_like(run_ref)

    tm = x.shape[0]
    oh1 = jnp.where(lane == i1, 1.0, 0.0)
    oh2 = jnp.where(lane == i2, 1.0, 0.0)
    ohs = oh1 + oh2
    lower = (lax.broadcasted_iota(jnp.int32, (tm, tm), 0) > lax.broadcasted_iota(jnp.int32, (tm, tm), 1))
    before = _dot(jnp.where(lower, 1.0, 0.0).astype(BF16), ohs.astype(BF16)) + run_ref[...]
    rank1 = jnp.sum(oh1 * before, axis=1, keepdims=True)
    rank2 = jnp.sum(oh2 * before, axis=1, keepdims=True)
    run_ref[...] = run_ref[...] + jnp.sum(ohs, axis=0, keepdims=True)
    counts_ref[...] = run_ref[...]

    rl = lax.broadcasted_iota(jnp.int32, (tm, ROUTE_OUT), 1)
    rec = jnp.zeros((tm, ROUTE_OUT), F32)
    for slot, val in enumerate((e1, e2, w1, w2, rank1, rank2)):
        rec = jnp.where(rl == slot, val, rec)
    route_ref[...] = rec


def _merge_route(x2d, yc, ya, yg, gmix, wg, wb, wo, gffn, wr_hi, wr_lo, br):
    t, d = x2d.shape
    tm = 256
    tok = lambda w: pl.BlockSpec((tm, w), lambda i: (i, 0))
    full = lambda arr: pl.BlockSpec(arr.shape, lambda i: (0,) * arr.ndim)
    ins = (x2d, yc, ya, yg, gmix, wg, wb, wo, gffn, wr_hi, wr_lo, br)
    return pl.pallas_call(
        _merge_route_kernel,
        grid=(t // tm,),
        in_specs=[tok(d), tok(CONV_DIM), tok(ATTN_DIM), tok(GLA_V_DIM)] + [full(a) for a in ins[4:]],
        out_specs=(tok(d), tok(d), tok(ROUTE_OUT), pl.BlockSpec((1, ROUTE_LANES), lambda i: (0, 0))),
        out_shape=(jax.ShapeDtypeStruct((t, d), F32), jax.ShapeDtypeStruct((t, d), F32),
                   jax.ShapeDtypeStruct((t, ROUTE_OUT), F32), jax.ShapeDtypeStruct((1, ROUTE_LANES), F32)),
        scratch_shapes=[pltpu.VMEM((1, ROUTE_LANES), F32)],
        compiler_params=_cparams(1),
        name="merge_route",
    )(*ins)


def _tile_slots(dest, tc):
    nt = dest.shape[0] // tc
    return dest.reshape(nt, tc, TOP_K).transpose(0, 2, 1).reshape(nt, 1, TOP_K * tc)


def _dispatch_kernel(dest_ref, h2_ref, xs_init_hbm, xs_hbm, sem):
    del xs_init_hbm
    tc = h2_ref.shape[0]

    def row_copy(r, kk):
        dst = dest_ref[0, kk * tc + r]
        return pltpu.make_async_copy(h2_ref.at[pl.ds(r, 1)], xs_hbm.at[pl.ds(dst, 1)], sem)

    def issue(r, carry):
        row_copy(r, 0).start(priority=0)
        row_copy(r, 1).start(priority=1)
        return carry

    lax.fori_loop(0, tc, issue, 0, unroll=ROW_DMA_UNROLL)
    for _ in range(TOP_K):
        pltpu.make_async_copy(h2_ref, xs_hbm.at[pl.ds(0, tc)], sem).wait()


def _dispatch(dest, h2, n_rows):
    t, d = h2.shape
    tc = 256
    return pl.pallas_call(
        _dispatch_kernel,
        grid=(t // tc,),
        in_specs=[
            pl.BlockSpec((None, 1, TOP_K * tc), lambda i: (i, 0, 0), memory_space=pltpu.SMEM),
            pl.BlockSpec((tc, d), lambda i: (i, 0)),
            pl.BlockSpec(memory_space=pl.ANY),
        ],
        out_specs=pl.BlockSpec(memory_space=pl.ANY),
        out_shape=jax.ShapeDtypeStruct((n_rows, d), F32),
        scratch_shapes=[pltpu.SemaphoreType.DMA(())],
        input_output_aliases={2: 0},
        compiler_params=_cparams(1),
        name="moe_dispatch",
    )(_tile_slots(dest, tc), h2, jnp.zeros((n_rows, d), F32))


def _expert_kernel(blk_expert_ref, n_used_ref, xs_ref, wg_ref, wu_ref, wd_ref, ys_ref):
    i = pl.program_id(0)

    @pl.when(i < n_used_ref[0])
    def _():
        xb = xs_ref[...].astype(BF16)
        gate = _dot(xb, wg_ref[...])
        up = _dot(xb, wu_ref[...])
        act = (gate * jax.nn.sigmoid(gate) * up).astype(BF16)
        ys_ref[...] = _dot(act, wd_ref[...])

    @pl.when(i >= n_used_ref[0])
    def _():
        ys_ref[...] = jnp.zeros_like(ys_ref)


def _experts(blk_expert, n_used, xs, wg, wu, wd):
    n_rows, d = xs.shape
    n_blocks = blk_expert.shape[0]
    rb = EXPERT_BLOCK
    grid_spec = pltpu.PrefetchScalarGridSpec(
        num_scalar_prefetch=2,
        grid=(n_blocks,),
        in_specs=[
            pl.BlockSpec((rb, d), lambda i, be, nu: (jnp.minimum(i, nu[0] - 1), 0)),
            pl.BlockSpec((None, d, EXPERT_FF), lambda i, be, nu: (be[i], 0, 0)),
            pl.BlockSpec((None, d, EXPERT_FF), lambda i, be, nu: (be[i], 0, 0)),
            pl.BlockSpec((None, EXPERT_FF, d), lambda i, be, nu: (be[i], 0, 0)),
        ],
        out_specs=pl.BlockSpec((rb, d), lambda i, be, nu: (i, 0)),
    )
    return pl.pallas_call(
        _expert_kernel,
        grid_spec=grid_spec,
        out_shape=jax.ShapeDtypeStruct((n_rows, d), F32),
        compiler_params=_cparams(1),
        name="moe_experts",
    )(blk_expert, n_used, xs, wg, wu, wd)


def _combine_kernel(dest_ref, x_ref, route_ref, ys_hbm, o_ref, buf, sem):
    tc = x_ref.shape[0]

    def row_copy(r, kk):
        src = dest_ref[0, kk * tc + r]
        return pltpu.make_async_copy(ys_hbm.at[pl.ds(src, 1)], buf.at[kk, pl.ds(r, 1)], sem)

    def issue(r, carry):
        row_copy(r, 0).start(priority=0)
        row_copy(r, 1).start(priority=1)
        return carry

    lax.fori_loop(0, tc, issue, 0, unroll=ROW_DMA_UNROLL)
    for kk in range(TOP_K):
        pltpu.make_async_copy(ys_hbm.at[pl.ds(0, tc)], buf.at[kk], sem).wait()
    route = route_ref[...]
    w1 = route[:, TOP_K:TOP_K + 1]
    w2 = route[:, TOP_K + 1:TOP_K + 2]
    o_ref[...] = x_ref[...] + (buf[0] * w1 + buf[1] * w2)


def _combine(dest, x2d, route, ys):
    t, d = x2d.shape
    tc = 256
    return pl.pallas_call(
        _combine_kernel,
        grid=(t // tc,),
        in_specs=[
            pl.BlockSpec((None, 1, TOP_K * tc), lambda i: (i, 0, 0), memory_space=pltpu.SMEM),
            pl.BlockSpec((tc, d), lambda i: (i, 0)),
            pl.BlockSpec((tc, ROUTE_OUT), lambda i: (i, 0)),
            pl.BlockSpec(memory_space=pl.ANY),
        ],
        out_specs=pl.BlockSpec((tc, d), lambda i: (i, 0)),
        out_shape=jax.ShapeDtypeStruct((t, d), F32),
        scratch_shapes=[pltpu.VMEM((TOP_K, tc, d), F32), pltpu.SemaphoreType.DMA(())],
        compiler_params=_cparams(1),
        name="moe_combine",
    )(_tile_slots(dest, tc), x2d, route, ys)


def _dispatch_plan(route, counts, t):
    e_ids = route[:, :TOP_K].astype(jnp.int32)
    rank = route[:, 2 * TOP_K:3 * TOP_K].astype(jnp.int32)
    counts = counts[0, N_GROUPS:N_GROUPS + N_EXPERTS].astype(jnp.int32)
    padded = ((counts + EXPERT_BLOCK - 1) // EXPERT_BLOCK) * EXPERT_BLOCK
    pad_end = jnp.cumsum(padded)
    pad_start = pad_end - padded
    onehot = e_ids[:, :, None] == jnp.arange(N_EXPERTS, dtype=jnp.int32)
    dest = rank + jnp.sum(jnp.where(onehot, pad_start, 0), axis=-1)
    n_blocks = -(-(t * TOP_K) // EXPERT_BLOCK) + N_EXPERTS
    blk_start = jnp.arange(n_blocks, dtype=jnp.int32) * EXPERT_BLOCK
    blk_expert = jnp.minimum(jnp.sum(blk_start[:, None] >= pad_end[None, :], axis=1), N_EXPERTS - 1)
    n_used = jnp.maximum(pad_end[-1:] // EXPERT_BLOCK, 1).astype(jnp.int32)
    return blk_expert.astype(jnp.int32), n_used, dest, n_blocks * EXPERT_BLOCK


def kernel(x, rel_bias_table, norm_mix, w_in, conv_w, conv_b, q_norm, k_norm, w_gla_alpha, b_gla_alpha,
           gla_out_norm, w_merge_gate, w_branch, w_out, norm_ffn, w_router_group, b_router_group,
           w_router_expert, b_router_expert, w_expert_gate, w_expert_up, w_expert_down):
    b, s, d = x.shape
    t = b * s
    depth = w_in.shape[0]
    assert s % MOBA_BLOCK == 0 and t % EXPERT_BLOCK == 0
    bias0, bias1 = _attn_biases(rel_bias_table)
    head_id = jnp.arange(ATTN_DIM) // ATTN_HEAD_DIM
    hsum = (head_id[:, None] == head_id[None, :]).astype(BF16)
    c3 = 3 * CONV_DIM
    a3 = c3 + 3 * ATTN_DIM
    g3 = a3 + 2 * GLA_QK_DIM + 2 * GLA_V_DIM
    for l in range(depth):
        w_l = w_in[l].astype(BF16)
        w_gz = jnp.pad(w_l[:, g3:], ((0, 0), (0, GZ_PAD - GLA_GATE_RANK)))
        w_alpha = jnp.pad(w_gla_alpha[l].astype(BF16), ((0, GZ_PAD - GLA_GATE_RANK), (0, 0)))
        yconv, q, k, v, kmean, gq, gk, gv, gr, gz = _inproj(
            x, norm_mix[l][None], w_l[:, :c3], w_l[:, c3:a3], w_l[:, a3:g3], w_gz,
            conv_w[l], conv_b[l][None],
            jnp.tile(q_norm[l], ATTN_HEADS)[None], jnp.tile(k_norm[l], ATTN_HEADS)[None], hsum)
        qp, kp = _select(q, k, kmean.reshape(b, s // MOBA_BLOCK, ATTN_DIM))
        yattn = _attention(qp, kp, v, bias0, bias1)
        ygla = _gla(gq, gk, gv, gr, gz, w_alpha, b_gla_alpha[l][None],
                    gla_out_norm[l][None])
        w_r = jnp.concatenate([w_router_group[l], w_router_expert[l]], axis=1)
        w_r = jnp.pad(w_r, ((0, 0), (0, ROUTE_LANES - w_r.shape[1])))
        b_r = jnp.pad(jnp.concatenate([b_router_group[l], b_router_expert[l]]),
                      (0, ROUTE_LANES - N_GROUPS - N_EXPERTS))[None]
        wr_hi, wr_lo = _split_bf16(w_r)
        xo, h2, route, counts = _merge_route(
            x.reshape(t, d), yconv.reshape(t, -1), yattn.reshape(t, -1), ygla.reshape(t, -1),
            norm_mix[l][None], w_merge_gate[l].astype(BF16), w_branch[l].astype(BF16),
            w_out[l].astype(BF16), norm_ffn[l][None], wr_hi, wr_lo, b_r)
        blk_expert, n_used, dest, n_rows = _dispatch_plan(route, counts, t)
        xs = _dispatch(dest, h2, n_rows)
        ys = _experts(blk_expert, n_used, xs, w_expert_gate[l].astype(BF16),
                      w_expert_up[l].astype(BF16), w_expert_down[l].astype(BF16))
        x = _combine(dest, xo, route, ys).reshape(b, s, d)
    return x
```

```python
import functools
import math

import jax
import jax.numpy as jnp
import numpy as np
from jax import lax
from jax.experimental import pallas as pl
from jax.experimental.pallas import tpu as pltpu

CONV_DIM = 512
CONV_WIDTH = 3
ATTN_HEADS = 8
ATTN_HEAD_DIM = 64
ATTN_DIM = ATTN_HEADS * ATTN_HEAD_DIM
MOBA_BLOCK = 256
MOBA_TOPK = 3
REL_BUCKETS = 32
REL_MAX_DISTANCE = 128
GLA_HEADS = 4
GLA_KEY_DIM = 64
GLA_VALUE_DIM = 128
GLA_QK_DIM = GLA_HEADS * GLA_KEY_DIM
GLA_V_DIM = GLA_HEADS * GLA_VALUE_DIM
GLA_GATE_RANK = 16
GLA_GATE_TEMP = 16.0
GLA_CHUNK = 64
N_GROUPS = 4
EXPERTS_PER_GROUP = 8
N_EXPERTS = N_GROUPS * EXPERTS_PER_GROUP
TOP_K = 2
EXPERT_FF = 512
EXPERT_BLOCK = 256
RMS_EPS = 1e-6

LANES = 128
VMEM_LIMIT_BYTES = 56 * 1024 * 1024

MASK_VALUE = -1e30
ROUTE_LANES = 128
ROUTE_OUT = 8
ATTN_FAR_CHUNK = 4
ATTN_TAIL = 5
ROW_DMA_UNROLL = 8
GZ_PAD = LANES

F32 = jnp.float32
BF16 = jnp.bfloat16


def _cparams(n_axes):
    return pltpu.CompilerParams(
        dimension_semantics=("arbitrary",) * n_axes,
        vmem_limit_bytes=VMEM_LIMIT_BYTES,
    )


def _rms(x, gain):
    return x * lax.rsqrt(jnp.mean(x * x, axis=-1, keepdims=True) + RMS_EPS) * gain


def _split_bf16(x):
    hi = x.astype(BF16)
    lo = (x - hi.astype(F32)).astype(BF16)
    return hi, lo


def _dot(a, b):
    return jnp.dot(a, b, preferred_element_type=F32)


def _dot_nt(a, b):
    return lax.dot_general(a, b, (((1,), (1,)), ((), ())), preferred_element_type=F32)


SUBLANES = 8


def _rows_to_tiles(dst_ref, x):
    n = x.shape[0]
    for j in range(SUBLANES):
        dst_ref[pl.ds(j, n, stride=SUBLANES), :] = x[:, j * LANES:(j + 1) * LANES]


def _tiles_to_rows(src_ref, n):
    return [src_ref[pl.ds(j, n, stride=SUBLANES), :] for j in range(SUBLANES)]


def _dot_tn(a, b):
    return lax.dot_general(a, b, (((0,), (0,)), ((), ())), preferred_element_type=F32)


def _inproj_kernel(x_ref, gmix_ref, wconv_ref, wattn_ref, wgla_ref, wgz_ref, convw_ref, convb_ref,
                   qn_ref, kn_ref, hsum_ref,
                   yconv_ref, q_ref, k_ref, v_ref, kmean_ref, gq_ref, gk_ref, gv_ref, gr_ref, gz_ref,
                   carry_ref):
    s_idx = pl.program_id(1)
    ts = x_ref.shape[0]
    h = _rms(x_ref[...], gmix_ref[...]).astype(BF16)

    c = _dot(h, wconv_ref[...])
    cb = c[:, :CONV_DIM]
    u = c[:, CONV_DIM:2 * CONV_DIM] * c[:, 2 * CONV_DIM:]

    @pl.when(s_idx == 0)
    def _():
        carry_ref[...] = jnp.zeros_like(carry_ref)

    prev = carry_ref[...]
    row = lax.broadcasted_iota(jnp.int32, u.shape, 0)
    u1 = pltpu.roll(u, 1, 0)
    u1 = jnp.where(row == 0, prev[7:8, :], u1)
    u2 = pltpu.roll(u, 2, 0)
    u2 = jnp.where(row == 0, prev[6:7, :], jnp.where(row == 1, prev[7:8, :], u2))
    carry_ref[...] = u[ts - 8:, :]
    y = convb_ref[...] + convw_ref[0:1, :] * u2
    y = y + convw_ref[1:2, :] * u1
    y = y + convw_ref[2:3, :] * u
    yconv_ref[...] = (cb * y).astype(BF16)

    a = _dot(h, wattn_ref[...])
    hsum = hsum_ref[...]

    def head_norm(t, gain):
        hi, lo = _split_bf16(t * t)
        ss = _dot(hi, hsum) + _dot(lo, hsum)
        return t * lax.rsqrt(ss * (1.0 / ATTN_HEAD_DIM) + RMS_EPS) * gain

    qn = head_norm(a[:, :ATTN_DIM], qn_ref[...])
    kn = head_norm(a[:, ATTN_DIM:2 * ATTN_DIM], kn_ref[...])
    q_ref[...] = (qn * (ATTN_HEAD_DIM ** -0.5)).astype(BF16)
    k_ref[...] = kn.astype(BF16)
    v_ref[...] = a[:, 2 * ATTN_DIM:].astype(BF16)
    kmean_ref[...] = jnp.mean(kn, axis=0, keepdims=True)

    g = _dot(h, wgla_ref[...])
    gq_ref[...] = g[:, :GLA_QK_DIM]
    gk_ref[...] = g[:, GLA_QK_DIM:2 * GLA_QK_DIM]
    gv_ref[...] = g[:, 2 * GLA_QK_DIM:2 * GLA_QK_DIM + GLA_V_DIM]
    gr_ref[...] = g[:, 2 * GLA_QK_DIM + GLA_V_DIM:]
    gz_ref[...] = _dot(h, wgz_ref[...])


def _inproj(x, gmix, wconv, wattn, wgla, wgz, convw, convb, qn, kn, hsum):
    b, s, d = x.shape
    ts = MOBA_BLOCK
    nb = s // ts
    tok = lambda w: pl.BlockSpec((None, ts, w), lambda bi, si: (bi, si, 0))
    full = lambda arr: pl.BlockSpec(arr.shape, lambda bi, si: (0,) * arr.ndim)
    out_shapes = (
        jax.ShapeDtypeStruct((b, s, CONV_DIM), BF16),
        jax.ShapeDtypeStruct((b, s, ATTN_DIM), BF16),
        jax.ShapeDtypeStruct((b, s, ATTN_DIM), BF16),
        jax.ShapeDtypeStruct((b, s, ATTN_DIM), BF16),
        jax.ShapeDtypeStruct((b, nb, 1, ATTN_DIM), F32),
        jax.ShapeDtypeStruct((b, s, GLA_QK_DIM), F32),
        jax.ShapeDtypeStruct((b, s, GLA_QK_DIM), F32),
        jax.ShapeDtypeStruct((b, s, GLA_V_DIM), F32),
        jax.ShapeDtypeStruct((b, s, GLA_V_DIM), F32),
        jax.ShapeDtypeStruct((b, s, GZ_PAD), F32),
    )
    out_specs = (
        tok(CONV_DIM), tok(ATTN_DIM), tok(ATTN_DIM), tok(ATTN_DIM),
        pl.BlockSpec((None, None, 1, ATTN_DIM), lambda bi, si: (bi, si, 0, 0)),
        tok(GLA_QK_DIM), tok(GLA_QK_DIM), tok(GLA_V_DIM), tok(GLA_V_DIM), tok(GZ_PAD),
    )
    ins = (x, gmix, wconv, wattn, wgla, wgz, convw, convb, qn, kn, hsum)
    in_specs = [tok(d)] + [full(a) for a in ins[1:]]
    return pl.pallas_call(
        _inproj_kernel,
        grid=(b, nb),
        in_specs=in_specs,
        out_specs=out_specs,
        out_shape=out_shapes,
        scratch_shapes=[pltpu.VMEM((8, CONV_DIM), F32)],
        compiler_params=_cparams(2),
        name="inproj",
    )(*ins)


def _select_kernel(q_ref, k_ref, kmean_ref, qp_ref, kp_ref):
    own = pl.program_id(1)
    tq = q_ref.shape[0]
    nb = kmean_ref.shape[0]
    half = ATTN_HEAD_DIM
    lane = lax.broadcasted_iota(jnp.int32, (tq, LANES), 1)
    blk = lax.broadcasted_iota(jnp.int32, (LANES, tq), 0).astype(F32)
    own_f = own.astype(F32)
    onehot = jnp.where(lane - half == own, 1.0, 0.0).astype(F32)
    kmean = kmean_ref[...].astype(BF16)
    for p in range(ATTN_HEADS // 2):
        qpair = q_ref[:, p * LANES:(p + 1) * LANES]
        kpair = k_ref[:, p * LANES:(p + 1) * LANES].astype(F32)
        kmpair = kmean[:, p * LANES:(p + 1) * LANES]
        qpair_f = qpair.astype(F32)
        for sub in range(2):
            h = 2 * p + sub
            lane_sel = (lane >= sub * half) & (lane < (sub + 1) * half)
            qh = jnp.where(lane_sel, qpair_f, 0.0).astype(BF16)
            gate_t = _dot_nt(kmpair, qh)
            gate_t = jnp.concatenate(
                [gate_t, jnp.full((LANES - nb, tq), -jnp.inf, F32)], axis=0)
            g = jnp.where(blk < own_f, gate_t, -jnp.inf)
            alive = jnp.where(blk < nb, 1.0, 0.0)
            sel = jnp.where(blk == own_f, 1.0, 0.0)
            for r in range(MOBA_TOPK):
                ga = jnp.where(alive > 0.0, g, -jnp.inf)
                mx = jnp.max(ga, axis=0, keepdims=True)
                cand = jnp.where((alive > 0.0) & (g == mx), blk, 2.0 * LANES)
                first = jnp.min(cand, axis=0, keepdims=True)
                hit = blk == first
                sel = jnp.where(hit, jnp.maximum(sel, jnp.where(own_f > r, 1.0, 0.0)), sel)
                alive = jnp.where(hit, 0.0, alive)
            m_t = jnp.where(sel > 0.0, 0.0, MASK_VALUE).astype(F32)
            m = jnp.transpose(m_t)
            m = pltpu.roll(m, half, 1)
            m = jnp.where((lane >= half) & (lane < half + nb), m, 0.0)
            qs = qpair_f if sub == 0 else pltpu.roll(qpair_f, half, 1)
            ks = kpair if sub == 0 else pltpu.roll(kpair, half, 1)
            qp_ref[h] = jnp.where(lane < half, qs, m).astype(BF16)
            kp_ref[h] = jnp.where(lane < half, ks, onehot).astype(BF16)


def _select(q, k, kmean):
    b, s, _ = q.shape
    tq = MOBA_BLOCK
    nb = s // tq
    assert nb <= LANES - ATTN_HEAD_DIM, "block one-hot must fit beside the head dim in one lane tile"
    tok = pl.BlockSpec((None, tq, ATTN_DIM), lambda bi, si: (bi, si, 0))
    slab = pl.BlockSpec((None, ATTN_HEADS, tq, LANES), lambda bi, si: (bi, 0, si, 0))
    shp = jax.ShapeDtypeStruct((b, ATTN_HEADS, s, LANES), BF16)
    return pl.pallas_call(
        _select_kernel,
        grid=(b, nb),
        in_specs=[tok, tok, pl.BlockSpec((None, nb, ATTN_DIM), lambda bi, si: (bi, 0, 0))],
        out_specs=(slab, slab),
        out_shape=(shp, shp),
        compiler_params=_cparams(2),
        name="moba_select",
    )(q, k, kmean)


def _attn_kernel(qp_ref, kp_ref, v_ref, bias0_ref, bias1_ref, o_ref, m_ref, l_ref, acc_ref,
                 sa_ref, sb_ref, st_ref):
    own = pl.program_id(2)
    tq = qp_ref.shape[1]
    far_rows = ATTN_FAR_CHUNK * tq
    lane = lax.broadcasted_iota(jnp.int32, (tq, LANES), 1)
    n_far = jnp.maximum(own - 1, 0) // ATTN_FAR_CHUNK

    def scores(hh, start, n):
        return _dot_nt(qp_ref[hh], kp_ref[hh, pl.ds(start, n * tq), :])

    def far_scores(dst_ref, chunk):
        start = pl.multiple_of(chunk * far_rows, far_rows)
        for hh in range(2):
            dst_ref[hh] = scores(hh, start, ATTN_FAR_CHUNK)

    def softmax_pv(hh, s_ref, start, n):
        pieces = [s_ref[hh, :, i * LANES:(i + 1) * LANES] for i in range(n * tq // LANES)]
        m_prev = m_ref[hh]
        m_new = jnp.maximum(m_prev, jnp.max(functools.reduce(jnp.maximum, pieces), axis=1, keepdims=True))
        alpha = jnp.exp(m_prev - m_new)
        ps = [jnp.exp(piece - m_new) for piece in pieces]
        l_ref[hh] = alpha * l_ref[hh] + functools.reduce(jnp.add, ps)
        p = jnp.concatenate([x.astype(BF16) for x in ps], axis=1)
        acc_ref[hh] = alpha * acc_ref[hh] + _dot(p, v_ref[pl.ds(start, n * tq), :])
        m_ref[hh] = m_new

    for hh in range(2):
        m_ref[hh] = jnp.full((tq, LANES), MASK_VALUE, F32)
        l_ref[hh] = jnp.zeros((tq, LANES), F32)
        acc_ref[hh] = jnp.zeros((tq, LANES), F32)

    for own_small in range(ATTN_TAIL - 1):
        @pl.when(own == own_small)
        def _():
            for hh in range(2):
                s = scores(hh, 0, own_small + 1)
                for w in range(own_small + 1):
                    sw = s[:, w * tq:(w + 1) * tq]
                    if w == own_small:
                        sw = sw + bias0_ref[hh]
                    elif w == own_small - 1:
                        sw = sw + bias1_ref[hh]
                    st_ref[hh, :, w * tq:(w + 1) * tq] = sw

    @pl.when(own >= ATTN_TAIL - 1)
    def _():
        first = own - (ATTN_TAIL - 1)
        done = n_far * ATTN_FAR_CHUNK
        for hh in range(2):
            s = scores(hh, pl.multiple_of(first * tq, tq), ATTN_TAIL)
            for w in range(ATTN_TAIL):
                sw = s[:, w * tq:(w + 1) * tq]
                if w == ATTN_TAIL - 1:
                    sw = sw + bias0_ref[hh]
                elif w == ATTN_TAIL - 2:
                    sw = sw + bias1_ref[hh]
                else:
                    sw = sw + jnp.where(first + w < done, MASK_VALUE, 0.0).astype(F32)
                st_ref[hh, :, w * tq:(w + 1) * tq] = sw

    @pl.when(n_far > 0)
    def _():
        far_scores(sa_ref, 0)

    def far_pair(j, carry):
        c0 = 2 * j
        far_scores(sb_ref, jnp.minimum(c0 + 1, n_far - 1))
        for hh in range(2):
            softmax_pv(hh, sa_ref, pl.multiple_of(c0 * far_rows, far_rows), ATTN_FAR_CHUNK)

        @pl.when(c0 + 1 < n_far)
        def _():
            far_scores(sa_ref, jnp.minimum(c0 + 2, n_far - 1))
            for hh in range(2):
                softmax_pv(hh, sb_ref, pl.multiple_of((c0 + 1) * far_rows, far_rows), ATTN_FAR_CHUNK)

        return carry

    lax.fori_loop(0, (n_far + 1) // 2, far_pair, 0)

    for own_small in range(ATTN_TAIL - 1):
        @pl.when(own == own_small)
        def _():
            for hh in range(2):
                softmax_pv(hh, st_ref, 0, own_small + 1)

    @pl.when(own >= ATTN_TAIL - 1)
    def _():
        start = pl.multiple_of((own - (ATTN_TAIL - 1)) * tq, tq)
        for hh in range(2):
            softmax_pv(hh, st_ref, start, ATTN_TAIL)

    outs = [acc_ref[hh] / jnp.sum(l_ref[hh], axis=1, keepdims=True) for hh in range(2)]
    o_ref[...] = jnp.where(lane < ATTN_HEAD_DIM, outs[0], outs[1]).astype(BF16)


def _attention(qp, kp, v, bias0, bias1):
    b, nh, s, _ = qp.shape
    tq = MOBA_BLOCK
    nq = s // tq
    return pl.pallas_call(
        _attn_kernel,
        grid=(b, nh // 2, nq),
        in_specs=[
            pl.BlockSpec((None, 2, tq, LANES), lambda bi, pi, qi: (bi, pi, qi, 0)),
            pl.BlockSpec((None, 2, s, LANES), lambda bi, pi, qi: (bi, pi, 0, 0)),
            pl.BlockSpec((None, s, LANES), lambda bi, pi, qi: (bi, 0, pi)),
            pl.BlockSpec((2, tq, tq), lambda bi, pi, qi: (pi, 0, 0)),
            pl.BlockSpec((2, tq, tq), lambda bi, pi, qi: (pi, 0, 0)),
        ],
        out_specs=pl.BlockSpec((None, tq, LANES), lambda bi, pi, qi: (bi, qi, pi)),
        out_shape=jax.ShapeDtypeStruct((b, s, ATTN_DIM), BF16),
        scratch_shapes=[
            pltpu.VMEM((2, tq, LANES), F32),
            pltpu.VMEM((2, tq, LANES), F32),
            pltpu.VMEM((2, tq, LANES), F32),
            pltpu.VMEM((2, tq, ATTN_FAR_CHUNK * tq), F32),
            pltpu.VMEM((2, tq, ATTN_FAR_CHUNK * tq), F32),
            pltpu.VMEM((2, tq, ATTN_TAIL * tq), F32),
        ],
        compiler_params=_cparams(3),
        name="moba_attention",
    )(qp, kp, v, bias0, bias1)


def _t5_bucket(rel):
    n = jnp.maximum(rel, 0)
    max_exact = REL_BUCKETS // 2
    scaled = (jnp.log(jnp.maximum(n, max_exact).astype(F32) / max_exact)
              / math.log(REL_MAX_DISTANCE / max_exact))
    large = jnp.minimum(max_exact + (scaled * (REL_BUCKETS - max_exact)).astype(jnp.int32), REL_BUCKETS - 1)
    return jnp.where(n < max_exact, n, large)


def _attn_biases(rel_bias_table):
    table = rel_bias_table.astype(F32).T
    pos = jnp.arange(MOBA_BLOCK)
    rel0 = pos[:, None] - pos[None, :]
    far = table[:, REL_BUCKETS - 1][:, None, None]

    def lookup(bucket):
        hit = bucket[None, :, :, None] == jnp.arange(REL_BUCKETS)
        return jnp.sum(jnp.where(hit, table[:, None, None, :], 0.0), axis=-1)

    bias0 = jnp.where(rel0[None] >= 0, lookup(_t5_bucket(rel0)) - far, MASK_VALUE)
    bias1 = lookup(_t5_bucket(rel0 + MOBA_BLOCK)) - far
    return bias0, bias1


def _gla_kernel(gq_ref, gk_ref, gv_ref, gr_ref, gz_ref, wa_ref, ba_ref, on_ref, y_ref, state_ref):
    s_idx = pl.program_id(1)
    tg = gq_ref.shape[0]
    dk, dv, ck = GLA_KEY_DIM, GLA_VALUE_DIM, GLA_CHUNK

    @pl.when(s_idx == 0)
    def _():
        state_ref[...] = jnp.zeros_like(state_ref)

    z = gz_ref[...].astype(BF16)
    log_a = jax.nn.log_sigmoid(_dot(z, wa_ref[...]) + ba_ref[...]) / GLA_GATE_TEMP
    row = lax.broadcasted_iota(jnp.int32, log_a.shape, 0) % ck
    bcum = log_a
    shift = 1
    while shift < ck:
        bcum = bcum + jnp.where(row >= shift, pltpu.roll(bcum, shift, 0), 0.0)
        shift *= 2
    q = gq_ref[...] * (dk ** -0.5)
    k = gk_ref[...]
    tri = (lax.broadcasted_iota(jnp.int32, (ck, ck), 0) >= lax.broadcasted_iota(jnp.int32, (ck, ck), 1))
    for c in range(tg // ck):
        rows = slice(c * ck, (c + 1) * ck)
        for h in range(GLA_HEADS):
            kcols = slice(h * dk, (h + 1) * dk)
            vcols = slice(h * dv, (h + 1) * dv)
            bc = bcum[rows, kcols]
            btot = bc[ck - 1:ck, :]
            qe = (q[rows, kcols] * jnp.exp(bc)).astype(BF16)
            ke = (k[rows, kcols] * jnp.exp(-bc)).astype(BF16)
            kd = (k[rows, kcols] * jnp.exp(btot - bc)).astype(BF16)
            vc = gv_ref[rows, vcols].astype(BF16)
            att = jnp.where(tri, _dot_nt(qe, ke), 0.0).astype(BF16)
            st = state_ref[h]
            o = _dot(att, vc) + _dot_nt(qe, st.astype(BF16))
            state_ref[h] = st * jnp.exp(btot) + _dot_tn(vc, kd)
            o = _rms(o, on_ref[...])
            r = gr_ref[rows, vcols]
            y_ref[rows, vcols] = (o * (r * jax.nn.sigmoid(r))).astype(BF16)


def _gla(gq, gk, gv, gr, gz, wa, ba, on):
    b, s, _ = gq.shape
    tg = 256
    tok = lambda w: pl.BlockSpec((None, tg, w), lambda bi, si: (bi, si, 0))
    full = lambda arr: pl.BlockSpec(arr.shape, lambda bi, si: (0,) * arr.ndim)
    return pl.pallas_call(
        _gla_kernel,
        grid=(b, s // tg),
        in_specs=[tok(GLA_QK_DIM), tok(GLA_QK_DIM), tok(GLA_V_DIM), tok(GLA_V_DIM), tok(GZ_PAD),
                  full(wa), full(ba), full(on)],
        out_specs=tok(GLA_V_DIM),
        out_shape=jax.ShapeDtypeStruct((b, s, GLA_V_DIM), BF16),
        scratch_shapes=[pltpu.VMEM((GLA_HEADS, GLA_VALUE_DIM, GLA_KEY_DIM), F32)],
        compiler_params=_cparams(2),
        name="gla",
    )(gq, gk, gv, gr, gz, wa, ba, on)


def _merge_route_kernel(x_ref, yc_ref, ya_ref, yg_ref, gmix_ref, wg_ref, wb_ref, wo_ref, gffn_ref,
                        wr_hi_ref, wr_lo_ref, br_ref, xo_ref, h2_ref, route_ref, counts_ref, run_ref):
    x = x_ref[...]
    h = _rms(x, gmix_ref[...]).astype(BF16)
    merged = None
    for n, y_ref in enumerate((yc_ref, ya_ref, yg_ref)):
        term = jax.nn.sigmoid(_dot(h, wg_ref[n])) * _dot(y_ref[...], wb_ref[n])
        merged = term if merged is None else merged + term
    xo = x + _dot(merged.astype(BF16), wo_ref[...])
    xo_ref[...] = xo
    h2 = _rms(xo, gffn_ref[...])
    _rows_to_tiles(h2_ref, h2)

    h_hi, h_lo = _split_bf16(h2)
    logits = (_dot(h_hi, wr_hi_ref[...]) + _dot(h_lo, wr_hi_ref[...]) + _dot(h_hi, wr_lo_ref[...])
              + br_ref[...])
    lane = lax.broadcasted_iota(jnp.int32, logits.shape, 1).astype(F32)
    big = 4.0 * ROUTE_LANES
    lg = jnp.where(lane < N_GROUPS, logits, -jnp.inf)
    gmax = jnp.max(lg, axis=1, keepdims=True)
    gidx = jnp.min(jnp.where(lg == gmax, lane, big), axis=1, keepdims=True)
    p_group_top = 1.0 / jnp.sum(jnp.exp(lg - gmax), axis=1, keepdims=True)
    lo_lane = N_GROUPS + gidx * EXPERTS_PER_GROUP
    le = jnp.where((lane >= lo_lane) & (lane < lo_lane + EXPERTS_PER_GROUP), logits, -jnp.inf)
    emax = jnp.max(le, axis=1, keepdims=True)
    i1 = jnp.min(jnp.where(le == emax, lane, big), axis=1, keepdims=True)
    esum = jnp.sum(jnp.exp(le - emax), axis=1, keepdims=True)
    le2 = jnp.where(lane == i1, -jnp.inf, le)
    emax2 = jnp.max(le2, axis=1, keepdims=True)
    i2 = jnp.min(jnp.where(le2 == emax2, lane, big), axis=1, keepdims=True)
    p1 = 1.0 / esum
    p2 = jnp.exp(emax2 - emax) / esum
    psum = p1 + p2
    w1 = p_group_top * p1 / psum
    w2 = p_group_top * p2 / psum
    e1 = i1 - N_GROUPS
    e2 = i2 - N_GROUPS

    @pl.when(pl.program_id(0) == 0)
    def _():
        run_ref[...] = jnp.zeros_like(run_ref)

    tm = x.shape[0]
    oh1 = jnp.where(lane == i1, 1.0, 0.0)
    oh2 = jnp.where(lane == i2, 1.0, 0.0)
    ohs = oh1 + oh2
    lower = (lax.broadcasted_iota(jnp.int32, (tm, tm), 0) > lax.broadcasted_iota(jnp.int32, (tm, tm), 1))
    before = _dot(jnp.where(lower, 1.0, 0.0).astype(BF16), ohs.astype(BF16)) + run_ref[...]
    rank1 = jnp.sum(oh1 * before, axis=1, keepdims=True)
    rank2 = jnp.sum(oh2 * before, axis=1, keepdims=True)
    run_ref[...] = run_ref[...] + jnp.sum(ohs, axis=0, keepdims=True)
    counts_ref[...] = run_ref[...]

    rl = lax.broadcasted_iota(jnp.int32, (tm, ROUTE_OUT), 1)
    rec = jnp.zeros((tm, ROUTE_OUT), F32)
    for slot, val in enumerate((e1, e2, w1, w2, rank1, rank2)):
        rec = jnp.where(rl == slot, val, rec)
    route_ref[...] = rec


def _merge_route(x2d, yc, ya, yg, gmix, wg, wb, wo, gffn, wr_hi, wr_lo, br):
    t, d = x2d.shape
    tm = 256
    tok = lambda w: pl.BlockSpec((tm, w), lambda i: (i, 0))
    full = lambda arr: pl.BlockSpec(arr.shape, lambda i: (0,) * arr.ndim)
    ins = (x2d, yc, ya, yg, gmix, wg, wb, wo, gffn, wr_hi, wr_lo, br)
    return pl.pallas_call(
        _merge_route_kernel,
        grid=(t // tm,),
        in_specs=[tok(d), tok(CONV_DIM), tok(ATTN_DIM), tok(GLA_V_DIM)] + [full(a) for a in ins[4:]],
        out_specs=(tok(d), pl.BlockSpec((tm * SUBLANES, LANES), lambda i: (i, 0)), tok(ROUTE_OUT),
                   pl.BlockSpec((1, ROUTE_LANES), lambda i: (0, 0))),
        out_shape=(jax.ShapeDtypeStruct((t, d), F32), jax.ShapeDtypeStruct((t * SUBLANES, LANES), F32),
                   jax.ShapeDtypeStruct((t, ROUTE_OUT), F32), jax.ShapeDtypeStruct((1, ROUTE_LANES), F32)),
        scratch_shapes=[pltpu.VMEM((1, ROUTE_LANES), F32)],
        compiler_params=_cparams(1),
        name="merge_route",
    )(*ins)


def _tile_slots(dest, tc):
    nt = dest.shape[0] // tc
    return dest.reshape(nt, tc, TOP_K).transpose(0, 2, 1).reshape(nt, 1, TOP_K * tc)


def _dispatch_kernel(dest_ref, h2_ref, xs_init_hbm, xs_hbm, sem):
    del xs_init_hbm
    tc = h2_ref.shape[0] // SUBLANES

    def row_copy(r, kk):
        src = pl.multiple_of(r * SUBLANES, SUBLANES)
        dst = pl.multiple_of(dest_ref[0, kk * tc + r] * SUBLANES, SUBLANES)
        return pltpu.make_async_copy(h2_ref.at[pl.ds(src, SUBLANES)], xs_hbm.at[pl.ds(dst, SUBLANES)], sem)

    def issue(r, carry):
        row_copy(r, 0).start(priority=0)
        row_copy(r, 1).start(priority=1)
        return carry

    lax.fori_loop(0, tc, issue, 0, unroll=ROW_DMA_UNROLL)
    for _ in range(TOP_K):
        pltpu.make_async_copy(h2_ref, xs_hbm.at[pl.ds(0, tc * SUBLANES)], sem).wait()


def _dispatch(dest, h2_tiles, n_rows):
    t = h2_tiles.shape[0] // SUBLANES
    tc = 256
    return pl.pallas_call(
        _dispatch_kernel,
        grid=(t // tc,),
        in_specs=[
            pl.BlockSpec((None, 1, TOP_K * tc), lambda i: (i, 0, 0), memory_space=pltpu.SMEM),
            pl.BlockSpec((tc * SUBLANES, LANES), lambda i: (i, 0)),
            pl.BlockSpec(memory_space=pl.ANY),
        ],
        out_specs=pl.BlockSpec(memory_space=pl.ANY),
        out_shape=jax.ShapeDtypeStruct((n_rows * SUBLANES, LANES), F32),
        scratch_shapes=[pltpu.SemaphoreType.DMA(())],
        input_output_aliases={2: 0},
        compiler_params=_cparams(1),
        name="moe_dispatch",
    )(_tile_slots(dest, tc), h2_tiles, jnp.zeros((n_rows * SUBLANES, LANES), F32))


def _expert_kernel(blk_expert_ref, n_used_ref, xs_ref, wg_ref, wu_ref, wd_ref, ys_ref,
                   wg_bf, wu_bf, wd_bf):
    i = pl.program_id(0)
    rb = xs_ref.shape[0] // SUBLANES
    new_expert = (i == 0) | (blk_expert_ref[i] != blk_expert_ref[jnp.maximum(i - 1, 0)])

    @pl.when(new_expert)
    def _():
        wg_bf[...] = wg_ref[...].astype(BF16)
        wu_bf[...] = wu_ref[...].astype(BF16)
        wd_bf[...] = wd_ref[...].astype(BF16)

    @pl.when(i < n_used_ref[0])
    def _():
        xb = jnp.concatenate(_tiles_to_rows(xs_ref, rb), axis=1).astype(BF16)
        gate = _dot(xb, wg_bf[...])
        up = _dot(xb, wu_bf[...])
        act = (gate * jax.nn.sigmoid(gate) * up).astype(BF16)
        _rows_to_tiles(ys_ref, _dot(act, wd_bf[...]))

    @pl.when(i >= n_used_ref[0])
    def _():
        ys_ref[...] = jnp.zeros_like(ys_ref)


def _experts(blk_expert, n_used, xs_tiles, wg, wu, wd):
    d = SUBLANES * LANES
    n_blocks = blk_expert.shape[0]
    rb = EXPERT_BLOCK
    row_block = (rb * SUBLANES, LANES)
    grid_spec = pltpu.PrefetchScalarGridSpec(
        num_scalar_prefetch=2,
        grid=(n_blocks,),
        in_specs=[
            pl.BlockSpec(row_block, lambda i, be, nu: (jnp.minimum(i, nu[0] - 1), 0)),
            pl.BlockSpec((None, d, EXPERT_FF), lambda i, be, nu: (be[i], 0, 0)),
            pl.BlockSpec((None, d, EXPERT_FF), lambda i, be, nu: (be[i], 0, 0)),
            pl.BlockSpec((None, EXPERT_FF, d), lambda i, be, nu: (be[i], 0, 0)),
        ],
        out_specs=pl.BlockSpec(row_block, lambda i, be, nu: (i, 0)),
        scratch_shapes=[pltpu.VMEM((d, EXPERT_FF), BF16), pltpu.VMEM((d, EXPERT_FF), BF16),
                        pltpu.VMEM((EXPERT_FF, d), BF16)],
    )
    return pl.pallas_call(
        _expert_kernel,
        grid_spec=grid_spec,
        out_shape=jax.ShapeDtypeStruct(xs_tiles.shape, F32),
        compiler_params=_cparams(1),
        name="moe_experts",
    )(blk_expert, n_used, xs_tiles, wg, wu, wd)


def _combine_kernel(dest_ref, x_ref, route_ref, ys_hbm, o_ref, buf, sem):
    tc = x_ref.shape[0]

    def row_copy(r, kk):
        src = pl.multiple_of(dest_ref[0, kk * tc + r] * SUBLANES, SUBLANES)
        dst = pl.multiple_of(r * SUBLANES, SUBLANES)
        return pltpu.make_async_copy(ys_hbm.at[pl.ds(src, SUBLANES)], buf.at[kk, pl.ds(dst, SUBLANES)], sem)

    def issue(r, carry):
        row_copy(r, 0).start(priority=0)
        row_copy(r, 1).start(priority=1)
        return carry

    lax.fori_loop(0, tc, issue, 0, unroll=ROW_DMA_UNROLL)
    for kk in range(TOP_K):
        pltpu.make_async_copy(ys_hbm.at[pl.ds(0, tc * SUBLANES)], buf.at[kk], sem).wait()
    route = route_ref[...]
    w1 = route[:, TOP_K:TOP_K + 1]
    w2 = route[:, TOP_K + 1:TOP_K + 2]
    y1 = _tiles_to_rows(buf.at[0], tc)
    y2 = _tiles_to_rows(buf.at[1], tc)
    for j in range(SUBLANES):
        cols = slice(j * LANES, (j + 1) * LANES)
        o_ref[:, cols] = x_ref[:, cols] + (y1[j] * w1 + y2[j] * w2)


def _combine(dest, x2d, route, ys_tiles):
    t, d = x2d.shape
    tc = 256
    return pl.pallas_call(
        _combine_kernel,
        grid=(t // tc,),
        in_specs=[
            pl.BlockSpec((None, 1, TOP_K * tc), lambda i: (i, 0, 0), memory_space=pltpu.SMEM),
            pl.BlockSpec((tc, d), lambda i: (i, 0)),
            pl.BlockSpec((tc, ROUTE_OUT), lambda i: (i, 0)),
            pl.BlockSpec(memory_space=pl.ANY),
        ],
        out_specs=pl.BlockSpec((tc, d), lambda i: (i, 0)),
        out_shape=jax.ShapeDtypeStruct((t, d), F32),
        scratch_shapes=[pltpu.VMEM((TOP_K, tc * SUBLANES, LANES), F32), pltpu.SemaphoreType.DMA(())],
        compiler_params=_cparams(1),
        name="moe_combine",
    )(_tile_slots(dest, tc), x2d, route, ys_tiles)


def _dispatch_plan(route, counts, t):
    e_ids = route[:, :TOP_K].astype(jnp.int32)
    rank = route[:, 2 * TOP_K:3 * TOP_K].astype(jnp.int32)
    counts = counts[0, N_GROUPS:N_GROUPS + N_EXPERTS].astype(jnp.int32)
    padded = ((counts + EXPERT_BLOCK - 1) // EXPERT_BLOCK) * EXPERT_BLOCK
    pad_end = jnp.cumsum(padded)
    pad_start = pad_end - padded
    onehot = e_ids[:, :, None] == jnp.arange(N_EXPERTS, dtype=jnp.int32)
    dest = rank + jnp.sum(jnp.where(onehot, pad_start, 0), axis=-1)
    n_blocks = -(-(t * TOP_K) // EXPERT_BLOCK) + N_EXPERTS
    blk_start = jnp.arange(n_blocks, dtype=jnp.int32) * EXPERT_BLOCK
    blk_expert = jnp.minimum(jnp.sum(blk_start[:, None] >= pad_end[None, :], axis=1), N_EXPERTS - 1)
    n_used = jnp.maximum(pad_end[-1:] // EXPERT_BLOCK, 1).astype(jnp.int32)
    return blk_expert.astype(jnp.int32), n_used, dest, n_blocks * EXPERT_BLOCK


def kernel(x, rel_bias_table, norm_mix, w_in, conv_w, conv_b, q_norm, k_norm, w_gla_alpha, b_gla_alpha,
           gla_out_norm, w_merge_gate, w_branch, w_out, norm_ffn, w_router_group, b_router_group,
           w_router_expert, b_router_expert, w_expert_gate, w_expert_up, w_expert_down):
    b, s, d = x.shape
    t = b * s
    depth = w_in.shape[0]
    assert s % MOBA_BLOCK == 0 and t % EXPERT_BLOCK == 0
    assert d == SUBLANES * LANES, "row-granular DMAs store each activation row as one (8, 128) tile"
    bias0, bias1 = _attn_biases(rel_bias_table)
    head_id = jnp.arange(ATTN_DIM) // ATTN_HEAD_DIM
    hsum = (head_id[:, None] == head_id[None, :]).astype(BF16)
    c3 = 3 * CONV_DIM
    a3 = c3 + 3 * ATTN_DIM
    g3 = a3 + 2 * GLA_QK_DIM + 2 * GLA_V_DIM
    for l in range(depth):
        w_l = w_in[l].astype(BF16)
        w_gz = jnp.pad(w_l[:, g3:], ((0, 0), (0, GZ_PAD - GLA_GATE_RANK)))
        w_alpha = jnp.pad(w_gla_alpha[l].astype(BF16), ((0, GZ_PAD - GLA_GATE_RANK), (0, 0)))
        yconv, q, k, v, kmean, gq, gk, gv, gr, gz = _inproj(
            x, norm_mix[l][None], w_l[:, :c3], w_l[:, c3:a3], w_l[:, a3:g3], w_gz,
            conv_w[l], conv_b[l][None],
            jnp.tile(q_norm[l], ATTN_HEADS)[None], jnp.tile(k_norm[l], ATTN_HEADS)[None], hsum)
        qp, kp = _select(q, k, kmean.reshape(b, s // MOBA_BLOCK, ATTN_DIM))
        yattn = _attention(qp, kp, v, bias0, bias1)
        ygla = _gla(gq, gk, gv, gr, gz, w_alpha, b_gla_alpha[l][None],
                    gla_out_norm[l][None])
        w_r = jnp.concatenate([w_router_group[l], w_router_expert[l]], axis=1)
        w_r = jnp.pad(w_r, ((0, 0), (0, ROUTE_LANES - w_r.shape[1])))
        b_r = jnp.pad(jnp.concatenate([b_router_group[l], b_router_expert[l]]),
                      (0, ROUTE_LANES - N_GROUPS - N_EXPERTS))[None]
        wr_hi, wr_lo = _split_bf16(w_r)
        xo, h2, route, counts = _merge_route(
            x.reshape(t, d), yconv.reshape(t, -1), yattn.reshape(t, -1), ygla.reshape(t, -1),
            norm_mix[l][None], w_merge_gate[l].astype(BF16), w_branch[l].astype(BF16),
            w_out[l].astype(BF16), norm_ffn[l][None], wr_hi, wr_lo, b_r)
        blk_expert, n_used, dest, n_rows = _dispatch_plan(route, counts, t)
        xs = _dispatch(dest, h2, n_rows)
        ys = _experts(blk_expert, n_used, xs, w_expert_gate[l], w_expert_up[l], w_expert_down[l])
        x = _combine(dest, xo, route, ys).reshape(b, s, d)
    return x
```

```python
import functools
import math

import jax
import jax.numpy as jnp
import numpy as np
from jax import lax
from jax.experimental import pallas as pl
from jax.experimental.pallas import tpu as pltpu

CONV_DIM = 512
CONV_WIDTH = 3
ATTN_HEADS = 8
ATTN_HEAD_DIM = 64
ATTN_DIM = ATTN_HEADS * ATTN_HEAD_DIM
MOBA_BLOCK = 256
MOBA_TOPK = 3
REL_BUCKETS = 32
REL_MAX_DISTANCE = 128
GLA_HEADS = 4
GLA_KEY_DIM = 64
GLA_VALUE_DIM = 128
GLA_QK_DIM = GLA_HEADS * GLA_KEY_DIM
GLA_V_DIM = GLA_HEADS * GLA_VALUE_DIM
GLA_GATE_RANK = 16
GLA_GATE_TEMP = 16.0
GLA_CHUNK = 64
N_GROUPS = 4
EXPERTS_PER_GROUP = 8
N_EXPERTS = N_GROUPS * EXPERTS_PER_GROUP
TOP_K = 2
EXPERT_FF = 512
EXPERT_BLOCK = 256
RMS_EPS = 1e-6

LANES = 128
VMEM_LIMIT_BYTES = 56 * 1024 * 1024

MASK_VALUE = -1e30
ROUTE_LANES = 128
ROUTE_OUT = 8
ATTN_FAR_CHUNK = 4
ATTN_TAIL = 5
ROW_DMA_UNROLL = 8
GZ_PAD = LANES

F32 = jnp.float32
BF16 = jnp.bfloat16


def _cparams(n_axes):
    return pltpu.CompilerParams(
        dimension_semantics=("arbitrary",) * n_axes,
        vmem_limit_bytes=VMEM_LIMIT_BYTES,
    )


def _rms(x, gain):
    return x * lax.rsqrt(jnp.mean(x * x, axis=-1, keepdims=True) + RMS_EPS) * gain


def _split_bf16(x):
    hi = x.astype(BF16)
    lo = (x - hi.astype(F32)).astype(BF16)
    return hi, lo


def _dot(a, b):
    return jnp.dot(a, b, preferred_element_type=F32)


def _dot_nt(a, b):
    return lax.dot_general(a, b, (((1,), (1,)), ((), ())), preferred_element_type=F32)


SUBLANES = 8


def _rows_to_tiles(dst_ref, x):
    n = x.shape[0]
    for j in range(SUBLANES):
        dst_ref[pl.ds(j, n, stride=SUBLANES), :] = x[:, j * LANES:(j + 1) * LANES]


def _tiles_to_rows(src_ref, n):
    return [src_ref[pl.ds(j, n, stride=SUBLANES), :] for j in range(SUBLANES)]


def _dot_tn(a, b):
    return lax.dot_general(a, b, (((0,), (0,)), ((), ())), preferred_element_type=F32)


def _inproj_kernel(x_ref, gmix_ref, wconv_ref, wattn_ref, wgla_ref, wgz_ref, convw_ref, convb_ref,
                   qn_ref, kn_ref, hsum_ref,
                   yconv_ref, q_ref, k_ref, v_ref, kmean_ref, gq_ref, gk_ref, gv_ref, gr_ref, gz_ref,
                   carry_ref):
    s_idx = pl.program_id(1)
    ts = x_ref.shape[0]
    h = _rms(x_ref[...], gmix_ref[...]).astype(BF16)

    c = _dot(h, wconv_ref[...])
    cb = c[:, :CONV_DIM]
    u = c[:, CONV_DIM:2 * CONV_DIM] * c[:, 2 * CONV_DIM:]

    @pl.when(s_idx == 0)
    def _():
        carry_ref[...] = jnp.zeros_like(carry_ref)

    prev = carry_ref[...]
    row = lax.broadcasted_iota(jnp.int32, u.shape, 0)
    u1 = pltpu.roll(u, 1, 0)
    u1 = jnp.where(row == 0, prev[7:8, :], u1)
    u2 = pltpu.roll(u, 2, 0)
    u2 = jnp.where(row == 0, prev[6:7, :], jnp.where(row == 1, prev[7:8, :], u2))
    carry_ref[...] = u[ts - 8:, :]
    y = convb_ref[...] + convw_ref[0:1, :] * u2
    y = y + convw_ref[1:2, :] * u1
    y = y + convw_ref[2:3, :] * u
    yconv_ref[...] = (cb * y).astype(BF16)

    a = _dot(h, wattn_ref[...])
    hsum = hsum_ref[...]

    def head_norm(t, gain):
        hi, lo = _split_bf16(t * t)
        ss = _dot(hi, hsum) + _dot(lo, hsum)
        return t * lax.rsqrt(ss * (1.0 / ATTN_HEAD_DIM) + RMS_EPS) * gain

    qn = head_norm(a[:, :ATTN_DIM], qn_ref[...])
    kn = head_norm(a[:, ATTN_DIM:2 * ATTN_DIM], kn_ref[...])
    q_ref[...] = (qn * (ATTN_HEAD_DIM ** -0.5)).astype(BF16)
    k_ref[...] = kn.astype(BF16)
    v_ref[...] = jnp.transpose(a[:, 2 * ATTN_DIM:]).astype(BF16)
    kmean_ref[...] = jnp.mean(kn, axis=0, keepdims=True)

    g = _dot(h, wgla_ref[...])
    gq_ref[...] = g[:, :GLA_QK_DIM]
    gk_ref[...] = g[:, GLA_QK_DIM:2 * GLA_QK_DIM]
    gv_ref[...] = g[:, 2 * GLA_QK_DIM:2 * GLA_QK_DIM + GLA_V_DIM]
    gr_ref[...] = g[:, 2 * GLA_QK_DIM + GLA_V_DIM:]
    gz_ref[...] = _dot(h, wgz_ref[...])


def _inproj(x, gmix, wconv, wattn, wgla, wgz, convw, convb, qn, kn, hsum):
    b, s, d = x.shape
    ts = MOBA_BLOCK
    nb = s // ts
    tok = lambda w: pl.BlockSpec((None, ts, w), lambda bi, si: (bi, si, 0))
    full = lambda arr: pl.BlockSpec(arr.shape, lambda bi, si: (0,) * arr.ndim)
    out_shapes = (
        jax.ShapeDtypeStruct((b, s, CONV_DIM), BF16),
        jax.ShapeDtypeStruct((b, s, ATTN_DIM), BF16),
        jax.ShapeDtypeStruct((b, s, ATTN_DIM), BF16),
        jax.ShapeDtypeStruct((b, ATTN_DIM, s), BF16),
        jax.ShapeDtypeStruct((b, nb, 1, ATTN_DIM), F32),
        jax.ShapeDtypeStruct((b, s, GLA_QK_DIM), F32),
        jax.ShapeDtypeStruct((b, s, GLA_QK_DIM), F32),
        jax.ShapeDtypeStruct((b, s, GLA_V_DIM), F32),
        jax.ShapeDtypeStruct((b, s, GLA_V_DIM), F32),
        jax.ShapeDtypeStruct((b, s, GZ_PAD), F32),
    )
    out_specs = (
        tok(CONV_DIM), tok(ATTN_DIM), tok(ATTN_DIM),
        pl.BlockSpec((None, ATTN_DIM, ts), lambda bi, si: (bi, 0, si)),
        pl.BlockSpec((None, None, 1, ATTN_DIM), lambda bi, si: (bi, si, 0, 0)),
        tok(GLA_QK_DIM), tok(GLA_QK_DIM), tok(GLA_V_DIM), tok(GLA_V_DIM), tok(GZ_PAD),
    )
    ins = (x, gmix, wconv, wattn, wgla, wgz, convw, convb, qn, kn, hsum)
    in_specs = [tok(d)] + [full(a) for a in ins[1:]]
    return pl.pallas_call(
        _inproj_kernel,
        grid=(b, nb),
        in_specs=in_specs,
        out_specs=out_specs,
        out_shape=out_shapes,
        scratch_shapes=[pltpu.VMEM((8, CONV_DIM), F32)],
        compiler_params=_cparams(2),
        name="inproj",
    )(*ins)


def _select_kernel(q_ref, k_ref, kmean_ref, qp_ref, kp_ref):
    own = pl.program_id(1)
    tq = q_ref.shape[0]
    nb = kmean_ref.shape[0]
    half = ATTN_HEAD_DIM
    lane = lax.broadcasted_iota(jnp.int32, (tq, LANES), 1)
    blk = lax.broadcasted_iota(jnp.int32, (LANES, tq), 0).astype(F32)
    own_f = own.astype(F32)
    onehot = jnp.where(lane - half == own, 1.0, 0.0).astype(F32)
    kmean = kmean_ref[...].astype(BF16)
    for p in range(ATTN_HEADS // 2):
        qpair = q_ref[:, p * LANES:(p + 1) * LANES]
        kpair = k_ref[:, p * LANES:(p + 1) * LANES].astype(F32)
        kmpair = kmean[:, p * LANES:(p + 1) * LANES]
        qpair_f = qpair.astype(F32)
        for sub in range(2):
            h = 2 * p + sub
            lane_sel = (lane >= sub * half) & (lane < (sub + 1) * half)
            qh = jnp.where(lane_sel, qpair_f, 0.0).astype(BF16)
            gate_t = _dot_nt(kmpair, qh)
            gate_t = jnp.concatenate(
                [gate_t, jnp.full((LANES - nb, tq), -jnp.inf, F32)], axis=0)
            g = jnp.where(blk < own_f, gate_t, -jnp.inf)
            alive = jnp.where(blk < nb, 1.0, 0.0)
            sel = jnp.where(blk == own_f, 1.0, 0.0)
            for r in range(MOBA_TOPK):
                ga = jnp.where(alive > 0.0, g, -jnp.inf)
                mx = jnp.max(ga, axis=0, keepdims=True)
                cand = jnp.where((alive > 0.0) & (g == mx), blk, 2.0 * LANES)
                first = jnp.min(cand, axis=0, keepdims=True)
                hit = blk == first
                sel = jnp.where(hit, jnp.maximum(sel, jnp.where(own_f > r, 1.0, 0.0)), sel)
                alive = jnp.where(hit, 0.0, alive)
            m_t = jnp.where(sel > 0.0, 0.0, MASK_VALUE).astype(F32)
            m = jnp.transpose(m_t)
            m = pltpu.roll(m, half, 1)
            m = jnp.where((lane >= half) & (lane < half + nb), m, 0.0)
            qs = qpair_f if sub == 0 else pltpu.roll(qpair_f, half, 1)
            ks = kpair if sub == 0 else pltpu.roll(kpair, half, 1)
            qp_ref[h] = jnp.where(lane < half, qs, m).astype(BF16)
            kp_ref[h] = jnp.where(lane < half, ks, onehot).astype(BF16)


def _select(q, k, kmean):
    b, s, _ = q.shape
    tq = MOBA_BLOCK
    nb = s // tq
    assert nb <= LANES - ATTN_HEAD_DIM, "block one-hot must fit beside the head dim in one lane tile"
    tok = pl.BlockSpec((None, tq, ATTN_DIM), lambda bi, si: (bi, si, 0))
    slab = pl.BlockSpec((None, ATTN_HEADS, tq, LANES), lambda bi, si: (bi, 0, si, 0))
    shp = jax.ShapeDtypeStruct((b, ATTN_HEADS, s, LANES), BF16)
    return pl.pallas_call(
        _select_kernel,
        grid=(b, nb),
        in_specs=[tok, tok, pl.BlockSpec((None, nb, ATTN_DIM), lambda bi, si: (bi, 0, 0))],
        out_specs=(slab, slab),
        out_shape=(shp, shp),
        compiler_params=_cparams(2),
        name="moba_select",
    )(q, k, kmean)


def _attn_kernel(qp_ref, kp_ref, vt_ref, bias0_ref, bias1_ref, o_ref, m_ref, l_ref, acc_ref,
                 sa_ref, sb_ref, st_ref):
    own = pl.program_id(2)
    tq = qp_ref.shape[1]
    far_rows = ATTN_FAR_CHUNK * tq
    sub = lax.broadcasted_iota(jnp.int32, (LANES, tq), 0)
    n_far = jnp.maximum(own - 1, 0) // ATTN_FAR_CHUNK

    def scores(hh, start, n):
        return _dot_nt(kp_ref[hh, pl.ds(start, n * tq), :], qp_ref[hh])

    def far_scores(dst_ref, chunk):
        start = pl.multiple_of(chunk * far_rows, far_rows)
        for hh in range(2):
            dst_ref[hh] = scores(hh, start, ATTN_FAR_CHUNK)

    def softmax_pv(hh, s_ref, start, n):
        blocks = [s_ref[hh, w * tq:(w + 1) * tq, :] for w in range(n)]
        m_prev = m_ref[hh]
        m_new = jnp.maximum(m_prev, jnp.max(functools.reduce(jnp.maximum, blocks), axis=0, keepdims=True))
        alpha = jnp.exp(m_prev - m_new)
        ps = [jnp.exp(blk - m_new) for blk in blocks]
        l_ref[hh] = alpha * l_ref[hh] + jnp.sum(functools.reduce(jnp.add, ps), axis=0, keepdims=True)
        pt = jnp.concatenate([x.astype(BF16) for x in ps], axis=0)
        acc_ref[hh] = alpha * acc_ref[hh] + _dot(vt_ref[:, pl.ds(start, n * tq)], pt)
        m_ref[hh] = m_new

    for hh in range(2):
        m_ref[hh] = jnp.full((1, tq), MASK_VALUE, F32)
        l_ref[hh] = jnp.zeros((1, tq), F32)
        acc_ref[hh] = jnp.zeros((LANES, tq), F32)

    for own_small in range(ATTN_TAIL - 1):
        @pl.when(own == own_small)
        def _():
            for hh in range(2):
                s = scores(hh, 0, own_small + 1)
                for w in range(own_small + 1):
                    sw = s[w * tq:(w + 1) * tq, :]
                    if w == own_small:
                        sw = sw + bias0_ref[hh]
                    elif w == own_small - 1:
                        sw = sw + bias1_ref[hh]
                    st_ref[hh, w * tq:(w + 1) * tq, :] = sw

    @pl.when(own >= ATTN_TAIL - 1)
    def _():
        first = own - (ATTN_TAIL - 1)
        done = n_far * ATTN_FAR_CHUNK
        for hh in range(2):
            s = scores(hh, pl.multiple_of(first * tq, tq), ATTN_TAIL)
            for w in range(ATTN_TAIL):
                sw = s[w * tq:(w + 1) * tq, :]
                if w == ATTN_TAIL - 1:
                    sw = sw + bias0_ref[hh]
                elif w == ATTN_TAIL - 2:
                    sw = sw + bias1_ref[hh]
                else:
                    sw = sw + jnp.where(first + w < done, MASK_VALUE, 0.0).astype(F32)
                st_ref[hh, w * tq:(w + 1) * tq, :] = sw

    @pl.when(n_far > 0)
    def _():
        far_scores(sa_ref, 0)

    def far_pair(j, carry):
        c0 = 2 * j
        far_scores(sb_ref, jnp.minimum(c0 + 1, n_far - 1))
        for hh in range(2):
            softmax_pv(hh, sa_ref, pl.multiple_of(c0 * far_rows, far_rows), ATTN_FAR_CHUNK)

        @pl.when(c0 + 1 < n_far)
        def _():
            far_scores(sa_ref, jnp.minimum(c0 + 2, n_far - 1))
            for hh in range(2):
                softmax_pv(hh, sb_ref, pl.multiple_of((c0 + 1) * far_rows, far_rows), ATTN_FAR_CHUNK)

        return carry

    lax.fori_loop(0, (n_far + 1) // 2, far_pair, 0)

    for own_small in range(ATTN_TAIL - 1):
        @pl.when(own == own_small)
        def _():
            for hh in range(2):
                softmax_pv(hh, st_ref, 0, own_small + 1)

    @pl.when(own >= ATTN_TAIL - 1)
    def _():
        start = pl.multiple_of((own - (ATTN_TAIL - 1)) * tq, tq)
        for hh in range(2):
            softmax_pv(hh, st_ref, start, ATTN_TAIL)

    outs = [acc_ref[hh] / l_ref[hh] for hh in range(2)]
    o_t = jnp.where(sub < ATTN_HEAD_DIM, outs[0], outs[1])
    o_ref[...] = jnp.transpose(o_t).astype(BF16)


def _attention(qp, kp, vt, bias0, bias1):
    b, nh, s, _ = qp.shape
    tq = MOBA_BLOCK
    nq = s // tq
    return pl.pallas_call(
        _attn_kernel,
        grid=(b, nh // 2, nq),
        in_specs=[
            pl.BlockSpec((None, 2, tq, LANES), lambda bi, pi, qi: (bi, pi, qi, 0)),
            pl.BlockSpec((None, 2, s, LANES), lambda bi, pi, qi: (bi, pi, 0, 0)),
            pl.BlockSpec((None, LANES, s), lambda bi, pi, qi: (bi, pi, 0)),
            pl.BlockSpec((2, tq, tq), lambda bi, pi, qi: (pi, 0, 0)),
            pl.BlockSpec((2, tq, tq), lambda bi, pi, qi: (pi, 0, 0)),
        ],
        out_specs=pl.BlockSpec((None, tq, LANES), lambda bi, pi, qi: (bi, qi, pi)),
        out_shape=jax.ShapeDtypeStruct((b, s, ATTN_DIM), BF16),
        scratch_shapes=[
            pltpu.VMEM((2, 1, tq), F32),
            pltpu.VMEM((2, 1, tq), F32),
            pltpu.VMEM((2, LANES, tq), F32),
            pltpu.VMEM((2, ATTN_FAR_CHUNK * tq, tq), F32),
            pltpu.VMEM((2, ATTN_FAR_CHUNK * tq, tq), F32),
            pltpu.VMEM((2, ATTN_TAIL * tq, tq), F32),
        ],
        compiler_params=_cparams(3),
        name="moba_attention",
    )(qp, kp, vt, bias0, bias1)


def _t5_bucket(rel):
    n = jnp.maximum(rel, 0)
    max_exact = REL_BUCKETS // 2
    scaled = (jnp.log(jnp.maximum(n, max_exact).astype(F32) / max_exact)
              / math.log(REL_MAX_DISTANCE / max_exact))
    large = jnp.minimum(max_exact + (scaled * (REL_BUCKETS - max_exact)).astype(jnp.int32), REL_BUCKETS - 1)
    return jnp.where(n < max_exact, n, large)


def _attn_biases(rel_bias_table):
    table = rel_bias_table.astype(F32).T
    pos = jnp.arange(MOBA_BLOCK)
    rel0 = pos[None, :] - pos[:, None]
    far = table[:, REL_BUCKETS - 1][:, None, None]

    def lookup(bucket):
        hit = bucket[None, :, :, None] == jnp.arange(REL_BUCKETS)
        return jnp.sum(jnp.where(hit, table[:, None, None, :], 0.0), axis=-1)

    bias0 = jnp.where(rel0[None] >= 0, lookup(_t5_bucket(rel0)) - far, MASK_VALUE)
    bias1 = lookup(_t5_bucket(rel0 + MOBA_BLOCK)) - far
    return bias0, bias1


def _gla_kernel(gq_ref, gk_ref, gv_ref, gr_ref, gz_ref, wa_ref, ba_ref, on_ref, y_ref, state_ref):
    s_idx = pl.program_id(1)
    tg = gq_ref.shape[0]
    dk, dv, ck = GLA_KEY_DIM, GLA_VALUE_DIM, GLA_CHUNK

    @pl.when(s_idx == 0)
    def _():
        state_ref[...] = jnp.zeros_like(state_ref)

    z = gz_ref[...].astype(BF16)
    log_a = jax.nn.log_sigmoid(_dot(z, wa_ref[...]) + ba_ref[...]) / GLA_GATE_TEMP
    row = lax.broadcasted_iota(jnp.int32, log_a.shape, 0) % ck
    bcum = log_a
    shift = 1
    while shift < ck:
        bcum = bcum + jnp.where(row >= shift, pltpu.roll(bcum, shift, 0), 0.0)
        shift *= 2
    q = gq_ref[...] * (dk ** -0.5)
    k = gk_ref[...]
    tri = (lax.broadcasted_iota(jnp.int32, (ck, ck), 0) >= lax.broadcasted_iota(jnp.int32, (ck, ck), 1))
    for c in range(tg // ck):
        rows = slice(c * ck, (c + 1) * ck)
        for h in range(GLA_HEADS):
            kcols = slice(h * dk, (h + 1) * dk)
            vcols = slice(h * dv, (h + 1) * dv)
            bc = bcum[rows, kcols]
            btot = bc[ck - 1:ck, :]
            qe = (q[rows, kcols] * jnp.exp(bc)).astype(BF16)
            ke = (k[rows, kcols] * jnp.exp(-bc)).astype(BF16)
            kd = (k[rows, kcols] * jnp.exp(btot - bc)).astype(BF16)
            vc = gv_ref[rows, vcols].astype(BF16)
            att = jnp.where(tri, _dot_nt(qe, ke), 0.0).astype(BF16)
            st = state_ref[h]
            o = _dot(att, vc) + _dot_nt(qe, st.astype(BF16))
            state_ref[h] = st * jnp.exp(btot) + _dot_tn(vc, kd)
            o = _rms(o, on_ref[...])
            r = gr_ref[rows, vcols]
            y_ref[rows, vcols] = (o * (r * jax.nn.sigmoid(r))).astype(BF16)


def _gla(gq, gk, gv, gr, gz, wa, ba, on):
    b, s, _ = gq.shape
    tg = 256
    tok = lambda w: pl.BlockSpec((None, tg, w), lambda bi, si: (bi, si, 0))
    full = lambda arr: pl.BlockSpec(arr.shape, lambda bi, si: (0,) * arr.ndim)
    return pl.pallas_call(
        _gla_kernel,
        grid=(b, s // tg),
        in_specs=[tok(GLA_QK_DIM), tok(GLA_QK_DIM), tok(GLA_V_DIM), tok(GLA_V_DIM), tok(GZ_PAD),
                  full(wa), full(ba), full(on)],
        out_specs=tok(GLA_V_DIM),
        out_shape=jax.ShapeDtypeStruct((b, s, GLA_V_DIM), BF16),
        scratch_shapes=[pltpu.VMEM((GLA_HEADS, GLA_VALUE_DIM, GLA_KEY_DIM), F32)],
        compiler_params=_cparams(2),
        name="gla",
    )(gq, gk, gv, gr, gz, wa, ba, on)


def _merge_route_kernel(x_ref, yc_ref, ya_ref, yg_ref, gmix_ref, wg_ref, wb_ref, wo_ref, gffn_ref,
                        wr_hi_ref, wr_lo_ref, br_ref, xo_ref, h2_ref, route_ref, counts_ref, run_ref):
    x = x_ref[...]
    h = _rms(x, gmix_ref[...]).astype(BF16)
    merged = None
    for n, y_ref in enumerate((yc_ref, ya_ref, yg_ref)):
        term = jax.nn.sigmoid(_dot(h, wg_ref[n])) * _dot(y_ref[...], wb_ref[n])
        merged = term if merged is None else merged + term
    xo = x + _dot(merged.astype(BF16), wo_ref[...])
    xo_ref[...] = xo
    h2 = _rms(xo, gffn_ref[...])
    _rows_to_tiles(h2_ref, h2)

    h_hi, h_lo = _split_bf16(h2)
    logits = (_dot(h_hi, wr_hi_ref[...]) + _dot(h_lo, wr_hi_ref[...]) + _dot(h_hi, wr_lo_ref[...])
              + br_ref[...])
    lane = lax.broadcasted_iota(jnp.int32, logits.shape, 1).astype(F32)
    big = 4.0 * ROUTE_LANES
    lg = jnp.where(lane < N_GROUPS, logits, -jnp.inf)
    gmax = jnp.max(lg, axis=1, keepdims=True)
    gidx = jnp.min(jnp.where(lg == gmax, lane, big), axis=1, keepdims=True)
    p_group_top = 1.0 / jnp.sum(jnp.exp(lg - gmax), axis=1, keepdims=True)
    lo_lane = N_GROUPS + gidx * EXPERTS_PER_GROUP
    le = jnp.where((lane >= lo_lane) & (lane < lo_lane + EXPERTS_PER_GROUP), logits, -jnp.inf)
    emax = jnp.max(le, axis=1, keepdims=True)
    i1 = jnp.min(jnp.where(le == emax, lane, big), axis=1, keepdims=True)
    esum = jnp.sum(jnp.exp(le - emax), axis=1, keepdims=True)
    le2 = jnp.where(lane == i1, -jnp.inf, le)
    emax2 = jnp.max(le2, axis=1, keepdims=True)
    i2 = jnp.min(jnp.where(le2 == emax2, lane, big), axis=1, keepdims=True)
    p1 = 1.0 / esum
    p2 = jnp.exp(emax2 - emax) / esum
    psum = p1 + p2
    w1 = p_group_top * p1 / psum
    w2 = p_group_top * p2 / psum
    e1 = i1 - N_GROUPS
    e2 = i2 - N_GROUPS

    @pl.when(pl.program_id(0) == 0)
    def _():
        run_ref[...] = jnp.zeros_like(run_ref)

    tm = x.shape[0]
    oh1 = jnp.where(lane == i1, 1.0, 0.0)
    oh2 = jnp.where(lane == i2, 1.0, 0.0)
    ohs = oh1 + oh2
    lower = (lax.broadcasted_iota(jnp.int32, (tm, tm), 0) > lax.broadcasted_iota(jnp.int32, (tm, tm), 1))
    before = _dot(jnp.where(lower, 1.0, 0.0).astype(BF16), ohs.astype(BF16)) + run_ref[...]
    rank1 = jnp.sum(oh1 * before, axis=1, keepdims=True)
    rank2 = jnp.sum(oh2 * before, axis=1, keepdims=True)
    run_ref[...] = run_ref[...] + jnp.sum(ohs, axis=0, keepdims=True)
    counts_ref[...] = run_ref[...]

    rl = lax.broadcasted_iota(jnp.int32, (tm, ROUTE_OUT), 1)
    rec = jnp.zeros((tm, ROUTE_OUT), F32)
    for slot, val in enumerate((e1, e2, w1, w2, rank1, rank2)):
        rec = jnp.where(rl == slot, val, rec)
    route_ref[...] = rec


def _merge_route(x2d, yc, ya, yg, gmix, wg, wb, wo, gffn, wr_hi, wr_lo, br):
    t, d = x2d.shape
    tm = 256
    tok = lambda w: pl.BlockSpec((tm, w), lambda i: (i, 0))
    full = lambda arr: pl.BlockSpec(arr.shape, lambda i: (0,) * arr.ndim)
    ins = (x2d, yc, ya, yg, gmix, wg, wb, wo, gffn, wr_hi, wr_lo, br)
    return pl.pallas_call(
        _merge_route_kernel,
        grid=(t // tm,),
        in_specs=[tok(d), tok(CONV_DIM), tok(ATTN_DIM), tok(GLA_V_DIM)] + [full(a) for a in ins[4:]],
        out_specs=(tok(d), pl.BlockSpec((tm * SUBLANES, LANES), lambda i: (i, 0)), tok(ROUTE_OUT),
                   pl.BlockSpec((1, ROUTE_LANES), lambda i: (0, 0))),
        out_shape=(jax.ShapeDtypeStruct((t, d), F32), jax.ShapeDtypeStruct((t * SUBLANES, LANES), F32),
                   jax.ShapeDtypeStruct((t, ROUTE_OUT), F32), jax.ShapeDtypeStruct((1, ROUTE_LANES), F32)),
        scratch_shapes=[pltpu.VMEM((1, ROUTE_LANES), F32)],
        compiler_params=_cparams(1),
        name="merge_route",
    )(*ins)


def _tile_slots(dest, tc):
    nt = dest.shape[0] // tc
    return dest.reshape(nt, tc, TOP_K).transpose(0, 2, 1).reshape(nt, 1, TOP_K * tc)


def _dispatch_kernel(dest_ref, h2_ref, xs_init_hbm, xs_hbm, sem):
    del xs_init_hbm
    tc = h2_ref.shape[0] // SUBLANES

    def row_copy(r, kk):
        src = pl.multiple_of(r * SUBLANES, SUBLANES)
        dst = pl.multiple_of(dest_ref[0, kk * tc + r] * SUBLANES, SUBLANES)
        return pltpu.make_async_copy(h2_ref.at[pl.ds(src, SUBLANES)], xs_hbm.at[pl.ds(dst, SUBLANES)], sem)

    def issue(r, carry):
        row_copy(r, 0).start(priority=0)
        row_copy(r, 1).start(priority=1)
        return carry

    lax.fori_loop(0, tc, issue, 0, unroll=ROW_DMA_UNROLL)
    for _ in range(TOP_K):
        pltpu.make_async_copy(h2_ref, xs_hbm.at[pl.ds(0, tc * SUBLANES)], sem).wait()


def _dispatch(dest, h2_tiles, n_rows):
    t = h2_tiles.shape[0] // SUBLANES
    tc = 256
    return pl.pallas_call(
        _dispatch_kernel,
        grid=(t // tc,),
        in_specs=[
            pl.BlockSpec((None, 1, TOP_K * tc), lambda i: (i, 0, 0), memory_space=pltpu.SMEM),
            pl.BlockSpec((tc * SUBLANES, LANES), lambda i: (i, 0)),
            pl.BlockSpec(memory_space=pl.ANY),
        ],
        out_specs=pl.BlockSpec(memory_space=pl.ANY),
        out_shape=jax.ShapeDtypeStruct((n_rows * SUBLANES, LANES), F32),
        scratch_shapes=[pltpu.SemaphoreType.DMA(())],
        input_output_aliases={2: 0},
        compiler_params=_cparams(1),
        name="moe_dispatch",
    )(_tile_slots(dest, tc), h2_tiles, jnp.zeros((n_rows * SUBLANES, LANES), F32))


def _expert_kernel(blk_expert_ref, n_used_ref, xs_ref, wg_ref, wu_ref, wd_ref, ys_ref,
                   wg_bf, wu_bf, wd_bf):
    i = pl.program_id(0)
    rb = xs_ref.shape[0] // SUBLANES
    new_expert = (i == 0) | (blk_expert_ref[i] != blk_expert_ref[jnp.maximum(i - 1, 0)])

    @pl.when(new_expert)
    def _():
        wg_bf[...] = wg_ref[...].astype(BF16)
        wu_bf[...] = wu_ref[...].astype(BF16)
        wd_bf[...] = wd_ref[...].astype(BF16)

    @pl.when(i < n_used_ref[0])
    def _():
        xb = jnp.concatenate(_tiles_to_rows(xs_ref, rb), axis=1).astype(BF16)
        gate = _dot(xb, wg_bf[...])
        up = _dot(xb, wu_bf[...])
        act = (gate * jax.nn.sigmoid(gate) * up).astype(BF16)
        _rows_to_tiles(ys_ref, _dot(act, wd_bf[...]))

    @pl.when(i >= n_used_ref[0])
    def _():
        ys_ref[...] = jnp.zeros_like(ys_ref)


def _experts(blk_expert, n_used, xs_tiles, wg, wu, wd, layer):
    d = SUBLANES * LANES
    n_blocks = blk_expert.shape[0]
    rb = EXPERT_BLOCK
    row_block = (rb * SUBLANES, LANES)
    grid_spec = pltpu.PrefetchScalarGridSpec(
        num_scalar_prefetch=2,
        grid=(n_blocks,),
        in_specs=[
            pl.BlockSpec(row_block, lambda i, be, nu: (jnp.minimum(i, nu[0] - 1), 0)),
            pl.BlockSpec((None, None, d, EXPERT_FF), lambda i, be, nu: (layer, be[i], 0, 0)),
            pl.BlockSpec((None, None, d, EXPERT_FF), lambda i, be, nu: (layer, be[i], 0, 0)),
            pl.BlockSpec((None, None, EXPERT_FF, d), lambda i, be, nu: (layer, be[i], 0, 0)),
        ],
        out_specs=pl.BlockSpec(row_block, lambda i, be, nu: (i, 0)),
        scratch_shapes=[pltpu.VMEM((d, EXPERT_FF), BF16), pltpu.VMEM((d, EXPERT_FF), BF16),
                        pltpu.VMEM((EXPERT_FF, d), BF16)],
    )
    return pl.pallas_call(
        _expert_kernel,
        grid_spec=grid_spec,
        out_shape=jax.ShapeDtypeStruct(xs_tiles.shape, F32),
        compiler_params=_cparams(1),
        name="moe_experts",
    )(blk_expert, n_used, xs_tiles, wg, wu, wd)


def _combine_kernel(dest_ref, x_ref, route_ref, ys_hbm, o_ref, buf, sem):
    tc = x_ref.shape[0]

    def row_copy(r, kk):
        src = pl.multiple_of(dest_ref[0, kk * tc + r] * SUBLANES, SUBLANES)
        dst = pl.multiple_of(r * SUBLANES, SUBLANES)
        return pltpu.make_async_copy(ys_hbm.at[pl.ds(src, SUBLANES)], buf.at[kk, pl.ds(dst, SUBLANES)], sem)

    def issue(r, carry):
        row_copy(r, 0).start(priority=0)
        row_copy(r, 1).start(priority=1)
        return carry

    lax.fori_loop(0, tc, issue, 0, unroll=ROW_DMA_UNROLL)
    for kk in range(TOP_K):
        pltpu.make_async_copy(ys_hbm.at[pl.ds(0, tc * SUBLANES)], buf.at[kk], sem).wait()
    route = route_ref[...]
    w1 = route[:, TOP_K:TOP_K + 1]
    w2 = route[:, TOP_K + 1:TOP_K + 2]
    y1 = _tiles_to_rows(buf.at[0], tc)
    y2 = _tiles_to_rows(buf.at[1], tc)
    for j in range(SUBLANES):
        cols = slice(j * LANES, (j + 1) * LANES)
        o_ref[:, cols] = x_ref[:, cols] + (y1[j] * w1 + y2[j] * w2)


def _combine(dest, x2d, route, ys_tiles):
    t, d = x2d.shape
    tc = 256
    return pl.pallas_call(
        _combine_kernel,
        grid=(t // tc,),
        in_specs=[
            pl.BlockSpec((None, 1, TOP_K * tc), lambda i: (i, 0, 0), memory_space=pltpu.SMEM),
            pl.BlockSpec((tc, d), lambda i: (i, 0)),
            pl.BlockSpec((tc, ROUTE_OUT), lambda i: (i, 0)),
            pl.BlockSpec(memory_space=pl.ANY),
        ],
        out_specs=pl.BlockSpec((tc, d), lambda i: (i, 0)),
        out_shape=jax.ShapeDtypeStruct((t, d), F32),
        scratch_shapes=[pltpu.VMEM((TOP_K, tc * SUBLANES, LANES), F32), pltpu.SemaphoreType.DMA(())],
        compiler_params=_cparams(1),
        name="moe_combine",
    )(_tile_slots(dest, tc), x2d, route, ys_tiles)


def _dispatch_plan(route, counts, t):
    e_ids = route[:, :TOP_K].astype(jnp.int32)
    rank = route[:, 2 * TOP_K:3 * TOP_K].astype(jnp.int32)
    counts = counts[0, N_GROUPS:N_GROUPS + N_EXPERTS].astype(jnp.int32)
    padded = ((counts + EXPERT_BLOCK - 1) // EXPERT_BLOCK) * EXPERT_BLOCK
    pad_end = jnp.cumsum(padded)
    pad_start = pad_end - padded
    onehot = e_ids[:, :, None] == jnp.arange(N_EXPERTS, dtype=jnp.int32)
    dest = rank + jnp.sum(jnp.where(onehot, pad_start, 0), axis=-1)
    n_blocks = -(-(t * TOP_K) // EXPERT_BLOCK) + N_EXPERTS
    blk_start = jnp.arange(n_blocks, dtype=jnp.int32) * EXPERT_BLOCK
    blk_expert = jnp.minimum(jnp.sum(blk_start[:, None] >= pad_end[None, :], axis=1), N_EXPERTS - 1)
    n_used = jnp.maximum(pad_end[-1:] // EXPERT_BLOCK, 1).astype(jnp.int32)
    return blk_expert.astype(jnp.int32), n_used, dest, n_blocks * EXPERT_BLOCK


def kernel(x, rel_bias_table, norm_mix, w_in, conv_w, conv_b, q_norm, k_norm, w_gla_alpha, b_gla_alpha,
           gla_out_norm, w_merge_gate, w_branch, w_out, norm_ffn, w_router_group, b_router_group,
           w_router_expert, b_router_expert, w_expert_gate, w_expert_up, w_expert_down):
    b, s, d = x.shape
    t = b * s
    depth = w_in.shape[0]
    assert s % MOBA_BLOCK == 0 and t % EXPERT_BLOCK == 0
    assert d == SUBLANES * LANES, "row-granular DMAs store each activation row as one (8, 128) tile"
    bias0, bias1 = _attn_biases(rel_bias_table)
    head_id = jnp.arange(ATTN_DIM) // ATTN_HEAD_DIM
    hsum = (head_id[:, None] == head_id[None, :]).astype(BF16)
    c3 = 3 * CONV_DIM
    a3 = c3 + 3 * ATTN_DIM
    g3 = a3 + 2 * GLA_QK_DIM + 2 * GLA_V_DIM
    for l in range(depth):
        w_l = w_in[l].astype(BF16)
        w_gz = jnp.pad(w_l[:, g3:], ((0, 0), (0, GZ_PAD - GLA_GATE_RANK)))
        w_alpha = jnp.pad(w_gla_alpha[l].astype(BF16), ((0, GZ_PAD - GLA_GATE_RANK), (0, 0)))
        yconv, q, k, v, kmean, gq, gk, gv, gr, gz = _inproj(
            x, norm_mix[l][None], w_l[:, :c3], w_l[:, c3:a3], w_l[:, a3:g3], w_gz,
            conv_w[l], conv_b[l][None],
            jnp.tile(q_norm[l], ATTN_HEADS)[None], jnp.tile(k_norm[l], ATTN_HEADS)[None], hsum)
        qp, kp = _select(q, k, kmean.reshape(b, s // MOBA_BLOCK, ATTN_DIM))
        yattn = _attention(qp, kp, v, bias0, bias1)
        ygla = _gla(gq, gk, gv, gr, gz, w_alpha, b_gla_alpha[l][None],
                    gla_out_norm[l][None])
        w_r = jnp.concatenate([w_router_group[l], w_router_expert[l]], axis=1)
        w_r = jnp.pad(w_r, ((0, 0), (0, ROUTE_LANES - w_r.shape[1])))
        b_r = jnp.pad(jnp.concatenate([b_router_group[l], b_router_expert[l]]),
                      (0, ROUTE_LANES - N_GROUPS - N_EXPERTS))[None]
        wr_hi, wr_lo = _split_bf16(w_r)
        xo, h2, route, counts = _merge_route(
            x.reshape(t, d), yconv.reshape(t, -1), yattn.reshape(t, -1), ygla.reshape(t, -1),
            norm_mix[l][None], w_merge_gate[l].astype(BF16), w_branch[l].astype(BF16),
            w_out[l].astype(BF16), norm_ffn[l][None], wr_hi, wr_lo, b_r)
        blk_expert, n_used, dest, n_rows = _dispatch_plan(route, counts, t)
        xs = _dispatch(dest, h2, n_rows)
        ys = _experts(blk_expert, n_used, xs, w_expert_gate, w_expert_up, w_expert_down, l)
        x = _combine(dest, xo, route, ys).reshape(b, s, d)
    return x
```

```python
import functools
import math

import jax
import jax.numpy as jnp
import numpy as np
from jax import lax
from jax.experimental import pallas as pl
from jax.experimental.pallas import tpu as pltpu

CONV_DIM = 512
CONV_WIDTH = 3
ATTN_HEADS = 8
ATTN_HEAD_DIM = 64
ATTN_DIM = ATTN_HEADS * ATTN_HEAD_DIM
MOBA_BLOCK = 256
MOBA_TOPK = 3
REL_BUCKETS = 32
REL_MAX_DISTANCE = 128
GLA_HEADS = 4
GLA_KEY_DIM = 64
GLA_VALUE_DIM = 128
GLA_QK_DIM = GLA_HEADS * GLA_KEY_DIM
GLA_V_DIM = GLA_HEADS * GLA_VALUE_DIM
GLA_GATE_RANK = 16
GLA_GATE_TEMP = 16.0
GLA_CHUNK = 64
N_GROUPS = 4
EXPERTS_PER_GROUP = 8
N_EXPERTS = N_GROUPS * EXPERTS_PER_GROUP
TOP_K = 2
EXPERT_FF = 512
EXPERT_BLOCK = 256
RMS_EPS = 1e-6

LANES = 128
VMEM_LIMIT_BYTES = 56 * 1024 * 1024

MASK_VALUE = -1e30
ROUTE_LANES = 128
ROUTE_OUT = 8
ATTN_FAR_CHUNK = 4
ATTN_TAIL = 5
ROW_DMA_UNROLL = 8
GZ_PAD = LANES

F32 = jnp.float32
BF16 = jnp.bfloat16


def _cparams(n_axes):
    return pltpu.CompilerParams(
        dimension_semantics=("arbitrary",) * n_axes,
        vmem_limit_bytes=VMEM_LIMIT_BYTES,
    )


def _rms(x, gain):
    return x * lax.rsqrt(jnp.mean(x * x, axis=-1, keepdims=True) + RMS_EPS) * gain


def _split_bf16(x):
    hi = x.astype(BF16)
    lo = (x - hi.astype(F32)).astype(BF16)
    return hi, lo


def _dot(a, b):
    return jnp.dot(a, b, preferred_element_type=F32)


def _dot_nt(a, b):
    return lax.dot_general(a, b, (((1,), (1,)), ((), ())), preferred_element_type=F32)


SUBLANES = 8


def _rows_to_tiles(dst_ref, x):
    n = x.shape[0]
    for j in range(SUBLANES):
        dst_ref[pl.ds(j, n, stride=SUBLANES), :] = x[:, j * LANES:(j + 1) * LANES]


def _tiles_to_rows(src_ref, n):
    return [src_ref[pl.ds(j, n, stride=SUBLANES), :] for j in range(SUBLANES)]


def _dot_tn(a, b):
    return lax.dot_general(a, b, (((0,), (0,)), ((), ())), preferred_element_type=F32)


def _inproj_kernel(x_ref, gmix_ref, wconv_ref, wattn_ref, wgla_ref, wgz_ref, convw_ref, convb_ref,
                   qn_ref, kn_ref, hsum_ref,
                   yconv_ref, q_ref, k_ref, v_ref, kmean_ref, gq_ref, gk_ref, gv_ref, gr_ref, gz_ref,
                   carry_ref):
    s_idx = pl.program_id(1)
    ts = x_ref.shape[0]
    h = _rms(x_ref[...], gmix_ref[...]).astype(BF16)

    c = _dot(h, wconv_ref[...])
    cb = c[:, :CONV_DIM]
    u = c[:, CONV_DIM:2 * CONV_DIM] * c[:, 2 * CONV_DIM:]

    @pl.when(s_idx == 0)
    def _():
        carry_ref[...] = jnp.zeros_like(carry_ref)

    prev = carry_ref[...]
    row = lax.broadcasted_iota(jnp.int32, u.shape, 0)
    u1 = pltpu.roll(u, 1, 0)
    u1 = jnp.where(row == 0, prev[7:8, :], u1)
    u2 = pltpu.roll(u, 2, 0)
    u2 = jnp.where(row == 0, prev[6:7, :], jnp.where(row == 1, prev[7:8, :], u2))
    carry_ref[...] = u[ts - 8:, :]
    y = convb_ref[...] + convw_ref[0:1, :] * u2
    y = y + convw_ref[1:2, :] * u1
    y = y + convw_ref[2:3, :] * u
    yconv_ref[...] = (cb * y).astype(BF16)

    a = _dot(h, wattn_ref[...])
    hsum = hsum_ref[...]

    def head_norm(t, gain):
        hi, lo = _split_bf16(t * t)
        ss = _dot(hi, hsum) + _dot(lo, hsum)
        return t * lax.rsqrt(ss * (1.0 / ATTN_HEAD_DIM) + RMS_EPS) * gain

    qn = head_norm(a[:, :ATTN_DIM], qn_ref[...])
    kn = head_norm(a[:, ATTN_DIM:2 * ATTN_DIM], kn_ref[...])
    q_ref[...] = (qn * (ATTN_HEAD_DIM ** -0.5)).astype(BF16)
    k_ref[...] = kn.astype(BF16)
    v_ref[...] = jnp.transpose(a[:, 2 * ATTN_DIM:]).astype(BF16)
    kmean_ref[...] = jnp.mean(kn, axis=0, keepdims=True)

    g = _dot(h, wgla_ref[...])
    gq_ref[...] = g[:, :GLA_QK_DIM]
    gk_ref[...] = g[:, GLA_QK_DIM:2 * GLA_QK_DIM]
    gv_ref[...] = g[:, 2 * GLA_QK_DIM:2 * GLA_QK_DIM + GLA_V_DIM]
    gr_ref[...] = g[:, 2 * GLA_QK_DIM + GLA_V_DIM:]
    gz_ref[...] = _dot(h, wgz_ref[...])


def _inproj(x, gmix, wconv, wattn, wgla, wgz, convw, convb, qn, kn, hsum):
    b, s, d = x.shape
    ts = MOBA_BLOCK
    nb = s // ts
    tok = lambda w: pl.BlockSpec((None, ts, w), lambda bi, si: (bi, si, 0))
    full = lambda arr: pl.BlockSpec(arr.shape, lambda bi, si: (0,) * arr.ndim)
    out_shapes = (
        jax.ShapeDtypeStruct((b, s, CONV_DIM), BF16),
        jax.ShapeDtypeStruct((b, s, ATTN_DIM), BF16),
        jax.ShapeDtypeStruct((b, s, ATTN_DIM), BF16),
        jax.ShapeDtypeStruct((b, ATTN_DIM, s), BF16),
        jax.ShapeDtypeStruct((b, nb, 1, ATTN_DIM), F32),
        jax.ShapeDtypeStruct((b, s, GLA_QK_DIM), F32),
        jax.ShapeDtypeStruct((b, s, GLA_QK_DIM), F32),
        jax.ShapeDtypeStruct((b, s, GLA_V_DIM), F32),
        jax.ShapeDtypeStruct((b, s, GLA_V_DIM), F32),
        jax.ShapeDtypeStruct((b, s, GZ_PAD), F32),
    )
    out_specs = (
        tok(CONV_DIM), tok(ATTN_DIM), tok(ATTN_DIM),
        pl.BlockSpec((None, ATTN_DIM, ts), lambda bi, si: (bi, 0, si)),
        pl.BlockSpec((None, None, 1, ATTN_DIM), lambda bi, si: (bi, si, 0, 0)),
        tok(GLA_QK_DIM), tok(GLA_QK_DIM), tok(GLA_V_DIM), tok(GLA_V_DIM), tok(GZ_PAD),
    )
    ins = (x, gmix, wconv, wattn, wgla, wgz, convw, convb, qn, kn, hsum)
    in_specs = [tok(d)] + [full(a) for a in ins[1:]]
    return pl.pallas_call(
        _inproj_kernel,
        grid=(b, nb),
        in_specs=in_specs,
        out_specs=out_specs,
        out_shape=out_shapes,
        scratch_shapes=[pltpu.VMEM((8, CONV_DIM), F32)],
        compiler_params=_cparams(2),
        name="inproj",
    )(*ins)


def _select_kernel(q_ref, k_ref, kmean_ref, qp_ref, kp_ref):
    own = pl.program_id(1)
    tq = q_ref.shape[0]
    nb = kmean_ref.shape[0]
    half = ATTN_HEAD_DIM
    lane = lax.broadcasted_iota(jnp.int32, (tq, LANES), 1)
    blk = lax.broadcasted_iota(jnp.int32, (LANES, tq), 0).astype(F32)
    own_f = own.astype(F32)
    onehot = jnp.where(lane - half == own, 1.0, 0.0).astype(F32)
    kmean = kmean_ref[...].astype(BF16)
    for p in range(ATTN_HEADS // 2):
        qpair = q_ref[:, p * LANES:(p + 1) * LANES]
        kpair = k_ref[:, p * LANES:(p + 1) * LANES].astype(F32)
        kmpair = kmean[:, p * LANES:(p + 1) * LANES]
        qpair_f = qpair.astype(F32)
        for sub in range(2):
            h = 2 * p + sub
            lane_sel = (lane >= sub * half) & (lane < (sub + 1) * half)
            qh = jnp.where(lane_sel, qpair_f, 0.0).astype(BF16)
            gate_t = _dot_nt(kmpair, qh)
            gate_t = jnp.concatenate(
                [gate_t, jnp.full((LANES - nb, tq), -jnp.inf, F32)], axis=0)
            g = jnp.where(blk < own_f, gate_t, -jnp.inf)
            alive = jnp.where(blk < nb, 1.0, 0.0)
            sel = jnp.where(blk == own_f, 1.0, 0.0)
            for r in range(MOBA_TOPK):
                ga = jnp.where(alive > 0.0, g, -jnp.inf)
                mx = jnp.max(ga, axis=0, keepdims=True)
                cand = jnp.where((alive > 0.0) & (g == mx), blk, 2.0 * LANES)
                first = jnp.min(cand, axis=0, keepdims=True)
                hit = blk == first
                sel = jnp.where(hit, jnp.maximum(sel, jnp.where(own_f > r, 1.0, 0.0)), sel)
                alive = jnp.where(hit, 0.0, alive)
            m_t = jnp.where(sel > 0.0, 0.0, MASK_VALUE).astype(F32)
            m = jnp.transpose(m_t)
            m = pltpu.roll(m, half, 1)
            m = jnp.where((lane >= half) & (lane < half + nb), m, 0.0)
            qs = qpair_f if sub == 0 else pltpu.roll(qpair_f, half, 1)
            ks = kpair if sub == 0 else pltpu.roll(kpair, half, 1)
            qp_ref[h] = jnp.where(lane < half, qs, m).astype(BF16)
            kp_ref[h] = jnp.where(lane < half, ks, onehot).astype(BF16)


def _select(q, k, kmean):
    b, s, _ = q.shape
    tq = MOBA_BLOCK
    nb = s // tq
    assert nb <= LANES - ATTN_HEAD_DIM, "block one-hot must fit beside the head dim in one lane tile"
    tok = pl.BlockSpec((None, tq, ATTN_DIM), lambda bi, si: (bi, si, 0))
    slab = pl.BlockSpec((None, ATTN_HEADS, tq, LANES), lambda bi, si: (bi, 0, si, 0))
    shp = jax.ShapeDtypeStruct((b, ATTN_HEADS, s, LANES), BF16)
    return pl.pallas_call(
        _select_kernel,
        grid=(b, nb),
        in_specs=[tok, tok, pl.BlockSpec((None, nb, ATTN_DIM), lambda bi, si: (bi, 0, 0))],
        out_specs=(slab, slab),
        out_shape=(shp, shp),
        compiler_params=_cparams(2),
        name="moba_select",
    )(q, k, kmean)


def _attn_kernel(qp_ref, kp_ref, vt_ref, bias0_ref, bias1_ref, o_ref, m_ref, l_ref, acc_ref,
                 sa_ref, sb_ref, st_ref):
    own = pl.program_id(2)
    tq = qp_ref.shape[1]
    far_rows = ATTN_FAR_CHUNK * tq
    sub = lax.broadcasted_iota(jnp.int32, (LANES, tq), 0)
    n_far = jnp.maximum(own - 1, 0) // ATTN_FAR_CHUNK

    def scores(hh, start, n):
        return _dot_nt(kp_ref[hh, pl.ds(start, n * tq), :], qp_ref[hh])

    def far_scores(dst_ref, chunk):
        start = pl.multiple_of(chunk * far_rows, far_rows)
        for hh in range(2):
            dst_ref[hh] = scores(hh, start, ATTN_FAR_CHUNK)

    def softmax_pv(hh, s_ref, start, n):
        blocks = [s_ref[hh, w * tq:(w + 1) * tq, :] for w in range(n)]
        m_prev = m_ref[hh]
        m_new = jnp.maximum(m_prev, jnp.max(functools.reduce(jnp.maximum, blocks), axis=0, keepdims=True))
        alpha = jnp.exp(m_prev - m_new)
        ps = [jnp.exp(blk - m_new) for blk in blocks]
        l_ref[hh] = alpha * l_ref[hh] + jnp.sum(functools.reduce(jnp.add, ps), axis=0, keepdims=True)
        pt = jnp.concatenate([x.astype(BF16) for x in ps], axis=0)
        acc_ref[hh] = alpha * acc_ref[hh] + _dot(vt_ref[:, pl.ds(start, n * tq)], pt)
        m_ref[hh] = m_new

    for hh in range(2):
        m_ref[hh] = jnp.full((1, tq), MASK_VALUE, F32)
        l_ref[hh] = jnp.zeros((1, tq), F32)
        acc_ref[hh] = jnp.zeros((LANES, tq), F32)

    for own_small in range(ATTN_TAIL - 1):
        @pl.when(own == own_small)
        def _():
            for hh in range(2):
                s = scores(hh, 0, own_small + 1)
                for w in range(own_small + 1):
                    sw = s[w * tq:(w + 1) * tq, :]
                    if w == own_small:
                        sw = sw + bias0_ref[hh]
                    elif w == own_small - 1:
                        sw = sw + bias1_ref[hh]
                    st_ref[hh, w * tq:(w + 1) * tq, :] = sw

    @pl.when(own >= ATTN_TAIL - 1)
    def _():
        first = own - (ATTN_TAIL - 1)
        done = n_far * ATTN_FAR_CHUNK
        for hh in range(2):
            s = scores(hh, pl.multiple_of(first * tq, tq), ATTN_TAIL)
            for w in range(ATTN_TAIL):
                sw = s[w * tq:(w + 1) * tq, :]
                if w == ATTN_TAIL - 1:
                    sw = sw + bias0_ref[hh]
                elif w == ATTN_TAIL - 2:
                    sw = sw + bias1_ref[hh]
                else:
                    sw = sw + jnp.where(first + w < done, MASK_VALUE, 0.0).astype(F32)
                st_ref[hh, w * tq:(w + 1) * tq, :] = sw

    @pl.when(n_far > 0)
    def _():
        far_scores(sa_ref, 0)

    def far_pair(j, carry):
        c0 = 2 * j
        far_scores(sb_ref, jnp.minimum(c0 + 1, n_far - 1))
        for hh in range(2):
            softmax_pv(hh, sa_ref, pl.multiple_of(c0 * far_rows, far_rows), ATTN_FAR_CHUNK)

        @pl.when(c0 + 1 < n_far)
        def _():
            far_scores(sa_ref, jnp.minimum(c0 + 2, n_far - 1))
            for hh in range(2):
                softmax_pv(hh, sb_ref, pl.multiple_of((c0 + 1) * far_rows, far_rows), ATTN_FAR_CHUNK)

        return carry

    lax.fori_loop(0, (n_far + 1) // 2, far_pair, 0)

    for own_small in range(ATTN_TAIL - 1):
        @pl.when(own == own_small)
        def _():
            for hh in range(2):
                softmax_pv(hh, st_ref, 0, own_small + 1)

    @pl.when(own >= ATTN_TAIL - 1)
    def _():
        start = pl.multiple_of((own - (ATTN_TAIL - 1)) * tq, tq)
        for hh in range(2):
            softmax_pv(hh, st_ref, start, ATTN_TAIL)

    outs = [acc_ref[hh] / l_ref[hh] for hh in range(2)]
    o_t = jnp.where(sub < ATTN_HEAD_DIM, outs[0], outs[1])
    o_ref[...] = jnp.transpose(o_t).astype(BF16)


def _attention(qp, kp, vt, bias0, bias1):
    b, nh, s, _ = qp.shape
    tq = MOBA_BLOCK
    nq = s // tq
    return pl.pallas_call(
        _attn_kernel,
        grid=(b, nh // 2, nq),
        in_specs=[
            pl.BlockSpec((None, 2, tq, LANES), lambda bi, pi, qi: (bi, pi, qi, 0)),
            pl.BlockSpec((None, 2, s, LANES), lambda bi, pi, qi: (bi, pi, 0, 0)),
            pl.BlockSpec((None, LANES, s), lambda bi, pi, qi: (bi, pi, 0)),
            pl.BlockSpec((2, tq, tq), lambda bi, pi, qi: (pi, 0, 0)),
            pl.BlockSpec((2, tq, tq), lambda bi, pi, qi: (pi, 0, 0)),
        ],
        out_specs=pl.BlockSpec((None, tq, LANES), lambda bi, pi, qi: (bi, qi, pi)),
        out_shape=jax.ShapeDtypeStruct((b, s, ATTN_DIM), BF16),
        scratch_shapes=[
            pltpu.VMEM((2, 1, tq), F32),
            pltpu.VMEM((2, 1, tq), F32),
            pltpu.VMEM((2, LANES, tq), F32),
            pltpu.VMEM((2, ATTN_FAR_CHUNK * tq, tq), F32),
            pltpu.VMEM((2, ATTN_FAR_CHUNK * tq, tq), F32),
            pltpu.VMEM((2, ATTN_TAIL * tq, tq), F32),
        ],
        compiler_params=_cparams(3),
        name="moba_attention",
    )(qp, kp, vt, bias0, bias1)


def _t5_bucket(rel):
    n = jnp.maximum(rel, 0)
    max_exact = REL_BUCKETS // 2
    scaled = (jnp.log(jnp.maximum(n, max_exact).astype(F32) / max_exact)
              / math.log(REL_MAX_DISTANCE / max_exact))
    large = jnp.minimum(max_exact + (scaled * (REL_BUCKETS - max_exact)).astype(jnp.int32), REL_BUCKETS - 1)
    return jnp.where(n < max_exact, n, large)


def _attn_biases(rel_bias_table):
    table = rel_bias_table.astype(F32).T
    pos = jnp.arange(MOBA_BLOCK)
    rel0 = pos[None, :] - pos[:, None]
    far = table[:, REL_BUCKETS - 1][:, None, None]

    rel = jnp.arange(1 - 2 * MOBA_BLOCK, 2 * MOBA_BLOCK)
    hit = _t5_bucket(rel)[None, :, None] == jnp.arange(REL_BUCKETS)
    by_rel = jnp.sum(jnp.where(hit, table[:, None, :], 0.0), axis=-1)
    zero = 2 * MOBA_BLOCK - 1

    def tile(offset):
        rows = [by_rel[:, zero + offset - kk:zero + offset - kk + MOBA_BLOCK] for kk in range(MOBA_BLOCK)]
        return jnp.stack(rows, axis=1)

    bias0 = jnp.where(rel0[None] >= 0, tile(0) - far, MASK_VALUE)
    bias1 = tile(MOBA_BLOCK) - far
    return bias0, bias1


def _gla_kernel(gq_ref, gk_ref, gv_ref, gr_ref, gz_ref, wa_ref, ba_ref, on_ref, y_ref, state_ref):
    s_idx = pl.program_id(1)
    tg = gq_ref.shape[0]
    dk, dv, ck = GLA_KEY_DIM, GLA_VALUE_DIM, GLA_CHUNK

    @pl.when(s_idx == 0)
    def _():
        state_ref[...] = jnp.zeros_like(state_ref)

    z = gz_ref[...].astype(BF16)
    log_a = jax.nn.log_sigmoid(_dot(z, wa_ref[...]) + ba_ref[...]) / GLA_GATE_TEMP
    row = lax.broadcasted_iota(jnp.int32, log_a.shape, 0) % ck
    bcum = log_a
    shift = 1
    while shift < ck:
        bcum = bcum + jnp.where(row >= shift, pltpu.roll(bcum, shift, 0), 0.0)
        shift *= 2
    q = gq_ref[...] * (dk ** -0.5)
    k = gk_ref[...]
    tri = (lax.broadcasted_iota(jnp.int32, (ck, ck), 0) >= lax.broadcasted_iota(jnp.int32, (ck, ck), 1))
    for c in range(tg // ck):
        rows = slice(c * ck, (c + 1) * ck)
        for h in range(GLA_HEADS):
            kcols = slice(h * dk, (h + 1) * dk)
            vcols = slice(h * dv, (h + 1) * dv)
            bc = bcum[rows, kcols]
            btot = bc[ck - 1:ck, :]
            qe = (q[rows, kcols] * jnp.exp(bc)).astype(BF16)
            ke = (k[rows, kcols] * jnp.exp(-bc)).astype(BF16)
            kd = (k[rows, kcols] * jnp.exp(btot - bc)).astype(BF16)
            vc = gv_ref[rows, vcols].astype(BF16)
            att = jnp.where(tri, _dot_nt(qe, ke), 0.0).astype(BF16)
            st = state_ref[h]
            o = _dot(att, vc) + _dot_nt(qe, st.astype(BF16))
            state_ref[h] = st * jnp.exp(btot) + _dot_tn(vc, kd)
            o = _rms(o, on_ref[...])
            r = gr_ref[rows, vcols]
            y_ref[rows, vcols] = (o * (r * jax.nn.sigmoid(r))).astype(BF16)


def _gla(gq, gk, gv, gr, gz, wa, ba, on):
    b, s, _ = gq.shape
    tg = 256
    tok = lambda w: pl.BlockSpec((None, tg, w), lambda bi, si: (bi, si, 0))
    full = lambda arr: pl.BlockSpec(arr.shape, lambda bi, si: (0,) * arr.ndim)
    return pl.pallas_call(
        _gla_kernel,
        grid=(b, s // tg),
        in_specs=[tok(GLA_QK_DIM), tok(GLA_QK_DIM), tok(GLA_V_DIM), tok(GLA_V_DIM), tok(GZ_PAD),
                  full(wa), full(ba), full(on)],
        out_specs=tok(GLA_V_DIM),
        out_shape=jax.ShapeDtypeStruct((b, s, GLA_V_DIM), BF16),
        scratch_shapes=[pltpu.VMEM((GLA_HEADS, GLA_VALUE_DIM, GLA_KEY_DIM), F32)],
        compiler_params=_cparams(2),
        name="gla",
    )(gq, gk, gv, gr, gz, wa, ba, on)


def _merge_route_kernel(x_ref, yc_ref, ya_ref, yg_ref, gmix_ref, wg_ref, wb_ref, wo_ref, gffn_ref,
                        wr_hi_ref, wr_lo_ref, br_ref, xo_ref, h2_ref, route_ref, counts_ref, run_ref):
    x = x_ref[...]
    h = _rms(x, gmix_ref[...]).astype(BF16)
    merged = None
    for n, y_ref in enumerate((yc_ref, ya_ref, yg_ref)):
        term = jax.nn.sigmoid(_dot(h, wg_ref[n])) * _dot(y_ref[...], wb_ref[n])
        merged = term if merged is None else merged + term
    xo = x + _dot(merged.astype(BF16), wo_ref[...])
    xo_ref[...] = xo
    h2 = _rms(xo, gffn_ref[...])
    _rows_to_tiles(h2_ref, h2)

    h_hi, h_lo = _split_bf16(h2)
    logits = (_dot(h_hi, wr_hi_ref[...]) + _dot(h_lo, wr_hi_ref[...]) + _dot(h_hi, wr_lo_ref[...])
              + br_ref[...])
    lane = lax.broadcasted_iota(jnp.int32, logits.shape, 1).astype(F32)
    big = 4.0 * ROUTE_LANES
    lg = jnp.where(lane < N_GROUPS, logits, -jnp.inf)
    gmax = jnp.max(lg, axis=1, keepdims=True)
    gidx = jnp.min(jnp.where(lg == gmax, lane, big), axis=1, keepdims=True)
    p_group_top = 1.0 / jnp.sum(jnp.exp(lg - gmax), axis=1, keepdims=True)
    lo_lane = N_GROUPS + gidx * EXPERTS_PER_GROUP
    le = jnp.where((lane >= lo_lane) & (lane < lo_lane + EXPERTS_PER_GROUP), logits, -jnp.inf)
    emax = jnp.max(le, axis=1, keepdims=True)
    i1 = jnp.min(jnp.where(le == emax, lane, big), axis=1, keepdims=True)
    esum = jnp.sum(jnp.exp(le - emax), axis=1, keepdims=True)
    le2 = jnp.where(lane == i1, -jnp.inf, le)
    emax2 = jnp.max(le2, axis=1, keepdims=True)
    i2 = jnp.min(jnp.where(le2 == emax2, lane, big), axis=1, keepdims=True)
    p1 = 1.0 / esum
    p2 = jnp.exp(emax2 - emax) / esum
    psum = p1 + p2
    w1 = p_group_top * p1 / psum
    w2 = p_group_top * p2 / psum
    e1 = i1 - N_GROUPS
    e2 = i2 - N_GROUPS

    @pl.when(pl.program_id(0) == 0)
    def _():
        run_ref[...] = jnp.zeros_like(run_ref)

    tm = x.shape[0]
    oh1 = jnp.where(lane == i1, 1.0, 0.0)
    oh2 = jnp.where(lane == i2, 1.0, 0.0)
    ohs = oh1 + oh2
    lower = (lax.broadcasted_iota(jnp.int32, (tm, tm), 0) > lax.broadcasted_iota(jnp.int32, (tm, tm), 1))
    before = _dot(jnp.where(lower, 1.0, 0.0).astype(BF16), ohs.astype(BF16)) + run_ref[...]
    rank1 = jnp.sum(oh1 * before, axis=1, keepdims=True)
    rank2 = jnp.sum(oh2 * before, axis=1, keepdims=True)
    run_ref[...] = run_ref[...] + jnp.sum(ohs, axis=0, keepdims=True)
    counts_ref[...] = run_ref[...]

    rl = lax.broadcasted_iota(jnp.int32, (tm, ROUTE_OUT), 1)
    rec = jnp.zeros((tm, ROUTE_OUT), F32)
    for slot, val in enumerate((e1, e2, w1, w2, rank1, rank2)):
        rec = jnp.where(rl == slot, val, rec)
    route_ref[...] = rec


def _merge_route(x2d, yc, ya, yg, gmix, wg, wb, wo, gffn, wr_hi, wr_lo, br):
    t, d = x2d.shape
    tm = 512
    tok = lambda w: pl.BlockSpec((tm, w), lambda i: (i, 0))
    full = lambda arr: pl.BlockSpec(arr.shape, lambda i: (0,) * arr.ndim)
    ins = (x2d, yc, ya, yg, gmix, wg, wb, wo, gffn, wr_hi, wr_lo, br)
    return pl.pallas_call(
        _merge_route_kernel,
        grid=(t // tm,),
        in_specs=[tok(d), tok(CONV_DIM), tok(ATTN_DIM), tok(GLA_V_DIM)] + [full(a) for a in ins[4:]],
        out_specs=(tok(d), pl.BlockSpec((tm * SUBLANES, LANES), lambda i: (i, 0)), tok(ROUTE_OUT),
                   pl.BlockSpec((1, ROUTE_LANES), lambda i: (0, 0))),
        out_shape=(jax.ShapeDtypeStruct((t, d), F32), jax.ShapeDtypeStruct((t * SUBLANES, LANES), F32),
                   jax.ShapeDtypeStruct((t, ROUTE_OUT), F32), jax.ShapeDtypeStruct((1, ROUTE_LANES), F32)),
        scratch_shapes=[pltpu.VMEM((1, ROUTE_LANES), F32)],
        compiler_params=_cparams(1),
        name="merge_route",
    )(*ins)


def _tile_slots(dest, tc):
    nt = dest.shape[0] // tc
    return dest.reshape(nt, tc, TOP_K).transpose(0, 2, 1).reshape(nt, 1, TOP_K * tc)


def _dispatch_kernel(last_blk_ref, dest_ref, h2_ref, xs_hbm, zeros_ref, sem):
    tc = h2_ref.shape[0] // SUBLANES

    @pl.when(pl.program_id(0) == 0)
    def _():
        zeros_ref[...] = jnp.zeros_like(zeros_ref)
        blk_rows = EXPERT_BLOCK * SUBLANES

        def zero_block(first_row):
            fill = pltpu.make_async_copy(
                zeros_ref, xs_hbm.at[pl.ds(pl.multiple_of(first_row * SUBLANES, SUBLANES), blk_rows)], sem)
            fill.start()
            fill.wait()

        for e in range(N_EXPERTS):
            @pl.when(last_blk_ref[e] >= 0)
            def _():
                zero_block(last_blk_ref[e])

        def zero_unused(blk, carry):
            zero_block(blk * EXPERT_BLOCK)
            return carry

        lax.fori_loop(last_blk_ref[N_EXPERTS], xs_hbm.shape[0] // blk_rows, zero_unused, 0)

    def row_copy(r, kk):
        src = pl.multiple_of(r * SUBLANES, SUBLANES)
        dst = pl.multiple_of(dest_ref[0, kk * tc + r] * SUBLANES, SUBLANES)
        return pltpu.make_async_copy(h2_ref.at[pl.ds(src, SUBLANES)], xs_hbm.at[pl.ds(dst, SUBLANES)], sem)

    def issue(r, carry):
        row_copy(r, 0).start(priority=0)
        row_copy(r, 1).start(priority=1)
        return carry

    lax.fori_loop(0, tc, issue, 0, unroll=ROW_DMA_UNROLL)
    for _ in range(TOP_K):
        pltpu.make_async_copy(h2_ref, xs_hbm.at[pl.ds(0, tc * SUBLANES)], sem).wait()


def _dispatch(dest, h2_tiles, n_rows, last_blk):
    t = h2_tiles.shape[0] // SUBLANES
    tc = 256
    grid_spec = pltpu.PrefetchScalarGridSpec(
        num_scalar_prefetch=1,
        grid=(t // tc,),
        in_specs=[
            pl.BlockSpec((None, 1, TOP_K * tc), lambda i, lb: (i, 0, 0), memory_space=pltpu.SMEM),
            pl.BlockSpec((tc * SUBLANES, LANES), lambda i, lb: (i, 0)),
        ],
        out_specs=pl.BlockSpec(memory_space=pl.ANY),
        scratch_shapes=[pltpu.VMEM((EXPERT_BLOCK * SUBLANES, LANES), F32), pltpu.SemaphoreType.DMA(())],
    )
    return pl.pallas_call(
        _dispatch_kernel,
        grid_spec=grid_spec,
        out_shape=jax.ShapeDtypeStruct((n_rows * SUBLANES, LANES), F32),
        compiler_params=_cparams(1),
        name="moe_dispatch",
    )(last_blk, _tile_slots(dest, tc), h2_tiles)


def _expert_kernel(blk_expert_ref, n_used_ref, xs_ref, wg_ref, wu_ref, wd_ref, ys_ref,
                   wg_bf, wu_bf, wd_bf):
    i = pl.program_id(0)
    rb = xs_ref.shape[0] // SUBLANES
    new_expert = (i == 0) | (blk_expert_ref[i] != blk_expert_ref[jnp.maximum(i - 1, 0)])

    @pl.when(new_expert)
    def _():
        wg_bf[...] = wg_ref[...].astype(BF16)
        wu_bf[...] = wu_ref[...].astype(BF16)
        wd_bf[...] = wd_ref[...].astype(BF16)

    @pl.when(i < n_used_ref[0])
    def _():
        xb = jnp.concatenate(_tiles_to_rows(xs_ref, rb), axis=1).astype(BF16)
        gate = _dot(xb, wg_bf[...])
        up = _dot(xb, wu_bf[...])
        act = (gate * jax.nn.sigmoid(gate) * up).astype(BF16)
        _rows_to_tiles(ys_ref, _dot(act, wd_bf[...]))

    @pl.when(i >= n_used_ref[0])
    def _():
        ys_ref[...] = jnp.zeros_like(ys_ref)


def _experts(blk_expert, n_used, xs_tiles, wg, wu, wd, layer):
    d = SUBLANES * LANES
    n_blocks = blk_expert.shape[0]
    rb = EXPERT_BLOCK
    row_block = (rb * SUBLANES, LANES)
    grid_spec = pltpu.PrefetchScalarGridSpec(
        num_scalar_prefetch=2,
        grid=(n_blocks,),
        in_specs=[
            pl.BlockSpec(row_block, lambda i, be, nu: (jnp.minimum(i, nu[0] - 1), 0)),
            pl.BlockSpec((None, None, d, EXPERT_FF), lambda i, be, nu: (layer, be[i], 0, 0)),
            pl.BlockSpec((None, None, d, EXPERT_FF), lambda i, be, nu: (layer, be[i], 0, 0)),
            pl.BlockSpec((None, None, EXPERT_FF, d), lambda i, be, nu: (layer, be[i], 0, 0)),
        ],
        out_specs=pl.BlockSpec(row_block, lambda i, be, nu: (i, 0)),
        scratch_shapes=[pltpu.VMEM((d, EXPERT_FF), BF16), pltpu.VMEM((d, EXPERT_FF), BF16),
                        pltpu.VMEM((EXPERT_FF, d), BF16)],
    )
    return pl.pallas_call(
        _expert_kernel,
        grid_spec=grid_spec,
        out_shape=jax.ShapeDtypeStruct(xs_tiles.shape, F32),
        compiler_params=_cparams(1),
        name="moe_experts",
    )(blk_expert, n_used, xs_tiles, wg, wu, wd)


def _combine_kernel(dest_ref, x_ref, route_ref, ys_hbm, o_ref, buf, sem):
    tc = x_ref.shape[0]

    def row_copy(r, kk):
        src = pl.multiple_of(dest_ref[0, kk * tc + r] * SUBLANES, SUBLANES)
        dst = pl.multiple_of(r * SUBLANES, SUBLANES)
        return pltpu.make_async_copy(ys_hbm.at[pl.ds(src, SUBLANES)], buf.at[kk, pl.ds(dst, SUBLANES)], sem)

    def issue(r, carry):
        row_copy(r, 0).start(priority=0)
        row_copy(r, 1).start(priority=1)
        return carry

    lax.fori_loop(0, tc, issue, 0, unroll=ROW_DMA_UNROLL)
    for kk in range(TOP_K):
        pltpu.make_async_copy(ys_hbm.at[pl.ds(0, tc * SUBLANES)], buf.at[kk], sem).wait()
    route = route_ref[...]
    w1 = route[:, TOP_K:TOP_K + 1]
    w2 = route[:, TOP_K + 1:TOP_K + 2]
    y1 = _tiles_to_rows(buf.at[0], tc)
    y2 = _tiles_to_rows(buf.at[1], tc)
    for j in range(SUBLANES):
        cols = slice(j * LANES, (j + 1) * LANES)
        o_ref[:, cols] = x_ref[:, cols] + (y1[j] * w1 + y2[j] * w2)


def _combine(dest, x2d, route, ys_tiles):
    t, d = x2d.shape
    tc = 256
    return pl.pallas_call(
        _combine_kernel,
        grid=(t // tc,),
        in_specs=[
            pl.BlockSpec((None, 1, TOP_K * tc), lambda i: (i, 0, 0), memory_space=pltpu.SMEM),
            pl.BlockSpec((tc, d), lambda i: (i, 0)),
            pl.BlockSpec((tc, ROUTE_OUT), lambda i: (i, 0)),
            pl.BlockSpec(memory_space=pl.ANY),
        ],
        out_specs=pl.BlockSpec((tc, d), lambda i: (i, 0)),
        out_shape=jax.ShapeDtypeStruct((t, d), F32),
        scratch_shapes=[pltpu.VMEM((TOP_K, tc * SUBLANES, LANES), F32), pltpu.SemaphoreType.DMA(())],
        compiler_params=_cparams(1),
        name="moe_combine",
    )(_tile_slots(dest, tc), x2d, route, ys_tiles)


def _dispatch_plan(route, counts, t):
    e_ids = route[:, :TOP_K].astype(jnp.int32)
    rank = route[:, 2 * TOP_K:3 * TOP_K].astype(jnp.int32)
    counts = counts[0, N_GROUPS:N_GROUPS + N_EXPERTS].astype(jnp.int32)
    padded = ((counts + EXPERT_BLOCK - 1) // EXPERT_BLOCK) * EXPERT_BLOCK
    pad_end = jnp.cumsum(padded)
    pad_start = pad_end - padded
    onehot = e_ids[:, :, None] == jnp.arange(N_EXPERTS, dtype=jnp.int32)
    dest = rank + jnp.sum(jnp.where(onehot, pad_start, 0), axis=-1)
    n_blocks = -(-(t * TOP_K) // EXPERT_BLOCK) + N_EXPERTS
    blk_start = jnp.arange(n_blocks, dtype=jnp.int32) * EXPERT_BLOCK
    blk_expert = jnp.minimum(jnp.sum(blk_start[:, None] >= pad_end[None, :], axis=1), N_EXPERTS - 1)
    n_used = jnp.maximum(pad_end[-1:] // EXPERT_BLOCK, 1).astype(jnp.int32)
    last_blk = jnp.concatenate([jnp.where(padded > 0, pad_end - EXPERT_BLOCK, -1), n_used]).astype(jnp.int32)
    return blk_expert.astype(jnp.int32), n_used, dest, n_blocks * EXPERT_BLOCK, last_blk


def kernel(x, rel_bias_table, norm_mix, w_in, conv_w, conv_b, q_norm, k_norm, w_gla_alpha, b_gla_alpha,
           gla_out_norm, w_merge_gate, w_branch, w_out, norm_ffn, w_router_group, b_router_group,
           w_router_expert, b_router_expert, w_expert_gate, w_expert_up, w_expert_down):
    b, s, d = x.shape
    t = b * s
    depth = w_in.shape[0]
    assert s % MOBA_BLOCK == 0 and t % EXPERT_BLOCK == 0
    assert d == SUBLANES * LANES, "row-granular DMAs store each activation row as one (8, 128) tile"
    bias0, bias1 = _attn_biases(rel_bias_table)
    head_id = jnp.arange(ATTN_DIM) // ATTN_HEAD_DIM
    hsum = (head_id[:, None] == head_id[None, :]).astype(BF16)
    c3 = 3 * CONV_DIM
    a3 = c3 + 3 * ATTN_DIM
    g3 = a3 + 2 * GLA_QK_DIM + 2 * GLA_V_DIM
    for l in range(depth):
        w_l = w_in[l].astype(BF16)
        w_gz = jnp.pad(w_l[:, g3:], ((0, 0), (0, GZ_PAD - GLA_GATE_RANK)))
        w_alpha = jnp.pad(w_gla_alpha[l].astype(BF16), ((0, GZ_PAD - GLA_GATE_RANK), (0, 0)))
        yconv, q, k, v, kmean, gq, gk, gv, gr, gz = _inproj(
            x, norm_mix[l][None], w_l[:, :c3], w_l[:, c3:a3], w_l[:, a3:g3], w_gz,
            conv_w[l], conv_b[l][None],
            jnp.tile(q_norm[l], ATTN_HEADS)[None], jnp.tile(k_norm[l], ATTN_HEADS)[None], hsum)
        qp, kp = _select(q, k, kmean.reshape(b, s // MOBA_BLOCK, ATTN_DIM))
        yattn = _attention(qp, kp, v, bias0, bias1)
        ygla = _gla(gq, gk, gv, gr, gz, w_alpha, b_gla_alpha[l][None],
                    gla_out_norm[l][None])
        w_r = jnp.concatenate([w_router_group[l], w_router_expert[l]], axis=1)
        w_r = jnp.pad(w_r, ((0, 0), (0, ROUTE_LANES - w_r.shape[1])))
        b_r = jnp.pad(jnp.concatenate([b_router_group[l], b_router_expert[l]]),
                      (0, ROUTE_LANES - N_GROUPS - N_EXPERTS))[None]
        wr_hi, wr_lo = _split_bf16(w_r)
        xo, h2, route, counts = _merge_route(
            x.reshape(t, d), yconv.reshape(t, -1), yattn.reshape(t, -1), ygla.reshape(t, -1),
            norm_mix[l][None], w_merge_gate[l].astype(BF16), w_branch[l].astype(BF16),
            w_out[l].astype(BF16), norm_ffn[l][None], wr_hi, wr_lo, b_r)
        blk_expert, n_used, dest, n_rows, last_blk = _dispatch_plan(route, counts, t)
        xs = _dispatch(dest, h2, n_rows, last_blk)
        ys = _experts(blk_expert, n_used, xs, w_expert_gate, w_expert_up, w_expert_down, l)
        x = _combine(dest, xo, route, ys).reshape(b, s, d)
    return x
```

```python
import functools
import math

import jax
import jax.numpy as jnp
import numpy as np
from jax import lax
from jax.experimental import pallas as pl
from jax.experimental.pallas import tpu as pltpu

CONV_DIM = 512
CONV_WIDTH = 3
ATTN_HEADS = 8
ATTN_HEAD_DIM = 64
ATTN_DIM = ATTN_HEADS * ATTN_HEAD_DIM
MOBA_BLOCK = 256
MOBA_TOPK = 3
REL_BUCKETS = 32
REL_MAX_DISTANCE = 128
GLA_HEADS = 4
GLA_KEY_DIM = 64
GLA_VALUE_DIM = 128
GLA_QK_DIM = GLA_HEADS * GLA_KEY_DIM
GLA_V_DIM = GLA_HEADS * GLA_VALUE_DIM
GLA_GATE_RANK = 16
GLA_GATE_TEMP = 16.0
GLA_CHUNK = 64
N_GROUPS = 4
EXPERTS_PER_GROUP = 8
N_EXPERTS = N_GROUPS * EXPERTS_PER_GROUP
TOP_K = 2
EXPERT_FF = 512
EXPERT_BLOCK = 256
RMS_EPS = 1e-6

LANES = 128
VMEM_LIMIT_BYTES = 56 * 1024 * 1024

MASK_VALUE = -1e30
ROUTE_LANES = 128
ROUTE_OUT = 8
ATTN_FAR_CHUNK = 4
ATTN_TAIL = 5
ROW_DMA_UNROLL = 8
GZ_PAD = LANES

F32 = jnp.float32
BF16 = jnp.bfloat16


def _cparams(n_axes):
    return pltpu.CompilerParams(
        dimension_semantics=("arbitrary",) * n_axes,
        vmem_limit_bytes=VMEM_LIMIT_BYTES,
    )


def _rms(x, gain):
    return x * lax.rsqrt(jnp.mean(x * x, axis=-1, keepdims=True) + RMS_EPS) * gain


def _split_bf16(x):
    hi = x.astype(BF16)
    lo = (x - hi.astype(F32)).astype(BF16)
    return hi, lo


def _dot(a, b):
    return jnp.dot(a, b, preferred_element_type=F32)


def _dot_nt(a, b):
    return lax.dot_general(a, b, (((1,), (1,)), ((), ())), preferred_element_type=F32)


SUBLANES = 8


def _rows_to_tiles(dst_ref, x):
    n = x.shape[0]
    for j in range(SUBLANES):
        dst_ref[pl.ds(j, n, stride=SUBLANES), :] = x[:, j * LANES:(j + 1) * LANES]


def _tiles_to_rows(src_ref, n):
    return [src_ref[pl.ds(j, n, stride=SUBLANES), :] for j in range(SUBLANES)]


def _dot_tn(a, b):
    return lax.dot_general(a, b, (((0,), (0,)), ((), ())), preferred_element_type=F32)


def _inproj_kernel(x_ref, gmix_ref, wconv_ref, wattn_ref, wgla_ref, wgz_ref, convw_ref, convb_ref,
                   qn_ref, kn_ref, hsum_ref,
                   yconv_ref, q_ref, k_ref, v_ref, kmean_ref, gq_ref, gk_ref, gv_ref, gr_ref, gz_ref,
                   carry_ref):
    s_idx = pl.program_id(1)
    ts = x_ref.shape[0]
    h = _rms(x_ref[...], gmix_ref[...]).astype(BF16)

    c = _dot(h, wconv_ref[...])
    cb = c[:, :CONV_DIM]
    u = c[:, CONV_DIM:2 * CONV_DIM] * c[:, 2 * CONV_DIM:]

    @pl.when(s_idx == 0)
    def _():
        carry_ref[...] = jnp.zeros_like(carry_ref)

    prev = carry_ref[...]
    row = lax.broadcasted_iota(jnp.int32, u.shape, 0)
    u1 = pltpu.roll(u, 1, 0)
    u1 = jnp.where(row == 0, prev[7:8, :], u1)
    u2 = pltpu.roll(u, 2, 0)
    u2 = jnp.where(row == 0, prev[6:7, :], jnp.where(row == 1, prev[7:8, :], u2))
    carry_ref[...] = u[ts - 8:, :]
    y = convb_ref[...] + convw_ref[0:1, :] * u2
    y = y + convw_ref[1:2, :] * u1
    y = y + convw_ref[2:3, :] * u
    yconv_ref[...] = (cb * y).astype(BF16)

    a = _dot(h, wattn_ref[...])
    hsum = hsum_ref[...]

    def head_norm(t, gain):
        hi, lo = _split_bf16(t * t)
        ss = _dot(hi, hsum) + _dot(lo, hsum)
        return t * lax.rsqrt(ss * (1.0 / ATTN_HEAD_DIM) + RMS_EPS) * gain

    qn = head_norm(a[:, :ATTN_DIM], qn_ref[...])
    kn = head_norm(a[:, ATTN_DIM:2 * ATTN_DIM], kn_ref[...])
    q_ref[...] = (qn * (ATTN_HEAD_DIM ** -0.5)).astype(BF16)
    k_ref[...] = kn.astype(BF16)
    v_ref[...] = jnp.transpose(a[:, 2 * ATTN_DIM:]).astype(BF16)
    for j in range(ts // MOBA_BLOCK):
        kmean_ref[j] = jnp.mean(kn[j * MOBA_BLOCK:(j + 1) * MOBA_BLOCK], axis=0, keepdims=True)

    g = _dot(h, wgla_ref[...])
    gq_ref[...] = g[:, :GLA_QK_DIM]
    gk_ref[...] = g[:, GLA_QK_DIM:2 * GLA_QK_DIM]
    gv_ref[...] = g[:, 2 * GLA_QK_DIM:2 * GLA_QK_DIM + GLA_V_DIM]
    gr_ref[...] = g[:, 2 * GLA_QK_DIM + GLA_V_DIM:]
    gz_ref[...] = _dot(h, wgz_ref[...])


def _inproj(x, gmix, wconv, wattn, wgla, wgz, convw, convb, qn, kn, hsum):
    b, s, d = x.shape
    blocks_per_tile = 2
    ts = blocks_per_tile * MOBA_BLOCK
    assert s % ts == 0
    nb = s // MOBA_BLOCK
    tok = lambda w: pl.BlockSpec((None, ts, w), lambda bi, si: (bi, si, 0))
    full = lambda arr: pl.BlockSpec(arr.shape, lambda bi, si: (0,) * arr.ndim)
    out_shapes = (
        jax.ShapeDtypeStruct((b, s, CONV_DIM), BF16),
        jax.ShapeDtypeStruct((b, s, ATTN_DIM), BF16),
        jax.ShapeDtypeStruct((b, s, ATTN_DIM), BF16),
        jax.ShapeDtypeStruct((b, ATTN_DIM, s), BF16),
        jax.ShapeDtypeStruct((b, nb, 1, ATTN_DIM), F32),
        jax.ShapeDtypeStruct((b, s, GLA_QK_DIM), F32),
        jax.ShapeDtypeStruct((b, s, GLA_QK_DIM), F32),
        jax.ShapeDtypeStruct((b, s, GLA_V_DIM), F32),
        jax.ShapeDtypeStruct((b, s, GLA_V_DIM), F32),
        jax.ShapeDtypeStruct((b, s, GZ_PAD), F32),
    )
    out_specs = (
        tok(CONV_DIM), tok(ATTN_DIM), tok(ATTN_DIM),
        pl.BlockSpec((None, ATTN_DIM, ts), lambda bi, si: (bi, 0, si)),
        pl.BlockSpec((None, blocks_per_tile, 1, ATTN_DIM), lambda bi, si: (bi, si, 0, 0)),
        tok(GLA_QK_DIM), tok(GLA_QK_DIM), tok(GLA_V_DIM), tok(GLA_V_DIM), tok(GZ_PAD),
    )
    ins = (x, gmix, wconv, wattn, wgla, wgz, convw, convb, qn, kn, hsum)
    in_specs = [tok(d)] + [full(a) for a in ins[1:]]
    return pl.pallas_call(
        _inproj_kernel,
        grid=(b, s // ts),
        in_specs=in_specs,
        out_specs=out_specs,
        out_shape=out_shapes,
        scratch_shapes=[pltpu.VMEM((8, CONV_DIM), F32)],
        compiler_params=_cparams(2),
        name="inproj",
    )(*ins)


def _select_kernel(q_ref, k_ref, kmean_ref, qp_ref, kp_ref):
    own = pl.program_id(1)
    tq = q_ref.shape[0]
    nb = kmean_ref.shape[0]
    half = ATTN_HEAD_DIM
    lane = lax.broadcasted_iota(jnp.int32, (tq, LANES), 1)
    blk = lax.broadcasted_iota(jnp.int32, (nb, tq), 0).astype(F32)
    own_f = own.astype(F32)
    onehot = jnp.where(lane - half == own, 1.0, 0.0).astype(F32)
    kmean = kmean_ref[...].astype(BF16)
    for p in range(ATTN_HEADS // 2):
        qpair = q_ref[:, p * LANES:(p + 1) * LANES]
        kpair = k_ref[:, p * LANES:(p + 1) * LANES].astype(F32)
        kmpair = kmean[:, p * LANES:(p + 1) * LANES]
        qpair_f = qpair.astype(F32)
        for sub in range(2):
            h = 2 * p + sub
            lane_sel = (lane >= sub * half) & (lane < (sub + 1) * half)
            qh = jnp.where(lane_sel, qpair_f, 0.0).astype(BF16)
            gate_t = _dot_nt(kmpair, qh)
            g = jnp.where(blk < own_f, gate_t, -jnp.inf)
            alive = jnp.ones((nb, tq), F32)
            sel = jnp.where(blk == own_f, 1.0, 0.0)
            for r in range(MOBA_TOPK):
                ga = jnp.where(alive > 0.0, g, -jnp.inf)
                mx = jnp.max(ga, axis=0, keepdims=True)
                cand = jnp.where((alive > 0.0) & (g == mx), blk, 2.0 * LANES)
                first = jnp.min(cand, axis=0, keepdims=True)
                hit = blk == first
                sel = jnp.where(hit, jnp.maximum(sel, jnp.where(own_f > r, 1.0, 0.0)), sel)
                alive = jnp.where(hit, 0.0, alive)
            m_t = jnp.where(sel > 0.0, 0.0, MASK_VALUE).astype(F32)
            m_t = jnp.concatenate([m_t, jnp.zeros((LANES - nb, tq), F32)], axis=0)
            m = jnp.transpose(m_t)
            m = pltpu.roll(m, half, 1)
            m = jnp.where((lane >= half) & (lane < half + nb), m, 0.0)
            qs = qpair_f if sub == 0 else pltpu.roll(qpair_f, half, 1)
            ks = kpair if sub == 0 else pltpu.roll(kpair, half, 1)
            qp_ref[h] = jnp.where(lane < half, qs, m).astype(BF16)
            kp_ref[h] = jnp.where(lane < half, ks, onehot).astype(BF16)


def _select(q, k, kmean):
    b, s, _ = q.shape
    tq = MOBA_BLOCK
    nb = s // tq
    assert nb <= LANES - ATTN_HEAD_DIM, "block one-hot must fit beside the head dim in one lane tile"
    tok = pl.BlockSpec((None, tq, ATTN_DIM), lambda bi, si: (bi, si, 0))
    slab = pl.BlockSpec((None, ATTN_HEADS, tq, LANES), lambda bi, si: (bi, 0, si, 0))
    shp = jax.ShapeDtypeStruct((b, ATTN_HEADS, s, LANES), BF16)
    return pl.pallas_call(
        _select_kernel,
        grid=(b, nb),
        in_specs=[tok, tok, pl.BlockSpec((None, nb, ATTN_DIM), lambda bi, si: (bi, 0, 0))],
        out_specs=(slab, slab),
        out_shape=(shp, shp),
        compiler_params=_cparams(2),
        name="moba_select",
    )(q, k, kmean)


def _attn_kernel(qp_ref, kp_ref, vt_ref, bias0_ref, bias1_ref, o_ref, m_ref, l_ref, acc_ref,
                 sa_ref, sb_ref, st_ref):
    own = pl.program_id(2)
    tq = qp_ref.shape[1]
    far_rows = ATTN_FAR_CHUNK * tq
    sub = lax.broadcasted_iota(jnp.int32, (LANES, tq), 0)
    n_far = jnp.maximum(own - 1, 0) // ATTN_FAR_CHUNK

    def scores(hh, start, n):
        return _dot_nt(kp_ref[hh, pl.ds(start, n * tq), :], qp_ref[hh])

    def far_scores(dst_ref, chunk):
        start = pl.multiple_of(chunk * far_rows, far_rows)
        for hh in range(2):
            dst_ref[hh] = scores(hh, start, ATTN_FAR_CHUNK)

    def softmax_pv(hh, s_ref, start, n):
        blocks = [s_ref[hh, w * tq:(w + 1) * tq, :] for w in range(n)]
        m_prev = m_ref[hh]
        m_new = jnp.maximum(m_prev, jnp.max(functools.reduce(jnp.maximum, blocks), axis=0, keepdims=True))
        alpha = jnp.exp(m_prev - m_new)
        ps = [jnp.exp(blk - m_new) for blk in blocks]
        l_ref[hh] = alpha * l_ref[hh] + jnp.sum(functools.reduce(jnp.add, ps), axis=0, keepdims=True)
        pt = jnp.concatenate([x.astype(BF16) for x in ps], axis=0)
        acc_ref[hh] = alpha * acc_ref[hh] + _dot(vt_ref[:, pl.ds(start, n * tq)], pt)
        m_ref[hh] = m_new

    for hh in range(2):
        m_ref[hh] = jnp.full((1, tq), MASK_VALUE, F32)
        l_ref[hh] = jnp.zeros((1, tq), F32)
        acc_ref[hh] = jnp.zeros((LANES, tq), F32)

    for own_small in range(ATTN_TAIL - 1):
        @pl.when(own == own_small)
        def _():
            for hh in range(2):
                s = scores(hh, 0, own_small + 1)
                for w in range(own_small + 1):
                    sw = s[w * tq:(w + 1) * tq, :]
                    if w == own_small:
                        sw = sw + bias0_ref[hh]
                    elif w == own_small - 1:
                        sw = sw + bias1_ref[hh]
                    st_ref[hh, w * tq:(w + 1) * tq, :] = sw

    @pl.when(own >= ATTN_TAIL - 1)
    def _():
        first = own - (ATTN_TAIL - 1)
        done = n_far * ATTN_FAR_CHUNK
        for hh in range(2):
            s = scores(hh, pl.multiple_of(first * tq, tq), ATTN_TAIL)
            for w in range(ATTN_TAIL):
                sw = s[w * tq:(w + 1) * tq, :]
                if w == ATTN_TAIL - 1:
                    sw = sw + bias0_ref[hh]
                elif w == ATTN_TAIL - 2:
                    sw = sw + bias1_ref[hh]
                else:
                    sw = sw + jnp.where(first + w < done, MASK_VALUE, 0.0).astype(F32)
                st_ref[hh, w * tq:(w + 1) * tq, :] = sw

    @pl.when(n_far > 0)
    def _():
        far_scores(sa_ref, 0)

    def far_pair(j, carry):
        c0 = 2 * j
        far_scores(sb_ref, jnp.minimum(c0 + 1, n_far - 1))
        for hh in range(2):
            softmax_pv(hh, sa_ref, pl.multiple_of(c0 * far_rows, far_rows), ATTN_FAR_CHUNK)

        @pl.when(c0 + 1 < n_far)
        def _():
            far_scores(sa_ref, jnp.minimum(c0 + 2, n_far - 1))
            for hh in range(2):
                softmax_pv(hh, sb_ref, pl.multiple_of((c0 + 1) * far_rows, far_rows), ATTN_FAR_CHUNK)

        return carry

    lax.fori_loop(0, (n_far + 1) // 2, far_pair, 0)

    for own_small in range(ATTN_TAIL - 1):
        @pl.when(own == own_small)
        def _():
            for hh in range(2):
                softmax_pv(hh, st_ref, 0, own_small + 1)

    @pl.when(own >= ATTN_TAIL - 1)
    def _():
        start = pl.multiple_of((own - (ATTN_TAIL - 1)) * tq, tq)
        for hh in range(2):
            softmax_pv(hh, st_ref, start, ATTN_TAIL)

    outs = [acc_ref[hh] / l_ref[hh] for hh in range(2)]
    o_t = jnp.where(sub < ATTN_HEAD_DIM, outs[0], outs[1])
    o_ref[...] = jnp.transpose(o_t).astype(BF16)


def _attention(qp, kp, vt, bias0, bias1):
    b, nh, s, _ = qp.shape
    tq = MOBA_BLOCK
    nq = s // tq
    return pl.pallas_call(
        _attn_kernel,
        grid=(b, nh // 2, nq),
        in_specs=[
            pl.BlockSpec((None, 2, tq, LANES), lambda bi, pi, qi: (bi, pi, qi, 0)),
            pl.BlockSpec((None, 2, s, LANES), lambda bi, pi, qi: (bi, pi, 0, 0)),
            pl.BlockSpec((None, LANES, s), lambda bi, pi, qi: (bi, pi, 0)),
            pl.BlockSpec((2, tq, tq), lambda bi, pi, qi: (pi, 0, 0)),
            pl.BlockSpec((2, tq, tq), lambda bi, pi, qi: (pi, 0, 0)),
        ],
        out_specs=pl.BlockSpec((None, tq, LANES), lambda bi, pi, qi: (bi, qi, pi)),
        out_shape=jax.ShapeDtypeStruct((b, s, ATTN_DIM), BF16),
        scratch_shapes=[
            pltpu.VMEM((2, 1, tq), F32),
            pltpu.VMEM((2, 1, tq), F32),
            pltpu.VMEM((2, LANES, tq), F32),
            pltpu.VMEM((2, ATTN_FAR_CHUNK * tq, tq), F32),
            pltpu.VMEM((2, ATTN_FAR_CHUNK * tq, tq), F32),
            pltpu.VMEM((2, ATTN_TAIL * tq, tq), F32),
        ],
        compiler_params=_cparams(3),
        name="moba_attention",
    )(qp, kp, vt, bias0, bias1)


def _t5_bucket(rel):
    n = jnp.maximum(rel, 0)
    max_exact = REL_BUCKETS // 2
    scaled = (jnp.log(jnp.maximum(n, max_exact).astype(F32) / max_exact)
              / math.log(REL_MAX_DISTANCE / max_exact))
    large = jnp.minimum(max_exact + (scaled * (REL_BUCKETS - max_exact)).astype(jnp.int32), REL_BUCKETS - 1)
    return jnp.where(n < max_exact, n, large)


def _attn_biases(rel_bias_table):
    table = rel_bias_table.astype(F32).T
    pos = jnp.arange(MOBA_BLOCK)
    rel0 = pos[None, :] - pos[:, None]
    far = table[:, REL_BUCKETS - 1][:, None, None]

    rel = jnp.arange(1 - 2 * MOBA_BLOCK, 2 * MOBA_BLOCK)
    hit = _t5_bucket(rel)[None, :, None] == jnp.arange(REL_BUCKETS)
    by_rel = jnp.sum(jnp.where(hit, table[:, None, :], 0.0), axis=-1)
    zero = 2 * MOBA_BLOCK - 1

    def tile(offset):
        rows = [by_rel[:, zero + offset - kk:zero + offset - kk + MOBA_BLOCK] for kk in range(MOBA_BLOCK)]
        return jnp.stack(rows, axis=1)

    bias0 = jnp.where(rel0[None] >= 0, tile(0) - far, MASK_VALUE)
    bias1 = tile(MOBA_BLOCK) - far
    return bias0, bias1


def _gla_kernel(gq_ref, gk_ref, gv_ref, gr_ref, gz_ref, wa_ref, ba_ref, on_ref, y_ref, state_ref):
    s_idx = pl.program_id(1)
    tg = gq_ref.shape[0]
    dk, dv, ck = GLA_KEY_DIM, GLA_VALUE_DIM, GLA_CHUNK

    @pl.when(s_idx == 0)
    def _():
        state_ref[...] = jnp.zeros_like(state_ref)

    z = gz_ref[...].astype(BF16)
    log_a = jax.nn.log_sigmoid(_dot(z, wa_ref[...]) + ba_ref[...]) / GLA_GATE_TEMP
    row = lax.broadcasted_iota(jnp.int32, log_a.shape, 0) % ck
    bcum = log_a
    shift = 1
    while shift < ck:
        bcum = bcum + jnp.where(row >= shift, pltpu.roll(bcum, shift, 0), 0.0)
        shift *= 2
    q = gq_ref[...] * (dk ** -0.5)
    k = gk_ref[...]
    tri = (lax.broadcasted_iota(jnp.int32, (ck, ck), 0) >= lax.broadcasted_iota(jnp.int32, (ck, ck), 1))
    for c in range(tg // ck):
        rows = slice(c * ck, (c + 1) * ck)
        for h in range(GLA_HEADS):
            kcols = slice(h * dk, (h + 1) * dk)
            vcols = slice(h * dv, (h + 1) * dv)
            bc = bcum[rows, kcols]
            btot = bc[ck - 1:ck, :]
            qe = (q[rows, kcols] * jnp.exp(bc)).astype(BF16)
            ke = (k[rows, kcols] * jnp.exp(-bc)).astype(BF16)
            kd = (k[rows, kcols] * jnp.exp(btot - bc)).astype(BF16)
            vc = gv_ref[rows, vcols].astype(BF16)
            att = jnp.where(tri, _dot_nt(qe, ke), 0.0).astype(BF16)
            st = state_ref[h]
            o = _dot(att, vc) + _dot_nt(qe, st.astype(BF16))
            state_ref[h] = st * jnp.exp(btot) + _dot_tn(vc, kd)
            o = _rms(o, on_ref[...])
            r = gr_ref[rows, vcols]
            y_ref[rows, vcols] = (o * (r * jax.nn.sigmoid(r))).astype(BF16)


def _gla(gq, gk, gv, gr, gz, wa, ba, on):
    b, s, _ = gq.shape
    tg = 256
    tok = lambda w: pl.BlockSpec((None, tg, w), lambda bi, si: (bi, si, 0))
    full = lambda arr: pl.BlockSpec(arr.shape, lambda bi, si: (0,) * arr.ndim)
    return pl.pallas_call(
        _gla_kernel,
        grid=(b, s // tg),
        in_specs=[tok(GLA_QK_DIM), tok(GLA_QK_DIM), tok(GLA_V_DIM), tok(GLA_V_DIM), tok(GZ_PAD),
                  full(wa), full(ba), full(on)],
        out_specs=tok(GLA_V_DIM),
        out_shape=jax.ShapeDtypeStruct((b, s, GLA_V_DIM), BF16),
        scratch_shapes=[pltpu.VMEM((GLA_HEADS, GLA_VALUE_DIM, GLA_KEY_DIM), F32)],
        compiler_params=_cparams(2),
        name="gla",
    )(gq, gk, gv, gr, gz, wa, ba, on)


def _merge_route_kernel(x_ref, yc_ref, ya_ref, yg_ref, gmix_ref, wg_ref, wb_ref, wo_ref, gffn_ref,
                        wr_hi_ref, wr_lo_ref, br_ref, xo_ref, h2_ref, route_ref, counts_ref, run_ref):
    x = x_ref[...]
    h = _rms(x, gmix_ref[...]).astype(BF16)
    merged = None
    for n, y_ref in enumerate((yc_ref, ya_ref, yg_ref)):
        term = jax.nn.sigmoid(_dot(h, wg_ref[n])) * _dot(y_ref[...], wb_ref[n])
        merged = term if merged is None else merged + term
    xo = x + _dot(merged.astype(BF16), wo_ref[...])
    xo_ref[...] = xo
    h2 = _rms(xo, gffn_ref[...])
    _rows_to_tiles(h2_ref, h2)

    h_hi, h_lo = _split_bf16(h2)
    logits = (_dot(h_hi, wr_hi_ref[...]) + _dot(h_lo, wr_hi_ref[...]) + _dot(h_hi, wr_lo_ref[...])
              + br_ref[...])
    lane = lax.broadcasted_iota(jnp.int32, logits.shape, 1).astype(F32)
    big = 4.0 * ROUTE_LANES
    lg = jnp.where(lane < N_GROUPS, logits, -jnp.inf)
    gmax = jnp.max(lg, axis=1, keepdims=True)
    gidx = jnp.min(jnp.where(lg == gmax, lane, big), axis=1, keepdims=True)
    p_group_top = 1.0 / jnp.sum(jnp.exp(lg - gmax), axis=1, keepdims=True)
    lo_lane = N_GROUPS + gidx * EXPERTS_PER_GROUP
    le = jnp.where((lane >= lo_lane) & (lane < lo_lane + EXPERTS_PER_GROUP), logits, -jnp.inf)
    emax = jnp.max(le, axis=1, keepdims=True)
    i1 = jnp.min(jnp.where(le == emax, lane, big), axis=1, keepdims=True)
    esum = jnp.sum(jnp.exp(le - emax), axis=1, keepdims=True)
    le2 = jnp.where(lane == i1, -jnp.inf, le)
    emax2 = jnp.max(le2, axis=1, keepdims=True)
    i2 = jnp.min(jnp.where(le2 == emax2, lane, big), axis=1, keepdims=True)
    p1 = 1.0 / esum
    p2 = jnp.exp(emax2 - emax) / esum
    psum = p1 + p2
    w1 = p_group_top * p1 / psum
    w2 = p_group_top * p2 / psum
    e1 = i1 - N_GROUPS
    e2 = i2 - N_GROUPS

    @pl.when(pl.program_id(0) == 0)
    def _():
        run_ref[...] = jnp.zeros_like(run_ref)

    tm = x.shape[0]
    oh1 = jnp.where(lane == i1, 1.0, 0.0)
    oh2 = jnp.where(lane == i2, 1.0, 0.0)
    ohs = oh1 + oh2
    lower = (lax.broadcasted_iota(jnp.int32, (tm, tm), 0) > lax.broadcasted_iota(jnp.int32, (tm, tm), 1))
    before = _dot(jnp.where(lower, 1.0, 0.0).astype(BF16), ohs.astype(BF16)) + run_ref[...]
    rank1 = jnp.sum(oh1 * before, axis=1, keepdims=True)
    rank2 = jnp.sum(oh2 * before, axis=1, keepdims=True)
    run_ref[...] = run_ref[...] + jnp.sum(ohs, axis=0, keepdims=True)
    counts_ref[...] = run_ref[...]

    rl = lax.broadcasted_iota(jnp.int32, (tm, ROUTE_OUT), 1)
    rec = jnp.zeros((tm, ROUTE_OUT), F32)
    for slot, val in enumerate((e1, e2, w1, w2, rank1, rank2)):
        rec = jnp.where(rl == slot, val, rec)
    route_ref[...] = rec


def _merge_route(x2d, yc, ya, yg, gmix, wg, wb, wo, gffn, wr_hi, wr_lo, br):
    t, d = x2d.shape
    tm = 512
    tok = lambda w: pl.BlockSpec((tm, w), lambda i: (i, 0))
    full = lambda arr: pl.BlockSpec(arr.shape, lambda i: (0,) * arr.ndim)
    ins = (x2d, yc, ya, yg, gmix, wg, wb, wo, gffn, wr_hi, wr_lo, br)
    return pl.pallas_call(
        _merge_route_kernel,
        grid=(t // tm,),
        in_specs=[tok(d), tok(CONV_DIM), tok(ATTN_DIM), tok(GLA_V_DIM)] + [full(a) for a in ins[4:]],
        out_specs=(tok(d), pl.BlockSpec((tm * SUBLANES, LANES), lambda i: (i, 0)), tok(ROUTE_OUT),
                   pl.BlockSpec((1, ROUTE_LANES), lambda i: (0, 0))),
        out_shape=(jax.ShapeDtypeStruct((t, d), F32), jax.ShapeDtypeStruct((t * SUBLANES, LANES), F32),
                   jax.ShapeDtypeStruct((t, ROUTE_OUT), F32), jax.ShapeDtypeStruct((1, ROUTE_LANES), F32)),
        scratch_shapes=[pltpu.VMEM((1, ROUTE_LANES), F32)],
        compiler_params=_cparams(1),
        name="merge_route",
    )(*ins)


def _tile_slots(dest, tc):
    nt = dest.shape[0] // tc
    return dest.reshape(nt, tc, TOP_K).transpose(0, 2, 1).reshape(nt, 1, TOP_K * tc)


def _dispatch_kernel(last_blk_ref, dest_ref, h2_ref, xs_hbm, zeros_ref, sem):
    tc = h2_ref.shape[0] // SUBLANES

    @pl.when(pl.program_id(0) == 0)
    def _():
        zeros_ref[...] = jnp.zeros_like(zeros_ref)
        blk_rows = EXPERT_BLOCK * SUBLANES

        def zero_fill(first_row):
            return pltpu.make_async_copy(
                zeros_ref, xs_hbm.at[pl.ds(pl.multiple_of(first_row * SUBLANES, SUBLANES), blk_rows)], sem)

        for go in (lambda c: c.start(), lambda c: c.wait()):
            for e in range(N_EXPERTS):
                @pl.when(last_blk_ref[e] >= 0)
                def _():
                    go(zero_fill(last_blk_ref[e]))

            def unused(blk, carry):
                go(zero_fill(blk * EXPERT_BLOCK))
                return carry

            lax.fori_loop(last_blk_ref[N_EXPERTS], xs_hbm.shape[0] // blk_rows, unused, 0)

    def row_copy(r, kk):
        src = pl.multiple_of(r * SUBLANES, SUBLANES)
        dst = pl.multiple_of(dest_ref[0, kk * tc + r] * SUBLANES, SUBLANES)
        return pltpu.make_async_copy(h2_ref.at[pl.ds(src, SUBLANES)], xs_hbm.at[pl.ds(dst, SUBLANES)], sem)

    def issue(r, carry):
        row_copy(r, 0).start(priority=0)
        row_copy(r, 1).start(priority=1)
        return carry

    lax.fori_loop(0, tc, issue, 0, unroll=ROW_DMA_UNROLL)
    for _ in range(TOP_K):
        pltpu.make_async_copy(h2_ref, xs_hbm.at[pl.ds(0, tc * SUBLANES)], sem).wait()


def _dispatch(dest, h2_tiles, n_rows, last_blk):
    t = h2_tiles.shape[0] // SUBLANES
    tc = 256
    grid_spec = pltpu.PrefetchScalarGridSpec(
        num_scalar_prefetch=1,
        grid=(t // tc,),
        in_specs=[
            pl.BlockSpec((None, 1, TOP_K * tc), lambda i, lb: (i, 0, 0), memory_space=pltpu.SMEM),
            pl.BlockSpec((tc * SUBLANES, LANES), lambda i, lb: (i, 0)),
        ],
        out_specs=pl.BlockSpec(memory_space=pl.ANY),
        scratch_shapes=[pltpu.VMEM((EXPERT_BLOCK * SUBLANES, LANES), F32), pltpu.SemaphoreType.DMA(())],
    )
    return pl.pallas_call(
        _dispatch_kernel,
        grid_spec=grid_spec,
        out_shape=jax.ShapeDtypeStruct((n_rows * SUBLANES, LANES), F32),
        compiler_params=_cparams(1),
        name="moe_dispatch",
    )(last_blk, _tile_slots(dest, tc), h2_tiles)


def _expert_kernel(blk_expert_ref, n_used_ref, xs_ref, wg_ref, wu_ref, wd_ref, ys_ref,
                   wg_bf, wu_bf, wd_bf):
    i = pl.program_id(0)
    rb = xs_ref.shape[0] // SUBLANES
    new_expert = (i == 0) | (blk_expert_ref[i] != blk_expert_ref[jnp.maximum(i - 1, 0)])

    @pl.when(new_expert)
    def _():
        wg_bf[...] = wg_ref[...].astype(BF16)
        wu_bf[...] = wu_ref[...].astype(BF16)
        wd_bf[...] = wd_ref[...].astype(BF16)

    @pl.when(i < n_used_ref[0])
    def _():
        xb = jnp.concatenate(_tiles_to_rows(xs_ref, rb), axis=1).astype(BF16)
        gate = _dot(xb, wg_bf[...])
        up = _dot(xb, wu_bf[...])
        act = (gate * jax.nn.sigmoid(gate) * up).astype(BF16)
        _rows_to_tiles(ys_ref, _dot(act, wd_bf[...]))

    @pl.when(i >= n_used_ref[0])
    def _():
        ys_ref[...] = jnp.zeros_like(ys_ref)


def _experts(blk_expert, n_used, xs_tiles, wg, wu, wd, layer):
    d = SUBLANES * LANES
    n_blocks = blk_expert.shape[0]
    rb = EXPERT_BLOCK
    row_block = (rb * SUBLANES, LANES)
    grid_spec = pltpu.PrefetchScalarGridSpec(
        num_scalar_prefetch=2,
        grid=(n_blocks,),
        in_specs=[
            pl.BlockSpec(row_block, lambda i, be, nu: (jnp.minimum(i, nu[0] - 1), 0)),
            pl.BlockSpec((None, None, d, EXPERT_FF), lambda i, be, nu: (layer, be[i], 0, 0)),
            pl.BlockSpec((None, None, d, EXPERT_FF), lambda i, be, nu: (layer, be[i], 0, 0)),
            pl.BlockSpec((None, None, EXPERT_FF, d), lambda i, be, nu: (layer, be[i], 0, 0)),
        ],
        out_specs=pl.BlockSpec(row_block, lambda i, be, nu: (i, 0)),
        scratch_shapes=[pltpu.VMEM((d, EXPERT_FF), BF16), pltpu.VMEM((d, EXPERT_FF), BF16),
                        pltpu.VMEM((EXPERT_FF, d), BF16)],
    )
    return pl.pallas_call(
        _expert_kernel,
        grid_spec=grid_spec,
        out_shape=jax.ShapeDtypeStruct(xs_tiles.shape, F32),
        compiler_params=_cparams(1),
        name="moe_experts",
    )(blk_expert, n_used, xs_tiles, wg, wu, wd)


def _combine_kernel(dest_ref, x_ref, route_ref, ys_hbm, o_ref, buf, sem):
    tc = x_ref.shape[0]

    def row_copy(r, kk):
        src = pl.multiple_of(dest_ref[0, kk * tc + r] * SUBLANES, SUBLANES)
        dst = pl.multiple_of(r * SUBLANES, SUBLANES)
        return pltpu.make_async_copy(ys_hbm.at[pl.ds(src, SUBLANES)], buf.at[kk, pl.ds(dst, SUBLANES)], sem)

    def issue(r, carry):
        row_copy(r, 0).start(priority=0)
        row_copy(r, 1).start(priority=1)
        return carry

    lax.fori_loop(0, tc, issue, 0, unroll=ROW_DMA_UNROLL)
    for kk in range(TOP_K):
        pltpu.make_async_copy(ys_hbm.at[pl.ds(0, tc * SUBLANES)], buf.at[kk], sem).wait()
    route = route_ref[...]
    w1 = route[:, TOP_K:TOP_K + 1]
    w2 = route[:, TOP_K + 1:TOP_K + 2]
    y1 = _tiles_to_rows(buf.at[0], tc)
    y2 = _tiles_to_rows(buf.at[1], tc)
    for j in range(SUBLANES):
        cols = slice(j * LANES, (j + 1) * LANES)
        o_ref[:, cols] = x_ref[:, cols] + (y1[j] * w1 + y2[j] * w2)


def _combine(dest, x2d, route, ys_tiles):
    t, d = x2d.shape
    tc = 256
    return pl.pallas_call(
        _combine_kernel,
        grid=(t // tc,),
        in_specs=[
            pl.BlockSpec((None, 1, TOP_K * tc), lambda i: (i, 0, 0), memory_space=pltpu.SMEM),
            pl.BlockSpec((tc, d), lambda i: (i, 0)),
            pl.BlockSpec((tc, ROUTE_OUT), lambda i: (i, 0)),
            pl.BlockSpec(memory_space=pl.ANY),
        ],
        out_specs=pl.BlockSpec((tc, d), lambda i: (i, 0)),
        out_shape=jax.ShapeDtypeStruct((t, d), F32),
        scratch_shapes=[pltpu.VMEM((TOP_K, tc * SUBLANES, LANES), F32), pltpu.SemaphoreType.DMA(())],
        compiler_params=_cparams(1),
        name="moe_combine",
    )(_tile_slots(dest, tc), x2d, route, ys_tiles)


def _dispatch_plan(route, counts, t):
    e_ids = route[:, :TOP_K].astype(jnp.int32)
    rank = route[:, 2 * TOP_K:3 * TOP_K].astype(jnp.int32)
    counts = counts[0, N_GROUPS:N_GROUPS + N_EXPERTS].astype(jnp.int32)
    padded = ((counts + EXPERT_BLOCK - 1) // EXPERT_BLOCK) * EXPERT_BLOCK
    pad_end = jnp.cumsum(padded)
    pad_start = pad_end - padded
    onehot = e_ids[:, :, None] == jnp.arange(N_EXPERTS, dtype=jnp.int32)
    dest = rank + jnp.sum(jnp.where(onehot, pad_start, 0), axis=-1)
    n_blocks = -(-(t * TOP_K) // EXPERT_BLOCK) + N_EXPERTS
    blk_start = jnp.arange(n_blocks, dtype=jnp.int32) * EXPERT_BLOCK
    blk_expert = jnp.minimum(jnp.sum(blk_start[:, None] >= pad_end[None, :], axis=1), N_EXPERTS - 1)
    n_used = jnp.maximum(pad_end[-1:] // EXPERT_BLOCK, 1).astype(jnp.int32)
    last_blk = jnp.concatenate([jnp.where(padded > 0, pad_end - EXPERT_BLOCK, -1), n_used]).astype(jnp.int32)
    return blk_expert.astype(jnp.int32), n_used, dest, n_blocks * EXPERT_BLOCK, last_blk


def kernel(x, rel_bias_table, norm_mix, w_in, conv_w, conv_b, q_norm, k_norm, w_gla_alpha, b_gla_alpha,
           gla_out_norm, w_merge_gate, w_branch, w_out, norm_ffn, w_router_group, b_router_group,
           w_router_expert, b_router_expert, w_expert_gate, w_expert_up, w_expert_down):
    b, s, d = x.shape
    t = b * s
    depth = w_in.shape[0]
    assert s % MOBA_BLOCK == 0 and t % EXPERT_BLOCK == 0
    assert d == SUBLANES * LANES, "row-granular DMAs store each activation row as one (8, 128) tile"
    bias0, bias1 = _attn_biases(rel_bias_table)
    head_id = jnp.arange(ATTN_DIM) // ATTN_HEAD_DIM
    hsum = (head_id[:, None] == head_id[None, :]).astype(BF16)
    c3 = 3 * CONV_DIM
    a3 = c3 + 3 * ATTN_DIM
    g3 = a3 + 2 * GLA_QK_DIM + 2 * GLA_V_DIM
    for l in range(depth):
        w_l = w_in[l].astype(BF16)
        w_gz = jnp.pad(w_l[:, g3:], ((0, 0), (0, GZ_PAD - GLA_GATE_RANK)))
        w_alpha = jnp.pad(w_gla_alpha[l].astype(BF16), ((0, GZ_PAD - GLA_GATE_RANK), (0, 0)))
        yconv, q, k, v, kmean, gq, gk, gv, gr, gz = _inproj(
            x, norm_mix[l][None], w_l[:, :c3], w_l[:, c3:a3], w_l[:, a3:g3], w_gz,
            conv_w[l], conv_b[l][None],
            jnp.tile(q_norm[l], ATTN_HEADS)[None], jnp.tile(k_norm[l], ATTN_HEADS)[None], hsum)
        qp, kp = _select(q, k, kmean.reshape(b, s // MOBA_BLOCK, ATTN_DIM))
        yattn = _attention(qp, kp, v, bias0, bias1)
        ygla = _gla(gq, gk, gv, gr, gz, w_alpha, b_gla_alpha[l][None],
                    gla_out_norm[l][None])
        w_r = jnp.concatenate([w_router_group[l], w_router_expert[l]], axis=1)
        w_r = jnp.pad(w_r, ((0, 0), (0, ROUTE_LANES - w_r.shape[1])))
        b_r = jnp.pad(jnp.concatenate([b_router_group[l], b_router_expert[l]]),
                      (0, ROUTE_LANES - N_GROUPS - N_EXPERTS))[None]
        wr_hi, wr_lo = _split_bf16(w_r)
        xo, h2, route, counts = _merge_route(
            x.reshape(t, d), yconv.reshape(t, -1), yattn.reshape(t, -1), ygla.reshape(t, -1),
            norm_mix[l][None], w_merge_gate[l].astype(BF16), w_branch[l].astype(BF16),
            w_out[l].astype(BF16), norm_ffn[l][None], wr_hi, wr_lo, b_r)
        blk_expert, n_used, dest, n_rows, last_blk = _dispatch_plan(route, counts, t)
        xs = _dispatch(dest, h2, n_rows, last_blk)
        ys = _experts(blk_expert, n_used, xs, w_expert_gate, w_expert_up, w_expert_down, l)
        x = _combine(dest, xo, route, ys).reshape(b, s, d)
    return x
```

```python
import functools
import math

import jax
import jax.numpy as jnp
import numpy as np
from jax import lax
from jax.experimental import pallas as pl
from jax.experimental.pallas import tpu as pltpu

CONV_DIM = 512
CONV_WIDTH = 3
ATTN_HEADS = 8
ATTN_HEAD_DIM = 64
ATTN_DIM = ATTN_HEADS * ATTN_HEAD_DIM
MOBA_BLOCK = 256
MOBA_TOPK = 3
REL_BUCKETS = 32
REL_MAX_DISTANCE = 128
GLA_HEADS = 4
GLA_KEY_DIM = 64
GLA_VALUE_DIM = 128
GLA_QK_DIM = GLA_HEADS * GLA_KEY_DIM
GLA_V_DIM = GLA_HEADS * GLA_VALUE_DIM
GLA_GATE_RANK = 16
GLA_GATE_TEMP = 16.0
GLA_CHUNK = 64
N_GROUPS = 4
EXPERTS_PER_GROUP = 8
N_EXPERTS = N_GROUPS * EXPERTS_PER_GROUP
TOP_K = 2
EXPERT_FF = 512
EXPERT_BLOCK = 256
RMS_EPS = 1e-6

LANES = 128
VMEM_LIMIT_BYTES = 56 * 1024 * 1024

MASK_VALUE = -1e30
ROUTE_LANES = 128
ROUTE_OUT = 8
ATTN_FAR_CHUNK = 4
ATTN_TAIL = 5
ROW_DMA_UNROLL = 8
GZ_PAD = LANES

F32 = jnp.float32
BF16 = jnp.bfloat16


def _cparams(n_axes):
    return pltpu.CompilerParams(
        dimension_semantics=("arbitrary",) * n_axes,
        vmem_limit_bytes=VMEM_LIMIT_BYTES,
    )


def _rms(x, gain):
    return x * lax.rsqrt(jnp.mean(x * x, axis=-1, keepdims=True) + RMS_EPS) * gain


def _split_bf16(x):
    hi = x.astype(BF16)
    lo = (x - hi.astype(F32)).astype(BF16)
    return hi, lo


def _dot(a, b):
    return jnp.dot(a, b, preferred_element_type=F32)


def _dot_nt(a, b):
    return lax.dot_general(a, b, (((1,), (1,)), ((), ())), preferred_element_type=F32)


SUBLANES = 8


def _rows_to_tiles(dst_ref, x):
    n = x.shape[0]
    for j in range(SUBLANES):
        dst_ref[pl.ds(j, n, stride=SUBLANES), :] = x[:, j * LANES:(j + 1) * LANES]


def _tiles_to_rows(src_ref, n):
    return [src_ref[pl.ds(j, n, stride=SUBLANES), :] for j in range(SUBLANES)]


def _dot_tn(a, b):
    return lax.dot_general(a, b, (((0,), (0,)), ((), ())), preferred_element_type=F32)


def _inproj_kernel(x_ref, gmix_ref, wconv_ref, wattn_ref, wgla_ref, wgz_ref, convw_ref, convb_ref,
                   qn_ref, kn_ref, hsum_ref,
                   yconv_ref, q_ref, k_ref, v_ref, kmean_ref, gq_ref, gk_ref, gv_ref, gr_ref, gz_ref,
                   carry_ref):
    s_idx = pl.program_id(1)
    ts = x_ref.shape[0]
    h = _rms(x_ref[...], gmix_ref[...]).astype(BF16)

    c = _dot(h, wconv_ref[...])
    cb = c[:, :CONV_DIM]
    u = c[:, CONV_DIM:2 * CONV_DIM] * c[:, 2 * CONV_DIM:]

    @pl.when(s_idx == 0)
    def _():
        carry_ref[...] = jnp.zeros_like(carry_ref)

    prev = carry_ref[...]
    row = lax.broadcasted_iota(jnp.int32, u.shape, 0)
    u1 = pltpu.roll(u, 1, 0)
    u1 = jnp.where(row == 0, prev[7:8, :], u1)
    u2 = pltpu.roll(u, 2, 0)
    u2 = jnp.where(row == 0, prev[6:7, :], jnp.where(row == 1, prev[7:8, :], u2))
    carry_ref[...] = u[ts - 8:, :]
    y = convb_ref[...] + convw_ref[0:1, :] * u2
    y = y + convw_ref[1:2, :] * u1
    y = y + convw_ref[2:3, :] * u
    yconv_ref[...] = (cb * y).astype(BF16)

    a = _dot(h, wattn_ref[...])
    hsum = hsum_ref[...]

    def head_norm(t, gain):
        hi, lo = _split_bf16(t * t)
        ss = _dot(hi, hsum) + _dot(lo, hsum)
        return t * lax.rsqrt(ss * (1.0 / ATTN_HEAD_DIM) + RMS_EPS) * gain

    qn = head_norm(a[:, :ATTN_DIM], qn_ref[...])
    kn = head_norm(a[:, ATTN_DIM:2 * ATTN_DIM], kn_ref[...])
    q_ref[...] = (qn * (ATTN_HEAD_DIM ** -0.5)).astype(BF16)
    k_ref[...] = kn.astype(BF16)
    v_ref[...] = jnp.transpose(a[:, 2 * ATTN_DIM:]).astype(BF16)
    for j in range(ts // MOBA_BLOCK):
        kmean_ref[j] = jnp.mean(kn[j * MOBA_BLOCK:(j + 1) * MOBA_BLOCK], axis=0, keepdims=True)

    g = _dot(h, wgla_ref[...])
    gq_ref[...] = g[:, :GLA_QK_DIM]
    gk_ref[...] = g[:, GLA_QK_DIM:2 * GLA_QK_DIM]
    gv_ref[...] = g[:, 2 * GLA_QK_DIM:2 * GLA_QK_DIM + GLA_V_DIM]
    gr_ref[...] = g[:, 2 * GLA_QK_DIM + GLA_V_DIM:]
    gz_ref[...] = _dot(h, wgz_ref[...])


def _inproj(x, gmix, wconv, wattn, wgla, wgz, convw, convb, qn, kn, hsum):
    b, s, d = x.shape
    blocks_per_tile = 2
    ts = blocks_per_tile * MOBA_BLOCK
    assert s % ts == 0
    nb = s // MOBA_BLOCK
    tok = lambda w: pl.BlockSpec((None, ts, w), lambda bi, si: (bi, si, 0))
    full = lambda arr: pl.BlockSpec(arr.shape, lambda bi, si: (0,) * arr.ndim)
    out_shapes = (
        jax.ShapeDtypeStruct((b, s, CONV_DIM), BF16),
        jax.ShapeDtypeStruct((b, s, ATTN_DIM), BF16),
        jax.ShapeDtypeStruct((b, s, ATTN_DIM), BF16),
        jax.ShapeDtypeStruct((b, ATTN_DIM, s), BF16),
        jax.ShapeDtypeStruct((b, nb, 1, ATTN_DIM), F32),
        jax.ShapeDtypeStruct((b, s, GLA_QK_DIM), F32),
        jax.ShapeDtypeStruct((b, s, GLA_QK_DIM), F32),
        jax.ShapeDtypeStruct((b, s, GLA_V_DIM), F32),
        jax.ShapeDtypeStruct((b, s, GLA_V_DIM), F32),
        jax.ShapeDtypeStruct((b, s, GZ_PAD), F32),
    )
    out_specs = (
        tok(CONV_DIM), tok(ATTN_DIM), tok(ATTN_DIM),
        pl.BlockSpec((None, ATTN_DIM, ts), lambda bi, si: (bi, 0, si)),
        pl.BlockSpec((None, blocks_per_tile, 1, ATTN_DIM), lambda bi, si: (bi, si, 0, 0)),
        tok(GLA_QK_DIM), tok(GLA_QK_DIM), tok(GLA_V_DIM), tok(GLA_V_DIM), tok(GZ_PAD),
    )
    ins = (x, gmix, wconv, wattn, wgla, wgz, convw, convb, qn, kn, hsum)
    in_specs = [tok(d)] + [full(a) for a in ins[1:]]
    return pl.pallas_call(
        _inproj_kernel,
        grid=(b, s // ts),
        in_specs=in_specs,
        out_specs=out_specs,
        out_shape=out_shapes,
        scratch_shapes=[pltpu.VMEM((8, CONV_DIM), F32)],
        compiler_params=_cparams(2),
        name="inproj",
    )(*ins)


def _select_kernel(q_ref, k_ref, kmean_ref, qp_ref, kp_ref):
    own = pl.program_id(1)
    tq = q_ref.shape[0]
    nb = kmean_ref.shape[0]
    half = ATTN_HEAD_DIM
    lane = lax.broadcasted_iota(jnp.int32, (tq, LANES), 1)
    blk = lax.broadcasted_iota(jnp.int32, (nb, tq), 0).astype(F32)
    own_f = own.astype(F32)
    onehot = jnp.where(lane - half == own, 1.0, 0.0).astype(F32)
    kmean = kmean_ref[...].astype(BF16)
    for p in range(ATTN_HEADS // 2):
        qpair = q_ref[:, p * LANES:(p + 1) * LANES]
        kpair = k_ref[:, p * LANES:(p + 1) * LANES].astype(F32)
        kmpair = kmean[:, p * LANES:(p + 1) * LANES]
        qpair_f = qpair.astype(F32)
        for sub in range(2):
            h = 2 * p + sub
            lane_sel = (lane >= sub * half) & (lane < (sub + 1) * half)
            qh = jnp.where(lane_sel, qpair_f, 0.0).astype(BF16)
            gate_t = _dot_nt(kmpair, qh)
            g = jnp.where(blk < own_f, gate_t, -jnp.inf)
            alive = jnp.ones((nb, tq), F32)
            sel = jnp.where(blk == own_f, 1.0, 0.0)
            for r in range(MOBA_TOPK):
                ga = jnp.where(alive > 0.0, g, -jnp.inf)
                mx = jnp.max(ga, axis=0, keepdims=True)
                cand = jnp.where((alive > 0.0) & (g == mx), blk, 2.0 * LANES)
                first = jnp.min(cand, axis=0, keepdims=True)
                hit = blk == first
                sel = jnp.where(hit, jnp.maximum(sel, jnp.where(own_f > r, 1.0, 0.0)), sel)
                alive = jnp.where(hit, 0.0, alive)
            m_t = jnp.where(sel > 0.0, 0.0, MASK_VALUE).astype(F32)
            m_t = jnp.concatenate([m_t, jnp.zeros((LANES - nb, tq), F32)], axis=0)
            m = jnp.transpose(m_t)
            m = pltpu.roll(m, half, 1)
            m = jnp.where((lane >= half) & (lane < half + nb), m, 0.0)
            qs = qpair_f if sub == 0 else pltpu.roll(qpair_f, half, 1)
            ks = kpair if sub == 0 else pltpu.roll(kpair, half, 1)
            qp_ref[h] = jnp.where(lane < half, qs, m).astype(BF16)
            kp_ref[h] = jnp.where(lane < half, ks, onehot).astype(BF16)


def _select(q, k, kmean):
    b, s, _ = q.shape
    tq = MOBA_BLOCK
    nb = s // tq
    assert nb <= LANES - ATTN_HEAD_DIM, "block one-hot must fit beside the head dim in one lane tile"
    tok = pl.BlockSpec((None, tq, ATTN_DIM), lambda bi, si: (bi, si, 0))
    slab = pl.BlockSpec((None, ATTN_HEADS, tq, LANES), lambda bi, si: (bi, 0, si, 0))
    shp = jax.ShapeDtypeStruct((b, ATTN_HEADS, s, LANES), BF16)
    return pl.pallas_call(
        _select_kernel,
        grid=(b, nb),
        in_specs=[tok, tok, pl.BlockSpec((None, nb, ATTN_DIM), lambda bi, si: (bi, 0, 0))],
        out_specs=(slab, slab),
        out_shape=(shp, shp),
        compiler_params=_cparams(2),
        name="moba_select",
    )(q, k, kmean)


def _attn_kernel(qp_ref, kp_ref, vt_ref, bias0_ref, bias1_ref, o_ref, m_ref, l_ref, acc_ref,
                 sa_ref, sb_ref, st_ref):
    own = pl.program_id(2)
    tq = qp_ref.shape[1]
    far_rows = ATTN_FAR_CHUNK * tq
    sub = lax.broadcasted_iota(jnp.int32, (LANES, tq), 0)
    n_far = jnp.maximum(own - 1, 0) // ATTN_FAR_CHUNK

    def scores(hh, start, n):
        return _dot_nt(kp_ref[hh, pl.ds(start, n * tq), :], qp_ref[hh])

    def far_scores(dst_ref, chunk):
        start = pl.multiple_of(chunk * far_rows, far_rows)
        for hh in range(2):
            dst_ref[hh] = scores(hh, start, ATTN_FAR_CHUNK)

    def softmax_pv(hh, s_ref, start, n):
        blocks = [s_ref[hh, w * tq:(w + 1) * tq, :] for w in range(n)]
        m_prev = m_ref[hh]
        m_new = jnp.maximum(m_prev, jnp.max(functools.reduce(jnp.maximum, blocks), axis=0, keepdims=True))
        alpha = jnp.exp(m_prev - m_new)
        ps = [jnp.exp(blk - m_new) for blk in blocks]
        l_ref[hh] = alpha * l_ref[hh] + jnp.sum(functools.reduce(jnp.add, ps), axis=0, keepdims=True)
        pt = jnp.concatenate([x.astype(BF16) for x in ps], axis=0)
        acc_ref[hh] = alpha * acc_ref[hh] + _dot(vt_ref[:, pl.ds(start, n * tq)], pt)
        m_ref[hh] = m_new

    for hh in range(2):
        m_ref[hh] = jnp.full((1, tq), MASK_VALUE, F32)
        l_ref[hh] = jnp.zeros((1, tq), F32)
        acc_ref[hh] = jnp.zeros((LANES, tq), F32)

    left = jnp.maximum(own - 1, 0) % ATTN_FAR_CHUNK
    tail_start = pl.multiple_of(n_far * far_rows, tq)

    def tail_variants(fn):
        @pl.when(own == 0)
        def _():
            fn(1, 0)

        for n_left in range(ATTN_FAR_CHUNK):
            @pl.when((own >= 1) & (left == n_left))
            def _():
                fn(n_left + 2, tail_start)

    def tail_scores(n, start):
        for hh in range(2):
            s = scores(hh, start, n)
            for w in range(n):
                sw = s[w * tq:(w + 1) * tq, :]
                if w == n - 1:
                    sw = sw + bias0_ref[hh]
                elif w == n - 2:
                    sw = sw + bias1_ref[hh]
                st_ref[hh, w * tq:(w + 1) * tq, :] = sw

    tail_variants(tail_scores)

    @pl.when(n_far > 0)
    def _():
        far_scores(sa_ref, 0)

    def far_pair(j, carry):
        c0 = 2 * j
        far_scores(sb_ref, jnp.minimum(c0 + 1, n_far - 1))
        for hh in range(2):
            softmax_pv(hh, sa_ref, pl.multiple_of(c0 * far_rows, far_rows), ATTN_FAR_CHUNK)

        @pl.when(c0 + 1 < n_far)
        def _():
            far_scores(sa_ref, jnp.minimum(c0 + 2, n_far - 1))
            for hh in range(2):
                softmax_pv(hh, sb_ref, pl.multiple_of((c0 + 1) * far_rows, far_rows), ATTN_FAR_CHUNK)

        return carry

    lax.fori_loop(0, (n_far + 1) // 2, far_pair, 0)

    def tail_softmax(n, start):
        for hh in range(2):
            softmax_pv(hh, st_ref, start, n)

    tail_variants(tail_softmax)

    outs = [acc_ref[hh] / l_ref[hh] for hh in range(2)]
    o_t = jnp.where(sub < ATTN_HEAD_DIM, outs[0], outs[1])
    o_ref[...] = jnp.transpose(o_t).astype(BF16)


def _attention(qp, kp, vt, bias0, bias1):
    b, nh, s, _ = qp.shape
    tq = MOBA_BLOCK
    nq = s // tq
    return pl.pallas_call(
        _attn_kernel,
        grid=(b, nh // 2, nq),
        in_specs=[
            pl.BlockSpec((None, 2, tq, LANES), lambda bi, pi, qi: (bi, pi, qi, 0)),
            pl.BlockSpec((None, 2, s, LANES), lambda bi, pi, qi: (bi, pi, 0, 0)),
            pl.BlockSpec((None, LANES, s), lambda bi, pi, qi: (bi, pi, 0)),
            pl.BlockSpec((2, tq, tq), lambda bi, pi, qi: (pi, 0, 0)),
            pl.BlockSpec((2, tq, tq), lambda bi, pi, qi: (pi, 0, 0)),
        ],
        out_specs=pl.BlockSpec((None, tq, LANES), lambda bi, pi, qi: (bi, qi, pi)),
        out_shape=jax.ShapeDtypeStruct((b, s, ATTN_DIM), BF16),
        scratch_shapes=[
            pltpu.VMEM((2, 1, tq), F32),
            pltpu.VMEM((2, 1, tq), F32),
            pltpu.VMEM((2, LANES, tq), F32),
            pltpu.VMEM((2, ATTN_FAR_CHUNK * tq, tq), F32),
            pltpu.VMEM((2, ATTN_FAR_CHUNK * tq, tq), F32),
            pltpu.VMEM((2, ATTN_TAIL * tq, tq), F32),
        ],
        compiler_params=_cparams(3),
        name="moba_attention",
    )(qp, kp, vt, bias0, bias1)


def _t5_bucket(rel):
    n = jnp.maximum(rel, 0)
    max_exact = REL_BUCKETS // 2
    scaled = (jnp.log(jnp.maximum(n, max_exact).astype(F32) / max_exact)
              / math.log(REL_MAX_DISTANCE / max_exact))
    large = jnp.minimum(max_exact + (scaled * (REL_BUCKETS - max_exact)).astype(jnp.int32), REL_BUCKETS - 1)
    return jnp.where(n < max_exact, n, large)


def _attn_biases(rel_bias_table):
    table = rel_bias_table.astype(F32).T
    pos = jnp.arange(MOBA_BLOCK)
    rel0 = pos[None, :] - pos[:, None]
    far = table[:, REL_BUCKETS - 1][:, None, None]

    rel = jnp.arange(1 - 2 * MOBA_BLOCK, 2 * MOBA_BLOCK)
    hit = _t5_bucket(rel)[None, :, None] == jnp.arange(REL_BUCKETS)
    by_rel = jnp.sum(jnp.where(hit, table[:, None, :], 0.0), axis=-1)
    zero = 2 * MOBA_BLOCK - 1

    def tile(offset):
        rows = [by_rel[:, zero + offset - kk:zero + offset - kk + MOBA_BLOCK] for kk in range(MOBA_BLOCK)]
        return jnp.stack(rows, axis=1)

    bias0 = jnp.where(rel0[None] >= 0, tile(0) - far, MASK_VALUE)
    bias1 = tile(MOBA_BLOCK) - far
    return bias0, bias1


def _gla_kernel(gq_ref, gk_ref, gv_ref, gr_ref, gz_ref, wa_ref, ba_ref, on_ref, y_ref, state_ref):
    s_idx = pl.program_id(1)
    tg = gq_ref.shape[0]
    dk, dv, ck = GLA_KEY_DIM, GLA_VALUE_DIM, GLA_CHUNK

    @pl.when(s_idx == 0)
    def _():
        state_ref[...] = jnp.zeros_like(state_ref)

    z = gz_ref[...].astype(BF16)
    log_a = jax.nn.log_sigmoid(_dot(z, wa_ref[...]) + ba_ref[...]) / GLA_GATE_TEMP
    row = lax.broadcasted_iota(jnp.int32, log_a.shape, 0) % ck
    bcum = log_a
    shift = 1
    while shift < ck:
        bcum = bcum + jnp.where(row >= shift, pltpu.roll(bcum, shift, 0), 0.0)
        shift *= 2
    q = gq_ref[...] * (dk ** -0.5)
    k = gk_ref[...]
    tri = (lax.broadcasted_iota(jnp.int32, (ck, ck), 0) >= lax.broadcasted_iota(jnp.int32, (ck, ck), 1))
    states = [state_ref[h] for h in range(GLA_HEADS)]
    for c in range(tg // ck):
        rows = slice(c * ck, (c + 1) * ck)
        for h in range(GLA_HEADS):
            kcols = slice(h * dk, (h + 1) * dk)
            vcols = slice(h * dv, (h + 1) * dv)
            bc = bcum[rows, kcols]
            btot = bc[ck - 1:ck, :]
            qe = (q[rows, kcols] * jnp.exp(bc)).astype(BF16)
            ke = (k[rows, kcols] * jnp.exp(-bc)).astype(BF16)
            kd = (k[rows, kcols] * jnp.exp(btot - bc)).astype(BF16)
            vc = gv_ref[rows, vcols].astype(BF16)
            att = jnp.where(tri, _dot_nt(qe, ke), 0.0).astype(BF16)
            st = states[h]
            o = _dot(att, vc) + _dot_nt(qe, st.astype(BF16))
            states[h] = st * jnp.exp(btot) + _dot_tn(vc, kd)
            o = _rms(o, on_ref[...])
            r = gr_ref[rows, vcols]
            y_ref[rows, vcols] = (o * (r * jax.nn.sigmoid(r))).astype(BF16)
    for h in range(GLA_HEADS):
        state_ref[h] = states[h]


def _gla(gq, gk, gv, gr, gz, wa, ba, on):
    b, s, _ = gq.shape
    tg = 256
    tok = lambda w: pl.BlockSpec((None, tg, w), lambda bi, si: (bi, si, 0))
    full = lambda arr: pl.BlockSpec(arr.shape, lambda bi, si: (0,) * arr.ndim)
    return pl.pallas_call(
        _gla_kernel,
        grid=(b, s // tg),
        in_specs=[tok(GLA_QK_DIM), tok(GLA_QK_DIM), tok(GLA_V_DIM), tok(GLA_V_DIM), tok(GZ_PAD),
                  full(wa), full(ba), full(on)],
        out_specs=tok(GLA_V_DIM),
        out_shape=jax.ShapeDtypeStruct((b, s, GLA_V_DIM), BF16),
        scratch_shapes=[pltpu.VMEM((GLA_HEADS, GLA_VALUE_DIM, GLA_KEY_DIM), F32)],
        compiler_params=_cparams(2),
        name="gla",
    )(gq, gk, gv, gr, gz, wa, ba, on)


def _merge_route_kernel(x_ref, yc_ref, ya_ref, yg_ref, gmix_ref, wg_ref, wb_ref, wo_ref, gffn_ref,
                        wr_hi_ref, wr_lo_ref, br_ref, xo_ref, h2_ref, route_ref, counts_ref, run_ref):
    x = x_ref[...]
    h = _rms(x, gmix_ref[...]).astype(BF16)
    merged = None
    for n, y_ref in enumerate((yc_ref, ya_ref, yg_ref)):
        term = jax.nn.sigmoid(_dot(h, wg_ref[n])) * _dot(y_ref[...], wb_ref[n])
        merged = term if merged is None else merged + term
    xo = x + _dot(merged.astype(BF16), wo_ref[...])
    xo_ref[...] = xo
    h2 = _rms(xo, gffn_ref[...])
    _rows_to_tiles(h2_ref, h2)

    h_hi, h_lo = _split_bf16(h2)
    logits = (_dot(h_hi, wr_hi_ref[...]) + _dot(h_lo, wr_hi_ref[...]) + _dot(h_hi, wr_lo_ref[...])
              + br_ref[...])
    lane = lax.broadcasted_iota(jnp.int32, logits.shape, 1).astype(F32)
    big = 4.0 * ROUTE_LANES
    lg = jnp.where(lane < N_GROUPS, logits, -jnp.inf)
    gmax = jnp.max(lg, axis=1, keepdims=True)
    gidx = jnp.min(jnp.where(lg == gmax, lane, big), axis=1, keepdims=True)
    p_group_top = 1.0 / jnp.sum(jnp.exp(lg - gmax), axis=1, keepdims=True)
    lo_lane = N_GROUPS + gidx * EXPERTS_PER_GROUP
    le = jnp.where((lane >= lo_lane) & (lane < lo_lane + EXPERTS_PER_GROUP), logits, -jnp.inf)
    emax = jnp.max(le, axis=1, keepdims=True)
    i1 = jnp.min(jnp.where(le == emax, lane, big), axis=1, keepdims=True)
    esum = jnp.sum(jnp.exp(le - emax), axis=1, keepdims=True)
    le2 = jnp.where(lane == i1, -jnp.inf, le)
    emax2 = jnp.max(le2, axis=1, keepdims=True)
    i2 = jnp.min(jnp.where(le2 == emax2, lane, big), axis=1, keepdims=True)
    p1 = 1.0 / esum
    p2 = jnp.exp(emax2 - emax) / esum
    psum = p1 + p2
    w1 = p_group_top * p1 / psum
    w2 = p_group_top * p2 / psum
    e1 = i1 - N_GROUPS
    e2 = i2 - N_GROUPS

    @pl.when(pl.program_id(0) == 0)
    def _():
        run_ref[...] = jnp.zeros_like(run_ref)

    tm = x.shape[0]
    oh1 = jnp.where(lane == i1, 1.0, 0.0)
    oh2 = jnp.where(lane == i2, 1.0, 0.0)
    ohs = oh1 + oh2
    lower = (lax.broadcasted_iota(jnp.int32, (tm, tm), 0) > lax.broadcasted_iota(jnp.int32, (tm, tm), 1))
    before = _dot(jnp.where(lower, 1.0, 0.0).astype(BF16), ohs.astype(BF16)) + run_ref[...]
    rank1 = jnp.sum(oh1 * before, axis=1, keepdims=True)
    rank2 = jnp.sum(oh2 * before, axis=1, keepdims=True)
    run_ref[...] = run_ref[...] + jnp.sum(ohs, axis=0, keepdims=True)
    counts_ref[...] = run_ref[...]

    rl = lax.broadcasted_iota(jnp.int32, (tm, ROUTE_OUT), 1)
    rec = jnp.zeros((tm, ROUTE_OUT), F32)
    for slot, val in enumerate((e1, e2, w1, w2, rank1, rank2)):
        rec = jnp.where(rl == slot, val, rec)
    route_ref[...] = rec


def _merge_route(x2d, yc, ya, yg, gmix, wg, wb, wo, gffn, wr_hi, wr_lo, br):
    t, d = x2d.shape
    tm = 512
    tok = lambda w: pl.BlockSpec((tm, w), lambda i: (i, 0))
    full = lambda arr: pl.BlockSpec(arr.shape, lambda i: (0,) * arr.ndim)
    ins = (x2d, yc, ya, yg, gmix, wg, wb, wo, gffn, wr_hi, wr_lo, br)
    return pl.pallas_call(
        _merge_route_kernel,
        grid=(t // tm,),
        in_specs=[tok(d), tok(CONV_DIM), tok(ATTN_DIM), tok(GLA_V_DIM)] + [full(a) for a in ins[4:]],
        out_specs=(tok(d), pl.BlockSpec((tm * SUBLANES, LANES), lambda i: (i, 0)), tok(ROUTE_OUT),
                   pl.BlockSpec((1, ROUTE_LANES), lambda i: (0, 0))),
        out_shape=(jax.ShapeDtypeStruct((t, d), F32), jax.ShapeDtypeStruct((t * SUBLANES, LANES), F32),
                   jax.ShapeDtypeStruct((t, ROUTE_OUT), F32), jax.ShapeDtypeStruct((1, ROUTE_LANES), F32)),
        scratch_shapes=[pltpu.VMEM((1, ROUTE_LANES), F32)],
        compiler_params=_cparams(1),
        name="merge_route",
    )(*ins)


def _tile_slots(dest, tc):
    nt = dest.shape[0] // tc
    return dest.reshape(nt, tc, TOP_K).transpose(0, 2, 1).reshape(nt, 1, TOP_K * tc)


def _dispatch_kernel(last_blk_ref, dest_ref, h2_ref, xs_hbm, zeros_ref, sem):
    tc = h2_ref.shape[0] // SUBLANES

    @pl.when(pl.program_id(0) == 0)
    def _():
        zeros_ref[...] = jnp.zeros_like(zeros_ref)
        blk_rows = EXPERT_BLOCK * SUBLANES

        def zero_fill(first_row):
            return pltpu.make_async_copy(
                zeros_ref, xs_hbm.at[pl.ds(pl.multiple_of(first_row * SUBLANES, SUBLANES), blk_rows)], sem)

        for go in (lambda c: c.start(), lambda c: c.wait()):
            for e in range(N_EXPERTS):
                @pl.when(last_blk_ref[e] >= 0)
                def _():
                    go(zero_fill(last_blk_ref[e]))

            def unused(blk, carry):
                go(zero_fill(blk * EXPERT_BLOCK))
                return carry

            lax.fori_loop(last_blk_ref[N_EXPERTS], xs_hbm.shape[0] // blk_rows, unused, 0)

    def row_copy(r, kk):
        src = pl.multiple_of(r * SUBLANES, SUBLANES)
        dst = pl.multiple_of(dest_ref[0, kk * tc + r] * SUBLANES, SUBLANES)
        return pltpu.make_async_copy(h2_ref.at[pl.ds(src, SUBLANES)], xs_hbm.at[pl.ds(dst, SUBLANES)], sem)

    def issue(r, carry):
        row_copy(r, 0).start(priority=0)
        row_copy(r, 1).start(priority=1)
        return carry

    lax.fori_loop(0, tc, issue, 0, unroll=ROW_DMA_UNROLL)
    for _ in range(TOP_K):
        pltpu.make_async_copy(h2_ref, xs_hbm.at[pl.ds(0, tc * SUBLANES)], sem).wait()


def _dispatch(dest, h2_tiles, n_rows, last_blk):
    t = h2_tiles.shape[0] // SUBLANES
    tc = 256
    grid_spec = pltpu.PrefetchScalarGridSpec(
        num_scalar_prefetch=1,
        grid=(t // tc,),
        in_specs=[
            pl.BlockSpec((None, 1, TOP_K * tc), lambda i, lb: (i, 0, 0), memory_space=pltpu.SMEM),
            pl.BlockSpec((tc * SUBLANES, LANES), lambda i, lb: (i, 0)),
        ],
        out_specs=pl.BlockSpec(memory_space=pl.ANY),
        scratch_shapes=[pltpu.VMEM((EXPERT_BLOCK * SUBLANES, LANES), F32), pltpu.SemaphoreType.DMA(())],
    )
    return pl.pallas_call(
        _dispatch_kernel,
        grid_spec=grid_spec,
        out_shape=jax.ShapeDtypeStruct((n_rows * SUBLANES, LANES), F32),
        compiler_params=_cparams(1),
        name="moe_dispatch",
    )(last_blk, _tile_slots(dest, tc), h2_tiles)


def _expert_kernel(blk_expert_ref, n_used_ref, xs_ref, wg_ref, wu_ref, wd_ref, ys_ref,
                   wg_bf, wu_bf, wd_bf):
    i = pl.program_id(0)
    rb = xs_ref.shape[0] // SUBLANES
    new_expert = (i == 0) | (blk_expert_ref[i] != blk_expert_ref[jnp.maximum(i - 1, 0)])

    @pl.when(new_expert)
    def _():
        wg_bf[...] = wg_ref[...].astype(BF16)
        wu_bf[...] = wu_ref[...].astype(BF16)
        wd_bf[...] = wd_ref[...].astype(BF16)

    @pl.when(i < n_used_ref[0])
    def _():
        xb = jnp.concatenate(_tiles_to_rows(xs_ref, rb), axis=1).astype(BF16)
        gate = _dot(xb, wg_bf[...])
        up = _dot(xb, wu_bf[...])
        act = (gate * jax.nn.sigmoid(gate) * up).astype(BF16)
        _rows_to_tiles(ys_ref, _dot(act, wd_bf[...]))

    @pl.when(i >= n_used_ref[0])
    def _():
        ys_ref[...] = jnp.zeros_like(ys_ref)


def _experts(blk_expert, n_used, xs_tiles, wg, wu, wd, layer):
    d = SUBLANES * LANES
    n_blocks = blk_expert.shape[0]
    rb = EXPERT_BLOCK
    row_block = (rb * SUBLANES, LANES)
    grid_spec = pltpu.PrefetchScalarGridSpec(
        num_scalar_prefetch=2,
        grid=(n_blocks,),
        in_specs=[
            pl.BlockSpec(row_block, lambda i, be, nu: (jnp.minimum(i, nu[0] - 1), 0)),
            pl.BlockSpec((None, None, d, EXPERT_FF), lambda i, be, nu: (layer, be[i], 0, 0)),
            pl.BlockSpec((None, None, d, EXPERT_FF), lambda i, be, nu: (layer, be[i], 0, 0)),
            pl.BlockSpec((None, None, EXPERT_FF, d), lambda i, be, nu: (layer, be[i], 0, 0)),
        ],
        out_specs=pl.BlockSpec(row_block, lambda i, be, nu: (i, 0)),
        scratch_shapes=[pltpu.VMEM((d, EXPERT_FF), BF16), pltpu.VMEM((d, EXPERT_FF), BF16),
                        pltpu.VMEM((EXPERT_FF, d), BF16)],
    )
    return pl.pallas_call(
        _expert_kernel,
        grid_spec=grid_spec,
        out_shape=jax.ShapeDtypeStruct(xs_tiles.shape, F32),
        compiler_params=_cparams(1),
        name="moe_experts",
    )(blk_expert, n_used, xs_tiles, wg, wu, wd)


def _combine_kernel(dest_ref, x_ref, route_ref, ys_hbm, o_ref, buf, sem):
    tc = x_ref.shape[0]

    def row_copy(r, kk):
        src = pl.multiple_of(dest_ref[0, kk * tc + r] * SUBLANES, SUBLANES)
        dst = pl.multiple_of(r * SUBLANES, SUBLANES)
        return pltpu.make_async_copy(ys_hbm.at[pl.ds(src, SUBLANES)], buf.at[kk, pl.ds(dst, SUBLANES)], sem)

    def issue(r, carry):
        row_copy(r, 0).start(priority=0)
        row_copy(r, 1).start(priority=1)
        return carry

    lax.fori_loop(0, tc, issue, 0, unroll=ROW_DMA_UNROLL)
    for kk in range(TOP_K):
        pltpu.make_async_copy(ys_hbm.at[pl.ds(0, tc * SUBLANES)], buf.at[kk], sem).wait()
    route = route_ref[...]
    w1 = route[:, TOP_K:TOP_K + 1]
    w2 = route[:, TOP_K + 1:TOP_K + 2]
    y1 = _tiles_to_rows(buf.at[0], tc)
    y2 = _tiles_to_rows(buf.at[1], tc)
    for j in range(SUBLANES):
        cols = slice(j * LANES, (j + 1) * LANES)
        o_ref[:, cols] = x_ref[:, cols] + (y1[j] * w1 + y2[j] * w2)


def _combine(dest, x2d, route, ys_tiles):
    t, d = x2d.shape
    tc = 256
    return pl.pallas_call(
        _combine_kernel,
        grid=(t // tc,),
        in_specs=[
            pl.BlockSpec((None, 1, TOP_K * tc), lambda i: (i, 0, 0), memory_space=pltpu.SMEM),
            pl.BlockSpec((tc, d), lambda i: (i, 0)),
            pl.BlockSpec((tc, ROUTE_OUT), lambda i: (i, 0)),
            pl.BlockSpec(memory_space=pl.ANY),
        ],
        out_specs=pl.BlockSpec((tc, d), lambda i: (i, 0)),
        out_shape=jax.ShapeDtypeStruct((t, d), F32),
        scratch_shapes=[pltpu.VMEM((TOP_K, tc * SUBLANES, LANES), F32), pltpu.SemaphoreType.DMA(())],
        compiler_params=_cparams(1),
        name="moe_combine",
    )(_tile_slots(dest, tc), x2d, route, ys_tiles)


def _dispatch_plan(route, counts, t):
    e_ids = route[:, :TOP_K].astype(jnp.int32)
    rank = route[:, 2 * TOP_K:3 * TOP_K].astype(jnp.int32)
    counts = counts[0, N_GROUPS:N_GROUPS + N_EXPERTS].astype(jnp.int32)
    padded = ((counts + EXPERT_BLOCK - 1) // EXPERT_BLOCK) * EXPERT_BLOCK
    pad_end = jnp.cumsum(padded)
    pad_start = pad_end - padded
    onehot = e_ids[:, :, None] == jnp.arange(N_EXPERTS, dtype=jnp.int32)
    dest = rank + jnp.sum(jnp.where(onehot, pad_start, 0), axis=-1)
    n_blocks = -(-(t * TOP_K) // EXPERT_BLOCK) + N_EXPERTS
    blk_start = jnp.arange(n_blocks, dtype=jnp.int32) * EXPERT_BLOCK
    blk_expert = jnp.minimum(jnp.sum(blk_start[:, None] >= pad_end[None, :], axis=1), N_EXPERTS - 1)
    n_used = jnp.maximum(pad_end[-1:] // EXPERT_BLOCK, 1).astype(jnp.int32)
    last_blk = jnp.concatenate([jnp.where(padded > 0, pad_end - EXPERT_BLOCK, -1), n_used]).astype(jnp.int32)
    return blk_expert.astype(jnp.int32), n_used, dest, n_blocks * EXPERT_BLOCK, last_blk


def kernel(x, rel_bias_table, norm_mix, w_in, conv_w, conv_b, q_norm, k_norm, w_gla_alpha, b_gla_alpha,
           gla_out_norm, w_merge_gate, w_branch, w_out, norm_ffn, w_router_group, b_router_group,
           w_router_expert, b_router_expert, w_expert_gate, w_expert_up, w_expert_down):
    b, s, d = x.shape
    t = b * s
    depth = w_in.shape[0]
    assert s % MOBA_BLOCK == 0 and t % EXPERT_BLOCK == 0
    assert d == SUBLANES * LANES, "row-granular DMAs store each activation row as one (8, 128) tile"
    bias0, bias1 = _attn_biases(rel_bias_table)
    head_id = jnp.arange(ATTN_DIM) // ATTN_HEAD_DIM
    hsum = (head_id[:, None] == head_id[None, :]).astype(BF16)
    c3 = 3 * CONV_DIM
    a3 = c3 + 3 * ATTN_DIM
    g3 = a3 + 2 * GLA_QK_DIM + 2 * GLA_V_DIM
    for l in range(depth):
        w_l = w_in[l].astype(BF16)
        w_gz = jnp.pad(w_l[:, g3:], ((0, 0), (0, GZ_PAD - GLA_GATE_RANK)))
        w_alpha = jnp.pad(w_gla_alpha[l].astype(BF16), ((0, GZ_PAD - GLA_GATE_RANK), (0, 0)))
        yconv, q, k, v, kmean, gq, gk, gv, gr, gz = _inproj(
            x, norm_mix[l][None], w_l[:, :c3], w_l[:, c3:a3], w_l[:, a3:g3], w_gz,
            conv_w[l], conv_b[l][None],
            jnp.tile(q_norm[l], ATTN_HEADS)[None], jnp.tile(k_norm[l], ATTN_HEADS)[None], hsum)
        qp, kp = _select(q, k, kmean.reshape(b, s // MOBA_BLOCK, ATTN_DIM))
        yattn = _attention(qp, kp, v, bias0, bias1)
        ygla = _gla(gq, gk, gv, gr, gz, w_alpha, b_gla_alpha[l][None],
                    gla_out_norm[l][None])
        w_r = jnp.concatenate([w_router_group[l], w_router_expert[l]], axis=1)
        w_r = jnp.pad(w_r, ((0, 0), (0, ROUTE_LANES - w_r.shape[1])))
        b_r = jnp.pad(jnp.concatenate([b_router_group[l], b_router_expert[l]]),
                      (0, ROUTE_LANES - N_GROUPS - N_EXPERTS))[None]
        wr_hi, wr_lo = _split_bf16(w_r)
        xo, h2, route, counts = _merge_route(
            x.reshape(t, d), yconv.reshape(t, -1), yattn.reshape(t, -1), ygla.reshape(t, -1),
            norm_mix[l][None], w_merge_gate[l].astype(BF16), w_branch[l].astype(BF16),
            w_out[l].astype(BF16), norm_ffn[l][None], wr_hi, wr_lo, b_r)
        blk_expert, n_used, dest, n_rows, last_blk = _dispatch_plan(route, counts, t)
        xs = _dispatch(dest, h2, n_rows, last_blk)
        ys = _experts(blk_expert, n_used, xs, w_expert_gate, w_expert_up, w_expert_down, l)
        x = _combine(dest, xo, route, ys).reshape(b, s, d)
    return x
```

```python
import functools
import math

import jax
import jax.numpy as jnp
import numpy as np
from jax import lax
from jax.experimental import pallas as pl
from jax.experimental.pallas import tpu as pltpu

CONV_DIM = 512
CONV_WIDTH = 3
ATTN_HEADS = 8
ATTN_HEAD_DIM = 64
ATTN_DIM = ATTN_HEADS * ATTN_HEAD_DIM
MOBA_BLOCK = 256
MOBA_TOPK = 3
REL_BUCKETS = 32
REL_MAX_DISTANCE = 128
GLA_HEADS = 4
GLA_KEY_DIM = 64
GLA_VALUE_DIM = 128
GLA_QK_DIM = GLA_HEADS * GLA_KEY_DIM
GLA_V_DIM = GLA_HEADS * GLA_VALUE_DIM
GLA_GATE_RANK = 16
GLA_GATE_TEMP = 16.0
GLA_CHUNK = 64
N_GROUPS = 4
EXPERTS_PER_GROUP = 8
N_EXPERTS = N_GROUPS * EXPERTS_PER_GROUP
TOP_K = 2
EXPERT_FF = 512
EXPERT_BLOCK = 256
RMS_EPS = 1e-6

LANES = 128
VMEM_LIMIT_BYTES = 56 * 1024 * 1024

MASK_VALUE = -1e30
ROUTE_LANES = 128
ROUTE_OUT = 8
ATTN_FAR_CHUNK = 4
ATTN_TAIL = 5
ROW_DMA_UNROLL = 8
GZ_PAD = LANES

F32 = jnp.float32
BF16 = jnp.bfloat16


def _cparams(n_axes):
    return pltpu.CompilerParams(
        dimension_semantics=("arbitrary",) * n_axes,
        vmem_limit_bytes=VMEM_LIMIT_BYTES,
    )


def _rms(x, gain):
    return x * lax.rsqrt(jnp.mean(x * x, axis=-1, keepdims=True) + RMS_EPS) * gain


def _split_bf16(x):
    hi = x.astype(BF16)
    lo = (x - hi.astype(F32)).astype(BF16)
    return hi, lo


def _dot(a, b):
    return jnp.dot(a, b, preferred_element_type=F32)


def _dot_nt(a, b):
    return lax.dot_general(a, b, (((1,), (1,)), ((), ())), preferred_element_type=F32)


SUBLANES = 8


def _rows_to_tiles(dst_ref, x):
    n = x.shape[0]
    for j in range(SUBLANES):
        dst_ref[pl.ds(j, n, stride=SUBLANES), :] = x[:, j * LANES:(j + 1) * LANES]


def _tiles_to_rows(src_ref, n):
    return [src_ref[pl.ds(j, n, stride=SUBLANES), :] for j in range(SUBLANES)]


def _dot_tn(a, b):
    return lax.dot_general(a, b, (((0,), (0,)), ((), ())), preferred_element_type=F32)


def _inproj_kernel(x_ref, gmix_ref, wconv_ref, wattn_ref, wgla_ref, wgz_ref, convw_ref, convb_ref,
                   qn_ref, kn_ref, hsum_ref,
                   yconv_ref, q_ref, k_ref, v_ref, kmean_ref, gq_ref, gk_ref, gv_ref, gr_ref, gz_ref,
                   carry_ref):
    s_idx = pl.program_id(1)
    ts = x_ref.shape[0]
    h = _rms(x_ref[...], gmix_ref[...]).astype(BF16)

    c = _dot(h, wconv_ref[...])
    cb = c[:, :CONV_DIM]
    u = c[:, CONV_DIM:2 * CONV_DIM] * c[:, 2 * CONV_DIM:]

    @pl.when(s_idx == 0)
    def _():
        carry_ref[...] = jnp.zeros_like(carry_ref)

    prev = carry_ref[...]
    row = lax.broadcasted_iota(jnp.int32, u.shape, 0)
    u1 = pltpu.roll(u, 1, 0)
    u1 = jnp.where(row == 0, prev[7:8, :], u1)
    u2 = pltpu.roll(u, 2, 0)
    u2 = jnp.where(row == 0, prev[6:7, :], jnp.where(row == 1, prev[7:8, :], u2))
    carry_ref[...] = u[ts - 8:, :]
    y = convb_ref[...] + convw_ref[0:1, :] * u2
    y = y + convw_ref[1:2, :] * u1
    y = y + convw_ref[2:3, :] * u
    yconv_ref[...] = (cb * y).astype(BF16)

    a = _dot(h, wattn_ref[...])
    hsum = hsum_ref[...]

    def head_norm(t, gain):
        hi, lo = _split_bf16(t * t)
        ss = _dot(hi, hsum) + _dot(lo, hsum)
        return t * lax.rsqrt(ss * (1.0 / ATTN_HEAD_DIM) + RMS_EPS) * gain

    qn = head_norm(a[:, :ATTN_DIM], qn_ref[...])
    kn = head_norm(a[:, ATTN_DIM:2 * ATTN_DIM], kn_ref[...])
    q_ref[...] = (qn * (ATTN_HEAD_DIM ** -0.5)).astype(BF16)
    k_ref[...] = kn.astype(BF16)
    v_ref[...] = jnp.transpose(a[:, 2 * ATTN_DIM:]).astype(BF16)
    for j in range(ts // MOBA_BLOCK):
        kmean_ref[j] = jnp.mean(kn[j * MOBA_BLOCK:(j + 1) * MOBA_BLOCK], axis=0, keepdims=True)

    g = _dot(h, wgla_ref[...])
    gq_ref[...] = g[:, :GLA_QK_DIM]
    gk_ref[...] = g[:, GLA_QK_DIM:2 * GLA_QK_DIM]
    gv_ref[...] = g[:, 2 * GLA_QK_DIM:2 * GLA_QK_DIM + GLA_V_DIM]
    gr_ref[...] = g[:, 2 * GLA_QK_DIM + GLA_V_DIM:]
    gz_ref[...] = _dot(h, wgz_ref[...])


def _inproj(x, gmix, wconv, wattn, wgla, wgz, convw, convb, qn, kn, hsum):
    b, s, d = x.shape
    blocks_per_tile = 2
    ts = blocks_per_tile * MOBA_BLOCK
    assert s % ts == 0
    nb = s // MOBA_BLOCK
    tok = lambda w: pl.BlockSpec((None, ts, w), lambda bi, si: (bi, si, 0))
    full = lambda arr: pl.BlockSpec(arr.shape, lambda bi, si: (0,) * arr.ndim)
    out_shapes = (
        jax.ShapeDtypeStruct((b, s, CONV_DIM), BF16),
        jax.ShapeDtypeStruct((b, s, ATTN_DIM), BF16),
        jax.ShapeDtypeStruct((b, s, ATTN_DIM), BF16),
        jax.ShapeDtypeStruct((b, ATTN_DIM, s), BF16),
        jax.ShapeDtypeStruct((b, nb, 1, ATTN_DIM), F32),
        jax.ShapeDtypeStruct((b, s, GLA_QK_DIM), F32),
        jax.ShapeDtypeStruct((b, s, GLA_QK_DIM), F32),
        jax.ShapeDtypeStruct((b, s, GLA_V_DIM), F32),
        jax.ShapeDtypeStruct((b, s, GLA_V_DIM), F32),
        jax.ShapeDtypeStruct((b, s, GZ_PAD), F32),
    )
    out_specs = (
        tok(CONV_DIM), tok(ATTN_DIM), tok(ATTN_DIM),
        pl.BlockSpec((None, ATTN_DIM, ts), lambda bi, si: (bi, 0, si)),
        pl.BlockSpec((None, blocks_per_tile, 1, ATTN_DIM), lambda bi, si: (bi, si, 0, 0)),
        tok(GLA_QK_DIM), tok(GLA_QK_DIM), tok(GLA_V_DIM), tok(GLA_V_DIM), tok(GZ_PAD),
    )
    ins = (x, gmix, wconv, wattn, wgla, wgz, convw, convb, qn, kn, hsum)
    in_specs = [tok(d)] + [full(a) for a in ins[1:]]
    return pl.pallas_call(
        _inproj_kernel,
        grid=(b, s // ts),
        in_specs=in_specs,
        out_specs=out_specs,
        out_shape=out_shapes,
        scratch_shapes=[pltpu.VMEM((8, CONV_DIM), F32)],
        compiler_params=_cparams(2),
        name="inproj",
    )(*ins)


def _select_kernel(q_ref, k_ref, kmean_ref, qp_ref, kp_ref):
    own = pl.program_id(1)
    tq = q_ref.shape[0]
    nb = kmean_ref.shape[0]
    half = ATTN_HEAD_DIM
    lane = lax.broadcasted_iota(jnp.int32, (tq, LANES), 1)
    blk = lax.broadcasted_iota(jnp.int32, (nb, tq), 0).astype(F32)
    own_f = own.astype(F32)
    onehot = jnp.where(lane - half == own, 1.0, 0.0).astype(F32)
    kmean = kmean_ref[...].astype(BF16)
    for p in range(ATTN_HEADS // 2):
        qpair = q_ref[:, p * LANES:(p + 1) * LANES]
        kpair = k_ref[:, p * LANES:(p + 1) * LANES].astype(F32)
        kmpair = kmean[:, p * LANES:(p + 1) * LANES]
        qpair_f = qpair.astype(F32)
        for sub in range(2):
            h = 2 * p + sub
            lane_sel = (lane >= sub * half) & (lane < (sub + 1) * half)
            qh = jnp.where(lane_sel, qpair_f, 0.0).astype(BF16)
            gate_t = _dot_nt(kmpair, qh)
            g = jnp.where(blk < own_f, gate_t, -jnp.inf)
            alive = jnp.ones((nb, tq), F32)
            sel = jnp.where(blk == own_f, 1.0, 0.0)
            for r in range(MOBA_TOPK):
                ga = jnp.where(alive > 0.0, g, -jnp.inf)
                mx = jnp.max(ga, axis=0, keepdims=True)
                cand = jnp.where((alive > 0.0) & (g == mx), blk, 2.0 * LANES)
                first = jnp.min(cand, axis=0, keepdims=True)
                hit = blk == first
                sel = jnp.where(hit, jnp.maximum(sel, jnp.where(own_f > r, 1.0, 0.0)), sel)
                alive = jnp.where(hit, 0.0, alive)
            m_t = jnp.where(sel > 0.0, 0.0, MASK_VALUE).astype(F32)
            m_t = jnp.concatenate([m_t, jnp.zeros((LANES - nb, tq), F32)], axis=0)
            m = jnp.transpose(m_t)
            m = pltpu.roll(m, half, 1)
            m = jnp.where((lane >= half) & (lane < half + nb), m, 0.0)
            qs = qpair_f if sub == 0 else pltpu.roll(qpair_f, half, 1)
            ks = kpair if sub == 0 else pltpu.roll(kpair, half, 1)
            qp_ref[h] = jnp.where(lane < half, qs, m).astype(BF16)
            kp_ref[h] = jnp.where(lane < half, ks, onehot).astype(BF16)


def _select(q, k, kmean):
    b, s, _ = q.shape
    tq = MOBA_BLOCK
    nb = s // tq
    assert nb <= LANES - ATTN_HEAD_DIM, "block one-hot must fit beside the head dim in one lane tile"
    tok = pl.BlockSpec((None, tq, ATTN_DIM), lambda bi, si: (bi, si, 0))
    slab = pl.BlockSpec((None, ATTN_HEADS, tq, LANES), lambda bi, si: (bi, 0, si, 0))
    shp = jax.ShapeDtypeStruct((b, ATTN_HEADS, s, LANES), BF16)
    return pl.pallas_call(
        _select_kernel,
        grid=(b, nb),
        in_specs=[tok, tok, pl.BlockSpec((None, nb, ATTN_DIM), lambda bi, si: (bi, 0, 0))],
        out_specs=(slab, slab),
        out_shape=(shp, shp),
        compiler_params=_cparams(2),
        name="moba_select",
    )(q, k, kmean)


def _attn_kernel(qp_ref, kp_ref, vt_ref, bias0_ref, bias1_ref, o_ref, m_ref, l_ref, acc_ref,
                 sa_ref, sb_ref, st_ref):
    own = pl.program_id(2)
    tq = qp_ref.shape[1]
    far_rows = ATTN_FAR_CHUNK * tq
    sub = lax.broadcasted_iota(jnp.int32, (LANES, tq), 0)
    n_far = jnp.maximum(own - 1, 0) // ATTN_FAR_CHUNK

    def scores(hh, start, n):
        return _dot_nt(kp_ref[hh, pl.ds(start, n * tq), :], qp_ref[hh])

    def far_scores(dst_ref, chunk):
        start = pl.multiple_of(chunk * far_rows, far_rows)
        for hh in range(2):
            dst_ref[hh] = scores(hh, start, ATTN_FAR_CHUNK)

    def softmax_pv(hh, s_ref, start, n):
        blocks = [s_ref[hh, w * tq:(w + 1) * tq, :] for w in range(n)]
        m_prev = m_ref[hh]
        m_new = jnp.maximum(m_prev, jnp.max(functools.reduce(jnp.maximum, blocks), axis=0, keepdims=True))
        alpha = jnp.exp(m_prev - m_new)
        ps = [jnp.exp(blk - m_new) for blk in blocks]
        l_ref[hh] = alpha * l_ref[hh] + jnp.sum(functools.reduce(jnp.add, ps), axis=0, keepdims=True)
        pt = jnp.concatenate([x.astype(BF16) for x in ps], axis=0)
        acc_ref[hh] = alpha * acc_ref[hh] + _dot(vt_ref[:, pl.ds(start, n * tq)], pt)
        m_ref[hh] = m_new

    for hh in range(2):
        m_ref[hh] = jnp.full((1, tq), MASK_VALUE, F32)
        l_ref[hh] = jnp.zeros((1, tq), F32)
        acc_ref[hh] = jnp.zeros((LANES, tq), F32)

    @pl.when(n_far > 0)
    def _():
        far_scores(sa_ref, 0)

    def far_softmax(s_ref, chunk):
        for hh in range(2):
            softmax_pv(hh, s_ref, pl.multiple_of(chunk * far_rows, far_rows), ATTN_FAR_CHUNK)

    def far_pair(j, carry):
        c0 = 2 * j
        far_scores(sb_ref, c0 + 1)
        far_softmax(sa_ref, c0)

        @pl.when(c0 + 2 < n_far)
        def _():
            far_scores(sa_ref, c0 + 2)
            far_softmax(sb_ref, c0 + 1)

        return carry

    lax.fori_loop(0, n_far // 2, far_pair, 0)

    left = jnp.maximum(own - 1, 0) % ATTN_FAR_CHUNK
    tail_start = pl.multiple_of(n_far * far_rows, tq)
    last_even = (n_far % 2) == 1

    def finish(n, start, last_ref):
        for hh in range(2):
            s = scores(hh, start, n)
            for w in range(n):
                sw = s[w * tq:(w + 1) * tq, :]
                if w == n - 1:
                    sw = sw + bias0_ref[hh]
                elif w == n - 2:
                    sw = sw + bias1_ref[hh]
                st_ref[hh, w * tq:(w + 1) * tq, :] = sw
        if last_ref is not None:
            far_softmax(last_ref, n_far - 1)
        for hh in range(2):
            softmax_pv(hh, st_ref, start, n)

    @pl.when(own == 0)
    def _():
        finish(1, 0, None)

    for n_left in range(ATTN_FAR_CHUNK):
        right_size = (own >= 1) & (left == n_left)

        @pl.when(right_size & (n_far == 0))
        def _():
            finish(n_left + 2, tail_start, None)

        @pl.when(right_size & (n_far > 0) & last_even)
        def _():
            finish(n_left + 2, tail_start, sa_ref)

        @pl.when(right_size & (n_far > 0) & jnp.logical_not(last_even))
        def _():
            finish(n_left + 2, tail_start, sb_ref)

    outs = [acc_ref[hh] / l_ref[hh] for hh in range(2)]
    o_t = jnp.where(sub < ATTN_HEAD_DIM, outs[0], outs[1])
    o_ref[...] = jnp.transpose(o_t).astype(BF16)


def _attention(qp, kp, vt, bias0, bias1):
    b, nh, s, _ = qp.shape
    tq = MOBA_BLOCK
    nq = s // tq
    return pl.pallas_call(
        _attn_kernel,
        grid=(b, nh // 2, nq),
        in_specs=[
            pl.BlockSpec((None, 2, tq, LANES), lambda bi, pi, qi: (bi, pi, qi, 0)),
            pl.BlockSpec((None, 2, s, LANES), lambda bi, pi, qi: (bi, pi, 0, 0)),
            pl.BlockSpec((None, LANES, s), lambda bi, pi, qi: (bi, pi, 0)),
            pl.BlockSpec((2, tq, tq), lambda bi, pi, qi: (pi, 0, 0)),
            pl.BlockSpec((2, tq, tq), lambda bi, pi, qi: (pi, 0, 0)),
        ],
        out_specs=pl.BlockSpec((None, tq, LANES), lambda bi, pi, qi: (bi, qi, pi)),
        out_shape=jax.ShapeDtypeStruct((b, s, ATTN_DIM), BF16),
        scratch_shapes=[
            pltpu.VMEM((2, 1, tq), F32),
            pltpu.VMEM((2, 1, tq), F32),
            pltpu.VMEM((2, LANES, tq), F32),
            pltpu.VMEM((2, ATTN_FAR_CHUNK * tq, tq), F32),
            pltpu.VMEM((2, ATTN_FAR_CHUNK * tq, tq), F32),
            pltpu.VMEM((2, ATTN_TAIL * tq, tq), F32),
        ],
        compiler_params=_cparams(3),
        name="moba_attention",
    )(qp, kp, vt, bias0, bias1)


def _t5_bucket(rel):
    n = jnp.maximum(rel, 0)
    max_exact = REL_BUCKETS // 2
    scaled = (jnp.log(jnp.maximum(n, max_exact).astype(F32) / max_exact)
              / math.log(REL_MAX_DISTANCE / max_exact))
    large = jnp.minimum(max_exact + (scaled * (REL_BUCKETS - max_exact)).astype(jnp.int32), REL_BUCKETS - 1)
    return jnp.where(n < max_exact, n, large)


def _attn_biases(rel_bias_table):
    table = rel_bias_table.astype(F32).T
    pos = jnp.arange(MOBA_BLOCK)
    rel0 = pos[None, :] - pos[:, None]
    far = table[:, REL_BUCKETS - 1][:, None, None]

    rel = jnp.arange(1 - 2 * MOBA_BLOCK, 2 * MOBA_BLOCK)
    hit = _t5_bucket(rel)[None, :, None] == jnp.arange(REL_BUCKETS)
    by_rel = jnp.sum(jnp.where(hit, table[:, None, :], 0.0), axis=-1)
    zero = 2 * MOBA_BLOCK - 1

    def tile(offset):
        rows = [by_rel[:, zero + offset - kk:zero + offset - kk + MOBA_BLOCK] for kk in range(MOBA_BLOCK)]
        return jnp.stack(rows, axis=1)

    bias0 = jnp.where(rel0[None] >= 0, tile(0) - far, MASK_VALUE)
    bias1 = tile(MOBA_BLOCK) - far
    return bias0, bias1


def _gla_kernel(gq_ref, gk_ref, gv_ref, gr_ref, gz_ref, wa_ref, ba_ref, on_ref, y_ref, state_ref):
    s_idx = pl.program_id(1)
    tg = gq_ref.shape[0]
    dk, dv, ck = GLA_KEY_DIM, GLA_VALUE_DIM, GLA_CHUNK

    @pl.when(s_idx == 0)
    def _():
        state_ref[...] = jnp.zeros_like(state_ref)

    z = gz_ref[...].astype(BF16)
    log_a = jax.nn.log_sigmoid(_dot(z, wa_ref[...]) + ba_ref[...]) / GLA_GATE_TEMP
    row = lax.broadcasted_iota(jnp.int32, log_a.shape, 0) % ck
    bcum = log_a
    shift = 1
    while shift < ck:
        bcum = bcum + jnp.where(row >= shift, pltpu.roll(bcum, shift, 0), 0.0)
        shift *= 2
    q = gq_ref[...] * (dk ** -0.5)
    k = gk_ref[...]
    tri = (lax.broadcasted_iota(jnp.int32, (ck, ck), 0) >= lax.broadcasted_iota(jnp.int32, (ck, ck), 1))
    states = [state_ref[h] for h in range(GLA_HEADS)]
    for c in range(tg // ck):
        rows = slice(c * ck, (c + 1) * ck)
        for h in range(GLA_HEADS):
            kcols = slice(h * dk, (h + 1) * dk)
            vcols = slice(h * dv, (h + 1) * dv)
            bc = bcum[rows, kcols]
            btot = bc[ck - 1:ck, :]
            qe = (q[rows, kcols] * jnp.exp(bc)).astype(BF16)
            ke = (k[rows, kcols] * jnp.exp(-bc)).astype(BF16)
            kd = (k[rows, kcols] * jnp.exp(btot - bc)).astype(BF16)
            vc = gv_ref[rows, vcols].astype(BF16)
            att = jnp.where(tri, _dot_nt(qe, ke), 0.0).astype(BF16)
            st = states[h]
            o = _dot(att, vc) + _dot_nt(qe, st.astype(BF16))
            states[h] = st * jnp.exp(btot) + _dot_tn(vc, kd)
            o = _rms(o, on_ref[...])
            r = gr_ref[rows, vcols]
            y_ref[rows, vcols] = (o * (r * jax.nn.sigmoid(r))).astype(BF16)
    for h in range(GLA_HEADS):
        state_ref[h] = states[h]


def _gla(gq, gk, gv, gr, gz, wa, ba, on):
    b, s, _ = gq.shape
    tg = 256
    tok = lambda w: pl.BlockSpec((None, tg, w), lambda bi, si: (bi, si, 0))
    full = lambda arr: pl.BlockSpec(arr.shape, lambda bi, si: (0,) * arr.ndim)
    return pl.pallas_call(
        _gla_kernel,
        grid=(b, s // tg),
        in_specs=[tok(GLA_QK_DIM), tok(GLA_QK_DIM), tok(GLA_V_DIM), tok(GLA_V_DIM), tok(GZ_PAD),
                  full(wa), full(ba), full(on)],
        out_specs=tok(GLA_V_DIM),
        out_shape=jax.ShapeDtypeStruct((b, s, GLA_V_DIM), BF16),
        scratch_shapes=[pltpu.VMEM((GLA_HEADS, GLA_VALUE_DIM, GLA_KEY_DIM), F32)],
        compiler_params=_cparams(2),
        name="gla",
    )(gq, gk, gv, gr, gz, wa, ba, on)


def _merge_route_kernel(x_ref, yc_ref, ya_ref, yg_ref, gmix_ref, wg_ref, wb_ref, wo_ref, gffn_ref,
                        wr_hi_ref, wr_lo_ref, br_ref, xo_ref, h2_ref, route_ref, counts_ref, run_ref):
    x = x_ref[...]
    h = _rms(x, gmix_ref[...]).astype(BF16)
    merged = None
    for n, y_ref in enumerate((yc_ref, ya_ref, yg_ref)):
        term = jax.nn.sigmoid(_dot(h, wg_ref[n])) * _dot(y_ref[...], wb_ref[n])
        merged = term if merged is None else merged + term
    xo = x + _dot(merged.astype(BF16), wo_ref[...])
    xo_ref[...] = xo
    h2 = _rms(xo, gffn_ref[...])
    _rows_to_tiles(h2_ref, h2)

    h_hi, h_lo = _split_bf16(h2)
    logits = (_dot(h_hi, wr_hi_ref[...]) + _dot(h_lo, wr_hi_ref[...]) + _dot(h_hi, wr_lo_ref[...])
              + br_ref[...])
    lane = lax.broadcasted_iota(jnp.int32, logits.shape, 1).astype(F32)
    big = 4.0 * ROUTE_LANES
    lg = jnp.where(lane < N_GROUPS, logits, -jnp.inf)
    gmax = jnp.max(lg, axis=1, keepdims=True)
    gidx = jnp.min(jnp.where(lg == gmax, lane, big), axis=1, keepdims=True)
    p_group_top = 1.0 / jnp.sum(jnp.exp(lg - gmax), axis=1, keepdims=True)
    lo_lane = N_GROUPS + gidx * EXPERTS_PER_GROUP
    le = jnp.where((lane >= lo_lane) & (lane < lo_lane + EXPERTS_PER_GROUP), logits, -jnp.inf)
    emax = jnp.max(le, axis=1, keepdims=True)
    i1 = jnp.min(jnp.where(le == emax, lane, big), axis=1, keepdims=True)
    esum = jnp.sum(jnp.exp(le - emax), axis=1, keepdims=True)
    le2 = jnp.where(lane == i1, -jnp.inf, le)
    emax2 = jnp.max(le2, axis=1, keepdims=True)
    i2 = jnp.min(jnp.where(le2 == emax2, lane, big), axis=1, keepdims=True)
    p1 = 1.0 / esum
    p2 = jnp.exp(emax2 - emax) / esum
    psum = p1 + p2
    w1 = p_group_top * p1 / psum
    w2 = p_group_top * p2 / psum
    e1 = i1 - N_GROUPS
    e2 = i2 - N_GROUPS

    @pl.when(pl.program_id(0) == 0)
    def _():
        run_ref[...] = jnp.zeros_like(run_ref)

    tm = x.shape[0]
    oh1 = jnp.where(lane == i1, 1.0, 0.0)
    oh2 = jnp.where(lane == i2, 1.0, 0.0)
    ohs = oh1 + oh2
    lower = (lax.broadcasted_iota(jnp.int32, (tm, tm), 0) > lax.broadcasted_iota(jnp.int32, (tm, tm), 1))
    before = _dot(jnp.where(lower, 1.0, 0.0).astype(BF16), ohs.astype(BF16)) + run_ref[...]
    rank1 = jnp.sum(oh1 * before, axis=1, keepdims=True)
    rank2 = jnp.sum(oh2 * before, axis=1, keepdims=True)
    run_ref[...] = run_ref[...] + jnp.sum(ohs, axis=0, keepdims=True)
    counts_ref[...] = run_ref[...]

    rl = lax.broadcasted_iota(jnp.int32, (tm, ROUTE_OUT), 1)
    rec = jnp.zeros((tm, ROUTE_OUT), F32)
    for slot, val in enumerate((e1, e2, w1, w2, rank1, rank2)):
        rec = jnp.where(rl == slot, val, rec)
    route_ref[...] = rec


def _merge_route(x2d, yc, ya, yg, gmix, wg, wb, wo, gffn, wr_hi, wr_lo, br):
    t, d = x2d.shape
    tm = 512
    tok = lambda w: pl.BlockSpec((tm, w), lambda i: (i, 0))
    full = lambda arr: pl.BlockSpec(arr.shape, lambda i: (0,) * arr.ndim)
    ins = (x2d, yc, ya, yg, gmix, wg, wb, wo, gffn, wr_hi, wr_lo, br)
    return pl.pallas_call(
        _merge_route_kernel,
        grid=(t // tm,),
        in_specs=[tok(d), tok(CONV_DIM), tok(ATTN_DIM), tok(GLA_V_DIM)] + [full(a) for a in ins[4:]],
        out_specs=(tok(d), pl.BlockSpec((tm * SUBLANES, LANES), lambda i: (i, 0)), tok(ROUTE_OUT),
                   pl.BlockSpec((1, ROUTE_LANES), lambda i: (0, 0))),
        out_shape=(jax.ShapeDtypeStruct((t, d), F32), jax.ShapeDtypeStruct((t * SUBLANES, LANES), F32),
                   jax.ShapeDtypeStruct((t, ROUTE_OUT), F32), jax.ShapeDtypeStruct((1, ROUTE_LANES), F32)),
        scratch_shapes=[pltpu.VMEM((1, ROUTE_LANES), F32)],
        compiler_params=_cparams(1),
        name="merge_route",
    )(*ins)


def _tile_slots(dest, tc):
    nt = dest.shape[0] // tc
    return dest.reshape(nt, tc, TOP_K).transpose(0, 2, 1).reshape(nt, 1, TOP_K * tc)


def _dispatch_kernel(last_blk_ref, dest_ref, h2_ref, xs_hbm, zeros_ref, sem):
    tc = h2_ref.shape[0] // SUBLANES

    @pl.when(pl.program_id(0) == 0)
    def _():
        zeros_ref[...] = jnp.zeros_like(zeros_ref)
        blk_rows = EXPERT_BLOCK * SUBLANES

        def zero_fill(first_row):
            return pltpu.make_async_copy(
                zeros_ref, xs_hbm.at[pl.ds(pl.multiple_of(first_row * SUBLANES, SUBLANES), blk_rows)], sem)

        for go in (lambda c: c.start(), lambda c: c.wait()):
            for e in range(N_EXPERTS):
                @pl.when(last_blk_ref[e] >= 0)
                def _():
                    go(zero_fill(last_blk_ref[e]))

            def unused(blk, carry):
                go(zero_fill(blk * EXPERT_BLOCK))
                return carry

            lax.fori_loop(last_blk_ref[N_EXPERTS], xs_hbm.shape[0] // blk_rows, unused, 0)

    def row_copy(r, kk):
        src = pl.multiple_of(r * SUBLANES, SUBLANES)
        dst = pl.multiple_of(dest_ref[0, kk * tc + r] * SUBLANES, SUBLANES)
        return pltpu.make_async_copy(h2_ref.at[pl.ds(src, SUBLANES)], xs_hbm.at[pl.ds(dst, SUBLANES)], sem)

    def issue(r, carry):
        row_copy(r, 0).start(priority=0)
        row_copy(r, 1).start(priority=1)
        return carry

    lax.fori_loop(0, tc, issue, 0, unroll=ROW_DMA_UNROLL)
    for _ in range(TOP_K):
        pltpu.make_async_copy(h2_ref, xs_hbm.at[pl.ds(0, tc * SUBLANES)], sem).wait()


def _dispatch(dest, h2_tiles, n_rows, last_blk):
    t = h2_tiles.shape[0] // SUBLANES
    tc = 256
    grid_spec = pltpu.PrefetchScalarGridSpec(
        num_scalar_prefetch=1,
        grid=(t // tc,),
        in_specs=[
            pl.BlockSpec((None, 1, TOP_K * tc), lambda i, lb: (i, 0, 0), memory_space=pltpu.SMEM),
            pl.BlockSpec((tc * SUBLANES, LANES), lambda i, lb: (i, 0)),
        ],
        out_specs=pl.BlockSpec(memory_space=pl.ANY),
        scratch_shapes=[pltpu.VMEM((EXPERT_BLOCK * SUBLANES, LANES), F32), pltpu.SemaphoreType.DMA(())],
    )
    return pl.pallas_call(
        _dispatch_kernel,
        grid_spec=grid_spec,
        out_shape=jax.ShapeDtypeStruct((n_rows * SUBLANES, LANES), F32),
        compiler_params=_cparams(1),
        name="moe_dispatch",
    )(last_blk, _tile_slots(dest, tc), h2_tiles)


def _expert_kernel(blk_expert_ref, n_used_ref, xs_ref, wg_ref, wu_ref, wd_ref, ys_ref,
                   wg_bf, wu_bf, wd_bf):
    i = pl.program_id(0)
    rb = xs_ref.shape[0] // SUBLANES
    new_expert = (i == 0) | (blk_expert_ref[i] != blk_expert_ref[jnp.maximum(i - 1, 0)])

    @pl.when(new_expert)
    def _():
        wg_bf[...] = wg_ref[...].astype(BF16)
        wu_bf[...] = wu_ref[...].astype(BF16)
        wd_bf[...] = wd_ref[...].astype(BF16)

    @pl.when(i < n_used_ref[0])
    def _():
        xb = jnp.concatenate(_tiles_to_rows(xs_ref, rb), axis=1).astype(BF16)
        gate = _dot(xb, wg_bf[...])
        up = _dot(xb, wu_bf[...])
        act = (gate * jax.nn.sigmoid(gate) * up).astype(BF16)
        _rows_to_tiles(ys_ref, _dot(act, wd_bf[...]))

    @pl.when(i >= n_used_ref[0])
    def _():
        ys_ref[...] = jnp.zeros_like(ys_ref)


def _experts(blk_expert, n_used, xs_tiles, wg, wu, wd, layer):
    d = SUBLANES * LANES
    n_blocks = blk_expert.shape[0]
    rb = EXPERT_BLOCK
    row_block = (rb * SUBLANES, LANES)
    grid_spec = pltpu.PrefetchScalarGridSpec(
        num_scalar_prefetch=2,
        grid=(n_blocks,),
        in_specs=[
            pl.BlockSpec(row_block, lambda i, be, nu: (jnp.minimum(i, nu[0] - 1), 0)),
            pl.BlockSpec((None, None, d, EXPERT_FF), lambda i, be, nu: (layer, be[i], 0, 0)),
            pl.BlockSpec((None, None, d, EXPERT_FF), lambda i, be, nu: (layer, be[i], 0, 0)),
            pl.BlockSpec((None, None, EXPERT_FF, d), lambda i, be, nu: (layer, be[i], 0, 0)),
        ],
        out_specs=pl.BlockSpec(row_block, lambda i, be, nu: (i, 0)),
        scratch_shapes=[pltpu.VMEM((d, EXPERT_FF), BF16), pltpu.VMEM((d, EXPERT_FF), BF16),
                        pltpu.VMEM((EXPERT_FF, d), BF16)],
    )
    return pl.pallas_call(
        _expert_kernel,
        grid_spec=grid_spec,
        out_shape=jax.ShapeDtypeStruct(xs_tiles.shape, F32),
        compiler_params=_cparams(1),
        name="moe_experts",
    )(blk_expert, n_used, xs_tiles, wg, wu, wd)


def _combine_kernel(dest_ref, x_ref, route_ref, ys_hbm, o_ref, buf, sem):
    tc = x_ref.shape[0]

    def row_copy(r, kk):
        src = pl.multiple_of(dest_ref[0, kk * tc + r] * SUBLANES, SUBLANES)
        dst = pl.multiple_of(r * SUBLANES, SUBLANES)
        return pltpu.make_async_copy(ys_hbm.at[pl.ds(src, SUBLANES)], buf.at[kk, pl.ds(dst, SUBLANES)], sem)

    def issue(r, carry):
        row_copy(r, 0).start(priority=0)
        row_copy(r, 1).start(priority=1)
        return carry

    lax.fori_loop(0, tc, issue, 0, unroll=ROW_DMA_UNROLL)
    for kk in range(TOP_K):
        pltpu.make_async_copy(ys_hbm.at[pl.ds(0, tc * SUBLANES)], buf.at[kk], sem).wait()
    route = route_ref[...]
    w1 = route[:, TOP_K:TOP_K + 1]
    w2 = route[:, TOP_K + 1:TOP_K + 2]
    y1 = _tiles_to_rows(buf.at[0], tc)
    y2 = _tiles_to_rows(buf.at[1], tc)
    for j in range(SUBLANES):
        cols = slice(j * LANES, (j + 1) * LANES)
        o_ref[:, cols] = x_ref[:, cols] + (y1[j] * w1 + y2[j] * w2)


def _combine(dest, x2d, route, ys_tiles):
    t, d = x2d.shape
    tc = 256
    return pl.pallas_call(
        _combine_kernel,
        grid=(t // tc,),
        in_specs=[
            pl.BlockSpec((None, 1, TOP_K * tc), lambda i: (i, 0, 0), memory_space=pltpu.SMEM),
            pl.BlockSpec((tc, d), lambda i: (i, 0)),
            pl.BlockSpec((tc, ROUTE_OUT), lambda i: (i, 0)),
            pl.BlockSpec(memory_space=pl.ANY),
        ],
        out_specs=pl.BlockSpec((tc, d), lambda i: (i, 0)),
        out_shape=jax.ShapeDtypeStruct((t, d), F32),
        scratch_shapes=[pltpu.VMEM((TOP_K, tc * SUBLANES, LANES), F32), pltpu.SemaphoreType.DMA(())],
        compiler_params=_cparams(1),
        name="moe_combine",
    )(_tile_slots(dest, tc), x2d, route, ys_tiles)


def _dispatch_plan(route, counts, t):
    e_ids = route[:, :TOP_K].astype(jnp.int32)
    rank = route[:, 2 * TOP_K:3 * TOP_K].astype(jnp.int32)
    counts = counts[0, N_GROUPS:N_GROUPS + N_EXPERTS].astype(jnp.int32)
    padded = ((counts + EXPERT_BLOCK - 1) // EXPERT_BLOCK) * EXPERT_BLOCK
    pad_end = jnp.cumsum(padded)
    pad_start = pad_end - padded
    onehot = e_ids[:, :, None] == jnp.arange(N_EXPERTS, dtype=jnp.int32)
    dest = rank + jnp.sum(jnp.where(onehot, pad_start, 0), axis=-1)
    n_blocks = -(-(t * TOP_K) // EXPERT_BLOCK) + N_EXPERTS
    blk_start = jnp.arange(n_blocks, dtype=jnp.int32) * EXPERT_BLOCK
    blk_expert = jnp.minimum(jnp.sum(blk_start[:, None] >= pad_end[None, :], axis=1), N_EXPERTS - 1)
    n_used = jnp.maximum(pad_end[-1:] // EXPERT_BLOCK, 1).astype(jnp.int32)
    last_blk = jnp.concatenate([jnp.where(padded > 0, pad_end - EXPERT_BLOCK, -1), n_used]).astype(jnp.int32)
    return blk_expert.astype(jnp.int32), n_used, dest, n_blocks * EXPERT_BLOCK, last_blk


def kernel(x, rel_bias_table, norm_mix, w_in, conv_w, conv_b, q_norm, k_norm, w_gla_alpha, b_gla_alpha,
           gla_out_norm, w_merge_gate, w_branch, w_out, norm_ffn, w_router_group, b_router_group,
           w_router_expert, b_router_expert, w_expert_gate, w_expert_up, w_expert_down):
    b, s, d = x.shape
    t = b * s
    depth = w_in.shape[0]
    assert s % MOBA_BLOCK == 0 and t % EXPERT_BLOCK == 0
    assert d == SUBLANES * LANES, "row-granular DMAs store each activation row as one (8, 128) tile"
    bias0, bias1 = _attn_biases(rel_bias_table)
    head_id = jnp.arange(ATTN_DIM) // ATTN_HEAD_DIM
    hsum = (head_id[:, None] == head_id[None, :]).astype(BF16)
    c3 = 3 * CONV_DIM
    a3 = c3 + 3 * ATTN_DIM
    g3 = a3 + 2 * GLA_QK_DIM + 2 * GLA_V_DIM
    for l in range(depth):
        w_l = w_in[l].astype(BF16)
        w_gz = jnp.pad(w_l[:, g3:], ((0, 0), (0, GZ_PAD - GLA_GATE_RANK)))
        w_alpha = jnp.pad(w_gla_alpha[l].astype(BF16), ((0, GZ_PAD - GLA_GATE_RANK), (0, 0)))
        yconv, q, k, v, kmean, gq, gk, gv, gr, gz = _inproj(
            x, norm_mix[l][None], w_l[:, :c3], w_l[:, c3:a3], w_l[:, a3:g3], w_gz,
            conv_w[l], conv_b[l][None],
            jnp.tile(q_norm[l], ATTN_HEADS)[None], jnp.tile(k_norm[l], ATTN_HEADS)[None], hsum)
        qp, kp = _select(q, k, kmean.reshape(b, s // MOBA_BLOCK, ATTN_DIM))
        yattn = _attention(qp, kp, v, bias0, bias1)
        ygla = _gla(gq, gk, gv, gr, gz, w_alpha, b_gla_alpha[l][None],
                    gla_out_norm[l][None])
        w_r = jnp.concatenate([w_router_group[l], w_router_expert[l]], axis=1)
        w_r = jnp.pad(w_r, ((0, 0), (0, ROUTE_LANES - w_r.shape[1])))
        b_r = jnp.pad(jnp.concatenate([b_router_group[l], b_router_expert[l]]),
                      (0, ROUTE_LANES - N_GROUPS - N_EXPERTS))[None]
        wr_hi, wr_lo = _split_bf16(w_r)
        xo, h2, route, counts = _merge_route(
            x.reshape(t, d), yconv.reshape(t, -1), yattn.reshape(t, -1), ygla.reshape(t, -1),
            norm_mix[l][None], w_merge_gate[l].astype(BF16), w_branch[l].astype(BF16),
            w_out[l].astype(BF16), norm_ffn[l][None], wr_hi, wr_lo, b_r)
        blk_expert, n_used, dest, n_rows, last_blk = _dispatch_plan(route, counts, t)
        xs = _dispatch(dest, h2, n_rows, last_blk)
        ys = _experts(blk_expert, n_used, xs, w_expert_gate, w_expert_up, w_expert_down, l)
        x = _combine(dest, xo, route, ys).reshape(b, s, d)
    return x
```

```python
import functools
import math

import jax
import jax.numpy as jnp
import numpy as np
from jax import lax
from jax.experimental import pallas as pl
from jax.experimental.pallas import tpu as pltpu

CONV_DIM = 512
CONV_WIDTH = 3
ATTN_HEADS = 8
ATTN_HEAD_DIM = 64
ATTN_DIM = ATTN_HEADS * ATTN_HEAD_DIM
MOBA_BLOCK = 256
MOBA_TOPK = 3
REL_BUCKETS = 32
REL_MAX_DISTANCE = 128
GLA_HEADS = 4
GLA_KEY_DIM = 64
GLA_VALUE_DIM = 128
GLA_QK_DIM = GLA_HEADS * GLA_KEY_DIM
GLA_V_DIM = GLA_HEADS * GLA_VALUE_DIM
GLA_GATE_RANK = 16
GLA_GATE_TEMP = 16.0
GLA_CHUNK = 64
N_GROUPS = 4
EXPERTS_PER_GROUP = 8
N_EXPERTS = N_GROUPS * EXPERTS_PER_GROUP
TOP_K = 2
EXPERT_FF = 512
EXPERT_BLOCK = 256
RMS_EPS = 1e-6

LANES = 128
VMEM_LIMIT_BYTES = 56 * 1024 * 1024

MASK_VALUE = -1e30
ROUTE_LANES = 128
ROUTE_OUT = 8
ATTN_FAR_CHUNK = 4
ATTN_TAIL = ATTN_FAR_CHUNK + 1
ROW_DMA_UNROLL = 8
GZ_PAD = LANES

F32 = jnp.float32
BF16 = jnp.bfloat16


def _cparams(n_axes):
    return pltpu.CompilerParams(
        dimension_semantics=("arbitrary",) * n_axes,
        vmem_limit_bytes=VMEM_LIMIT_BYTES,
    )


def _rms(x, gain):
    return x * lax.rsqrt(jnp.mean(x * x, axis=-1, keepdims=True) + RMS_EPS) * gain


def _split_bf16(x):
    hi = x.astype(BF16)
    lo = (x - hi.astype(F32)).astype(BF16)
    return hi, lo


def _dot(a, b):
    return jnp.dot(a, b, preferred_element_type=F32)


def _dot_nt(a, b):
    return lax.dot_general(a, b, (((1,), (1,)), ((), ())), preferred_element_type=F32)


SUBLANES = 8


def _rows_to_tiles(dst_ref, x):
    n = x.shape[0]
    for j in range(SUBLANES):
        dst_ref[pl.ds(j, n, stride=SUBLANES), :] = x[:, j * LANES:(j + 1) * LANES]


def _tiles_to_rows(src_ref, n):
    return [src_ref[pl.ds(j, n, stride=SUBLANES), :] for j in range(SUBLANES)]


def _dot_tn(a, b):
    return lax.dot_general(a, b, (((0,), (0,)), ((), ())), preferred_element_type=F32)


def _start_expert_row_gather(slots_ref, ys_hbm, buf, sem, n):
    def row_copy(r, kk):
        src = pl.multiple_of(slots_ref[0, kk * n + r] * SUBLANES, SUBLANES)
        dst = pl.multiple_of(r * SUBLANES, SUBLANES)
        return pltpu.make_async_copy(ys_hbm.at[pl.ds(src, SUBLANES)], buf.at[kk, pl.ds(dst, SUBLANES)], sem)

    def issue(r, carry):
        row_copy(r, 0).start(priority=0)
        row_copy(r, 1).start(priority=1)
        return carry

    lax.fori_loop(0, n, issue, 0, unroll=ROW_DMA_UNROLL)


def _finish_expert_row_gather(ys_hbm, buf, sem, n):
    for kk in range(TOP_K):
        pltpu.make_async_copy(ys_hbm.at[pl.ds(0, n * SUBLANES)], buf.at[kk], sem).wait()


def _moe_residual_chunks(x_ref, route_ref, buf, n):
    route = route_ref[...]
    w1 = route[:, TOP_K:TOP_K + 1]
    w2 = route[:, TOP_K + 1:TOP_K + 2]
    y1 = _tiles_to_rows(buf.at[0], n)
    y2 = _tiles_to_rows(buf.at[1], n)
    return [x_ref[:, j * LANES:(j + 1) * LANES] + (y1[j] * w1 + y2[j] * w2) for j in range(SUBLANES)]


def _inproj_kernel(*refs, combine):
    if combine:
        (slots_ref, next_slots_ref, x_ref, route_ref, ys_hbm), refs = refs[:5], refs[5:]
    else:
        x_ref, refs = refs[0], refs[1:]
    (gmix_ref, wconv_ref, wattn_ref, wgla_ref, wgz_ref, convw_ref, convb_ref, qn_ref, kn_ref, hsum_ref), refs = (
        refs[:10], refs[10:])
    if combine:
        xnew_ref, refs = refs[0], refs[1:]
    (yconv_ref, q_ref, k_ref, v_ref, kmean_ref, gq_ref, gk_ref, gv_ref, gr_ref, gz_ref, carry_ref), refs = (
        refs[:11], refs[11:])
    s_idx = pl.program_id(1)
    ts = x_ref.shape[0]
    if combine:
        rows_buf, sem = refs
        tile = pl.program_id(0) * pl.num_programs(1) + s_idx
        slot = tile % 2

        @pl.when(tile == 0)
        def _():
            _start_expert_row_gather(slots_ref, ys_hbm, rows_buf.at[0], sem.at[0], ts)

        @pl.when(tile + 1 < pl.num_programs(0) * pl.num_programs(1))
        def _():
            _start_expert_row_gather(next_slots_ref, ys_hbm, rows_buf.at[1 - slot], sem.at[1 - slot], ts)

        _finish_expert_row_gather(ys_hbm, rows_buf.at[slot], sem.at[slot], ts)
        x = jnp.concatenate(_moe_residual_chunks(x_ref, route_ref, rows_buf.at[slot], ts), axis=1)
        xnew_ref[...] = x
    else:
        x = x_ref[...]
    h = _rms(x, gmix_ref[...]).astype(BF16)

    c = _dot(h, wconv_ref[...])
    cb = c[:, :CONV_DIM]
    u = c[:, CONV_DIM:2 * CONV_DIM] * c[:, 2 * CONV_DIM:]

    @pl.when(s_idx == 0)
    def _():
        carry_ref[...] = jnp.zeros_like(carry_ref)

    prev = carry_ref[...]
    row = lax.broadcasted_iota(jnp.int32, u.shape, 0)
    u1 = pltpu.roll(u, 1, 0)
    u1 = jnp.where(row == 0, prev[7:8, :], u1)
    u2 = pltpu.roll(u, 2, 0)
    u2 = jnp.where(row == 0, prev[6:7, :], jnp.where(row == 1, prev[7:8, :], u2))
    carry_ref[...] = u[ts - 8:, :]
    y = convb_ref[...] + convw_ref[0:1, :] * u2
    y = y + convw_ref[1:2, :] * u1
    y = y + convw_ref[2:3, :] * u
    yconv_ref[...] = (cb * y).astype(BF16)

    a = _dot(h, wattn_ref[...])
    hsum = hsum_ref[...]

    def head_norm(t, gain):
        hi, lo = _split_bf16(t * t)
        ss = _dot(hi, hsum) + _dot(lo, hsum)
        return t * lax.rsqrt(ss * (1.0 / ATTN_HEAD_DIM) + RMS_EPS) * gain

    qn = head_norm(a[:, :ATTN_DIM], qn_ref[...])
    kn = head_norm(a[:, ATTN_DIM:2 * ATTN_DIM], kn_ref[...])
    q_ref[...] = (qn * (ATTN_HEAD_DIM ** -0.5)).astype(BF16)
    k_ref[...] = kn.astype(BF16)
    v_ref[...] = jnp.transpose(a[:, 2 * ATTN_DIM:]).astype(BF16)
    for j in range(ts // MOBA_BLOCK):
        kmean_ref[j] = jnp.mean(kn[j * MOBA_BLOCK:(j + 1) * MOBA_BLOCK], axis=0, keepdims=True)

    g = _dot(h, wgla_ref[...])
    gq_ref[...] = g[:, :GLA_QK_DIM]
    gk_ref[...] = g[:, GLA_QK_DIM:2 * GLA_QK_DIM]
    gv_ref[...] = g[:, 2 * GLA_QK_DIM:2 * GLA_QK_DIM + GLA_V_DIM]
    gr_ref[...] = g[:, 2 * GLA_QK_DIM + GLA_V_DIM:]
    gz_ref[...] = _dot(h, wgz_ref[...])


def _inproj(x, gmix, wconv, wattn, wgla, wgz, convw, convb, qn, kn, hsum, moe=None):
    b, s, d = x.shape
    blocks_per_tile = 2
    ts = blocks_per_tile * MOBA_BLOCK
    assert s % ts == 0
    nb = s // MOBA_BLOCK
    ns = s // ts
    tok = lambda w: pl.BlockSpec((None, ts, w), lambda bi, si: (bi, si, 0))
    full = lambda arr: pl.BlockSpec(arr.shape, lambda bi, si: (0,) * arr.ndim)
    out_shapes = (
        jax.ShapeDtypeStruct((b, s, CONV_DIM), BF16),
        jax.ShapeDtypeStruct((b, s, ATTN_DIM), BF16),
        jax.ShapeDtypeStruct((b, s, ATTN_DIM), BF16),
        jax.ShapeDtypeStruct((b, ATTN_DIM, s), BF16),
        jax.ShapeDtypeStruct((b, nb, 1, ATTN_DIM), F32),
        jax.ShapeDtypeStruct((b, s, GLA_QK_DIM), F32),
        jax.ShapeDtypeStruct((b, s, GLA_QK_DIM), F32),
        jax.ShapeDtypeStruct((b, s, GLA_V_DIM), F32),
        jax.ShapeDtypeStruct((b, s, GLA_V_DIM), F32),
        jax.ShapeDtypeStruct((b, s, GZ_PAD), F32),
    )
    out_specs = (
        tok(CONV_DIM), tok(ATTN_DIM), tok(ATTN_DIM),
        pl.BlockSpec((None, ATTN_DIM, ts), lambda bi, si: (bi, 0, si)),
        pl.BlockSpec((None, blocks_per_tile, 1, ATTN_DIM), lambda bi, si: (bi, si, 0, 0)),
        tok(GLA_QK_DIM), tok(GLA_QK_DIM), tok(GLA_V_DIM), tok(GLA_V_DIM), tok(GZ_PAD),
    )
    ins = (x, gmix, wconv, wattn, wgla, wgz, convw, convb, qn, kn, hsum)
    in_specs = [tok(d)] + [full(a) for a in ins[1:]]
    scratch = [pltpu.VMEM((8, CONV_DIM), F32)]
    if moe is not None:
        dest, route, ys_tiles = moe
        slots = _tile_slots(dest, ts)
        slot_block = lambda idx: pl.BlockSpec((None, 1, TOP_K * ts), idx, memory_space=pltpu.SMEM)
        last_tile = b * ns - 1
        ins = (slots, slots, x, route.reshape(b, s, ROUTE_OUT), ys_tiles) + ins[1:]
        in_specs = [
            slot_block(lambda bi, si: (bi * ns + si, 0, 0)),
            slot_block(lambda bi, si: (jnp.minimum(bi * ns + si + 1, last_tile), 0, 0)),
            tok(d), tok(ROUTE_OUT), pl.BlockSpec(memory_space=pl.ANY),
        ] + in_specs[1:]
        out_shapes = (jax.ShapeDtypeStruct((b, s, d), F32),) + out_shapes
        out_specs = (tok(d),) + out_specs
        scratch += [pltpu.VMEM((2, TOP_K, ts * SUBLANES, LANES), F32), pltpu.SemaphoreType.DMA((2,))]
    return pl.pallas_call(
        functools.partial(_inproj_kernel, combine=moe is not None),
        grid=(b, ns),
        in_specs=in_specs,
        out_specs=out_specs,
        out_shape=out_shapes,
        scratch_shapes=scratch,
        compiler_params=_cparams(2),
        name="inproj",
    )(*ins)


def _select_kernel(q_ref, k_ref, kmean_ref, qp_ref, kp_ref):
    own = pl.program_id(1)
    tq = q_ref.shape[0]
    nb = kmean_ref.shape[0]
    half = ATTN_HEAD_DIM
    lane = lax.broadcasted_iota(jnp.int32, (tq, LANES), 1)
    blk = lax.broadcasted_iota(jnp.int32, (nb, tq), 0).astype(F32)
    own_f = own.astype(F32)
    onehot = jnp.where(lane - half == own, 1.0, 0.0).astype(F32)
    kmean = kmean_ref[...].astype(BF16)
    for p in range(ATTN_HEADS // 2):
        qpair = q_ref[:, p * LANES:(p + 1) * LANES]
        kpair = k_ref[:, p * LANES:(p + 1) * LANES].astype(F32)
        kmpair = kmean[:, p * LANES:(p + 1) * LANES]
        qpair_f = qpair.astype(F32)
        for sub in range(2):
            h = 2 * p + sub
            lane_sel = (lane >= sub * half) & (lane < (sub + 1) * half)
            qh = jnp.where(lane_sel, qpair_f, 0.0).astype(BF16)
            gate_t = _dot_nt(kmpair, qh)
            g = jnp.where(blk < own_f, gate_t, -jnp.inf)
            alive = jnp.ones((nb, tq), F32)
            sel = jnp.where(blk == own_f, 1.0, 0.0)
            for r in range(MOBA_TOPK):
                ga = jnp.where(alive > 0.0, g, -jnp.inf)
                mx = jnp.max(ga, axis=0, keepdims=True)
                cand = jnp.where((alive > 0.0) & (g == mx), blk, 2.0 * LANES)
                first = jnp.min(cand, axis=0, keepdims=True)
                hit = blk == first
                sel = jnp.where(hit, jnp.maximum(sel, jnp.where(own_f > r, 1.0, 0.0)), sel)
                alive = jnp.where(hit, 0.0, alive)
            m_t = jnp.where(sel > 0.0, 0.0, MASK_VALUE).astype(F32)
            m_t = jnp.concatenate([m_t, jnp.zeros((LANES - nb, tq), F32)], axis=0)
            m = jnp.transpose(m_t)
            m = pltpu.roll(m, half, 1)
            m = jnp.where((lane >= half) & (lane < half + nb), m, 0.0)
            qs = qpair_f if sub == 0 else pltpu.roll(qpair_f, half, 1)
            ks = kpair if sub == 0 else pltpu.roll(kpair, half, 1)
            qp_ref[h] = jnp.where(lane < half, qs, m).astype(BF16)
            kp_ref[h] = jnp.where(lane < half, ks, onehot).astype(BF16)


def _select(q, k, kmean):
    b, s, _ = q.shape
    tq = MOBA_BLOCK
    nb = s // tq
    assert nb <= LANES - ATTN_HEAD_DIM, "block one-hot must fit beside the head dim in one lane tile"
    tok = pl.BlockSpec((None, tq, ATTN_DIM), lambda bi, si: (bi, si, 0))
    slab = pl.BlockSpec((None, ATTN_HEADS, tq, LANES), lambda bi, si: (bi, 0, si, 0))
    shp = jax.ShapeDtypeStruct((b, ATTN_HEADS, s, LANES), BF16)
    return pl.pallas_call(
        _select_kernel,
        grid=(b, nb),
        in_specs=[tok, tok, pl.BlockSpec((None, nb, ATTN_DIM), lambda bi, si: (bi, 0, 0))],
        out_specs=(slab, slab),
        out_shape=(shp, shp),
        compiler_params=_cparams(2),
        name="moba_select",
    )(q, k, kmean)


def _attn_kernel(qp_ref, kp_ref, vt_ref, bias0_ref, bias1_ref, o_ref, m_ref, l_ref, acc_ref,
                 sa_ref, sb_ref, st_ref):
    own = pl.program_id(2)
    tq = qp_ref.shape[1]
    far_rows = ATTN_FAR_CHUNK * tq
    sub = lax.broadcasted_iota(jnp.int32, (LANES, tq), 0)
    n_far = jnp.maximum(own - 1, 0) // ATTN_FAR_CHUNK

    def scores(hh, start, n):
        return _dot_nt(kp_ref[hh, pl.ds(start, n * tq), :], qp_ref[hh])

    def far_scores(dst_ref, chunk):
        start = pl.multiple_of(chunk * far_rows, far_rows)
        for hh in range(2):
            dst_ref[hh] = scores(hh, start, ATTN_FAR_CHUNK)

    def softmax_pv(hh, s_ref, start, n):
        blocks = [s_ref[hh, w * tq:(w + 1) * tq, :] for w in range(n)]
        m_prev = m_ref[hh]
        m_new = jnp.maximum(m_prev, jnp.max(functools.reduce(jnp.maximum, blocks), axis=0, keepdims=True))
        alpha = jnp.exp(m_prev - m_new)
        ps = [jnp.exp(blk - m_new) for blk in blocks]
        l_ref[hh] = alpha * l_ref[hh] + jnp.sum(functools.reduce(jnp.add, ps), axis=0, keepdims=True)
        pt = jnp.concatenate([x.astype(BF16) for x in ps], axis=0)
        acc_ref[hh] = alpha * acc_ref[hh] + _dot(vt_ref[:, pl.ds(start, n * tq)], pt)
        m_ref[hh] = m_new

    for hh in range(2):
        m_ref[hh] = jnp.full((1, tq), MASK_VALUE, F32)
        l_ref[hh] = jnp.zeros((1, tq), F32)
        acc_ref[hh] = jnp.zeros((LANES, tq), F32)

    @pl.when(n_far > 0)
    def _():
        far_scores(sa_ref, 0)

    def far_softmax(s_ref, chunk):
        for hh in range(2):
            softmax_pv(hh, s_ref, pl.multiple_of(chunk * far_rows, far_rows), ATTN_FAR_CHUNK)

    def far_pair(j, carry):
        c0 = 2 * j
        far_scores(sb_ref, c0 + 1)
        far_softmax(sa_ref, c0)

        @pl.when(c0 + 2 < n_far)
        def _():
            far_scores(sa_ref, c0 + 2)
            far_softmax(sb_ref, c0 + 1)

        return carry

    lax.fori_loop(0, n_far // 2, far_pair, 0)

    left = jnp.maximum(own - 1, 0) % ATTN_FAR_CHUNK
    tail_start = pl.multiple_of(n_far * far_rows, tq)
    last_even = (n_far % 2) == 1

    def finish(n, start, last_ref):
        for hh in range(2):
            s = scores(hh, start, n)
            for w in range(n):
                sw = s[w * tq:(w + 1) * tq, :]
                if w == n - 1:
                    sw = sw + bias0_ref[hh]
                elif w == n - 2:
                    sw = sw + bias1_ref[hh]
                st_ref[hh, w * tq:(w + 1) * tq, :] = sw
        if last_ref is not None:
            far_softmax(last_ref, n_far - 1)
        for hh in range(2):
            softmax_pv(hh, st_ref, start, n)

    @pl.when(own == 0)
    def _():
        finish(1, 0, None)

    for n_left in range(ATTN_FAR_CHUNK):
        right_size = (own >= 1) & (left == n_left)

        @pl.when(right_size & (n_far == 0))
        def _():
            finish(n_left + 2, tail_start, None)

        @pl.when(right_size & (n_far > 0) & last_even)
        def _():
            finish(n_left + 2, tail_start, sa_ref)

        @pl.when(right_size & (n_far > 0) & jnp.logical_not(last_even))
        def _():
            finish(n_left + 2, tail_start, sb_ref)

    outs = [acc_ref[hh] / l_ref[hh] for hh in range(2)]
    o_t = jnp.where(sub < ATTN_HEAD_DIM, outs[0], outs[1])
    o_ref[...] = jnp.transpose(o_t).astype(BF16)


def _attention(qp, kp, vt, bias0, bias1):
    b, nh, s, _ = qp.shape
    tq = MOBA_BLOCK
    nq = s // tq
    return pl.pallas_call(
        _attn_kernel,
        grid=(b, nh // 2, nq),
        in_specs=[
            pl.BlockSpec((None, 2, tq, LANES), lambda bi, pi, qi: (bi, pi, qi, 0)),
            pl.BlockSpec((None, 2, s, LANES), lambda bi, pi, qi: (bi, pi, 0, 0)),
            pl.BlockSpec((None, LANES, s), lambda bi, pi, qi: (bi, pi, 0)),
            pl.BlockSpec((2, tq, tq), lambda bi, pi, qi: (pi, 0, 0)),
            pl.BlockSpec((2, tq, tq), lambda bi, pi, qi: (pi, 0, 0)),
        ],
        out_specs=pl.BlockSpec((None, tq, LANES), lambda bi, pi, qi: (bi, qi, pi)),
        out_shape=jax.ShapeDtypeStruct((b, s, ATTN_DIM), BF16),
        scratch_shapes=[
            pltpu.VMEM((2, 1, tq), F32),
            pltpu.VMEM((2, 1, tq), F32),
            pltpu.VMEM((2, LANES, tq), F32),
            pltpu.VMEM((2, ATTN_FAR_CHUNK * tq, tq), F32),
            pltpu.VMEM((2, ATTN_FAR_CHUNK * tq, tq), F32),
            pltpu.VMEM((2, ATTN_TAIL * tq, tq), F32),
        ],
        compiler_params=_cparams(3),
        name="moba_attention",
    )(qp, kp, vt, bias0, bias1)


def _t5_bucket(rel):
    n = jnp.maximum(rel, 0)
    max_exact = REL_BUCKETS // 2
    scaled = (jnp.log(jnp.maximum(n, max_exact).astype(F32) / max_exact)
              / math.log(REL_MAX_DISTANCE / max_exact))
    large = jnp.minimum(max_exact + (scaled * (REL_BUCKETS - max_exact)).astype(jnp.int32), REL_BUCKETS - 1)
    return jnp.where(n < max_exact, n, large)


def _attn_biases(rel_bias_table):
    table = rel_bias_table.astype(F32).T
    pos = jnp.arange(MOBA_BLOCK)
    rel0 = pos[None, :] - pos[:, None]
    far = table[:, REL_BUCKETS - 1][:, None, None]

    rel = jnp.arange(1 - 2 * MOBA_BLOCK, 2 * MOBA_BLOCK)
    hit = _t5_bucket(rel)[None, :, None] == jnp.arange(REL_BUCKETS)
    by_rel = jnp.sum(jnp.where(hit, table[:, None, :], 0.0), axis=-1)
    zero = 2 * MOBA_BLOCK - 1

    def tile(offset):
        rows = [by_rel[:, zero + offset - kk:zero + offset - kk + MOBA_BLOCK] for kk in range(MOBA_BLOCK)]
        return jnp.stack(rows, axis=1)

    bias0 = jnp.where(rel0[None] >= 0, tile(0) - far, MASK_VALUE)
    bias1 = tile(MOBA_BLOCK) - far
    return bias0, bias1


def _gla_kernel(gq_ref, gk_ref, gv_ref, gr_ref, gz_ref, wa_ref, ba_ref, on_ref, y_ref, state_ref):
    s_idx = pl.program_id(1)
    tg = gq_ref.shape[0]
    dk, dv, ck = GLA_KEY_DIM, GLA_VALUE_DIM, GLA_CHUNK

    @pl.when(s_idx == 0)
    def _():
        state_ref[...] = jnp.zeros_like(state_ref)

    z = gz_ref[...].astype(BF16)
    log_a = jax.nn.log_sigmoid(_dot(z, wa_ref[...]) + ba_ref[...]) / GLA_GATE_TEMP
    row = lax.broadcasted_iota(jnp.int32, log_a.shape, 0) % ck
    bcum = log_a
    shift = 1
    while shift < ck:
        bcum = bcum + jnp.where(row >= shift, pltpu.roll(bcum, shift, 0), 0.0)
        shift *= 2
    q = gq_ref[...] * (dk ** -0.5)
    k = gk_ref[...]
    tri = (lax.broadcasted_iota(jnp.int32, (ck, ck), 0) >= lax.broadcasted_iota(jnp.int32, (ck, ck), 1))
    states = [state_ref[h] for h in range(GLA_HEADS)]
    for c in range(tg // ck):
        rows = slice(c * ck, (c + 1) * ck)
        for h in range(GLA_HEADS):
            kcols = slice(h * dk, (h + 1) * dk)
            vcols = slice(h * dv, (h + 1) * dv)
            bc = bcum[rows, kcols]
            btot = bc[ck - 1:ck, :]
            qe = (q[rows, kcols] * jnp.exp(bc)).astype(BF16)
            ke = (k[rows, kcols] * jnp.exp(-bc)).astype(BF16)
            kd = (k[rows, kcols] * jnp.exp(btot - bc)).astype(BF16)
            vc = gv_ref[rows, vcols].astype(BF16)
            att = jnp.where(tri, _dot_nt(qe, ke), 0.0).astype(BF16)
            st = states[h]
            o = _dot(att, vc) + _dot_nt(qe, st.astype(BF16))
            states[h] = st * jnp.exp(btot) + _dot_tn(vc, kd)
            o = _rms(o, on_ref[...])
            r = gr_ref[rows, vcols]
            y_ref[rows, vcols] = (o * (r * jax.nn.sigmoid(r))).astype(BF16)
    for h in range(GLA_HEADS):
        state_ref[h] = states[h]


def _gla(gq, gk, gv, gr, gz, wa, ba, on):
    b, s, _ = gq.shape
    tg = 256
    tok = lambda w: pl.BlockSpec((None, tg, w), lambda bi, si: (bi, si, 0))
    full = lambda arr: pl.BlockSpec(arr.shape, lambda bi, si: (0,) * arr.ndim)
    return pl.pallas_call(
        _gla_kernel,
        grid=(b, s // tg),
        in_specs=[tok(GLA_QK_DIM), tok(GLA_QK_DIM), tok(GLA_V_DIM), tok(GLA_V_DIM), tok(GZ_PAD),
                  full(wa), full(ba), full(on)],
        out_specs=tok(GLA_V_DIM),
        out_shape=jax.ShapeDtypeStruct((b, s, GLA_V_DIM), BF16),
        scratch_shapes=[pltpu.VMEM((GLA_HEADS, GLA_VALUE_DIM, GLA_KEY_DIM), F32)],
        compiler_params=_cparams(2),
        name="gla",
    )(gq, gk, gv, gr, gz, wa, ba, on)


def _merge_route_kernel(x_ref, yc_ref, ya_ref, yg_ref, gmix_ref, wg_ref, wb_ref, wo_ref, gffn_ref,
                        wr_hi_ref, wr_lo_ref, br_ref, xo_ref, h2_ref, route_ref, counts_ref, run_ref):
    x = x_ref[...]
    h = _rms(x, gmix_ref[...]).astype(BF16)
    merged = None
    for n, y_ref in enumerate((yc_ref, ya_ref, yg_ref)):
        term = jax.nn.sigmoid(_dot(h, wg_ref[n])) * _dot(y_ref[...], wb_ref[n])
        merged = term if merged is None else merged + term
    xo = x + _dot(merged.astype(BF16), wo_ref[...])
    xo_ref[...] = xo
    h2 = _rms(xo, gffn_ref[...])
    _rows_to_tiles(h2_ref, h2)

    h_hi, h_lo = _split_bf16(h2)
    logits = (_dot(h_hi, wr_hi_ref[...]) + _dot(h_lo, wr_hi_ref[...]) + _dot(h_hi, wr_lo_ref[...])
              + br_ref[...])
    lane = lax.broadcasted_iota(jnp.int32, logits.shape, 1).astype(F32)
    big = 4.0 * ROUTE_LANES
    lg = jnp.where(lane < N_GROUPS, logits, -jnp.inf)
    gmax = jnp.max(lg, axis=1, keepdims=True)
    gidx = jnp.min(jnp.where(lg == gmax, lane, big), axis=1, keepdims=True)
    p_group_top = 1.0 / jnp.sum(jnp.exp(lg - gmax), axis=1, keepdims=True)
    lo_lane = N_GROUPS + gidx * EXPERTS_PER_GROUP
    le = jnp.where((lane >= lo_lane) & (lane < lo_lane + EXPERTS_PER_GROUP), logits, -jnp.inf)
    emax = jnp.max(le, axis=1, keepdims=True)
    i1 = jnp.min(jnp.where(le == emax, lane, big), axis=1, keepdims=True)
    esum = jnp.sum(jnp.exp(le - emax), axis=1, keepdims=True)
    le2 = jnp.where(lane == i1, -jnp.inf, le)
    emax2 = jnp.max(le2, axis=1, keepdims=True)
    i2 = jnp.min(jnp.where(le2 == emax2, lane, big), axis=1, keepdims=True)
    p1 = 1.0 / esum
    p2 = jnp.exp(emax2 - emax) / esum
    psum = p1 + p2
    w1 = p_group_top * p1 / psum
    w2 = p_group_top * p2 / psum
    e1 = i1 - N_GROUPS
    e2 = i2 - N_GROUPS

    @pl.when(pl.program_id(0) == 0)
    def _():
        run_ref[...] = jnp.zeros_like(run_ref)

    tm = x.shape[0]
    oh1 = jnp.where(lane == i1, 1.0, 0.0)
    oh2 = jnp.where(lane == i2, 1.0, 0.0)
    ohs = oh1 + oh2
    lower = (lax.broadcasted_iota(jnp.int32, (tm, tm), 0) > lax.broadcasted_iota(jnp.int32, (tm, tm), 1))
    before = _dot(jnp.where(lower, 1.0, 0.0).astype(BF16), ohs.astype(BF16)) + run_ref[...]
    rank1 = jnp.sum(oh1 * before, axis=1, keepdims=True)
    rank2 = jnp.sum(oh2 * before, axis=1, keepdims=True)
    run_ref[...] = run_ref[...] + jnp.sum(ohs, axis=0, keepdims=True)
    counts_ref[...] = run_ref[...]

    rl = lax.broadcasted_iota(jnp.int32, (tm, ROUTE_OUT), 1)
    rec = jnp.zeros((tm, ROUTE_OUT), F32)
    for slot, val in enumerate((e1, e2, w1, w2, rank1, rank2)):
        rec = jnp.where(rl == slot, val, rec)
    route_ref[...] = rec


def _merge_route(x2d, yc, ya, yg, gmix, wg, wb, wo, gffn, wr_hi, wr_lo, br):
    t, d = x2d.shape
    tm = 512
    tok = lambda w: pl.BlockSpec((tm, w), lambda i: (i, 0))
    full = lambda arr: pl.BlockSpec(arr.shape, lambda i: (0,) * arr.ndim)
    ins = (x2d, yc, ya, yg, gmix, wg, wb, wo, gffn, wr_hi, wr_lo, br)
    return pl.pallas_call(
        _merge_route_kernel,
        grid=(t // tm,),
        in_specs=[tok(d), tok(CONV_DIM), tok(ATTN_DIM), tok(GLA_V_DIM)] + [full(a) for a in ins[4:]],
        out_specs=(tok(d), pl.BlockSpec((tm * SUBLANES, LANES), lambda i: (i, 0)), tok(ROUTE_OUT),
                   pl.BlockSpec((1, ROUTE_LANES), lambda i: (0, 0))),
        out_shape=(jax.ShapeDtypeStruct((t, d), F32), jax.ShapeDtypeStruct((t * SUBLANES, LANES), F32),
                   jax.ShapeDtypeStruct((t, ROUTE_OUT), F32), jax.ShapeDtypeStruct((1, ROUTE_LANES), F32)),
        scratch_shapes=[pltpu.VMEM((1, ROUTE_LANES), F32)],
        compiler_params=_cparams(1),
        name="merge_route",
    )(*ins)


def _tile_slots(dest, tc):
    nt = dest.shape[0] // tc
    return dest.reshape(nt, tc, TOP_K).transpose(0, 2, 1).reshape(nt, 1, TOP_K * tc)


def _dispatch_kernel(last_blk_ref, dest_ref, h2_ref, xs_hbm, zeros_ref, sem):
    tc = h2_ref.shape[0] // SUBLANES

    @pl.when(pl.program_id(0) == 0)
    def _():
        zeros_ref[...] = jnp.zeros_like(zeros_ref)
        blk_rows = EXPERT_BLOCK * SUBLANES

        def zero_fill(first_row):
            return pltpu.make_async_copy(
                zeros_ref, xs_hbm.at[pl.ds(pl.multiple_of(first_row * SUBLANES, SUBLANES), blk_rows)], sem)

        for go in (lambda c: c.start(), lambda c: c.wait()):
            for e in range(N_EXPERTS):
                @pl.when(last_blk_ref[e] >= 0)
                def _():
                    go(zero_fill(last_blk_ref[e]))

            def unused(blk, carry):
                go(zero_fill(blk * EXPERT_BLOCK))
                return carry

            lax.fori_loop(last_blk_ref[N_EXPERTS], xs_hbm.shape[0] // blk_rows, unused, 0)

    def row_copy(r, kk):
        src = pl.multiple_of(r * SUBLANES, SUBLANES)
        dst = pl.multiple_of(dest_ref[0, kk * tc + r] * SUBLANES, SUBLANES)
        return pltpu.make_async_copy(h2_ref.at[pl.ds(src, SUBLANES)], xs_hbm.at[pl.ds(dst, SUBLANES)], sem)

    def issue(r, carry):
        row_copy(r, 0).start(priority=0)
        row_copy(r, 1).start(priority=1)
        return carry

    lax.fori_loop(0, tc, issue, 0, unroll=ROW_DMA_UNROLL)
    for _ in range(TOP_K):
        pltpu.make_async_copy(h2_ref, xs_hbm.at[pl.ds(0, tc * SUBLANES)], sem).wait()


def _dispatch(dest, h2_tiles, n_rows, last_blk):
    t = h2_tiles.shape[0] // SUBLANES
    tc = 256
    grid_spec = pltpu.PrefetchScalarGridSpec(
        num_scalar_prefetch=1,
        grid=(t // tc,),
        in_specs=[
            pl.BlockSpec((None, 1, TOP_K * tc), lambda i, lb: (i, 0, 0), memory_space=pltpu.SMEM),
            pl.BlockSpec((tc * SUBLANES, LANES), lambda i, lb: (i, 0)),
        ],
        out_specs=pl.BlockSpec(memory_space=pl.ANY),
        scratch_shapes=[pltpu.VMEM((EXPERT_BLOCK * SUBLANES, LANES), F32), pltpu.SemaphoreType.DMA(())],
    )
    return pl.pallas_call(
        _dispatch_kernel,
        grid_spec=grid_spec,
        out_shape=jax.ShapeDtypeStruct((n_rows * SUBLANES, LANES), F32),
        compiler_params=_cparams(1),
        name="moe_dispatch",
    )(last_blk, _tile_slots(dest, tc), h2_tiles)


def _expert_kernel(blk_expert_ref, n_used_ref, xs_ref, wg_ref, wu_ref, wd_ref, ys_ref,
                   wg_bf, wu_bf, wd_bf):
    i = pl.program_id(0)
    rb = xs_ref.shape[0] // SUBLANES
    new_expert = (i == 0) | (blk_expert_ref[i] != blk_expert_ref[jnp.maximum(i - 1, 0)])

    @pl.when(new_expert)
    def _():
        wg_bf[...] = wg_ref[...].astype(BF16)
        wu_bf[...] = wu_ref[...].astype(BF16)
        wd_bf[...] = wd_ref[...].astype(BF16)

    @pl.when(i < n_used_ref[0])
    def _():
        xb = jnp.concatenate(_tiles_to_rows(xs_ref, rb), axis=1).astype(BF16)
        gate = _dot(xb, wg_bf[...])
        up = _dot(xb, wu_bf[...])
        act = (gate * jax.nn.sigmoid(gate) * up).astype(BF16)
        _rows_to_tiles(ys_ref, _dot(act, wd_bf[...]))

    @pl.when(i >= n_used_ref[0])
    def _():
        ys_ref[...] = jnp.zeros_like(ys_ref)


def _experts(blk_expert, n_used, xs_tiles, wg, wu, wd, layer):
    d = SUBLANES * LANES
    n_blocks = blk_expert.shape[0]
    rb = EXPERT_BLOCK
    row_block = (rb * SUBLANES, LANES)
    grid_spec = pltpu.PrefetchScalarGridSpec(
        num_scalar_prefetch=2,
        grid=(n_blocks,),
        in_specs=[
            pl.BlockSpec(row_block, lambda i, be, nu: (jnp.minimum(i, nu[0] - 1), 0)),
            pl.BlockSpec((None, None, d, EXPERT_FF), lambda i, be, nu: (layer, be[i], 0, 0)),
            pl.BlockSpec((None, None, d, EXPERT_FF), lambda i, be, nu: (layer, be[i], 0, 0)),
            pl.BlockSpec((None, None, EXPERT_FF, d), lambda i, be, nu: (layer, be[i], 0, 0)),
        ],
        out_specs=pl.BlockSpec(row_block, lambda i, be, nu: (i, 0)),
        scratch_shapes=[pltpu.VMEM((d, EXPERT_FF), BF16), pltpu.VMEM((d, EXPERT_FF), BF16),
                        pltpu.VMEM((EXPERT_FF, d), BF16)],
    )
    return pl.pallas_call(
        _expert_kernel,
        grid_spec=grid_spec,
        out_shape=jax.ShapeDtypeStruct(xs_tiles.shape, F32),
        compiler_params=_cparams(1),
        name="moe_experts",
    )(blk_expert, n_used, xs_tiles, wg, wu, wd)


def _combine_kernel(dest_ref, x_ref, route_ref, ys_hbm, o_ref, buf, sem):
    tc = x_ref.shape[0]
    _start_expert_row_gather(dest_ref, ys_hbm, buf, sem, tc)
    _finish_expert_row_gather(ys_hbm, buf, sem, tc)
    for j, chunk in enumerate(_moe_residual_chunks(x_ref, route_ref, buf, tc)):
        o_ref[:, j * LANES:(j + 1) * LANES] = chunk


def _combine(dest, x2d, route, ys_tiles):
    t, d = x2d.shape
    tc = 256
    return pl.pallas_call(
        _combine_kernel,
        grid=(t // tc,),
        in_specs=[
            pl.BlockSpec((None, 1, TOP_K * tc), lambda i: (i, 0, 0), memory_space=pltpu.SMEM),
            pl.BlockSpec((tc, d), lambda i: (i, 0)),
            pl.BlockSpec((tc, ROUTE_OUT), lambda i: (i, 0)),
            pl.BlockSpec(memory_space=pl.ANY),
        ],
        out_specs=pl.BlockSpec((tc, d), lambda i: (i, 0)),
        out_shape=jax.ShapeDtypeStruct((t, d), F32),
        scratch_shapes=[pltpu.VMEM((TOP_K, tc * SUBLANES, LANES), F32), pltpu.SemaphoreType.DMA(())],
        compiler_params=_cparams(1),
        name="moe_combine",
    )(_tile_slots(dest, tc), x2d, route, ys_tiles)


def _dispatch_plan(route, counts, t):
    e_ids = route[:, :TOP_K].astype(jnp.int32)
    rank = route[:, 2 * TOP_K:3 * TOP_K].astype(jnp.int32)
    counts = counts[0, N_GROUPS:N_GROUPS + N_EXPERTS].astype(jnp.int32)
    padded = ((counts + EXPERT_BLOCK - 1) // EXPERT_BLOCK) * EXPERT_BLOCK
    pad_end = jnp.cumsum(padded)
    pad_start = pad_end - padded
    onehot = e_ids[:, :, None] == jnp.arange(N_EXPERTS, dtype=jnp.int32)
    dest = rank + jnp.sum(jnp.where(onehot, pad_start, 0), axis=-1)
    n_blocks = -(-(t * TOP_K) // EXPERT_BLOCK) + N_EXPERTS
    blk_start = jnp.arange(n_blocks, dtype=jnp.int32) * EXPERT_BLOCK
    blk_expert = jnp.minimum(jnp.sum(blk_start[:, None] >= pad_end[None, :], axis=1), N_EXPERTS - 1)
    n_used = jnp.maximum(pad_end[-1:] // EXPERT_BLOCK, 1).astype(jnp.int32)
    last_blk = jnp.concatenate([jnp.where(padded > 0, pad_end - EXPERT_BLOCK, -1), n_used]).astype(jnp.int32)
    return blk_expert.astype(jnp.int32), n_used, dest, n_blocks * EXPERT_BLOCK, last_blk


def kernel(x, rel_bias_table, norm_mix, w_in, conv_w, conv_b, q_norm, k_norm, w_gla_alpha, b_gla_alpha,
           gla_out_norm, w_merge_gate, w_branch, w_out, norm_ffn, w_router_group, b_router_group,
           w_router_expert, b_router_expert, w_expert_gate, w_expert_up, w_expert_down):
    b, s, d = x.shape
    t = b * s
    depth = w_in.shape[0]
    assert s % MOBA_BLOCK == 0 and t % EXPERT_BLOCK == 0
    assert d == SUBLANES * LANES, "row-granular DMAs store each activation row as one (8, 128) tile"
    bias0, bias1 = _attn_biases(rel_bias_table)
    head_id = jnp.arange(ATTN_DIM) // ATTN_HEAD_DIM
    hsum = (head_id[:, None] == head_id[None, :]).astype(BF16)
    c3 = 3 * CONV_DIM
    a3 = c3 + 3 * ATTN_DIM
    g3 = a3 + 2 * GLA_QK_DIM + 2 * GLA_V_DIM
    moe = None
    for l in range(depth):
        w_l = w_in[l].astype(BF16)
        w_gz = jnp.pad(w_l[:, g3:], ((0, 0), (0, GZ_PAD - GLA_GATE_RANK)))
        w_alpha = jnp.pad(w_gla_alpha[l].astype(BF16), ((0, GZ_PAD - GLA_GATE_RANK), (0, 0)))
        outs = _inproj(
            x, norm_mix[l][None], w_l[:, :c3], w_l[:, c3:a3], w_l[:, a3:g3], w_gz,
            conv_w[l], conv_b[l][None],
            jnp.tile(q_norm[l], ATTN_HEADS)[None], jnp.tile(k_norm[l], ATTN_HEADS)[None], hsum, moe=moe)
        if moe is not None:
            x, outs = outs[0], outs[1:]
        yconv, q, k, v, kmean, gq, gk, gv, gr, gz = outs
        qp, kp = _select(q, k, kmean.reshape(b, s // MOBA_BLOCK, ATTN_DIM))
        yattn = _attention(qp, kp, v, bias0, bias1)
        ygla = _gla(gq, gk, gv, gr, gz, w_alpha, b_gla_alpha[l][None],
                    gla_out_norm[l][None])
        w_r = jnp.concatenate([w_router_group[l], w_router_expert[l]], axis=1)
        w_r = jnp.pad(w_r, ((0, 0), (0, ROUTE_LANES - w_r.shape[1])))
        b_r = jnp.pad(jnp.concatenate([b_router_group[l], b_router_expert[l]]),
                      (0, ROUTE_LANES - N_GROUPS - N_EXPERTS))[None]
        wr_hi, wr_lo = _split_bf16(w_r)
        xo, h2, route, counts = _merge_route(
            x.reshape(t, d), yconv.reshape(t, -1), yattn.reshape(t, -1), ygla.reshape(t, -1),
            norm_mix[l][None], w_merge_gate[l].astype(BF16), w_branch[l].astype(BF16),
            w_out[l].astype(BF16), norm_ffn[l][None], wr_hi, wr_lo, b_r)
        blk_expert, n_used, dest, n_rows, last_blk = _dispatch_plan(route, counts, t)
        xs = _dispatch(dest, h2, n_rows, last_blk)
        ys = _experts(blk_expert, n_used, xs, w_expert_gate, w_expert_up, w_expert_down, l)
        x = xo.reshape(b, s, d)
        moe = (dest, route, ys)
    return _combine(dest, xo, route, ys).reshape(b, s, d)
```

```python
import functools
import math

import jax
import jax.numpy as jnp
import numpy as np
from jax import lax
from jax.experimental import pallas as pl
from jax.experimental.pallas import tpu as pltpu

CONV_DIM = 512
CONV_WIDTH = 3
ATTN_HEADS = 8
ATTN_HEAD_DIM = 64
ATTN_DIM = ATTN_HEADS * ATTN_HEAD_DIM
MOBA_BLOCK = 256
MOBA_TOPK = 3
REL_BUCKETS = 32
REL_MAX_DISTANCE = 128
GLA_HEADS = 4
GLA_KEY_DIM = 64
GLA_VALUE_DIM = 128
GLA_QK_DIM = GLA_HEADS * GLA_KEY_DIM
GLA_V_DIM = GLA_HEADS * GLA_VALUE_DIM
GLA_GATE_RANK = 16
GLA_GATE_TEMP = 16.0
GLA_CHUNK = 64
N_GROUPS = 4
EXPERTS_PER_GROUP = 8
N_EXPERTS = N_GROUPS * EXPERTS_PER_GROUP
TOP_K = 2
EXPERT_FF = 512
EXPERT_BLOCK = 256
RMS_EPS = 1e-6

LANES = 128
VMEM_LIMIT_BYTES = 56 * 1024 * 1024

MASK_VALUE = -1e30
ROUTE_LANES = 128
ROUTE_OUT = 8
ATTN_FAR_CHUNK = 4
ATTN_TAIL = ATTN_FAR_CHUNK + 1
ROW_DMA_UNROLL = 8
GZ_PAD = LANES

F32 = jnp.float32
BF16 = jnp.bfloat16


def _cparams(n_axes):
    return pltpu.CompilerParams(
        dimension_semantics=("arbitrary",) * n_axes,
        vmem_limit_bytes=VMEM_LIMIT_BYTES,
    )


def _rms(x, gain):
    return x * lax.rsqrt(jnp.mean(x * x, axis=-1, keepdims=True) + RMS_EPS) * gain


def _split_bf16(x):
    hi = x.astype(BF16)
    lo = (x - hi.astype(F32)).astype(BF16)
    return hi, lo


def _dot(a, b):
    return jnp.dot(a, b, preferred_element_type=F32)


def _dot_nt(a, b):
    return lax.dot_general(a, b, (((1,), (1,)), ((), ())), preferred_element_type=F32)


SUBLANES = 8


def _rows_to_tiles(dst_ref, x):
    n = x.shape[0]
    for j in range(SUBLANES):
        dst_ref[pl.ds(j, n, stride=SUBLANES), :] = x[:, j * LANES:(j + 1) * LANES]


def _tiles_to_rows(src_ref, n):
    return [src_ref[pl.ds(j, n, stride=SUBLANES), :] for j in range(SUBLANES)]


def _dot_tn(a, b):
    return lax.dot_general(a, b, (((0,), (0,)), ((), ())), preferred_element_type=F32)


def _start_expert_row_gather(slots_ref, ys_hbm, buf, sem, n):
    def row_copy(r, kk):
        src = pl.multiple_of(slots_ref[0, kk * n + r] * SUBLANES, SUBLANES)
        dst = pl.multiple_of(r * SUBLANES, SUBLANES)
        return pltpu.make_async_copy(ys_hbm.at[pl.ds(src, SUBLANES)], buf.at[kk, pl.ds(dst, SUBLANES)], sem)

    def issue(r, carry):
        row_copy(r, 0).start(priority=0)
        row_copy(r, 1).start(priority=1)
        return carry

    lax.fori_loop(0, n, issue, 0, unroll=ROW_DMA_UNROLL)


def _finish_expert_row_gather(ys_hbm, buf, sem, n):
    for kk in range(TOP_K):
        pltpu.make_async_copy(ys_hbm.at[pl.ds(0, n * SUBLANES)], buf.at[kk], sem).wait()


def _pipelined_expert_rows(slots_ref, next_slots_ref, ys_hbm, rows_buf, sem, tile, n_tiles, n):
    slot = tile % 2

    @pl.when(tile == 0)
    def _():
        _start_expert_row_gather(slots_ref, ys_hbm, rows_buf.at[0], sem.at[0], n)

    @pl.when(tile + 1 < n_tiles)
    def _():
        _start_expert_row_gather(next_slots_ref, ys_hbm, rows_buf.at[1 - slot], sem.at[1 - slot], n)

    _finish_expert_row_gather(ys_hbm, rows_buf.at[slot], sem.at[slot], n)
    return rows_buf.at[slot]


def _moe_residual_chunks(x_ref, route_ref, buf, n):
    route = route_ref[...]
    w1 = route[:, TOP_K:TOP_K + 1]
    w2 = route[:, TOP_K + 1:TOP_K + 2]
    y1 = _tiles_to_rows(buf.at[0], n)
    y2 = _tiles_to_rows(buf.at[1], n)
    return [x_ref[:, j * LANES:(j + 1) * LANES] + (y1[j] * w1 + y2[j] * w2) for j in range(SUBLANES)]


def _inproj_kernel(*refs, combine):
    if combine:
        (slots_ref, next_slots_ref, x_ref, route_ref, ys_hbm), refs = refs[:5], refs[5:]
    else:
        x_ref, refs = refs[0], refs[1:]
    (gmix_ref, wconv_ref, wattn_ref, wgla_ref, wgz_ref, convw_ref, convb_ref, qn_ref, kn_ref, hsum_ref), refs = (
        refs[:10], refs[10:])
    if combine:
        xnew_ref, refs = refs[0], refs[1:]
    (yconv_ref, q_ref, k_ref, v_ref, kmean_ref, gq_ref, gk_ref, gv_ref, gr_ref, gz_ref, carry_ref), refs = (
        refs[:11], refs[11:])
    s_idx = pl.program_id(1)
    ts = x_ref.shape[0]
    if combine:
        rows_buf, sem = refs
        rows = _pipelined_expert_rows(
            slots_ref, next_slots_ref, ys_hbm, rows_buf, sem,
            pl.program_id(0) * pl.num_programs(1) + s_idx, pl.num_programs(0) * pl.num_programs(1), ts)
        x = jnp.concatenate(_moe_residual_chunks(x_ref, route_ref, rows, ts), axis=1)
        xnew_ref[...] = x
    else:
        x = x_ref[...]
    h = _rms(x, gmix_ref[...]).astype(BF16)

    c = _dot(h, wconv_ref[...])
    cb = c[:, :CONV_DIM]
    u = c[:, CONV_DIM:2 * CONV_DIM] * c[:, 2 * CONV_DIM:]

    @pl.when(s_idx == 0)
    def _():
        carry_ref[...] = jnp.zeros_like(carry_ref)

    prev = carry_ref[...]
    row = lax.broadcasted_iota(jnp.int32, u.shape, 0)
    u1 = pltpu.roll(u, 1, 0)
    u1 = jnp.where(row == 0, prev[7:8, :], u1)
    u2 = pltpu.roll(u, 2, 0)
    u2 = jnp.where(row == 0, prev[6:7, :], jnp.where(row == 1, prev[7:8, :], u2))
    carry_ref[...] = u[ts - 8:, :]
    y = convb_ref[...] + convw_ref[0:1, :] * u2
    y = y + convw_ref[1:2, :] * u1
    y = y + convw_ref[2:3, :] * u
    yconv_ref[...] = (cb * y).astype(BF16)

    a = _dot(h, wattn_ref[...])
    hsum = hsum_ref[...]

    def head_norm(t, gain):
        hi, lo = _split_bf16(t * t)
        ss = _dot(hi, hsum) + _dot(lo, hsum)
        return t * lax.rsqrt(ss * (1.0 / ATTN_HEAD_DIM) + RMS_EPS) * gain

    qn = head_norm(a[:, :ATTN_DIM], qn_ref[...])
    kn = head_norm(a[:, ATTN_DIM:2 * ATTN_DIM], kn_ref[...])
    q_ref[...] = (qn * (ATTN_HEAD_DIM ** -0.5)).astype(BF16)
    k_ref[...] = kn.astype(BF16)
    v_ref[...] = jnp.transpose(a[:, 2 * ATTN_DIM:]).astype(BF16)
    for j in range(ts // MOBA_BLOCK):
        kmean_ref[j] = jnp.mean(kn[j * MOBA_BLOCK:(j + 1) * MOBA_BLOCK], axis=0, keepdims=True)

    g = _dot(h, wgla_ref[...])
    gq_ref[...] = g[:, :GLA_QK_DIM]
    gk_ref[...] = g[:, GLA_QK_DIM:2 * GLA_QK_DIM]
    gv_ref[...] = g[:, 2 * GLA_QK_DIM:2 * GLA_QK_DIM + GLA_V_DIM]
    gr_ref[...] = g[:, 2 * GLA_QK_DIM + GLA_V_DIM:]
    gz_ref[...] = _dot(h, wgz_ref[...])


def _inproj(x, gmix, wconv, wattn, wgla, wgz, convw, convb, qn, kn, hsum, moe=None):
    b, s, d = x.shape
    blocks_per_tile = 2
    ts = blocks_per_tile * MOBA_BLOCK
    assert s % ts == 0
    nb = s // MOBA_BLOCK
    ns = s // ts
    tok = lambda w: pl.BlockSpec((None, ts, w), lambda bi, si: (bi, si, 0))
    full = lambda arr: pl.BlockSpec(arr.shape, lambda bi, si: (0,) * arr.ndim)
    out_shapes = (
        jax.ShapeDtypeStruct((b, s, CONV_DIM), BF16),
        jax.ShapeDtypeStruct((b, s, ATTN_DIM), BF16),
        jax.ShapeDtypeStruct((b, s, ATTN_DIM), BF16),
        jax.ShapeDtypeStruct((b, ATTN_DIM, s), BF16),
        jax.ShapeDtypeStruct((b, nb, 1, ATTN_DIM), F32),
        jax.ShapeDtypeStruct((b, s, GLA_QK_DIM), F32),
        jax.ShapeDtypeStruct((b, s, GLA_QK_DIM), F32),
        jax.ShapeDtypeStruct((b, s, GLA_V_DIM), F32),
        jax.ShapeDtypeStruct((b, s, GLA_V_DIM), F32),
        jax.ShapeDtypeStruct((b, s, GZ_PAD), F32),
    )
    out_specs = (
        tok(CONV_DIM), tok(ATTN_DIM), tok(ATTN_DIM),
        pl.BlockSpec((None, ATTN_DIM, ts), lambda bi, si: (bi, 0, si)),
        pl.BlockSpec((None, blocks_per_tile, 1, ATTN_DIM), lambda bi, si: (bi, si, 0, 0)),
        tok(GLA_QK_DIM), tok(GLA_QK_DIM), tok(GLA_V_DIM), tok(GLA_V_DIM), tok(GZ_PAD),
    )
    ins = (x, gmix, wconv, wattn, wgla, wgz, convw, convb, qn, kn, hsum)
    in_specs = [tok(d)] + [full(a) for a in ins[1:]]
    scratch = [pltpu.VMEM((8, CONV_DIM), F32)]
    if moe is not None:
        dest, route, ys_tiles = moe
        slots = _tile_slots(dest, ts)
        slot_block = lambda idx: pl.BlockSpec((None, 1, TOP_K * ts), idx, memory_space=pltpu.SMEM)
        last_tile = b * ns - 1
        ins = (slots, slots, x, route.reshape(b, s, ROUTE_OUT), ys_tiles) + ins[1:]
        in_specs = [
            slot_block(lambda bi, si: (bi * ns + si, 0, 0)),
            slot_block(lambda bi, si: (jnp.minimum(bi * ns + si + 1, last_tile), 0, 0)),
            tok(d), tok(ROUTE_OUT), pl.BlockSpec(memory_space=pl.ANY),
        ] + in_specs[1:]
        out_shapes = (jax.ShapeDtypeStruct((b, s, d), F32),) + out_shapes
        out_specs = (tok(d),) + out_specs
        scratch += [pltpu.VMEM((2, TOP_K, ts * SUBLANES, LANES), F32), pltpu.SemaphoreType.DMA((2,))]
    return pl.pallas_call(
        functools.partial(_inproj_kernel, combine=moe is not None),
        grid=(b, ns),
        in_specs=in_specs,
        out_specs=out_specs,
        out_shape=out_shapes,
        scratch_shapes=scratch,
        compiler_params=_cparams(2),
        name="inproj",
    )(*ins)


def _select_kernel(q_ref, k_ref, kmean_ref, qp_ref, kp_ref):
    own = pl.program_id(1)
    tq = q_ref.shape[0]
    nb = kmean_ref.shape[0]
    half = ATTN_HEAD_DIM
    lane = lax.broadcasted_iota(jnp.int32, (tq, LANES), 1)
    blk = lax.broadcasted_iota(jnp.int32, (nb, tq), 0).astype(F32)
    own_f = own.astype(F32)
    onehot = jnp.where(lane - half == own, 1.0, 0.0).astype(F32)
    kmean = kmean_ref[...].astype(BF16)
    for p in range(ATTN_HEADS // 2):
        qpair = q_ref[:, p * LANES:(p + 1) * LANES]
        kpair = k_ref[:, p * LANES:(p + 1) * LANES].astype(F32)
        kmpair = kmean[:, p * LANES:(p + 1) * LANES]
        qpair_f = qpair.astype(F32)
        for sub in range(2):
            h = 2 * p + sub
            lane_sel = (lane >= sub * half) & (lane < (sub + 1) * half)
            qh = jnp.where(lane_sel, qpair_f, 0.0).astype(BF16)
            gate_t = _dot_nt(kmpair, qh)
            g = jnp.where(blk < own_f, gate_t, -jnp.inf)
            alive = jnp.ones((nb, tq), F32)
            sel = jnp.where(blk == own_f, 1.0, 0.0)
            for r in range(MOBA_TOPK):
                ga = jnp.where(alive > 0.0, g, -jnp.inf)
                mx = jnp.max(ga, axis=0, keepdims=True)
                cand = jnp.where((alive > 0.0) & (g == mx), blk, 2.0 * LANES)
                first = jnp.min(cand, axis=0, keepdims=True)
                hit = blk == first
                sel = jnp.where(hit, jnp.maximum(sel, jnp.where(own_f > r, 1.0, 0.0)), sel)
                alive = jnp.where(hit, 0.0, alive)
            m_t = jnp.where(sel > 0.0, 0.0, MASK_VALUE).astype(F32)
            m_t = jnp.concatenate([m_t, jnp.zeros((LANES - nb, tq), F32)], axis=0)
            m = jnp.transpose(m_t)
            m = pltpu.roll(m, half, 1)
            m = jnp.where((lane >= half) & (lane < half + nb), m, 0.0)
            qs = qpair_f if sub == 0 else pltpu.roll(qpair_f, half, 1)
            ks = kpair if sub == 0 else pltpu.roll(kpair, half, 1)
            qp_ref[h] = jnp.where(lane < half, qs, m).astype(BF16)
            kp_ref[h] = jnp.where(lane < half, ks, onehot).astype(BF16)


def _select(q, k, kmean):
    b, s, _ = q.shape
    tq = MOBA_BLOCK
    nb = s // tq
    assert nb <= LANES - ATTN_HEAD_DIM, "block one-hot must fit beside the head dim in one lane tile"
    tok = pl.BlockSpec((None, tq, ATTN_DIM), lambda bi, si: (bi, si, 0))
    slab = pl.BlockSpec((None, ATTN_HEADS, tq, LANES), lambda bi, si: (bi, 0, si, 0))
    shp = jax.ShapeDtypeStruct((b, ATTN_HEADS, s, LANES), BF16)
    return pl.pallas_call(
        _select_kernel,
        grid=(b, nb),
        in_specs=[tok, tok, pl.BlockSpec((None, nb, ATTN_DIM), lambda bi, si: (bi, 0, 0))],
        out_specs=(slab, slab),
        out_shape=(shp, shp),
        compiler_params=_cparams(2),
        name="moba_select",
    )(q, k, kmean)


def _attn_kernel(qp_ref, kp_ref, vt_ref, bias0_ref, bias1_ref, o_ref, m_ref, l_ref, acc_ref,
                 sa_ref, sb_ref, st_ref):
    own = pl.program_id(2)
    tq = qp_ref.shape[1]
    far_rows = ATTN_FAR_CHUNK * tq
    sub = lax.broadcasted_iota(jnp.int32, (LANES, tq), 0)
    n_far = jnp.maximum(own - 1, 0) // ATTN_FAR_CHUNK

    def scores(hh, start, n):
        return _dot_nt(kp_ref[hh, pl.ds(start, n * tq), :], qp_ref[hh])

    def far_scores(dst_ref, chunk):
        start = pl.multiple_of(chunk * far_rows, far_rows)
        for hh in range(2):
            dst_ref[hh] = scores(hh, start, ATTN_FAR_CHUNK)

    def softmax_pv(hh, s_ref, start, n):
        blocks = [s_ref[hh, w * tq:(w + 1) * tq, :] for w in range(n)]
        m_prev = m_ref[hh]
        m_new = jnp.maximum(m_prev, jnp.max(functools.reduce(jnp.maximum, blocks), axis=0, keepdims=True))
        alpha = jnp.exp(m_prev - m_new)
        ps = [jnp.exp(blk - m_new) for blk in blocks]
        l_ref[hh] = alpha * l_ref[hh] + jnp.sum(functools.reduce(jnp.add, ps), axis=0, keepdims=True)
        pt = jnp.concatenate([x.astype(BF16) for x in ps], axis=0)
        acc_ref[hh] = alpha * acc_ref[hh] + _dot(vt_ref[:, pl.ds(start, n * tq)], pt)
        m_ref[hh] = m_new

    for hh in range(2):
        m_ref[hh] = jnp.full((1, tq), MASK_VALUE, F32)
        l_ref[hh] = jnp.zeros((1, tq), F32)
        acc_ref[hh] = jnp.zeros((LANES, tq), F32)

    @pl.when(n_far > 0)
    def _():
        far_scores(sa_ref, 0)

    def far_softmax(s_ref, chunk):
        for hh in range(2):
            softmax_pv(hh, s_ref, pl.multiple_of(chunk * far_rows, far_rows), ATTN_FAR_CHUNK)

    def far_pair(j, carry):
        c0 = 2 * j
        far_scores(sb_ref, c0 + 1)
        far_softmax(sa_ref, c0)

        @pl.when(c0 + 2 < n_far)
        def _():
            far_scores(sa_ref, c0 + 2)
            far_softmax(sb_ref, c0 + 1)

        return carry

    lax.fori_loop(0, n_far // 2, far_pair, 0)

    left = jnp.maximum(own - 1, 0) % ATTN_FAR_CHUNK
    tail_start = pl.multiple_of(n_far * far_rows, tq)
    last_even = (n_far % 2) == 1

    def finish(n, start, last_ref):
        for hh in range(2):
            s = scores(hh, start, n)
            for w in range(n):
                sw = s[w * tq:(w + 1) * tq, :]
                if w == n - 1:
                    sw = sw + bias0_ref[hh]
                elif w == n - 2:
                    sw = sw + bias1_ref[hh]
                st_ref[hh, w * tq:(w + 1) * tq, :] = sw
        if last_ref is not None:
            far_softmax(last_ref, n_far - 1)
        for hh in range(2):
            softmax_pv(hh, st_ref, start, n)

    @pl.when(own == 0)
    def _():
        finish(1, 0, None)

    for n_left in range(ATTN_FAR_CHUNK):
        right_size = (own >= 1) & (left == n_left)

        @pl.when(right_size & (n_far == 0))
        def _():
            finish(n_left + 2, tail_start, None)

        @pl.when(right_size & (n_far > 0) & last_even)
        def _():
            finish(n_left + 2, tail_start, sa_ref)

        @pl.when(right_size & (n_far > 0) & jnp.logical_not(last_even))
        def _():
            finish(n_left + 2, tail_start, sb_ref)

    outs = [acc_ref[hh] / l_ref[hh] for hh in range(2)]
    o_t = jnp.where(sub < ATTN_HEAD_DIM, outs[0], outs[1])
    o_ref[...] = jnp.transpose(o_t).astype(BF16)


def _attention(qp, kp, vt, bias0, bias1):
    b, nh, s, _ = qp.shape
    tq = MOBA_BLOCK
    nq = s // tq
    return pl.pallas_call(
        _attn_kernel,
        grid=(b, nh // 2, nq),
        in_specs=[
            pl.BlockSpec((None, 2, tq, LANES), lambda bi, pi, qi: (bi, pi, qi, 0)),
            pl.BlockSpec((None, 2, s, LANES), lambda bi, pi, qi: (bi, pi, 0, 0)),
            pl.BlockSpec((None, LANES, s), lambda bi, pi, qi: (bi, pi, 0)),
            pl.BlockSpec((2, tq, tq), lambda bi, pi, qi: (pi, 0, 0)),
            pl.BlockSpec((2, tq, tq), lambda bi, pi, qi: (pi, 0, 0)),
        ],
        out_specs=pl.BlockSpec((None, tq, LANES), lambda bi, pi, qi: (bi, qi, pi)),
        out_shape=jax.ShapeDtypeStruct((b, s, ATTN_DIM), BF16),
        scratch_shapes=[
            pltpu.VMEM((2, 1, tq), F32),
            pltpu.VMEM((2, 1, tq), F32),
            pltpu.VMEM((2, LANES, tq), F32),
            pltpu.VMEM((2, ATTN_FAR_CHUNK * tq, tq), F32),
            pltpu.VMEM((2, ATTN_FAR_CHUNK * tq, tq), F32),
            pltpu.VMEM((2, ATTN_TAIL * tq, tq), F32),
        ],
        compiler_params=_cparams(3),
        name="moba_attention",
    )(qp, kp, vt, bias0, bias1)


def _t5_bucket(rel):
    n = jnp.maximum(rel, 0)
    max_exact = REL_BUCKETS // 2
    scaled = (jnp.log(jnp.maximum(n, max_exact).astype(F32) / max_exact)
              / math.log(REL_MAX_DISTANCE / max_exact))
    large = jnp.minimum(max_exact + (scaled * (REL_BUCKETS - max_exact)).astype(jnp.int32), REL_BUCKETS - 1)
    return jnp.where(n < max_exact, n, large)


def _attn_biases(rel_bias_table):
    table = rel_bias_table.astype(F32).T
    pos = jnp.arange(MOBA_BLOCK)
    rel0 = pos[None, :] - pos[:, None]
    far = table[:, REL_BUCKETS - 1][:, None, None]

    rel = jnp.arange(1 - 2 * MOBA_BLOCK, 2 * MOBA_BLOCK)
    hit = _t5_bucket(rel)[None, :, None] == jnp.arange(REL_BUCKETS)
    by_rel = jnp.sum(jnp.where(hit, table[:, None, :], 0.0), axis=-1)
    zero = 2 * MOBA_BLOCK - 1

    def tile(offset):
        rows = [by_rel[:, zero + offset - kk:zero + offset - kk + MOBA_BLOCK] for kk in range(MOBA_BLOCK)]
        return jnp.stack(rows, axis=1)

    bias0 = jnp.where(rel0[None] >= 0, tile(0) - far, MASK_VALUE)
    bias1 = tile(MOBA_BLOCK) - far
    return bias0, bias1


def _gla_kernel(gq_ref, gk_ref, gv_ref, gr_ref, gz_ref, wa_ref, ba_ref, on_ref, y_ref, state_ref):
    s_idx = pl.program_id(1)
    tg = gq_ref.shape[0]
    dk, dv, ck = GLA_KEY_DIM, GLA_VALUE_DIM, GLA_CHUNK

    @pl.when(s_idx == 0)
    def _():
        state_ref[...] = jnp.zeros_like(state_ref)

    z = gz_ref[...].astype(BF16)
    log_a = jax.nn.log_sigmoid(_dot(z, wa_ref[...]) + ba_ref[...]) / GLA_GATE_TEMP
    row = lax.broadcasted_iota(jnp.int32, log_a.shape, 0) % ck
    bcum = log_a
    shift = 1
    while shift < ck:
        bcum = bcum + jnp.where(row >= shift, pltpu.roll(bcum, shift, 0), 0.0)
        shift *= 2
    q = gq_ref[...] * (dk ** -0.5)
    k = gk_ref[...]
    tri = (lax.broadcasted_iota(jnp.int32, (ck, ck), 0) >= lax.broadcasted_iota(jnp.int32, (ck, ck), 1))
    states = [state_ref[h] for h in range(GLA_HEADS)]
    for c in range(tg // ck):
        rows = slice(c * ck, (c + 1) * ck)
        for h in range(GLA_HEADS):
            kcols = slice(h * dk, (h + 1) * dk)
            vcols = slice(h * dv, (h + 1) * dv)
            bc = bcum[rows, kcols]
            btot = bc[ck - 1:ck, :]
            qe = (q[rows, kcols] * jnp.exp(bc)).astype(BF16)
            ke = (k[rows, kcols] * jnp.exp(-bc)).astype(BF16)
            kd = (k[rows, kcols] * jnp.exp(btot - bc)).astype(BF16)
            vc = gv_ref[rows, vcols].astype(BF16)
            att = jnp.where(tri, _dot_nt(qe, ke), 0.0).astype(BF16)
            st = states[h]
            o = _dot(att, vc) + _dot_nt(qe, st.astype(BF16))
            states[h] = st * jnp.exp(btot) + _dot_tn(vc, kd)
            o = _rms(o, on_ref[...])
            r = gr_ref[rows, vcols]
            y_ref[rows, vcols] = (o * (r * jax.nn.sigmoid(r))).astype(BF16)
    for h in range(GLA_HEADS):
        state_ref[h] = states[h]


def _gla(gq, gk, gv, gr, gz, wa, ba, on):
    b, s, _ = gq.shape
    tg = 256
    tok = lambda w: pl.BlockSpec((None, tg, w), lambda bi, si: (bi, si, 0))
    full = lambda arr: pl.BlockSpec(arr.shape, lambda bi, si: (0,) * arr.ndim)
    return pl.pallas_call(
        _gla_kernel,
        grid=(b, s // tg),
        in_specs=[tok(GLA_QK_DIM), tok(GLA_QK_DIM), tok(GLA_V_DIM), tok(GLA_V_DIM), tok(GZ_PAD),
                  full(wa), full(ba), full(on)],
        out_specs=tok(GLA_V_DIM),
        out_shape=jax.ShapeDtypeStruct((b, s, GLA_V_DIM), BF16),
        scratch_shapes=[pltpu.VMEM((GLA_HEADS, GLA_VALUE_DIM, GLA_KEY_DIM), F32)],
        compiler_params=_cparams(2),
        name="gla",
    )(gq, gk, gv, gr, gz, wa, ba, on)


def _merge_route_kernel(x_ref, yc_ref, ya_ref, yg_ref, gmix_ref, wg_ref, wb_ref, wo_ref, gffn_ref,
                        wr_hi_ref, wr_lo_ref, br_ref, xo_ref, h2_ref, route_ref, counts_ref, run_ref):
    x = x_ref[...]
    h = _rms(x, gmix_ref[...]).astype(BF16)
    merged = None
    for n, y_ref in enumerate((yc_ref, ya_ref, yg_ref)):
        term = jax.nn.sigmoid(_dot(h, wg_ref[n])) * _dot(y_ref[...], wb_ref[n])
        merged = term if merged is None else merged + term
    xo = x + _dot(merged.astype(BF16), wo_ref[...])
    xo_ref[...] = xo
    h2 = _rms(xo, gffn_ref[...])
    _rows_to_tiles(h2_ref, h2)

    h_hi, h_lo = _split_bf16(h2)
    logits = (_dot(h_hi, wr_hi_ref[...]) + _dot(h_lo, wr_hi_ref[...]) + _dot(h_hi, wr_lo_ref[...])
              + br_ref[...])
    lane = lax.broadcasted_iota(jnp.int32, logits.shape, 1).astype(F32)
    big = 4.0 * ROUTE_LANES
    lg = jnp.where(lane < N_GROUPS, logits, -jnp.inf)
    gmax = jnp.max(lg, axis=1, keepdims=True)
    gidx = jnp.min(jnp.where(lg == gmax, lane, big), axis=1, keepdims=True)
    p_group_top = 1.0 / jnp.sum(jnp.exp(lg - gmax), axis=1, keepdims=True)
    lo_lane = N_GROUPS + gidx * EXPERTS_PER_GROUP
    le = jnp.where((lane >= lo_lane) & (lane < lo_lane + EXPERTS_PER_GROUP), logits, -jnp.inf)
    emax = jnp.max(le, axis=1, keepdims=True)
    i1 = jnp.min(jnp.where(le == emax, lane, big), axis=1, keepdims=True)
    esum = jnp.sum(jnp.exp(le - emax), axis=1, keepdims=True)
    le2 = jnp.where(lane == i1, -jnp.inf, le)
    emax2 = jnp.max(le2, axis=1, keepdims=True)
    i2 = jnp.min(jnp.where(le2 == emax2, lane, big), axis=1, keepdims=True)
    p1 = 1.0 / esum
    p2 = jnp.exp(emax2 - emax) / esum
    psum = p1 + p2
    w1 = p_group_top * p1 / psum
    w2 = p_group_top * p2 / psum
    e1 = i1 - N_GROUPS
    e2 = i2 - N_GROUPS

    @pl.when(pl.program_id(0) == 0)
    def _():
        run_ref[...] = jnp.zeros_like(run_ref)

    tm = x.shape[0]
    oh1 = jnp.where(lane == i1, 1.0, 0.0)
    oh2 = jnp.where(lane == i2, 1.0, 0.0)
    ohs = oh1 + oh2
    lower = (lax.broadcasted_iota(jnp.int32, (tm, tm), 0) > lax.broadcasted_iota(jnp.int32, (tm, tm), 1))
    before = _dot(jnp.where(lower, 1.0, 0.0).astype(BF16), ohs.astype(BF16)) + run_ref[...]
    rank1 = jnp.sum(oh1 * before, axis=1, keepdims=True)
    rank2 = jnp.sum(oh2 * before, axis=1, keepdims=True)
    run_ref[...] = run_ref[...] + jnp.sum(ohs, axis=0, keepdims=True)
    counts_ref[...] = run_ref[...]

    rl = lax.broadcasted_iota(jnp.int32, (tm, ROUTE_OUT), 1)
    rec = jnp.zeros((tm, ROUTE_OUT), F32)
    for slot, val in enumerate((e1, e2, w1, w2, rank1, rank2)):
        rec = jnp.where(rl == slot, val, rec)
    route_ref[...] = rec


def _merge_route(x2d, yc, ya, yg, gmix, wg, wb, wo, gffn, wr_hi, wr_lo, br):
    t, d = x2d.shape
    tm = 512
    tok = lambda w: pl.BlockSpec((tm, w), lambda i: (i, 0))
    full = lambda arr: pl.BlockSpec(arr.shape, lambda i: (0,) * arr.ndim)
    ins = (x2d, yc, ya, yg, gmix, wg, wb, wo, gffn, wr_hi, wr_lo, br)
    return pl.pallas_call(
        _merge_route_kernel,
        grid=(t // tm,),
        in_specs=[tok(d), tok(CONV_DIM), tok(ATTN_DIM), tok(GLA_V_DIM)] + [full(a) for a in ins[4:]],
        out_specs=(tok(d), pl.BlockSpec((tm * SUBLANES, LANES), lambda i: (i, 0)), tok(ROUTE_OUT),
                   pl.BlockSpec((1, ROUTE_LANES), lambda i: (0, 0))),
        out_shape=(jax.ShapeDtypeStruct((t, d), F32), jax.ShapeDtypeStruct((t * SUBLANES, LANES), F32),
                   jax.ShapeDtypeStruct((t, ROUTE_OUT), F32), jax.ShapeDtypeStruct((1, ROUTE_LANES), F32)),
        scratch_shapes=[pltpu.VMEM((1, ROUTE_LANES), F32)],
        compiler_params=_cparams(1),
        name="merge_route",
    )(*ins)


def _tile_slots(dest, tc):
    nt = dest.shape[0] // tc
    return dest.reshape(nt, tc, TOP_K).transpose(0, 2, 1).reshape(nt, 1, TOP_K * tc)


def _dispatch_kernel(last_blk_ref, dest_ref, h2_ref, xs_hbm, zeros_ref, sem):
    tc = h2_ref.shape[0] // SUBLANES

    @pl.when(pl.program_id(0) == 0)
    def _():
        zeros_ref[...] = jnp.zeros_like(zeros_ref)
        blk_rows = EXPERT_BLOCK * SUBLANES

        def zero_fill(first_row):
            return pltpu.make_async_copy(
                zeros_ref, xs_hbm.at[pl.ds(pl.multiple_of(first_row * SUBLANES, SUBLANES), blk_rows)], sem)

        for go in (lambda c: c.start(), lambda c: c.wait()):
            for e in range(N_EXPERTS):
                @pl.when(last_blk_ref[e] >= 0)
                def _():
                    go(zero_fill(last_blk_ref[e]))

            def unused(blk, carry):
                go(zero_fill(blk * EXPERT_BLOCK))
                return carry

            lax.fori_loop(last_blk_ref[N_EXPERTS], xs_hbm.shape[0] // blk_rows, unused, 0)

    def row_copy(r, kk):
        src = pl.multiple_of(r * SUBLANES, SUBLANES)
        dst = pl.multiple_of(dest_ref[0, kk * tc + r] * SUBLANES, SUBLANES)
        return pltpu.make_async_copy(h2_ref.at[pl.ds(src, SUBLANES)], xs_hbm.at[pl.ds(dst, SUBLANES)], sem)

    def issue(r, carry):
        row_copy(r, 0).start(priority=0)
        row_copy(r, 1).start(priority=1)
        return carry

    lax.fori_loop(0, tc, issue, 0, unroll=ROW_DMA_UNROLL)
    for _ in range(TOP_K):
        pltpu.make_async_copy(h2_ref, xs_hbm.at[pl.ds(0, tc * SUBLANES)], sem).wait()


def _dispatch(dest, h2_tiles, n_rows, last_blk):
    t = h2_tiles.shape[0] // SUBLANES
    tc = 256
    grid_spec = pltpu.PrefetchScalarGridSpec(
        num_scalar_prefetch=1,
        grid=(t // tc,),
        in_specs=[
            pl.BlockSpec((None, 1, TOP_K * tc), lambda i, lb: (i, 0, 0), memory_space=pltpu.SMEM),
            pl.BlockSpec((tc * SUBLANES, LANES), lambda i, lb: (i, 0)),
        ],
        out_specs=pl.BlockSpec(memory_space=pl.ANY),
        scratch_shapes=[pltpu.VMEM((EXPERT_BLOCK * SUBLANES, LANES), F32), pltpu.SemaphoreType.DMA(())],
    )
    return pl.pallas_call(
        _dispatch_kernel,
        grid_spec=grid_spec,
        out_shape=jax.ShapeDtypeStruct((n_rows * SUBLANES, LANES), F32),
        compiler_params=_cparams(1),
        name="moe_dispatch",
    )(last_blk, _tile_slots(dest, tc), h2_tiles)


def _expert_kernel(blk_expert_ref, n_used_ref, xs_ref, wg_ref, wu_ref, wd_ref, ys_ref,
                   wg_bf, wu_bf, wd_bf):
    i = pl.program_id(0)
    rb = xs_ref.shape[0] // SUBLANES
    new_expert = (i == 0) | (blk_expert_ref[i] != blk_expert_ref[jnp.maximum(i - 1, 0)])

    @pl.when(new_expert)
    def _():
        wg_bf[...] = wg_ref[...].astype(BF16)
        wu_bf[...] = wu_ref[...].astype(BF16)
        wd_bf[...] = wd_ref[...].astype(BF16)

    @pl.when(i < n_used_ref[0])
    def _():
        xb = jnp.concatenate(_tiles_to_rows(xs_ref, rb), axis=1).astype(BF16)
        gate = _dot(xb, wg_bf[...])
        up = _dot(xb, wu_bf[...])
        act = (gate * jax.nn.sigmoid(gate) * up).astype(BF16)
        _rows_to_tiles(ys_ref, _dot(act, wd_bf[...]))

    @pl.when(i >= n_used_ref[0])
    def _():
        ys_ref[...] = jnp.zeros_like(ys_ref)


def _experts(blk_expert, n_used, xs_tiles, wg, wu, wd, layer):
    d = SUBLANES * LANES
    n_blocks = blk_expert.shape[0]
    rb = EXPERT_BLOCK
    row_block = (rb * SUBLANES, LANES)
    grid_spec = pltpu.PrefetchScalarGridSpec(
        num_scalar_prefetch=2,
        grid=(n_blocks,),
        in_specs=[
            pl.BlockSpec(row_block, lambda i, be, nu: (jnp.minimum(i, nu[0] - 1), 0)),
            pl.BlockSpec((None, None, d, EXPERT_FF), lambda i, be, nu: (layer, be[i], 0, 0)),
            pl.BlockSpec((None, None, d, EXPERT_FF), lambda i, be, nu: (layer, be[i], 0, 0)),
            pl.BlockSpec((None, None, EXPERT_FF, d), lambda i, be, nu: (layer, be[i], 0, 0)),
        ],
        out_specs=pl.BlockSpec(row_block, lambda i, be, nu: (i, 0)),
        scratch_shapes=[pltpu.VMEM((d, EXPERT_FF), BF16), pltpu.VMEM((d, EXPERT_FF), BF16),
                        pltpu.VMEM((EXPERT_FF, d), BF16)],
    )
    return pl.pallas_call(
        _expert_kernel,
        grid_spec=grid_spec,
        out_shape=jax.ShapeDtypeStruct(xs_tiles.shape, F32),
        compiler_params=_cparams(1),
        name="moe_experts",
    )(blk_expert, n_used, xs_tiles, wg, wu, wd)


def _combine_kernel(slots_ref, next_slots_ref, x_ref, route_ref, ys_hbm, o_ref, rows_buf, sem):
    tc = x_ref.shape[0]
    rows = _pipelined_expert_rows(slots_ref, next_slots_ref, ys_hbm, rows_buf, sem,
                                  pl.program_id(0), pl.num_programs(0), tc)
    for j, chunk in enumerate(_moe_residual_chunks(x_ref, route_ref, rows, tc)):
        o_ref[:, j * LANES:(j + 1) * LANES] = chunk


def _combine(dest, x2d, route, ys_tiles):
    t, d = x2d.shape
    tc = 256
    nt = t // tc
    slots = _tile_slots(dest, tc)
    slot_block = lambda idx: pl.BlockSpec((None, 1, TOP_K * tc), idx, memory_space=pltpu.SMEM)
    return pl.pallas_call(
        _combine_kernel,
        grid=(nt,),
        in_specs=[
            slot_block(lambda i: (i, 0, 0)),
            slot_block(lambda i: (jnp.minimum(i + 1, nt - 1), 0, 0)),
            pl.BlockSpec((tc, d), lambda i: (i, 0)),
            pl.BlockSpec((tc, ROUTE_OUT), lambda i: (i, 0)),
            pl.BlockSpec(memory_space=pl.ANY),
        ],
        out_specs=pl.BlockSpec((tc, d), lambda i: (i, 0)),
        out_shape=jax.ShapeDtypeStruct((t, d), F32),
        scratch_shapes=[pltpu.VMEM((2, TOP_K, tc * SUBLANES, LANES), F32), pltpu.SemaphoreType.DMA((2,))],
        compiler_params=_cparams(1),
        name="moe_combine",
    )(slots, slots, x2d, route, ys_tiles)


def _dispatch_plan(route, counts, t):
    e_ids = route[:, :TOP_K].astype(jnp.int32)
    rank = route[:, 2 * TOP_K:3 * TOP_K].astype(jnp.int32)
    counts = counts[0, N_GROUPS:N_GROUPS + N_EXPERTS].astype(jnp.int32)
    padded = ((counts + EXPERT_BLOCK - 1) // EXPERT_BLOCK) * EXPERT_BLOCK
    pad_end = jnp.cumsum(padded)
    pad_start = pad_end - padded
    onehot = e_ids[:, :, None] == jnp.arange(N_EXPERTS, dtype=jnp.int32)
    dest = rank + jnp.sum(jnp.where(onehot, pad_start, 0), axis=-1)
    n_blocks = -(-(t * TOP_K) // EXPERT_BLOCK) + N_EXPERTS
    blk_start = jnp.arange(n_blocks, dtype=jnp.int32) * EXPERT_BLOCK
    blk_expert = jnp.minimum(jnp.sum(blk_start[:, None] >= pad_end[None, :], axis=1), N_EXPERTS - 1)
    n_used = jnp.maximum(pad_end[-1:] // EXPERT_BLOCK, 1).astype(jnp.int32)
    last_blk = jnp.concatenate([jnp.where(padded > 0, pad_end - EXPERT_BLOCK, -1), n_used]).astype(jnp.int32)
    return blk_expert.astype(jnp.int32), n_used, dest, n_blocks * EXPERT_BLOCK, last_blk


def kernel(x, rel_bias_table, norm_mix, w_in, conv_w, conv_b, q_norm, k_norm, w_gla_alpha, b_gla_alpha,
           gla_out_norm, w_merge_gate, w_branch, w_out, norm_ffn, w_router_group, b_router_group,
           w_router_expert, b_router_expert, w_expert_gate, w_expert_up, w_expert_down):
    b, s, d = x.shape
    t = b * s
    depth = w_in.shape[0]
    assert s % MOBA_BLOCK == 0 and t % EXPERT_BLOCK == 0
    assert d == SUBLANES * LANES, "row-granular DMAs store each activation row as one (8, 128) tile"
    bias0, bias1 = _attn_biases(rel_bias_table)
    head_id = jnp.arange(ATTN_DIM) // ATTN_HEAD_DIM
    hsum = (head_id[:, None] == head_id[None, :]).astype(BF16)
    c3 = 3 * CONV_DIM
    a3 = c3 + 3 * ATTN_DIM
    g3 = a3 + 2 * GLA_QK_DIM + 2 * GLA_V_DIM
    moe = None
    for l in range(depth):
        w_l = w_in[l].astype(BF16)
        w_gz = jnp.pad(w_l[:, g3:], ((0, 0), (0, GZ_PAD - GLA_GATE_RANK)))
        w_alpha = jnp.pad(w_gla_alpha[l].astype(BF16), ((0, GZ_PAD - GLA_GATE_RANK), (0, 0)))
        outs = _inproj(
            x, norm_mix[l][None], w_l[:, :c3], w_l[:, c3:a3], w_l[:, a3:g3], w_gz,
            conv_w[l], conv_b[l][None],
            jnp.tile(q_norm[l], ATTN_HEADS)[None], jnp.tile(k_norm[l], ATTN_HEADS)[None], hsum, moe=moe)
        if moe is not None:
            x, outs = outs[0], outs[1:]
        yconv, q, k, v, kmean, gq, gk, gv, gr, gz = outs
        qp, kp = _select(q, k, kmean.reshape(b, s // MOBA_BLOCK, ATTN_DIM))
        yattn = _attention(qp, kp, v, bias0, bias1)
        ygla = _gla(gq, gk, gv, gr, gz, w_alpha, b_gla_alpha[l][None],
                    gla_out_norm[l][None])
        w_r = jnp.concatenate([w_router_group[l], w_router_expert[l]], axis=1)
        w_r = jnp.pad(w_r, ((0, 0), (0, ROUTE_LANES - w_r.shape[1])))
        b_r = jnp.pad(jnp.concatenate([b_router_group[l], b_router_expert[l]]),
                      (0, ROUTE_LANES - N_GROUPS - N_EXPERTS))[None]
        wr_hi, wr_lo = _split_bf16(w_r)
        xo, h2, route, counts = _merge_route(
            x.reshape(t, d), yconv.reshape(t, -1), yattn.reshape(t, -1), ygla.reshape(t, -1),
            norm_mix[l][None], w_merge_gate[l].astype(BF16), w_branch[l].astype(BF16),
            w_out[l].astype(BF16), norm_ffn[l][None], wr_hi, wr_lo, b_r)
        blk_expert, n_used, dest, n_rows, last_blk = _dispatch_plan(route, counts, t)
        xs = _dispatch(dest, h2, n_rows, last_blk)
        ys = _experts(blk_expert, n_used, xs, w_expert_gate, w_expert_up, w_expert_down, l)
        x = xo.reshape(b, s, d)
        moe = (dest, route, ys)
    return _combine(dest, xo, route, ys).reshape(b, s, d)
```

```python
import functools
import math

import jax
import jax.numpy as jnp
import numpy as np
from jax import lax
from jax.experimental import pallas as pl
from jax.experimental.pallas import tpu as pltpu

CONV_DIM = 512
CONV_WIDTH = 3
ATTN_HEADS = 8
ATTN_HEAD_DIM = 64
ATTN_DIM = ATTN_HEADS * ATTN_HEAD_DIM
MOBA_BLOCK = 256
MOBA_TOPK = 3
REL_BUCKETS = 32
REL_MAX_DISTANCE = 128
GLA_HEADS = 4
GLA_KEY_DIM = 64
GLA_VALUE_DIM = 128
GLA_QK_DIM = GLA_HEADS * GLA_KEY_DIM
GLA_V_DIM = GLA_HEADS * GLA_VALUE_DIM
GLA_GATE_RANK = 16
GLA_GATE_TEMP = 16.0
GLA_CHUNK = 64
N_GROUPS = 4
EXPERTS_PER_GROUP = 8
N_EXPERTS = N_GROUPS * EXPERTS_PER_GROUP
TOP_K = 2
EXPERT_FF = 512
EXPERT_BLOCK = 256
RMS_EPS = 1e-6

LANES = 128
VMEM_LIMIT_BYTES = 56 * 1024 * 1024

MASK_VALUE = -1e30
ROUTE_LANES = 128
ROUTE_OUT = 8
ATTN_FAR_CHUNK = 4
ATTN_TAIL = ATTN_FAR_CHUNK + 1
ROW_DMA_UNROLL = 8
GZ_PAD = LANES

F32 = jnp.float32
BF16 = jnp.bfloat16


def _cparams(n_axes):
    return pltpu.CompilerParams(
        dimension_semantics=("arbitrary",) * n_axes,
        vmem_limit_bytes=VMEM_LIMIT_BYTES,
    )


def _rms(x, gain):
    return x * lax.rsqrt(jnp.mean(x * x, axis=-1, keepdims=True) + RMS_EPS) * gain


def _split_bf16(x):
    hi = x.astype(BF16)
    lo = (x - hi.astype(F32)).astype(BF16)
    return hi, lo


def _dot(a, b):
    return jnp.dot(a, b, preferred_element_type=F32)


def _dot_nt(a, b):
    return lax.dot_general(a, b, (((1,), (1,)), ((), ())), preferred_element_type=F32)


SUBLANES = 8


def _rows_to_tiles(dst_ref, x):
    n = x.shape[0]
    for j in range(SUBLANES):
        dst_ref[pl.ds(j, n, stride=SUBLANES), :] = x[:, j * LANES:(j + 1) * LANES]


def _tiles_to_rows(src_ref, n):
    return [src_ref[pl.ds(j, n, stride=SUBLANES), :] for j in range(SUBLANES)]


def _dot_tn(a, b):
    return lax.dot_general(a, b, (((0,), (0,)), ((), ())), preferred_element_type=F32)


def _start_expert_row_gather(slots_ref, ys_hbm, buf, sem, n):
    def row_copy(r, kk):
        src = pl.multiple_of(slots_ref[0, kk * n + r] * SUBLANES, SUBLANES)
        dst = pl.multiple_of(r * SUBLANES, SUBLANES)
        return pltpu.make_async_copy(ys_hbm.at[pl.ds(src, SUBLANES)], buf.at[kk, pl.ds(dst, SUBLANES)], sem)

    def issue(r, carry):
        row_copy(r, 0).start(priority=0)
        row_copy(r, 1).start(priority=1)
        return carry

    lax.fori_loop(0, n, issue, 0, unroll=ROW_DMA_UNROLL)


def _finish_expert_row_gather(ys_hbm, buf, sem, n):
    for kk in range(TOP_K):
        pltpu.make_async_copy(ys_hbm.at[pl.ds(0, n * SUBLANES)], buf.at[kk], sem).wait()


def _pipelined_expert_rows(slots_ref, next_slots_ref, ys_hbm, rows_buf, sem, tile, n_tiles, n):
    slot = tile % 2

    @pl.when(tile == 0)
    def _():
        _start_expert_row_gather(slots_ref, ys_hbm, rows_buf.at[0], sem.at[0], n)

    @pl.when(tile + 1 < n_tiles)
    def _():
        _start_expert_row_gather(next_slots_ref, ys_hbm, rows_buf.at[1 - slot], sem.at[1 - slot], n)

    _finish_expert_row_gather(ys_hbm, rows_buf.at[slot], sem.at[slot], n)
    return rows_buf.at[slot]


def _moe_residual_chunks(x_ref, route_ref, buf, n):
    route = route_ref[...]
    w1 = route[:, TOP_K:TOP_K + 1]
    w2 = route[:, TOP_K + 1:TOP_K + 2]
    y1 = _tiles_to_rows(buf.at[0], n)
    y2 = _tiles_to_rows(buf.at[1], n)
    return [x_ref[:, j * LANES:(j + 1) * LANES] + (y1[j] * w1 + y2[j] * w2) for j in range(SUBLANES)]


def _inproj_kernel(x_ref, gmix_ref, wconv_ref, wattn_ref, wgla_ref, wgz_ref, convw_ref, convb_ref,
                   qn_ref, kn_ref, hsum_ref,
                   yconv_ref, q_ref, k_ref, v_ref, kmean_ref, gq_ref, gk_ref, gv_ref, gr_ref, gz_ref,
                   carry_ref):
    s_idx = pl.program_id(1)
    ts = x_ref.shape[0]
    h = _rms(x_ref[...], gmix_ref[...]).astype(BF16)

    c = _dot(h, wconv_ref[...])
    cb = c[:, :CONV_DIM]
    u = c[:, CONV_DIM:2 * CONV_DIM] * c[:, 2 * CONV_DIM:]

    @pl.when(s_idx == 0)
    def _():
        carry_ref[...] = jnp.zeros_like(carry_ref)

    prev = carry_ref[...]
    row = lax.broadcasted_iota(jnp.int32, u.shape, 0)
    u1 = pltpu.roll(u, 1, 0)
    u1 = jnp.where(row == 0, prev[7:8, :], u1)
    u2 = pltpu.roll(u, 2, 0)
    u2 = jnp.where(row == 0, prev[6:7, :], jnp.where(row == 1, prev[7:8, :], u2))
    carry_ref[...] = u[ts - 8:, :]
    y = convb_ref[...] + convw_ref[0:1, :] * u2
    y = y + convw_ref[1:2, :] * u1
    y = y + convw_ref[2:3, :] * u
    yconv_ref[...] = (cb * y).astype(BF16)

    a = _dot(h, wattn_ref[...])
    hsum = hsum_ref[...]

    def head_norm(t, gain):
        hi, lo = _split_bf16(t * t)
        ss = _dot(hi, hsum) + _dot(lo, hsum)
        return t * lax.rsqrt(ss * (1.0 / ATTN_HEAD_DIM) + RMS_EPS) * gain

    qn = head_norm(a[:, :ATTN_DIM], qn_ref[...])
    kn = head_norm(a[:, ATTN_DIM:2 * ATTN_DIM], kn_ref[...])
    q_ref[...] = (qn * (ATTN_HEAD_DIM ** -0.5)).astype(BF16)
    k_ref[...] = kn.astype(BF16)
    v_ref[...] = jnp.transpose(a[:, 2 * ATTN_DIM:]).astype(BF16)
    for j in range(ts // MOBA_BLOCK):
        kmean_ref[j] = jnp.mean(kn[j * MOBA_BLOCK:(j + 1) * MOBA_BLOCK], axis=0, keepdims=True)

    g = _dot(h, wgla_ref[...])
    gq_ref[...] = g[:, :GLA_QK_DIM]
    gk_ref[...] = g[:, GLA_QK_DIM:2 * GLA_QK_DIM]
    gv_ref[...] = g[:, 2 * GLA_QK_DIM:2 * GLA_QK_DIM + GLA_V_DIM]
    gr_ref[...] = g[:, 2 * GLA_QK_DIM + GLA_V_DIM:]
    gz_ref[...] = _dot(h, wgz_ref[...])


def _inproj(x, gmix, wconv, wattn, wgla, wgz, convw, convb, qn, kn, hsum):
    b, s, d = x.shape
    blocks_per_tile = 2
    ts = blocks_per_tile * MOBA_BLOCK
    assert s % ts == 0
    nb = s // MOBA_BLOCK
    tok = lambda w: pl.BlockSpec((None, ts, w), lambda bi, si: (bi, si, 0))
    full = lambda arr: pl.BlockSpec(arr.shape, lambda bi, si: (0,) * arr.ndim)
    out_shapes = (
        jax.ShapeDtypeStruct((b, s, CONV_DIM), BF16),
        jax.ShapeDtypeStruct((b, s, ATTN_DIM), BF16),
        jax.ShapeDtypeStruct((b, s, ATTN_DIM), BF16),
        jax.ShapeDtypeStruct((b, ATTN_DIM, s), BF16),
        jax.ShapeDtypeStruct((b, nb, 1, ATTN_DIM), F32),
        jax.ShapeDtypeStruct((b, s, GLA_QK_DIM), F32),
        jax.ShapeDtypeStruct((b, s, GLA_QK_DIM), F32),
        jax.ShapeDtypeStruct((b, s, GLA_V_DIM), F32),
        jax.ShapeDtypeStruct((b, s, GLA_V_DIM), F32),
        jax.ShapeDtypeStruct((b, s, GZ_PAD), F32),
    )
    out_specs = (
        tok(CONV_DIM), tok(ATTN_DIM), tok(ATTN_DIM),
        pl.BlockSpec((None, ATTN_DIM, ts), lambda bi, si: (bi, 0, si)),
        pl.BlockSpec((None, blocks_per_tile, 1, ATTN_DIM), lambda bi, si: (bi, si, 0, 0)),
        tok(GLA_QK_DIM), tok(GLA_QK_DIM), tok(GLA_V_DIM), tok(GLA_V_DIM), tok(GZ_PAD),
    )
    ins = (x, gmix, wconv, wattn, wgla, wgz, convw, convb, qn, kn, hsum)
    in_specs = [tok(d)] + [full(a) for a in ins[1:]]
    return pl.pallas_call(
        _inproj_kernel,
        grid=(b, s // ts),
        in_specs=in_specs,
        out_specs=out_specs,
        out_shape=out_shapes,
        scratch_shapes=[pltpu.VMEM((8, CONV_DIM), F32)],
        compiler_params=_cparams(2),
        name="inproj",
    )(*ins)


def _select_kernel(q_ref, k_ref, kmean_ref, qp_ref, kp_ref):
    own = pl.program_id(1)
    tq = q_ref.shape[0]
    nb = kmean_ref.shape[0]
    half = ATTN_HEAD_DIM
    lane = lax.broadcasted_iota(jnp.int32, (tq, LANES), 1)
    blk = lax.broadcasted_iota(jnp.int32, (nb, tq), 0).astype(F32)
    own_f = own.astype(F32)
    onehot = jnp.where(lane - half == own, 1.0, 0.0).astype(F32)
    kmean = kmean_ref[...].astype(BF16)
    for p in range(ATTN_HEADS // 2):
        qpair = q_ref[:, p * LANES:(p + 1) * LANES]
        kpair = k_ref[:, p * LANES:(p + 1) * LANES].astype(F32)
        kmpair = kmean[:, p * LANES:(p + 1) * LANES]
        qpair_f = qpair.astype(F32)
        for sub in range(2):
            h = 2 * p + sub
            lane_sel = (lane >= sub * half) & (lane < (sub + 1) * half)
            qh = jnp.where(lane_sel, qpair_f, 0.0).astype(BF16)
            gate_t = _dot_nt(kmpair, qh)
            g = jnp.where(blk < own_f, gate_t, -jnp.inf)
            alive = jnp.ones((nb, tq), F32)
            sel = jnp.where(blk == own_f, 1.0, 0.0)
            for r in range(MOBA_TOPK):
                ga = jnp.where(alive > 0.0, g, -jnp.inf)
                mx = jnp.max(ga, axis=0, keepdims=True)
                cand = jnp.where((alive > 0.0) & (g == mx), blk, 2.0 * LANES)
                first = jnp.min(cand, axis=0, keepdims=True)
                hit = blk == first
                sel = jnp.where(hit, jnp.maximum(sel, jnp.where(own_f > r, 1.0, 0.0)), sel)
                alive = jnp.where(hit, 0.0, alive)
            m_t = jnp.where(sel > 0.0, 0.0, MASK_VALUE).astype(F32)
            m_t = jnp.concatenate([m_t, jnp.zeros((LANES - nb, tq), F32)], axis=0)
            m = jnp.transpose(m_t)
            m = pltpu.roll(m, half, 1)
            m = jnp.where((lane >= half) & (lane < half + nb), m, 0.0)
            qs = qpair_f if sub == 0 else pltpu.roll(qpair_f, half, 1)
            ks = kpair if sub == 0 else pltpu.roll(kpair, half, 1)
            qp_ref[h] = jnp.where(lane < half, qs, m).astype(BF16)
            kp_ref[h] = jnp.where(lane < half, ks, onehot).astype(BF16)


def _select(q, k, kmean):
    b, s, _ = q.shape
    tq = MOBA_BLOCK
    nb = s // tq
    assert nb <= LANES - ATTN_HEAD_DIM, "block one-hot must fit beside the head dim in one lane tile"
    tok = pl.BlockSpec((None, tq, ATTN_DIM), lambda bi, si: (bi, si, 0))
    slab = pl.BlockSpec((None, ATTN_HEADS, tq, LANES), lambda bi, si: (bi, 0, si, 0))
    shp = jax.ShapeDtypeStruct((b, ATTN_HEADS, s, LANES), BF16)
    return pl.pallas_call(
        _select_kernel,
        grid=(b, nb),
        in_specs=[tok, tok, pl.BlockSpec((None, nb, ATTN_DIM), lambda bi, si: (bi, 0, 0))],
        out_specs=(slab, slab),
        out_shape=(shp, shp),
        compiler_params=_cparams(2),
        name="moba_select",
    )(q, k, kmean)


def _attn_kernel(qp_ref, kp_ref, vt_ref, bias0_ref, bias1_ref, o_ref, m_ref, l_ref, acc_ref,
                 sa_ref, sb_ref, st_ref):
    own = pl.program_id(2)
    tq = qp_ref.shape[1]
    far_rows = ATTN_FAR_CHUNK * tq
    sub = lax.broadcasted_iota(jnp.int32, (LANES, tq), 0)
    n_far = jnp.maximum(own - 1, 0) // ATTN_FAR_CHUNK

    def scores(hh, start, n):
        return _dot_nt(kp_ref[hh, pl.ds(start, n * tq), :], qp_ref[hh])

    def far_scores(dst_ref, chunk):
        start = pl.multiple_of(chunk * far_rows, far_rows)
        for hh in range(2):
            dst_ref[hh] = scores(hh, start, ATTN_FAR_CHUNK)

    def softmax_pv(hh, s_ref, start, n):
        blocks = [s_ref[hh, w * tq:(w + 1) * tq, :] for w in range(n)]
        m_prev = m_ref[hh]
        m_new = jnp.maximum(m_prev, jnp.max(functools.reduce(jnp.maximum, blocks), axis=0, keepdims=True))
        alpha = jnp.exp(m_prev - m_new)
        ps = [jnp.exp(blk - m_new) for blk in blocks]
        l_ref[hh] = alpha * l_ref[hh] + jnp.sum(functools.reduce(jnp.add, ps), axis=0, keepdims=True)
        pt = jnp.concatenate([x.astype(BF16) for x in ps], axis=0)
        acc_ref[hh] = alpha * acc_ref[hh] + _dot(vt_ref[:, pl.ds(start, n * tq)], pt)
        m_ref[hh] = m_new

    for hh in range(2):
        m_ref[hh] = jnp.full((1, tq), MASK_VALUE, F32)
        l_ref[hh] = jnp.zeros((1, tq), F32)
        acc_ref[hh] = jnp.zeros((LANES, tq), F32)

    @pl.when(n_far > 0)
    def _():
        far_scores(sa_ref, 0)

    def far_softmax(s_ref, chunk):
        for hh in range(2):
            softmax_pv(hh, s_ref, pl.multiple_of(chunk * far_rows, far_rows), ATTN_FAR_CHUNK)

    def far_pair(j, carry):
        c0 = 2 * j
        far_scores(sb_ref, c0 + 1)
        far_softmax(sa_ref, c0)

        @pl.when(c0 + 2 < n_far)
        def _():
            far_scores(sa_ref, c0 + 2)
            far_softmax(sb_ref, c0 + 1)

        return carry

    lax.fori_loop(0, n_far // 2, far_pair, 0)

    left = jnp.maximum(own - 1, 0) % ATTN_FAR_CHUNK
    tail_start = pl.multiple_of(n_far * far_rows, tq)
    last_even = (n_far % 2) == 1

    def finish(n, start, last_ref):
        for hh in range(2):
            s = scores(hh, start, n)
            for w in range(n):
                sw = s[w * tq:(w + 1) * tq, :]
                if w == n - 1:
                    sw = sw + bias0_ref[hh]
                elif w == n - 2:
                    sw = sw + bias1_ref[hh]
                st_ref[hh, w * tq:(w + 1) * tq, :] = sw
        if last_ref is not None:
            far_softmax(last_ref, n_far - 1)
        for hh in range(2):
            softmax_pv(hh, st_ref, start, n)

    @pl.when(own == 0)
    def _():
        finish(1, 0, None)

    for n_left in range(ATTN_FAR_CHUNK):
        right_size = (own >= 1) & (left == n_left)

        @pl.when(right_size & (n_far == 0))
        def _():
            finish(n_left + 2, tail_start, None)

        @pl.when(right_size & (n_far > 0) & last_even)
        def _():
            finish(n_left + 2, tail_start, sa_ref)

        @pl.when(right_size & (n_far > 0) & jnp.logical_not(last_even))
        def _():
            finish(n_left + 2, tail_start, sb_ref)

    outs = [acc_ref[hh] / l_ref[hh] for hh in range(2)]
    o_t = jnp.where(sub < ATTN_HEAD_DIM, outs[0], outs[1])
    o_ref[...] = jnp.transpose(o_t).astype(BF16)


def _attention(qp, kp, vt, bias0, bias1):
    b, nh, s, _ = qp.shape
    tq = MOBA_BLOCK
    nq = s // tq
    return pl.pallas_call(
        _attn_kernel,
        grid=(b, nh // 2, nq),
        in_specs=[
            pl.BlockSpec((None, 2, tq, LANES), lambda bi, pi, qi: (bi, pi, qi, 0)),
            pl.BlockSpec((None, 2, s, LANES), lambda bi, pi, qi: (bi, pi, 0, 0)),
            pl.BlockSpec((None, LANES, s), lambda bi, pi, qi: (bi, pi, 0)),
            pl.BlockSpec((2, tq, tq), lambda bi, pi, qi: (pi, 0, 0)),
            pl.BlockSpec((2, tq, tq), lambda bi, pi, qi: (pi, 0, 0)),
        ],
        out_specs=pl.BlockSpec((None, tq, LANES), lambda bi, pi, qi: (bi, qi, pi)),
        out_shape=jax.ShapeDtypeStruct((b, s, ATTN_DIM), BF16),
        scratch_shapes=[
            pltpu.VMEM((2, 1, tq), F32),
            pltpu.VMEM((2, 1, tq), F32),
            pltpu.VMEM((2, LANES, tq), F32),
            pltpu.VMEM((2, ATTN_FAR_CHUNK * tq, tq), F32),
            pltpu.VMEM((2, ATTN_FAR_CHUNK * tq, tq), F32),
            pltpu.VMEM((2, ATTN_TAIL * tq, tq), F32),
        ],
        compiler_params=_cparams(3),
        name="moba_attention",
    )(qp, kp, vt, bias0, bias1)


def _t5_bucket(rel):
    n = jnp.maximum(rel, 0)
    max_exact = REL_BUCKETS // 2
    scaled = (jnp.log(jnp.maximum(n, max_exact).astype(F32) / max_exact)
              / math.log(REL_MAX_DISTANCE / max_exact))
    large = jnp.minimum(max_exact + (scaled * (REL_BUCKETS - max_exact)).astype(jnp.int32), REL_BUCKETS - 1)
    return jnp.where(n < max_exact, n, large)


def _attn_biases(rel_bias_table):
    table = rel_bias_table.astype(F32).T
    pos = jnp.arange(MOBA_BLOCK)
    rel0 = pos[None, :] - pos[:, None]
    far = table[:, REL_BUCKETS - 1][:, None, None]

    rel = jnp.arange(1 - 2 * MOBA_BLOCK, 2 * MOBA_BLOCK)
    hit = _t5_bucket(rel)[None, :, None] == jnp.arange(REL_BUCKETS)
    by_rel = jnp.sum(jnp.where(hit, table[:, None, :], 0.0), axis=-1)
    zero = 2 * MOBA_BLOCK - 1

    def tile(offset):
        rows = [by_rel[:, zero + offset - kk:zero + offset - kk + MOBA_BLOCK] for kk in range(MOBA_BLOCK)]
        return jnp.stack(rows, axis=1)

    bias0 = jnp.where(rel0[None] >= 0, tile(0) - far, MASK_VALUE)
    bias1 = tile(MOBA_BLOCK) - far
    return bias0, bias1


def _gla_kernel(gq_ref, gk_ref, gv_ref, gr_ref, gz_ref, wa_ref, ba_ref, on_ref, y_ref, state_ref):
    s_idx = pl.program_id(1)
    tg = gq_ref.shape[0]
    dk, dv, ck = GLA_KEY_DIM, GLA_VALUE_DIM, GLA_CHUNK

    @pl.when(s_idx == 0)
    def _():
        state_ref[...] = jnp.zeros_like(state_ref)

    z = gz_ref[...].astype(BF16)
    log_a = jax.nn.log_sigmoid(_dot(z, wa_ref[...]) + ba_ref[...]) / GLA_GATE_TEMP
    row = lax.broadcasted_iota(jnp.int32, log_a.shape, 0) % ck
    bcum = log_a
    shift = 1
    while shift < ck:
        bcum = bcum + jnp.where(row >= shift, pltpu.roll(bcum, shift, 0), 0.0)
        shift *= 2
    q = gq_ref[...] * (dk ** -0.5)
    k = gk_ref[...]
    tri = (lax.broadcasted_iota(jnp.int32, (ck, ck), 0) >= lax.broadcasted_iota(jnp.int32, (ck, ck), 1))
    states = [state_ref[h] for h in range(GLA_HEADS)]
    for c in range(tg // ck):
        rows = slice(c * ck, (c + 1) * ck)
        for h in range(GLA_HEADS):
            kcols = slice(h * dk, (h + 1) * dk)
            vcols = slice(h * dv, (h + 1) * dv)
            bc = bcum[rows, kcols]
            btot = bc[ck - 1:ck, :]
            qe = (q[rows, kcols] * jnp.exp(bc)).astype(BF16)
            ke = (k[rows, kcols] * jnp.exp(-bc)).astype(BF16)
            kd = (k[rows, kcols] * jnp.exp(btot - bc)).astype(BF16)
            vc = gv_ref[rows, vcols].astype(BF16)
            att = jnp.where(tri, _dot_nt(qe, ke), 0.0).astype(BF16)
            st = states[h]
            o = _dot(att, vc) + _dot_nt(qe, st.astype(BF16))
            states[h] = st * jnp.exp(btot) + _dot_tn(vc, kd)
            o = _rms(o, on_ref[...])
            r = gr_ref[rows, vcols]
            y_ref[rows, vcols] = (o * (r * jax.nn.sigmoid(r))).astype(BF16)
    for h in range(GLA_HEADS):
        state_ref[h] = states[h]


def _gla(gq, gk, gv, gr, gz, wa, ba, on):
    b, s, _ = gq.shape
    tg = 256
    tok = lambda w: pl.BlockSpec((None, tg, w), lambda bi, si: (bi, si, 0))
    full = lambda arr: pl.BlockSpec(arr.shape, lambda bi, si: (0,) * arr.ndim)
    return pl.pallas_call(
        _gla_kernel,
        grid=(b, s // tg),
        in_specs=[tok(GLA_QK_DIM), tok(GLA_QK_DIM), tok(GLA_V_DIM), tok(GLA_V_DIM), tok(GZ_PAD),
                  full(wa), full(ba), full(on)],
        out_specs=tok(GLA_V_DIM),
        out_shape=jax.ShapeDtypeStruct((b, s, GLA_V_DIM), BF16),
        scratch_shapes=[pltpu.VMEM((GLA_HEADS, GLA_VALUE_DIM, GLA_KEY_DIM), F32)],
        compiler_params=_cparams(2),
        name="gla",
    )(gq, gk, gv, gr, gz, wa, ba, on)


def _merge_route_kernel(x_ref, yc_ref, ya_ref, yg_ref, gmix_ref, wg_ref, wb_ref, wo_ref, gffn_ref,
                        wr_hi_ref, wr_lo_ref, br_ref, xo_ref, h2_ref, route_ref, counts_ref, run_ref):
    x = x_ref[...]
    h = _rms(x, gmix_ref[...]).astype(BF16)
    merged = None
    for n, y_ref in enumerate((yc_ref, ya_ref, yg_ref)):
        term = jax.nn.sigmoid(_dot(h, wg_ref[n])) * _dot(y_ref[...], wb_ref[n])
        merged = term if merged is None else merged + term
    xo = x + _dot(merged.astype(BF16), wo_ref[...])
    xo_ref[...] = xo
    h2 = _rms(xo, gffn_ref[...])
    _rows_to_tiles(h2_ref, h2)

    h_hi, h_lo = _split_bf16(h2)
    logits = (_dot(h_hi, wr_hi_ref[...]) + _dot(h_lo, wr_hi_ref[...]) + _dot(h_hi, wr_lo_ref[...])
              + br_ref[...])
    lane = lax.broadcasted_iota(jnp.int32, logits.shape, 1).astype(F32)
    big = 4.0 * ROUTE_LANES
    lg = jnp.where(lane < N_GROUPS, logits, -jnp.inf)
    gmax = jnp.max(lg, axis=1, keepdims=True)
    gidx = jnp.min(jnp.where(lg == gmax, lane, big), axis=1, keepdims=True)
    p_group_top = 1.0 / jnp.sum(jnp.exp(lg - gmax), axis=1, keepdims=True)
    lo_lane = N_GROUPS + gidx * EXPERTS_PER_GROUP
    le = jnp.where((lane >= lo_lane) & (lane < lo_lane + EXPERTS_PER_GROUP), logits, -jnp.inf)
    emax = jnp.max(le, axis=1, keepdims=True)
    i1 = jnp.min(jnp.where(le == emax, lane, big), axis=1, keepdims=True)
    esum = jnp.sum(jnp.exp(le - emax), axis=1, keepdims=True)
    le2 = jnp.where(lane == i1, -jnp.inf, le)
    emax2 = jnp.max(le2, axis=1, keepdims=True)
    i2 = jnp.min(jnp.where(le2 == emax2, lane, big), axis=1, keepdims=True)
    p1 = 1.0 / esum
    p2 = jnp.exp(emax2 - emax) / esum
    psum = p1 + p2
    w1 = p_group_top * p1 / psum
    w2 = p_group_top * p2 / psum
    e1 = i1 - N_GROUPS
    e2 = i2 - N_GROUPS

    @pl.when(pl.program_id(0) == 0)
    def _():
        run_ref[...] = jnp.zeros_like(run_ref)

    tm = x.shape[0]
    oh1 = jnp.where(lane == i1, 1.0, 0.0)
    oh2 = jnp.where(lane == i2, 1.0, 0.0)
    ohs = oh1 + oh2
    lower = (lax.broadcasted_iota(jnp.int32, (tm, tm), 0) > lax.broadcasted_iota(jnp.int32, (tm, tm), 1))
    before = _dot(jnp.where(lower, 1.0, 0.0).astype(BF16), ohs.astype(BF16)) + run_ref[...]
    rank1 = jnp.sum(oh1 * before, axis=1, keepdims=True)
    rank2 = jnp.sum(oh2 * before, axis=1, keepdims=True)
    run_ref[...] = run_ref[...] + jnp.sum(ohs, axis=0, keepdims=True)
    counts_ref[...] = run_ref[...]

    rl = lax.broadcasted_iota(jnp.int32, (tm, ROUTE_OUT), 1)
    rec = jnp.zeros((tm, ROUTE_OUT), F32)
    for slot, val in enumerate((e1, e2, w1, w2, rank1, rank2)):
        rec = jnp.where(rl == slot, val, rec)
    route_ref[...] = rec


def _merge_route(x2d, yc, ya, yg, gmix, wg, wb, wo, gffn, wr_hi, wr_lo, br):
    t, d = x2d.shape
    tm = 512
    tok = lambda w: pl.BlockSpec((tm, w), lambda i: (i, 0))
    full = lambda arr: pl.BlockSpec(arr.shape, lambda i: (0,) * arr.ndim)
    ins = (x2d, yc, ya, yg, gmix, wg, wb, wo, gffn, wr_hi, wr_lo, br)
    return pl.pallas_call(
        _merge_route_kernel,
        grid=(t // tm,),
        in_specs=[tok(d), tok(CONV_DIM), tok(ATTN_DIM), tok(GLA_V_DIM)] + [full(a) for a in ins[4:]],
        out_specs=(tok(d), pl.BlockSpec((tm * SUBLANES, LANES), lambda i: (i, 0)), tok(ROUTE_OUT),
                   pl.BlockSpec((1, ROUTE_LANES), lambda i: (0, 0))),
        out_shape=(jax.ShapeDtypeStruct((t, d), F32), jax.ShapeDtypeStruct((t * SUBLANES, LANES), F32),
                   jax.ShapeDtypeStruct((t, ROUTE_OUT), F32), jax.ShapeDtypeStruct((1, ROUTE_LANES), F32)),
        scratch_shapes=[pltpu.VMEM((1, ROUTE_LANES), F32)],
        compiler_params=_cparams(1),
        name="merge_route",
    )(*ins)


def _tile_slots(dest, tc):
    nt = dest.shape[0] // tc
    return dest.reshape(nt, tc, TOP_K).transpose(0, 2, 1).reshape(nt, 1, TOP_K * tc)


def _dispatch_kernel(last_blk_ref, dest_ref, h2_ref, xs_hbm, zeros_ref, rows_buf, sems):
    tc = h2_ref.shape[0] // SUBLANES
    step = pl.program_id(0)
    slot = step % 2
    sem = sems.at[0]

    @pl.when(pl.program_id(0) == 0)
    def _():
        zeros_ref[...] = jnp.zeros_like(zeros_ref)
        blk_rows = EXPERT_BLOCK * SUBLANES

        def zero_fill(first_row):
            return pltpu.make_async_copy(
                zeros_ref, xs_hbm.at[pl.ds(pl.multiple_of(first_row * SUBLANES, SUBLANES), blk_rows)], sem)

        for go in (lambda c: c.start(), lambda c: c.wait()):
            for e in range(N_EXPERTS):
                @pl.when(last_blk_ref[e] >= 0)
                def _():
                    go(zero_fill(last_blk_ref[e]))

            def unused(blk, carry):
                go(zero_fill(blk * EXPERT_BLOCK))
                return carry

            lax.fori_loop(last_blk_ref[N_EXPERTS], xs_hbm.shape[0] // blk_rows, unused, 0)

    stage = rows_buf.at[slot]
    stage[...] = h2_ref[...]

    def row_copy(r, kk):
        src = pl.multiple_of(r * SUBLANES, SUBLANES)
        dst = pl.multiple_of(dest_ref[0, kk * tc + r] * SUBLANES, SUBLANES)
        return pltpu.make_async_copy(stage.at[pl.ds(src, SUBLANES)], xs_hbm.at[pl.ds(dst, SUBLANES)],
                                     sems.at[slot])

    def issue(r, carry):
        row_copy(r, 0).start(priority=0)
        row_copy(r, 1).start(priority=1)
        return carry

    lax.fori_loop(0, tc, issue, 0, unroll=ROW_DMA_UNROLL)

    def drain(which):
        for _ in range(TOP_K):
            pltpu.make_async_copy(rows_buf.at[which], xs_hbm.at[pl.ds(0, tc * SUBLANES)], sems.at[which]).wait()

    @pl.when(step > 0)
    def _():
        drain(1 - slot)

    @pl.when(step == pl.num_programs(0) - 1)
    def _():
        drain(slot)


def _dispatch(dest, h2_tiles, n_rows, last_blk):
    t = h2_tiles.shape[0] // SUBLANES
    tc = 256
    grid_spec = pltpu.PrefetchScalarGridSpec(
        num_scalar_prefetch=1,
        grid=(t // tc,),
        in_specs=[
            pl.BlockSpec((None, 1, TOP_K * tc), lambda i, lb: (i, 0, 0), memory_space=pltpu.SMEM),
            pl.BlockSpec((tc * SUBLANES, LANES), lambda i, lb: (i, 0)),
        ],
        out_specs=pl.BlockSpec(memory_space=pl.ANY),
        scratch_shapes=[pltpu.VMEM((EXPERT_BLOCK * SUBLANES, LANES), F32),
                        pltpu.VMEM((2, tc * SUBLANES, LANES), F32), pltpu.SemaphoreType.DMA((2,))],
    )
    return pl.pallas_call(
        _dispatch_kernel,
        grid_spec=grid_spec,
        out_shape=jax.ShapeDtypeStruct((n_rows * SUBLANES, LANES), F32),
        compiler_params=_cparams(1),
        name="moe_dispatch",
    )(last_blk, _tile_slots(dest, tc), h2_tiles)


def _expert_kernel(blk_expert_ref, n_used_ref, xs_ref, wg_ref, wu_ref, wd_ref, ys_ref,
                   wg_bf, wu_bf, wd_bf):
    i = pl.program_id(0)
    rb = xs_ref.shape[0] // SUBLANES
    new_expert = (i == 0) | (blk_expert_ref[i] != blk_expert_ref[jnp.maximum(i - 1, 0)])

    @pl.when(new_expert)
    def _():
        wg_bf[...] = wg_ref[...].astype(BF16)
        wu_bf[...] = wu_ref[...].astype(BF16)
        wd_bf[...] = wd_ref[...].astype(BF16)

    @pl.when(i < n_used_ref[0])
    def _():
        xb = jnp.concatenate(_tiles_to_rows(xs_ref, rb), axis=1).astype(BF16)
        gate = _dot(xb, wg_bf[...])
        up = _dot(xb, wu_bf[...])
        act = (gate * jax.nn.sigmoid(gate) * up).astype(BF16)
        _rows_to_tiles(ys_ref, _dot(act, wd_bf[...]))

    @pl.when(i >= n_used_ref[0])
    def _():
        ys_ref[...] = jnp.zeros_like(ys_ref)


def _experts(blk_expert, n_used, xs_tiles, wg, wu, wd, layer):
    d = SUBLANES * LANES
    n_blocks = blk_expert.shape[0]
    rb = EXPERT_BLOCK
    row_block = (rb * SUBLANES, LANES)
    grid_spec = pltpu.PrefetchScalarGridSpec(
        num_scalar_prefetch=2,
        grid=(n_blocks,),
        in_specs=[
            pl.BlockSpec(row_block, lambda i, be, nu: (jnp.minimum(i, nu[0] - 1), 0)),
            pl.BlockSpec((None, None, d, EXPERT_FF), lambda i, be, nu: (layer, be[i], 0, 0)),
            pl.BlockSpec((None, None, d, EXPERT_FF), lambda i, be, nu: (layer, be[i], 0, 0)),
            pl.BlockSpec((None, None, EXPERT_FF, d), lambda i, be, nu: (layer, be[i], 0, 0)),
        ],
        out_specs=pl.BlockSpec(row_block, lambda i, be, nu: (i, 0)),
        scratch_shapes=[pltpu.VMEM((d, EXPERT_FF), BF16), pltpu.VMEM((d, EXPERT_FF), BF16),
                        pltpu.VMEM((EXPERT_FF, d), BF16)],
    )
    return pl.pallas_call(
        _expert_kernel,
        grid_spec=grid_spec,
        out_shape=jax.ShapeDtypeStruct(xs_tiles.shape, F32),
        compiler_params=_cparams(1),
        name="moe_experts",
    )(blk_expert, n_used, xs_tiles, wg, wu, wd)


def _combine_kernel(slots_ref, next_slots_ref, x_ref, route_ref, ys_hbm, o_ref, rows_buf, sem):
    tc = x_ref.shape[0]
    rows = _pipelined_expert_rows(slots_ref, next_slots_ref, ys_hbm, rows_buf, sem,
                                  pl.program_id(0), pl.num_programs(0), tc)
    for j, chunk in enumerate(_moe_residual_chunks(x_ref, route_ref, rows, tc)):
        o_ref[:, j * LANES:(j + 1) * LANES] = chunk


def _combine(dest, x2d, route, ys_tiles):
    t, d = x2d.shape
    tc = 256
    nt = t // tc
    slots = _tile_slots(dest, tc)
    slot_block = lambda idx: pl.BlockSpec((None, 1, TOP_K * tc), idx, memory_space=pltpu.SMEM)
    return pl.pallas_call(
        _combine_kernel,
        grid=(nt,),
        in_specs=[
            slot_block(lambda i: (i, 0, 0)),
            slot_block(lambda i: (jnp.minimum(i + 1, nt - 1), 0, 0)),
            pl.BlockSpec((tc, d), lambda i: (i, 0)),
            pl.BlockSpec((tc, ROUTE_OUT), lambda i: (i, 0)),
            pl.BlockSpec(memory_space=pl.ANY),
        ],
        out_specs=pl.BlockSpec((tc, d), lambda i: (i, 0)),
        out_shape=jax.ShapeDtypeStruct((t, d), F32),
        scratch_shapes=[pltpu.VMEM((2, TOP_K, tc * SUBLANES, LANES), F32), pltpu.SemaphoreType.DMA((2,))],
        compiler_params=_cparams(1),
        name="moe_combine",
    )(slots, slots, x2d, route, ys_tiles)


def _dispatch_plan(route, counts, t):
    e_ids = route[:, :TOP_K].astype(jnp.int32)
    rank = route[:, 2 * TOP_K:3 * TOP_K].astype(jnp.int32)
    counts = counts[0, N_GROUPS:N_GROUPS + N_EXPERTS].astype(jnp.int32)
    padded = ((counts + EXPERT_BLOCK - 1) // EXPERT_BLOCK) * EXPERT_BLOCK
    pad_end = jnp.cumsum(padded)
    pad_start = pad_end - padded
    onehot = e_ids[:, :, None] == jnp.arange(N_EXPERTS, dtype=jnp.int32)
    dest = rank + jnp.sum(jnp.where(onehot, pad_start, 0), axis=-1)
    n_blocks = -(-(t * TOP_K) // EXPERT_BLOCK) + N_EXPERTS
    blk_start = jnp.arange(n_blocks, dtype=jnp.int32) * EXPERT_BLOCK
    blk_expert = jnp.minimum(jnp.sum(blk_start[:, None] >= pad_end[None, :], axis=1), N_EXPERTS - 1)
    n_used = jnp.maximum(pad_end[-1:] // EXPERT_BLOCK, 1).astype(jnp.int32)
    last_blk = jnp.concatenate([jnp.where(padded > 0, pad_end - EXPERT_BLOCK, -1), n_used]).astype(jnp.int32)
    return blk_expert.astype(jnp.int32), n_used, dest, n_blocks * EXPERT_BLOCK, last_blk


def kernel(x, rel_bias_table, norm_mix, w_in, conv_w, conv_b, q_norm, k_norm, w_gla_alpha, b_gla_alpha,
           gla_out_norm, w_merge_gate, w_branch, w_out, norm_ffn, w_router_group, b_router_group,
           w_router_expert, b_router_expert, w_expert_gate, w_expert_up, w_expert_down):
    b, s, d = x.shape
    t = b * s
    depth = w_in.shape[0]
    assert s % MOBA_BLOCK == 0 and t % EXPERT_BLOCK == 0
    assert d == SUBLANES * LANES, "row-granular DMAs store each activation row as one (8, 128) tile"
    bias0, bias1 = _attn_biases(rel_bias_table)
    head_id = jnp.arange(ATTN_DIM) // ATTN_HEAD_DIM
    hsum = (head_id[:, None] == head_id[None, :]).astype(BF16)
    c3 = 3 * CONV_DIM
    a3 = c3 + 3 * ATTN_DIM
    g3 = a3 + 2 * GLA_QK_DIM + 2 * GLA_V_DIM
    for l in range(depth):
        w_l = w_in[l].astype(BF16)
        w_gz = jnp.pad(w_l[:, g3:], ((0, 0), (0, GZ_PAD - GLA_GATE_RANK)))
        w_alpha = jnp.pad(w_gla_alpha[l].astype(BF16), ((0, GZ_PAD - GLA_GATE_RANK), (0, 0)))
        yconv, q, k, v, kmean, gq, gk, gv, gr, gz = _inproj(
            x, norm_mix[l][None], w_l[:, :c3], w_l[:, c3:a3], w_l[:, a3:g3], w_gz,
            conv_w[l], conv_b[l][None],
            jnp.tile(q_norm[l], ATTN_HEADS)[None], jnp.tile(k_norm[l], ATTN_HEADS)[None], hsum)
        qp, kp = _select(q, k, kmean.reshape(b, s // MOBA_BLOCK, ATTN_DIM))
        yattn = _attention(qp, kp, v, bias0, bias1)
        ygla = _gla(gq, gk, gv, gr, gz, w_alpha, b_gla_alpha[l][None],
                    gla_out_norm[l][None])
        w_r = jnp.concatenate([w_router_group[l], w_router_expert[l]], axis=1)
        w_r = jnp.pad(w_r, ((0, 0), (0, ROUTE_LANES - w_r.shape[1])))
        b_r = jnp.pad(jnp.concatenate([b_router_group[l], b_router_expert[l]]),
                      (0, ROUTE_LANES - N_GROUPS - N_EXPERTS))[None]
        wr_hi, wr_lo = _split_bf16(w_r)
        xo, h2, route, counts = _merge_route(
            x.reshape(t, d), yconv.reshape(t, -1), yattn.reshape(t, -1), ygla.reshape(t, -1),
            norm_mix[l][None], w_merge_gate[l].astype(BF16), w_branch[l].astype(BF16),
            w_out[l].astype(BF16), norm_ffn[l][None], wr_hi, wr_lo, b_r)
        blk_expert, n_used, dest, n_rows, last_blk = _dispatch_plan(route, counts, t)
        xs = _dispatch(dest, h2, n_rows, last_blk)
        ys = _experts(blk_expert, n_used, xs, w_expert_gate, w_expert_up, w_expert_down, l)
        x = _combine(dest, xo, route, ys).reshape(b, s, d)
    return x
```

```python
import functools
import math

import jax
import jax.numpy as jnp
import numpy as np
from jax import lax
from jax.experimental import pallas as pl
from jax.experimental.pallas import tpu as pltpu

CONV_DIM = 512
CONV_WIDTH = 3
ATTN_HEADS = 8
ATTN_HEAD_DIM = 64
ATTN_DIM = ATTN_HEADS * ATTN_HEAD_DIM
MOBA_BLOCK = 256
MOBA_TOPK = 3
REL_BUCKETS = 32
REL_MAX_DISTANCE = 128
GLA_HEADS = 4
GLA_KEY_DIM = 64
GLA_VALUE_DIM = 128
GLA_QK_DIM = GLA_HEADS * GLA_KEY_DIM
GLA_V_DIM = GLA_HEADS * GLA_VALUE_DIM
GLA_GATE_RANK = 16
GLA_GATE_TEMP = 16.0
GLA_CHUNK = 64
N_GROUPS = 4
EXPERTS_PER_GROUP = 8
N_EXPERTS = N_GROUPS * EXPERTS_PER_GROUP
TOP_K = 2
EXPERT_FF = 512
EXPERT_BLOCK = 256
RMS_EPS = 1e-6

LANES = 128
VMEM_LIMIT_BYTES = 56 * 1024 * 1024

MASK_VALUE = -1e30
ROUTE_LANES = 128
ROUTE_OUT = 8
ATTN_FAR_CHUNK = 4
ATTN_TAIL = ATTN_FAR_CHUNK + 1
ROW_DMA_UNROLL = 8
GZ_PAD = LANES

F32 = jnp.float32
BF16 = jnp.bfloat16


def _cparams(n_axes):
    return pltpu.CompilerParams(
        dimension_semantics=("arbitrary",) * n_axes,
        vmem_limit_bytes=VMEM_LIMIT_BYTES,
    )


def _rms(x, gain):
    return x * lax.rsqrt(jnp.mean(x * x, axis=-1, keepdims=True) + RMS_EPS) * gain


def _split_bf16(x):
    hi = x.astype(BF16)
    lo = (x - hi.astype(F32)).astype(BF16)
    return hi, lo


def _dot(a, b):
    return jnp.dot(a, b, preferred_element_type=F32)


def _dot_nt(a, b):
    return lax.dot_general(a, b, (((1,), (1,)), ((), ())), preferred_element_type=F32)


SUBLANES = 8


def _rows_to_tiles(dst_ref, x):
    n = x.shape[0]
    for j in range(SUBLANES):
        dst_ref[pl.ds(j, n, stride=SUBLANES), :] = x[:, j * LANES:(j + 1) * LANES]


def _tiles_to_rows(src_ref, n):
    return [src_ref[pl.ds(j, n, stride=SUBLANES), :] for j in range(SUBLANES)]


def _dot_tn(a, b):
    return lax.dot_general(a, b, (((0,), (0,)), ((), ())), preferred_element_type=F32)


def _start_expert_row_gather(slots_ref, ys_hbm, buf, sem, n):
    def row_copy(r, kk):
        src = pl.multiple_of(slots_ref[0, kk * n + r] * SUBLANES, SUBLANES)
        dst = pl.multiple_of(r * SUBLANES, SUBLANES)
        return pltpu.make_async_copy(ys_hbm.at[pl.ds(src, SUBLANES)], buf.at[kk, pl.ds(dst, SUBLANES)], sem)

    def issue(r, carry):
        row_copy(r, 0).start(priority=0)
        row_copy(r, 1).start(priority=1)
        return carry

    lax.fori_loop(0, n, issue, 0, unroll=ROW_DMA_UNROLL)


def _finish_expert_row_gather(ys_hbm, buf, sem, n):
    for kk in range(TOP_K):
        pltpu.make_async_copy(ys_hbm.at[pl.ds(0, n * SUBLANES)], buf.at[kk], sem).wait()


def _pipelined_expert_rows(slots_ref, next_slots_ref, ys_hbm, rows_buf, sem, tile, n_tiles, n):
    slot = tile % 2

    @pl.when(tile == 0)
    def _():
        _start_expert_row_gather(slots_ref, ys_hbm, rows_buf.at[0], sem.at[0], n)

    @pl.when(tile + 1 < n_tiles)
    def _():
        _start_expert_row_gather(next_slots_ref, ys_hbm, rows_buf.at[1 - slot], sem.at[1 - slot], n)

    _finish_expert_row_gather(ys_hbm, rows_buf.at[slot], sem.at[slot], n)
    return rows_buf.at[slot]


def _moe_residual_chunks(x_ref, route_ref, buf, n):
    route = route_ref[...]
    w1 = route[:, TOP_K:TOP_K + 1]
    w2 = route[:, TOP_K + 1:TOP_K + 2]
    y1 = _tiles_to_rows(buf.at[0], n)
    y2 = _tiles_to_rows(buf.at[1], n)
    return [x_ref[:, j * LANES:(j + 1) * LANES] + (y1[j] * w1 + y2[j] * w2) for j in range(SUBLANES)]


def _inproj_kernel(x_ref, gmix_ref, wconv_ref, wattn_ref, wgla_ref, wgz_ref, convw_ref, convb_ref,
                   qn_ref, kn_ref, hsum_ref,
                   yconv_ref, q_ref, k_ref, v_ref, kmean_ref, gq_ref, gk_ref, gv_ref, gr_ref, gz_ref,
                   carry_ref):
    s_idx = pl.program_id(1)
    ts = x_ref.shape[0]
    h = _rms(x_ref[...], gmix_ref[...]).astype(BF16)

    c = _dot(h, wconv_ref[...])
    cb = c[:, :CONV_DIM]
    u = c[:, CONV_DIM:2 * CONV_DIM] * c[:, 2 * CONV_DIM:]

    @pl.when(s_idx == 0)
    def _():
        carry_ref[...] = jnp.zeros_like(carry_ref)

    prev = carry_ref[...]
    row = lax.broadcasted_iota(jnp.int32, u.shape, 0)
    u1 = pltpu.roll(u, 1, 0)
    u1 = jnp.where(row == 0, prev[7:8, :], u1)
    u2 = pltpu.roll(u, 2, 0)
    u2 = jnp.where(row == 0, prev[6:7, :], jnp.where(row == 1, prev[7:8, :], u2))
    carry_ref[...] = u[ts - 8:, :]
    y = convb_ref[...] + convw_ref[0:1, :] * u2
    y = y + convw_ref[1:2, :] * u1
    y = y + convw_ref[2:3, :] * u
    yconv_ref[...] = (cb * y).astype(BF16)

    a = _dot(h, wattn_ref[...])
    hsum = hsum_ref[...]

    def head_norm(t, gain):
        hi, lo = _split_bf16(t * t)
        ss = _dot(hi, hsum) + _dot(lo, hsum)
        return t * lax.rsqrt(ss * (1.0 / ATTN_HEAD_DIM) + RMS_EPS) * gain

    qn = head_norm(a[:, :ATTN_DIM], qn_ref[...])
    kn = head_norm(a[:, ATTN_DIM:2 * ATTN_DIM], kn_ref[...])
    q_ref[...] = (qn * (ATTN_HEAD_DIM ** -0.5)).astype(BF16)
    k_ref[...] = kn.astype(BF16)
    v_ref[...] = jnp.transpose(a[:, 2 * ATTN_DIM:]).astype(BF16)
    for j in range(ts // MOBA_BLOCK):
        kmean_ref[j] = jnp.mean(kn[j * MOBA_BLOCK:(j + 1) * MOBA_BLOCK], axis=0, keepdims=True)

    g = _dot(h, wgla_ref[...])
    gq_ref[...] = g[:, :GLA_QK_DIM]
    gk_ref[...] = g[:, GLA_QK_DIM:2 * GLA_QK_DIM]
    gv_ref[...] = g[:, 2 * GLA_QK_DIM:2 * GLA_QK_DIM + GLA_V_DIM]
    gr_ref[...] = g[:, 2 * GLA_QK_DIM + GLA_V_DIM:]
    gz_ref[...] = _dot(h, wgz_ref[...])


def _inproj(x, gmix, wconv, wattn, wgla, wgz, convw, convb, qn, kn, hsum):
    b, s, d = x.shape
    blocks_per_tile = 2
    ts = blocks_per_tile * MOBA_BLOCK
    assert s % ts == 0
    nb = s // MOBA_BLOCK
    tok = lambda w: pl.BlockSpec((None, ts, w), lambda bi, si: (bi, si, 0))
    full = lambda arr: pl.BlockSpec(arr.shape, lambda bi, si: (0,) * arr.ndim)
    out_shapes = (
        jax.ShapeDtypeStruct((b, s, CONV_DIM), BF16),
        jax.ShapeDtypeStruct((b, s, ATTN_DIM), BF16),
        jax.ShapeDtypeStruct((b, s, ATTN_DIM), BF16),
        jax.ShapeDtypeStruct((b, ATTN_DIM, s), BF16),
        jax.ShapeDtypeStruct((b, nb, 1, ATTN_DIM), F32),
        jax.ShapeDtypeStruct((b, s, GLA_QK_DIM), F32),
        jax.ShapeDtypeStruct((b, s, GLA_QK_DIM), F32),
        jax.ShapeDtypeStruct((b, s, GLA_V_DIM), F32),
        jax.ShapeDtypeStruct((b, s, GLA_V_DIM), F32),
        jax.ShapeDtypeStruct((b, s, GZ_PAD), F32),
    )
    out_specs = (
        tok(CONV_DIM), tok(ATTN_DIM), tok(ATTN_DIM),
        pl.BlockSpec((None, ATTN_DIM, ts), lambda bi, si: (bi, 0, si)),
        pl.BlockSpec((None, blocks_per_tile, 1, ATTN_DIM), lambda bi, si: (bi, si, 0, 0)),
        tok(GLA_QK_DIM), tok(GLA_QK_DIM), tok(GLA_V_DIM), tok(GLA_V_DIM), tok(GZ_PAD),
    )
    ins = (x, gmix, wconv, wattn, wgla, wgz, convw, convb, qn, kn, hsum)
    in_specs = [tok(d)] + [full(a) for a in ins[1:]]
    return pl.pallas_call(
        _inproj_kernel,
        grid=(b, s // ts),
        in_specs=in_specs,
        out_specs=out_specs,
        out_shape=out_shapes,
        scratch_shapes=[pltpu.VMEM((8, CONV_DIM), F32)],
        compiler_params=_cparams(2),
        name="inproj",
    )(*ins)


def _select_kernel(q_ref, k_ref, kmean_ref, qp_ref, kp_ref):
    own = pl.program_id(1)
    tq = q_ref.shape[0]
    nb = kmean_ref.shape[0]
    half = ATTN_HEAD_DIM
    lane = lax.broadcasted_iota(jnp.int32, (tq, LANES), 1)
    blk = lax.broadcasted_iota(jnp.int32, (nb, tq), 0).astype(F32)
    own_f = own.astype(F32)
    onehot = jnp.where(lane - half == own, 1.0, 0.0).astype(F32)
    kmean = kmean_ref[...].astype(BF16)
    for p in range(ATTN_HEADS // 2):
        qpair = q_ref[:, p * LANES:(p + 1) * LANES]
        kpair = k_ref[:, p * LANES:(p + 1) * LANES].astype(F32)
        kmpair = kmean[:, p * LANES:(p + 1) * LANES]
        qpair_f = qpair.astype(F32)
        for sub in range(2):
            h = 2 * p + sub
            lane_sel = (lane >= sub * half) & (lane < (sub + 1) * half)
            qh = jnp.where(lane_sel, qpair_f, 0.0).astype(BF16)
            gate_t = _dot_nt(kmpair, qh)
            g = jnp.where(blk < own_f, gate_t, -jnp.inf)
            alive = jnp.ones((nb, tq), F32)
            sel = jnp.where(blk == own_f, 1.0, 0.0)
            for r in range(MOBA_TOPK):
                ga = jnp.where(alive > 0.0, g, -jnp.inf)
                mx = jnp.max(ga, axis=0, keepdims=True)
                cand = jnp.where((alive > 0.0) & (g == mx), blk, 2.0 * LANES)
                first = jnp.min(cand, axis=0, keepdims=True)
                hit = blk == first
                sel = jnp.where(hit, jnp.maximum(sel, jnp.where(own_f > r, 1.0, 0.0)), sel)
                alive = jnp.where(hit, 0.0, alive)
            m_t = jnp.where(sel > 0.0, 0.0, MASK_VALUE).astype(F32)
            m_t = jnp.concatenate([m_t, jnp.zeros((LANES - nb, tq), F32)], axis=0)
            m = jnp.transpose(m_t)
            m = pltpu.roll(m, half, 1)
            m = jnp.where((lane >= half) & (lane < half + nb), m, 0.0)
            qs = qpair_f if sub == 0 else pltpu.roll(qpair_f, half, 1)
            ks = kpair if sub == 0 else pltpu.roll(kpair, half, 1)
            qp_ref[h] = jnp.where(lane < half, qs, m).astype(BF16)
            kp_ref[h] = jnp.where(lane < half, ks, onehot).astype(BF16)


def _select(q, k, kmean):
    b, s, _ = q.shape
    tq = MOBA_BLOCK
    nb = s // tq
    assert nb <= LANES - ATTN_HEAD_DIM, "block one-hot must fit beside the head dim in one lane tile"
    tok = pl.BlockSpec((None, tq, ATTN_DIM), lambda bi, si: (bi, si, 0))
    slab = pl.BlockSpec((None, ATTN_HEADS, tq, LANES), lambda bi, si: (bi, 0, si, 0))
    shp = jax.ShapeDtypeStruct((b, ATTN_HEADS, s, LANES), BF16)
    return pl.pallas_call(
        _select_kernel,
        grid=(b, nb),
        in_specs=[tok, tok, pl.BlockSpec((None, nb, ATTN_DIM), lambda bi, si: (bi, 0, 0))],
        out_specs=(slab, slab),
        out_shape=(shp, shp),
        compiler_params=_cparams(2),
        name="moba_select",
    )(q, k, kmean)


def _attn_kernel(qp_ref, kp_ref, vt_ref, bias0_ref, bias1_ref, o_ref, m_ref, l_ref, acc_ref,
                 sa_ref, sb_ref, st_ref):
    own = pl.program_id(2)
    tq = qp_ref.shape[1]
    far_rows = ATTN_FAR_CHUNK * tq
    sub = lax.broadcasted_iota(jnp.int32, (LANES, tq), 0)
    n_far = jnp.maximum(own - 1, 0) // ATTN_FAR_CHUNK

    def scores(hh, start, n):
        return _dot_nt(kp_ref[hh, pl.ds(start, n * tq), :], qp_ref[hh])

    def far_scores(dst_ref, chunk):
        start = pl.multiple_of(chunk * far_rows, far_rows)
        for hh in range(2):
            dst_ref[hh] = scores(hh, start, ATTN_FAR_CHUNK)

    def softmax_pv(hh, s_ref, start, n):
        blocks = [s_ref[hh, w * tq:(w + 1) * tq, :] for w in range(n)]
        m_prev = m_ref[hh]
        m_new = jnp.maximum(m_prev, jnp.max(functools.reduce(jnp.maximum, blocks), axis=0, keepdims=True))
        alpha = jnp.exp(m_prev - m_new)
        ps = [jnp.exp(blk - m_new) for blk in blocks]
        l_ref[hh] = alpha * l_ref[hh] + jnp.sum(functools.reduce(jnp.add, ps), axis=0, keepdims=True)
        pt = jnp.concatenate([x.astype(BF16) for x in ps], axis=0)
        acc_ref[hh] = alpha * acc_ref[hh] + _dot(vt_ref[:, pl.ds(start, n * tq)], pt)
        m_ref[hh] = m_new

    for hh in range(2):
        m_ref[hh] = jnp.full((1, tq), MASK_VALUE, F32)
        l_ref[hh] = jnp.zeros((1, tq), F32)
        acc_ref[hh] = jnp.zeros((LANES, tq), F32)

    @pl.when(n_far > 0)
    def _():
        far_scores(sa_ref, 0)

    def far_softmax(s_ref, chunk):
        for hh in range(2):
            softmax_pv(hh, s_ref, pl.multiple_of(chunk * far_rows, far_rows), ATTN_FAR_CHUNK)

    def far_pair(j, carry):
        c0 = 2 * j
        far_scores(sb_ref, c0 + 1)
        far_softmax(sa_ref, c0)

        @pl.when(c0 + 2 < n_far)
        def _():
            far_scores(sa_ref, c0 + 2)
            far_softmax(sb_ref, c0 + 1)

        return carry

    lax.fori_loop(0, n_far // 2, far_pair, 0)

    left = jnp.maximum(own - 1, 0) % ATTN_FAR_CHUNK
    tail_start = pl.multiple_of(n_far * far_rows, tq)
    last_even = (n_far % 2) == 1

    def finish(n, start, last_ref):
        for hh in range(2):
            s = scores(hh, start, n)
            for w in range(n):
                sw = s[w * tq:(w + 1) * tq, :]
                if w == n - 1:
                    sw = sw + bias0_ref[hh]
                elif w == n - 2:
                    sw = sw + bias1_ref[hh]
                st_ref[hh, w * tq:(w + 1) * tq, :] = sw
        if last_ref is not None:
            far_softmax(last_ref, n_far - 1)
        for hh in range(2):
            softmax_pv(hh, st_ref, start, n)

    @pl.when(own == 0)
    def _():
        finish(1, 0, None)

    for n_left in range(ATTN_FAR_CHUNK):
        right_size = (own >= 1) & (left == n_left)

        @pl.when(right_size & (n_far == 0))
        def _():
            finish(n_left + 2, tail_start, None)

        @pl.when(right_size & (n_far > 0) & last_even)
        def _():
            finish(n_left + 2, tail_start, sa_ref)

        @pl.when(right_size & (n_far > 0) & jnp.logical_not(last_even))
        def _():
            finish(n_left + 2, tail_start, sb_ref)

    outs = [acc_ref[hh] / l_ref[hh] for hh in range(2)]
    o_t = jnp.where(sub < ATTN_HEAD_DIM, outs[0], outs[1])
    o_ref[...] = jnp.transpose(o_t).astype(BF16)


def _attention(qp, kp, vt, bias0, bias1):
    b, nh, s, _ = qp.shape
    tq = MOBA_BLOCK
    nq = s // tq
    return pl.pallas_call(
        _attn_kernel,
        grid=(b, nh // 2, nq),
        in_specs=[
            pl.BlockSpec((None, 2, tq, LANES), lambda bi, pi, qi: (bi, pi, qi, 0)),
            pl.BlockSpec((None, 2, s, LANES), lambda bi, pi, qi: (bi, pi, 0, 0)),
            pl.BlockSpec((None, LANES, s), lambda bi, pi, qi: (bi, pi, 0)),
            pl.BlockSpec((2, tq, tq), lambda bi, pi, qi: (pi, 0, 0)),
            pl.BlockSpec((2, tq, tq), lambda bi, pi, qi: (pi, 0, 0)),
        ],
        out_specs=pl.BlockSpec((None, tq, LANES), lambda bi, pi, qi: (bi, qi, pi)),
        out_shape=jax.ShapeDtypeStruct((b, s, ATTN_DIM), BF16),
        scratch_shapes=[
            pltpu.VMEM((2, 1, tq), F32),
            pltpu.VMEM((2, 1, tq), F32),
            pltpu.VMEM((2, LANES, tq), F32),
            pltpu.VMEM((2, ATTN_FAR_CHUNK * tq, tq), F32),
            pltpu.VMEM((2, ATTN_FAR_CHUNK * tq, tq), F32),
            pltpu.VMEM((2, ATTN_TAIL * tq, tq), F32),
        ],
        compiler_params=_cparams(3),
        name="moba_attention",
    )(qp, kp, vt, bias0, bias1)


def _t5_bucket(rel):
    n = jnp.maximum(rel, 0)
    max_exact = REL_BUCKETS // 2
    scaled = (jnp.log(jnp.maximum(n, max_exact).astype(F32) / max_exact)
              / math.log(REL_MAX_DISTANCE / max_exact))
    large = jnp.minimum(max_exact + (scaled * (REL_BUCKETS - max_exact)).astype(jnp.int32), REL_BUCKETS - 1)
    return jnp.where(n < max_exact, n, large)


def _attn_biases(rel_bias_table):
    table = rel_bias_table.astype(F32).T
    pos = jnp.arange(MOBA_BLOCK)
    rel0 = pos[None, :] - pos[:, None]
    far = table[:, REL_BUCKETS - 1][:, None, None]

    rel = jnp.arange(1 - 2 * MOBA_BLOCK, 2 * MOBA_BLOCK)
    hit = _t5_bucket(rel)[None, :, None] == jnp.arange(REL_BUCKETS)
    by_rel = jnp.sum(jnp.where(hit, table[:, None, :], 0.0), axis=-1)
    zero = 2 * MOBA_BLOCK - 1

    def tile(offset):
        n = MOBA_BLOCK
        w = by_rel[:, zero + offset - (n - 1):zero + offset + n]
        wp = jnp.pad(w, ((0, 0), (0, 1)))
        skew = jnp.broadcast_to(wp[:, None, :], (wp.shape[0], n, 2 * n)).reshape(wp.shape[0], -1)
        skew = skew[:, :n * (2 * n - 1)].reshape(wp.shape[0], n, 2 * n - 1)
        return skew[:, :, n - 1:]

    bias0 = jnp.where(rel0[None] >= 0, tile(0) - far, MASK_VALUE)
    bias1 = tile(MOBA_BLOCK) - far
    return bias0, bias1


def _gla_kernel(gq_ref, gk_ref, gv_ref, gr_ref, gz_ref, wa_ref, ba_ref, on_ref, y_ref, state_ref):
    s_idx = pl.program_id(1)
    tg = gq_ref.shape[0]
    dk, dv, ck = GLA_KEY_DIM, GLA_VALUE_DIM, GLA_CHUNK

    @pl.when(s_idx == 0)
    def _():
        state_ref[...] = jnp.zeros_like(state_ref)

    z = gz_ref[...].astype(BF16)
    log_a = jax.nn.log_sigmoid(_dot(z, wa_ref[...]) + ba_ref[...]) / GLA_GATE_TEMP
    row = lax.broadcasted_iota(jnp.int32, log_a.shape, 0) % ck
    bcum = log_a
    shift = 1
    while shift < ck:
        bcum = bcum + jnp.where(row >= shift, pltpu.roll(bcum, shift, 0), 0.0)
        shift *= 2
    q = gq_ref[...] * (dk ** -0.5)
    k = gk_ref[...]
    tri = (lax.broadcasted_iota(jnp.int32, (ck, ck), 0) >= lax.broadcasted_iota(jnp.int32, (ck, ck), 1))
    states = [state_ref[h] for h in range(GLA_HEADS)]
    for c in range(tg // ck):
        rows = slice(c * ck, (c + 1) * ck)
        for h in range(GLA_HEADS):
            kcols = slice(h * dk, (h + 1) * dk)
            vcols = slice(h * dv, (h + 1) * dv)
            bc = bcum[rows, kcols]
            btot = bc[ck - 1:ck, :]
            qe = (q[rows, kcols] * jnp.exp(bc)).astype(BF16)
            ke = (k[rows, kcols] * jnp.exp(-bc)).astype(BF16)
            kd = (k[rows, kcols] * jnp.exp(btot - bc)).astype(BF16)
            vc = gv_ref[rows, vcols].astype(BF16)
            att = jnp.where(tri, _dot_nt(qe, ke), 0.0).astype(BF16)
            st = states[h]
            o = _dot(att, vc) + _dot_nt(qe, st.astype(BF16))
            states[h] = st * jnp.exp(btot) + _dot_tn(vc, kd)
            o = _rms(o, on_ref[...])
            r = gr_ref[rows, vcols]
            y_ref[rows, vcols] = (o * (r * jax.nn.sigmoid(r))).astype(BF16)
    for h in range(GLA_HEADS):
        state_ref[h] = states[h]


def _gla(gq, gk, gv, gr, gz, wa, ba, on):
    b, s, _ = gq.shape
    tg = 256
    tok = lambda w: pl.BlockSpec((None, tg, w), lambda bi, si: (bi, si, 0))
    full = lambda arr: pl.BlockSpec(arr.shape, lambda bi, si: (0,) * arr.ndim)
    return pl.pallas_call(
        _gla_kernel,
        grid=(b, s // tg),
        in_specs=[tok(GLA_QK_DIM), tok(GLA_QK_DIM), tok(GLA_V_DIM), tok(GLA_V_DIM), tok(GZ_PAD),
                  full(wa), full(ba), full(on)],
        out_specs=tok(GLA_V_DIM),
        out_shape=jax.ShapeDtypeStruct((b, s, GLA_V_DIM), BF16),
        scratch_shapes=[pltpu.VMEM((GLA_HEADS, GLA_VALUE_DIM, GLA_KEY_DIM), F32)],
        compiler_params=_cparams(2),
        name="gla",
    )(gq, gk, gv, gr, gz, wa, ba, on)


def _merge_route_kernel(x_ref, yc_ref, ya_ref, yg_ref, gmix_ref, wg_ref, wb_ref, wo_ref, gffn_ref,
                        wr_hi_ref, wr_lo_ref, br_ref, xo_ref, h2_ref, route_ref, counts_ref, run_ref):
    x = x_ref[...]
    h = _rms(x, gmix_ref[...]).astype(BF16)
    merged = None
    for n, y_ref in enumerate((yc_ref, ya_ref, yg_ref)):
        term = jax.nn.sigmoid(_dot(h, wg_ref[n])) * _dot(y_ref[...], wb_ref[n])
        merged = term if merged is None else merged + term
    xo = x + _dot(merged.astype(BF16), wo_ref[...])
    xo_ref[...] = xo
    h2 = _rms(xo, gffn_ref[...])
    _rows_to_tiles(h2_ref, h2)

    h_hi, h_lo = _split_bf16(h2)
    logits = (_dot(h_hi, wr_hi_ref[...]) + _dot(h_lo, wr_hi_ref[...]) + _dot(h_hi, wr_lo_ref[...])
              + br_ref[...])
    lane = lax.broadcasted_iota(jnp.int32, logits.shape, 1).astype(F32)
    big = 4.0 * ROUTE_LANES
    lg = jnp.where(lane < N_GROUPS, logits, -jnp.inf)
    gmax = jnp.max(lg, axis=1, keepdims=True)
    gidx = jnp.min(jnp.where(lg == gmax, lane, big), axis=1, keepdims=True)
    p_group_top = 1.0 / jnp.sum(jnp.exp(lg - gmax), axis=1, keepdims=True)
    lo_lane = N_GROUPS + gidx * EXPERTS_PER_GROUP
    le = jnp.where((lane >= lo_lane) & (lane < lo_lane + EXPERTS_PER_GROUP), logits, -jnp.inf)
    emax = jnp.max(le, axis=1, keepdims=True)
    i1 = jnp.min(jnp.where(le == emax, lane, big), axis=1, keepdims=True)
    esum = jnp.sum(jnp.exp(le - emax), axis=1, keepdims=True)
    le2 = jnp.where(lane == i1, -jnp.inf, le)
    emax2 = jnp.max(le2, axis=1, keepdims=True)
    i2 = jnp.min(jnp.where(le2 == emax2, lane, big), axis=1, keepdims=True)
    p1 = 1.0 / esum
    p2 = jnp.exp(emax2 - emax) / esum
    psum = p1 + p2
    w1 = p_group_top * p1 / psum
    w2 = p_group_top * p2 / psum
    e1 = i1 - N_GROUPS
    e2 = i2 - N_GROUPS

    @pl.when(pl.program_id(0) == 0)
    def _():
        run_ref[...] = jnp.zeros_like(run_ref)

    tm = x.shape[0]
    oh1 = jnp.where(lane == i1, 1.0, 0.0)
    oh2 = jnp.where(lane == i2, 1.0, 0.0)
    ohs = oh1 + oh2
    lower = (lax.broadcasted_iota(jnp.int32, (tm, tm), 0) > lax.broadcasted_iota(jnp.int32, (tm, tm), 1))
    before = _dot(jnp.where(lower, 1.0, 0.0).astype(BF16), ohs.astype(BF16)) + run_ref[...]
    rank1 = jnp.sum(oh1 * before, axis=1, keepdims=True)
    rank2 = jnp.sum(oh2 * before, axis=1, keepdims=True)
    run_ref[...] = run_ref[...] + jnp.sum(ohs, axis=0, keepdims=True)
    counts_ref[...] = run_ref[...]

    rl = lax.broadcasted_iota(jnp.int32, (tm, ROUTE_OUT), 1)
    rec = jnp.zeros((tm, ROUTE_OUT), F32)
    for slot, val in enumerate((e1, e2, w1, w2, rank1, rank2)):
        rec = jnp.where(rl == slot, val, rec)
    route_ref[...] = rec


def _merge_route(x2d, yc, ya, yg, gmix, wg, wb, wo, gffn, wr_hi, wr_lo, br):
    t, d = x2d.shape
    tm = 512
    tok = lambda w: pl.BlockSpec((tm, w), lambda i: (i, 0))
    full = lambda arr: pl.BlockSpec(arr.shape, lambda i: (0,) * arr.ndim)
    ins = (x2d, yc, ya, yg, gmix, wg, wb, wo, gffn, wr_hi, wr_lo, br)
    return pl.pallas_call(
        _merge_route_kernel,
        grid=(t // tm,),
        in_specs=[tok(d), tok(CONV_DIM), tok(ATTN_DIM), tok(GLA_V_DIM)] + [full(a) for a in ins[4:]],
        out_specs=(tok(d), pl.BlockSpec((tm * SUBLANES, LANES), lambda i: (i, 0)), tok(ROUTE_OUT),
                   pl.BlockSpec((1, ROUTE_LANES), lambda i: (0, 0))),
        out_shape=(jax.ShapeDtypeStruct((t, d), F32), jax.ShapeDtypeStruct((t * SUBLANES, LANES), F32),
                   jax.ShapeDtypeStruct((t, ROUTE_OUT), F32), jax.ShapeDtypeStruct((1, ROUTE_LANES), F32)),
        scratch_shapes=[pltpu.VMEM((1, ROUTE_LANES), F32)],
        compiler_params=_cparams(1),
        name="merge_route",
    )(*ins)


def _tile_slots(dest, tc):
    nt = dest.shape[0] // tc
    return dest.reshape(nt, tc, TOP_K).transpose(0, 2, 1).reshape(nt, 1, TOP_K * tc)


def _dispatch_kernel(last_blk_ref, dest_ref, h2_ref, xs_hbm, zeros_ref, rows_buf, sems):
    tc = h2_ref.shape[0] // SUBLANES
    step = pl.program_id(0)
    slot = step % 2
    sem = sems.at[0]

    @pl.when(pl.program_id(0) == 0)
    def _():
        zeros_ref[...] = jnp.zeros_like(zeros_ref)
        blk_rows = EXPERT_BLOCK * SUBLANES

        def zero_fill(first_row):
            return pltpu.make_async_copy(
                zeros_ref, xs_hbm.at[pl.ds(pl.multiple_of(first_row * SUBLANES, SUBLANES), blk_rows)], sem)

        for go in (lambda c: c.start(), lambda c: c.wait()):
            for e in range(N_EXPERTS):
                @pl.when(last_blk_ref[e] >= 0)
                def _():
                    go(zero_fill(last_blk_ref[e]))

            def unused(blk, carry):
                go(zero_fill(blk * EXPERT_BLOCK))
                return carry

            lax.fori_loop(last_blk_ref[N_EXPERTS], xs_hbm.shape[0] // blk_rows, unused, 0)

    stage = rows_buf.at[slot]
    stage[...] = h2_ref[...]

    def row_copy(r, kk):
        src = pl.multiple_of(r * SUBLANES, SUBLANES)
        dst = pl.multiple_of(dest_ref[0, kk * tc + r] * SUBLANES, SUBLANES)
        return pltpu.make_async_copy(stage.at[pl.ds(src, SUBLANES)], xs_hbm.at[pl.ds(dst, SUBLANES)],
                                     sems.at[slot])

    def issue(r, carry):
        row_copy(r, 0).start(priority=0)
        row_copy(r, 1).start(priority=1)
        return carry

    lax.fori_loop(0, tc, issue, 0, unroll=ROW_DMA_UNROLL)

    def drain(which):
        for _ in range(TOP_K):
            pltpu.make_async_copy(rows_buf.at[which], xs_hbm.at[pl.ds(0, tc * SUBLANES)], sems.at[which]).wait()

    @pl.when(step > 0)
    def _():
        drain(1 - slot)

    @pl.when(step == pl.num_programs(0) - 1)
    def _():
        drain(slot)


def _dispatch(dest, h2_tiles, n_rows, last_blk):
    t = h2_tiles.shape[0] // SUBLANES
    tc = 256
    grid_spec = pltpu.PrefetchScalarGridSpec(
        num_scalar_prefetch=1,
        grid=(t // tc,),
        in_specs=[
            pl.BlockSpec((None, 1, TOP_K * tc), lambda i, lb: (i, 0, 0), memory_space=pltpu.SMEM),
            pl.BlockSpec((tc * SUBLANES, LANES), lambda i, lb: (i, 0)),
        ],
        out_specs=pl.BlockSpec(memory_space=pl.ANY),
        scratch_shapes=[pltpu.VMEM((EXPERT_BLOCK * SUBLANES, LANES), F32),
                        pltpu.VMEM((2, tc * SUBLANES, LANES), F32), pltpu.SemaphoreType.DMA((2,))],
    )
    return pl.pallas_call(
        _dispatch_kernel,
        grid_spec=grid_spec,
        out_shape=jax.ShapeDtypeStruct((n_rows * SUBLANES, LANES), F32),
        compiler_params=_cparams(1),
        name="moe_dispatch",
    )(last_blk, _tile_slots(dest, tc), h2_tiles)


def _expert_kernel(blk_expert_ref, n_used_ref, xs_ref, wg_ref, wu_ref, wd_ref, ys_ref,
                   wg_bf, wu_bf, wd_bf):
    i = pl.program_id(0)
    rb = xs_ref.shape[0] // SUBLANES
    new_expert = (i == 0) | (blk_expert_ref[i] != blk_expert_ref[jnp.maximum(i - 1, 0)])

    @pl.when(new_expert)
    def _():
        wg_bf[...] = wg_ref[...].astype(BF16)
        wu_bf[...] = wu_ref[...].astype(BF16)
        wd_bf[...] = wd_ref[...].astype(BF16)

    @pl.when(i < n_used_ref[0])
    def _():
        xb = jnp.concatenate(_tiles_to_rows(xs_ref, rb), axis=1).astype(BF16)
        gate = _dot(xb, wg_bf[...])
        up = _dot(xb, wu_bf[...])
        act = (gate * jax.nn.sigmoid(gate) * up).astype(BF16)
        _rows_to_tiles(ys_ref, _dot(act, wd_bf[...]))

    @pl.when(i >= n_used_ref[0])
    def _():
        ys_ref[...] = jnp.zeros_like(ys_ref)


def _experts(blk_expert, n_used, xs_tiles, wg, wu, wd, layer):
    d = SUBLANES * LANES
    n_blocks = blk_expert.shape[0]
    rb = EXPERT_BLOCK
    row_block = (rb * SUBLANES, LANES)
    grid_spec = pltpu.PrefetchScalarGridSpec(
        num_scalar_prefetch=2,
        grid=(n_blocks,),
        in_specs=[
            pl.BlockSpec(row_block, lambda i, be, nu: (jnp.minimum(i, nu[0] - 1), 0)),
            pl.BlockSpec((None, None, d, EXPERT_FF), lambda i, be, nu: (layer, be[i], 0, 0)),
            pl.BlockSpec((None, None, d, EXPERT_FF), lambda i, be, nu: (layer, be[i], 0, 0)),
            pl.BlockSpec((None, None, EXPERT_FF, d), lambda i, be, nu: (layer, be[i], 0, 0)),
        ],
        out_specs=pl.BlockSpec(row_block, lambda i, be, nu: (i, 0)),
        scratch_shapes=[pltpu.VMEM((d, EXPERT_FF), BF16), pltpu.VMEM((d, EXPERT_FF), BF16),
                        pltpu.VMEM((EXPERT_FF, d), BF16)],
    )
    return pl.pallas_call(
        _expert_kernel,
        grid_spec=grid_spec,
        out_shape=jax.ShapeDtypeStruct(xs_tiles.shape, F32),
        compiler_params=_cparams(1),
        name="moe_experts",
    )(blk_expert, n_used, xs_tiles, wg, wu, wd)


def _combine_kernel(slots_ref, next_slots_ref, x_ref, route_ref, ys_hbm, o_ref, rows_buf, sem):
    tc = x_ref.shape[0]
    rows = _pipelined_expert_rows(slots_ref, next_slots_ref, ys_hbm, rows_buf, sem,
                                  pl.program_id(0), pl.num_programs(0), tc)
    for j, chunk in enumerate(_moe_residual_chunks(x_ref, route_ref, rows, tc)):
        o_ref[:, j * LANES:(j + 1) * LANES] = chunk


def _combine(dest, x2d, route, ys_tiles):
    t, d = x2d.shape
    tc = 256
    nt = t // tc
    slots = _tile_slots(dest, tc)
    slot_block = lambda idx: pl.BlockSpec((None, 1, TOP_K * tc), idx, memory_space=pltpu.SMEM)
    return pl.pallas_call(
        _combine_kernel,
        grid=(nt,),
        in_specs=[
            slot_block(lambda i: (i, 0, 0)),
            slot_block(lambda i: (jnp.minimum(i + 1, nt - 1), 0, 0)),
            pl.BlockSpec((tc, d), lambda i: (i, 0)),
            pl.BlockSpec((tc, ROUTE_OUT), lambda i: (i, 0)),
            pl.BlockSpec(memory_space=pl.ANY),
        ],
        out_specs=pl.BlockSpec((tc, d), lambda i: (i, 0)),
        out_shape=jax.ShapeDtypeStruct((t, d), F32),
        scratch_shapes=[pltpu.VMEM((2, TOP_K, tc * SUBLANES, LANES), F32), pltpu.SemaphoreType.DMA((2,))],
        compiler_params=_cparams(1),
        name="moe_combine",
    )(slots, slots, x2d, route, ys_tiles)


def _dispatch_plan(route, counts, t):
    e_ids = route[:, :TOP_K].astype(jnp.int32)
    rank = route[:, 2 * TOP_K:3 * TOP_K].astype(jnp.int32)
    counts = counts[0, N_GROUPS:N_GROUPS + N_EXPERTS].astype(jnp.int32)
    padded = ((counts + EXPERT_BLOCK - 1) // EXPERT_BLOCK) * EXPERT_BLOCK
    pad_end = jnp.cumsum(padded)
    pad_start = pad_end - padded
    onehot = e_ids[:, :, None] == jnp.arange(N_EXPERTS, dtype=jnp.int32)
    dest = rank + jnp.sum(jnp.where(onehot, pad_start, 0), axis=-1)
    n_blocks = -(-(t * TOP_K) // EXPERT_BLOCK) + N_EXPERTS
    blk_start = jnp.arange(n_blocks, dtype=jnp.int32) * EXPERT_BLOCK
    blk_expert = jnp.minimum(jnp.sum(blk_start[:, None] >= pad_end[None, :], axis=1), N_EXPERTS - 1)
    n_used = jnp.maximum(pad_end[-1:] // EXPERT_BLOCK, 1).astype(jnp.int32)
    last_blk = jnp.concatenate([jnp.where(padded > 0, pad_end - EXPERT_BLOCK, -1), n_used]).astype(jnp.int32)
    return blk_expert.astype(jnp.int32), n_used, dest, n_blocks * EXPERT_BLOCK, last_blk


def kernel(x, rel_bias_table, norm_mix, w_in, conv_w, conv_b, q_norm, k_norm, w_gla_alpha, b_gla_alpha,
           gla_out_norm, w_merge_gate, w_branch, w_out, norm_ffn, w_router_group, b_router_group,
           w_router_expert, b_router_expert, w_expert_gate, w_expert_up, w_expert_down):
    b, s, d = x.shape
    t = b * s
    depth = w_in.shape[0]
    assert s % MOBA_BLOCK == 0 and t % EXPERT_BLOCK == 0
    assert d == SUBLANES * LANES, "row-granular DMAs store each activation row as one (8, 128) tile"
    bias0, bias1 = _attn_biases(rel_bias_table)
    head_id = jnp.arange(ATTN_DIM) // ATTN_HEAD_DIM
    hsum = (head_id[:, None] == head_id[None, :]).astype(BF16)
    c3 = 3 * CONV_DIM
    a3 = c3 + 3 * ATTN_DIM
    g3 = a3 + 2 * GLA_QK_DIM + 2 * GLA_V_DIM
    for l in range(depth):
        w_l = w_in[l].astype(BF16)
        w_gz = jnp.pad(w_l[:, g3:], ((0, 0), (0, GZ_PAD - GLA_GATE_RANK)))
        w_alpha = jnp.pad(w_gla_alpha[l].astype(BF16), ((0, GZ_PAD - GLA_GATE_RANK), (0, 0)))
        yconv, q, k, v, kmean, gq, gk, gv, gr, gz = _inproj(
            x, norm_mix[l][None], w_l[:, :c3], w_l[:, c3:a3], w_l[:, a3:g3], w_gz,
            conv_w[l], conv_b[l][None],
            jnp.tile(q_norm[l], ATTN_HEADS)[None], jnp.tile(k_norm[l], ATTN_HEADS)[None], hsum)
        qp, kp = _select(q, k, kmean.reshape(b, s // MOBA_BLOCK, ATTN_DIM))
        yattn = _attention(qp, kp, v, bias0, bias1)
        ygla = _gla(gq, gk, gv, gr, gz, w_alpha, b_gla_alpha[l][None],
                    gla_out_norm[l][None])
        w_r = jnp.concatenate([w_router_group[l], w_router_expert[l]], axis=1)
        w_r = jnp.pad(w_r, ((0, 0), (0, ROUTE_LANES - w_r.shape[1])))
        b_r = jnp.pad(jnp.concatenate([b_router_group[l], b_router_expert[l]]),
                      (0, ROUTE_LANES - N_GROUPS - N_EXPERTS))[None]
        wr_hi, wr_lo = _split_bf16(w_r)
        xo, h2, route, counts = _merge_route(
            x.reshape(t, d), yconv.reshape(t, -1), yattn.reshape(t, -1), ygla.reshape(t, -1),
            norm_mix[l][None], w_merge_gate[l].astype(BF16), w_branch[l].astype(BF16),
            w_out[l].astype(BF16), norm_ffn[l][None], wr_hi, wr_lo, b_r)
        blk_expert, n_used, dest, n_rows, last_blk = _dispatch_plan(route, counts, t)
        xs = _dispatch(dest, h2, n_rows, last_blk)
        ys = _experts(blk_expert, n_used, xs, w_expert_gate, w_expert_up, w_expert_down, l)
        x = _combine(dest, xo, route, ys).reshape(b, s, d)
    return x
```

```python
import functools
import math

import jax
import jax.numpy as jnp
import numpy as np
from jax import lax
from jax.experimental import pallas as pl
from jax.experimental.pallas import tpu as pltpu

CONV_DIM = 512
CONV_WIDTH = 3
ATTN_HEADS = 8
ATTN_HEAD_DIM = 64
ATTN_DIM = ATTN_HEADS * ATTN_HEAD_DIM
MOBA_BLOCK = 256
MOBA_TOPK = 3
REL_BUCKETS = 32
REL_MAX_DISTANCE = 128
GLA_HEADS = 4
GLA_KEY_DIM = 64
GLA_VALUE_DIM = 128
GLA_QK_DIM = GLA_HEADS * GLA_KEY_DIM
GLA_V_DIM = GLA_HEADS * GLA_VALUE_DIM
GLA_GATE_RANK = 16
GLA_GATE_TEMP = 16.0
GLA_CHUNK = 64
N_GROUPS = 4
EXPERTS_PER_GROUP = 8
N_EXPERTS = N_GROUPS * EXPERTS_PER_GROUP
TOP_K = 2
EXPERT_FF = 512
EXPERT_BLOCK = 256
RMS_EPS = 1e-6

LANES = 128
VMEM_LIMIT_BYTES = 56 * 1024 * 1024

MASK_VALUE = -1e30
ROUTE_LANES = 128
ROUTE_OUT = 8
ATTN_FAR_CHUNK = 4
ATTN_TAIL = ATTN_FAR_CHUNK + 1
ROW_DMA_UNROLL = 8
GZ_PAD = LANES

F32 = jnp.float32
BF16 = jnp.bfloat16


def _cparams(n_axes):
    return pltpu.CompilerParams(
        dimension_semantics=("arbitrary",) * n_axes,
        vmem_limit_bytes=VMEM_LIMIT_BYTES,
    )


def _rms(x, gain):
    return x * lax.rsqrt(jnp.mean(x * x, axis=-1, keepdims=True) + RMS_EPS) * gain


def _split_bf16(x):
    hi = x.astype(BF16)
    lo = (x - hi.astype(F32)).astype(BF16)
    return hi, lo


def _dot(a, b):
    return jnp.dot(a, b, preferred_element_type=F32)


def _dot_nt(a, b):
    return lax.dot_general(a, b, (((1,), (1,)), ((), ())), preferred_element_type=F32)


SUBLANES = 8


def _rows_to_tiles(dst_ref, x):
    n = x.shape[0]
    for j in range(SUBLANES):
        dst_ref[pl.ds(j, n, stride=SUBLANES), :] = x[:, j * LANES:(j + 1) * LANES]


def _tiles_to_rows(src_ref, n):
    return [src_ref[pl.ds(j, n, stride=SUBLANES), :] for j in range(SUBLANES)]


def _dot_tn(a, b):
    return lax.dot_general(a, b, (((0,), (0,)), ((), ())), preferred_element_type=F32)


def _start_expert_row_gather(slots_ref, ys_hbm, buf, sem, n):
    def row_copy(r, kk):
        src = pl.multiple_of(slots_ref[0, kk * n + r] * SUBLANES, SUBLANES)
        dst = pl.multiple_of(r * SUBLANES, SUBLANES)
        return pltpu.make_async_copy(ys_hbm.at[pl.ds(src, SUBLANES)], buf.at[kk, pl.ds(dst, SUBLANES)], sem)

    def issue(r, carry):
        row_copy(r, 0).start(priority=0)
        row_copy(r, 1).start(priority=1)
        return carry

    lax.fori_loop(0, n, issue, 0, unroll=ROW_DMA_UNROLL)


def _finish_expert_row_gather(ys_hbm, buf, sem, n):
    for kk in range(TOP_K):
        pltpu.make_async_copy(ys_hbm.at[pl.ds(0, n * SUBLANES)], buf.at[kk], sem).wait()


def _pipelined_expert_rows(slots_ref, next_slots_ref, ys_hbm, rows_buf, sem, tile, n_tiles, n):
    slot = tile % 2

    @pl.when(tile == 0)
    def _():
        _start_expert_row_gather(slots_ref, ys_hbm, rows_buf.at[0], sem.at[0], n)

    @pl.when(tile + 1 < n_tiles)
    def _():
        _start_expert_row_gather(next_slots_ref, ys_hbm, rows_buf.at[1 - slot], sem.at[1 - slot], n)

    _finish_expert_row_gather(ys_hbm, rows_buf.at[slot], sem.at[slot], n)
    return rows_buf.at[slot]


def _moe_residual_chunks(x_ref, route_ref, buf, n):
    route = route_ref[...]
    w1 = route[:, TOP_K:TOP_K + 1]
    w2 = route[:, TOP_K + 1:TOP_K + 2]
    y1 = _tiles_to_rows(buf.at[0], n)
    y2 = _tiles_to_rows(buf.at[1], n)
    return [x_ref[:, j * LANES:(j + 1) * LANES] + (y1[j] * w1 + y2[j] * w2) for j in range(SUBLANES)]


def _inproj_kernel(x_ref, gmix_ref, w_ref, convw_ref, convb_ref,
                   qn_ref, kn_ref, hsum_ref,
                   yconv_ref, q_ref, k_ref, v_ref, kmean_ref, gq_ref, gk_ref, gv_ref, gr_ref, gz_ref,
                   carry_ref):
    s_idx = pl.program_id(1)
    ts = x_ref.shape[0]
    h = _rms(x_ref[...], gmix_ref[...]).astype(BF16)

    c3 = 3 * CONV_DIM
    a3 = c3 + 3 * ATTN_DIM
    g3 = a3 + 2 * GLA_QK_DIM + 2 * GLA_V_DIM
    c = _dot(h, w_ref[:, :c3])
    cb = c[:, :CONV_DIM]
    u = c[:, CONV_DIM:2 * CONV_DIM] * c[:, 2 * CONV_DIM:]

    @pl.when(s_idx == 0)
    def _():
        carry_ref[...] = jnp.zeros_like(carry_ref)

    prev = carry_ref[...]
    row = lax.broadcasted_iota(jnp.int32, u.shape, 0)
    u1 = pltpu.roll(u, 1, 0)
    u1 = jnp.where(row == 0, prev[7:8, :], u1)
    u2 = pltpu.roll(u, 2, 0)
    u2 = jnp.where(row == 0, prev[6:7, :], jnp.where(row == 1, prev[7:8, :], u2))
    carry_ref[...] = u[ts - 8:, :]
    y = convb_ref[...] + convw_ref[0:1, :] * u2
    y = y + convw_ref[1:2, :] * u1
    y = y + convw_ref[2:3, :] * u
    yconv_ref[...] = (cb * y).astype(BF16)

    a = _dot(h, w_ref[:, c3:a3])
    hsum = hsum_ref[...]

    def head_norm(t, gain):
        hi, lo = _split_bf16(t * t)
        ss = _dot(hi, hsum) + _dot(lo, hsum)
        return t * lax.rsqrt(ss * (1.0 / ATTN_HEAD_DIM) + RMS_EPS) * gain

    qn = head_norm(a[:, :ATTN_DIM], qn_ref[...])
    kn = head_norm(a[:, ATTN_DIM:2 * ATTN_DIM], kn_ref[...])
    q_ref[...] = (qn * (ATTN_HEAD_DIM ** -0.5)).astype(BF16)
    k_ref[...] = kn.astype(BF16)
    v_ref[...] = jnp.transpose(a[:, 2 * ATTN_DIM:]).astype(BF16)
    for j in range(ts // MOBA_BLOCK):
        kmean_ref[j] = jnp.mean(kn[j * MOBA_BLOCK:(j + 1) * MOBA_BLOCK], axis=0, keepdims=True)

    g = _dot(h, w_ref[:, a3:g3])
    gq_ref[...] = g[:, :GLA_QK_DIM]
    gk_ref[...] = g[:, GLA_QK_DIM:2 * GLA_QK_DIM]
    gv_ref[...] = g[:, 2 * GLA_QK_DIM:2 * GLA_QK_DIM + GLA_V_DIM]
    gr_ref[...] = g[:, 2 * GLA_QK_DIM + GLA_V_DIM:]
    gz_ref[...] = _dot(h, w_ref[:, g3:])


def _inproj(x, gmix, w, convw, convb, qn, kn, hsum):
    b, s, d = x.shape
    blocks_per_tile = 2
    ts = blocks_per_tile * MOBA_BLOCK
    assert s % ts == 0
    nb = s // MOBA_BLOCK
    tok = lambda w: pl.BlockSpec((None, ts, w), lambda bi, si: (bi, si, 0))
    full = lambda arr: pl.BlockSpec(arr.shape, lambda bi, si: (0,) * arr.ndim)
    out_shapes = (
        jax.ShapeDtypeStruct((b, s, CONV_DIM), BF16),
        jax.ShapeDtypeStruct((b, s, ATTN_DIM), BF16),
        jax.ShapeDtypeStruct((b, s, ATTN_DIM), BF16),
        jax.ShapeDtypeStruct((b, ATTN_DIM, s), BF16),
        jax.ShapeDtypeStruct((b, nb, 1, ATTN_DIM), F32),
        jax.ShapeDtypeStruct((b, s, GLA_QK_DIM), F32),
        jax.ShapeDtypeStruct((b, s, GLA_QK_DIM), F32),
        jax.ShapeDtypeStruct((b, s, GLA_V_DIM), F32),
        jax.ShapeDtypeStruct((b, s, GLA_V_DIM), F32),
        jax.ShapeDtypeStruct((b, s, GZ_PAD), F32),
    )
    out_specs = (
        tok(CONV_DIM), tok(ATTN_DIM), tok(ATTN_DIM),
        pl.BlockSpec((None, ATTN_DIM, ts), lambda bi, si: (bi, 0, si)),
        pl.BlockSpec((None, blocks_per_tile, 1, ATTN_DIM), lambda bi, si: (bi, si, 0, 0)),
        tok(GLA_QK_DIM), tok(GLA_QK_DIM), tok(GLA_V_DIM), tok(GLA_V_DIM), tok(GZ_PAD),
    )
    ins = (x, gmix, w, convw, convb, qn, kn, hsum)
    in_specs = [tok(d)] + [full(a) for a in ins[1:]]
    return pl.pallas_call(
        _inproj_kernel,
        grid=(b, s // ts),
        in_specs=in_specs,
        out_specs=out_specs,
        out_shape=out_shapes,
        scratch_shapes=[pltpu.VMEM((8, CONV_DIM), F32)],
        compiler_params=_cparams(2),
        name="inproj",
    )(*ins)


def _select_kernel(q_ref, k_ref, kmean_ref, qp_ref, kp_ref):
    own = pl.program_id(1)
    tq = q_ref.shape[0]
    nb = kmean_ref.shape[0]
    half = ATTN_HEAD_DIM
    lane = lax.broadcasted_iota(jnp.int32, (tq, LANES), 1)
    blk = lax.broadcasted_iota(jnp.int32, (nb, tq), 0).astype(F32)
    own_f = own.astype(F32)
    onehot = jnp.where(lane - half == own, 1.0, 0.0).astype(F32)
    kmean = kmean_ref[...].astype(BF16)
    for p in range(ATTN_HEADS // 2):
        qpair = q_ref[:, p * LANES:(p + 1) * LANES]
        kpair = k_ref[:, p * LANES:(p + 1) * LANES].astype(F32)
        kmpair = kmean[:, p * LANES:(p + 1) * LANES]
        qpair_f = qpair.astype(F32)
        for sub in range(2):
            h = 2 * p + sub
            lane_sel = (lane >= sub * half) & (lane < (sub + 1) * half)
            qh = jnp.where(lane_sel, qpair_f, 0.0).astype(BF16)
            gate_t = _dot_nt(kmpair, qh)
            g = jnp.where(blk < own_f, gate_t, -jnp.inf)
            alive = jnp.ones((nb, tq), F32)
            sel = jnp.where(blk == own_f, 1.0, 0.0)
            for r in range(MOBA_TOPK):
                ga = jnp.where(alive > 0.0, g, -jnp.inf)
                mx = jnp.max(ga, axis=0, keepdims=True)
                cand = jnp.where((alive > 0.0) & (g == mx), blk, 2.0 * LANES)
                first = jnp.min(cand, axis=0, keepdims=True)
                hit = blk == first
                sel = jnp.where(hit, jnp.maximum(sel, jnp.where(own_f > r, 1.0, 0.0)), sel)
                alive = jnp.where(hit, 0.0, alive)
            m_t = jnp.where(sel > 0.0, 0.0, MASK_VALUE).astype(F32)
            m_t = jnp.concatenate([m_t, jnp.zeros((LANES - nb, tq), F32)], axis=0)
            m = jnp.transpose(m_t)
            m = pltpu.roll(m, half, 1)
            m = jnp.where((lane >= half) & (lane < half + nb), m, 0.0)
            qs = qpair_f if sub == 0 else pltpu.roll(qpair_f, half, 1)
            ks = kpair if sub == 0 else pltpu.roll(kpair, half, 1)
            qp_ref[h] = jnp.where(lane < half, qs, m).astype(BF16)
            kp_ref[h] = jnp.where(lane < half, ks, onehot).astype(BF16)


def _select(q, k, kmean):
    b, s, _ = q.shape
    tq = MOBA_BLOCK
    nb = s // tq
    assert nb <= LANES - ATTN_HEAD_DIM, "block one-hot must fit beside the head dim in one lane tile"
    tok = pl.BlockSpec((None, tq, ATTN_DIM), lambda bi, si: (bi, si, 0))
    slab = pl.BlockSpec((None, ATTN_HEADS, tq, LANES), lambda bi, si: (bi, 0, si, 0))
    shp = jax.ShapeDtypeStruct((b, ATTN_HEADS, s, LANES), BF16)
    return pl.pallas_call(
        _select_kernel,
        grid=(b, nb),
        in_specs=[tok, tok, pl.BlockSpec((None, nb, ATTN_DIM), lambda bi, si: (bi, 0, 0))],
        out_specs=(slab, slab),
        out_shape=(shp, shp),
        compiler_params=_cparams(2),
        name="moba_select",
    )(q, k, kmean)


def _attn_kernel(qp_ref, kp_ref, vt_ref, bias0_ref, bias1_ref, o_ref, m_ref, l_ref, acc_ref,
                 sa_ref, sb_ref, st_ref):
    own = pl.program_id(2)
    tq = qp_ref.shape[1]
    far_rows = ATTN_FAR_CHUNK * tq
    sub = lax.broadcasted_iota(jnp.int32, (LANES, tq), 0)
    n_far = jnp.maximum(own - 1, 0) // ATTN_FAR_CHUNK

    def scores(hh, start, n):
        return _dot_nt(kp_ref[hh, pl.ds(start, n * tq), :], qp_ref[hh])

    def far_scores(dst_ref, chunk):
        start = pl.multiple_of(chunk * far_rows, far_rows)
        for hh in range(2):
            dst_ref[hh] = scores(hh, start, ATTN_FAR_CHUNK)

    def softmax_pv(hh, s_ref, start, n):
        blocks = [s_ref[hh, w * tq:(w + 1) * tq, :] for w in range(n)]
        m_prev = m_ref[hh]
        m_new = jnp.maximum(m_prev, jnp.max(functools.reduce(jnp.maximum, blocks), axis=0, keepdims=True))
        alpha = jnp.exp(m_prev - m_new)
        ps = [jnp.exp(blk - m_new) for blk in blocks]
        l_ref[hh] = alpha * l_ref[hh] + jnp.sum(functools.reduce(jnp.add, ps), axis=0, keepdims=True)
        pt = jnp.concatenate([x.astype(BF16) for x in ps], axis=0)
        acc_ref[hh] = alpha * acc_ref[hh] + _dot(vt_ref[:, pl.ds(start, n * tq)], pt)
        m_ref[hh] = m_new

    for hh in range(2):
        m_ref[hh] = jnp.full((1, tq), MASK_VALUE, F32)
        l_ref[hh] = jnp.zeros((1, tq), F32)
        acc_ref[hh] = jnp.zeros((LANES, tq), F32)

    @pl.when(n_far > 0)
    def _():
        far_scores(sa_ref, 0)

    def far_softmax(s_ref, chunk):
        for hh in range(2):
            softmax_pv(hh, s_ref, pl.multiple_of(chunk * far_rows, far_rows), ATTN_FAR_CHUNK)

    def far_pair(j, carry):
        c0 = 2 * j
        far_scores(sb_ref, c0 + 1)
        far_softmax(sa_ref, c0)

        @pl.when(c0 + 2 < n_far)
        def _():
            far_scores(sa_ref, c0 + 2)
            far_softmax(sb_ref, c0 + 1)

        return carry

    lax.fori_loop(0, n_far // 2, far_pair, 0)

    left = jnp.maximum(own - 1, 0) % ATTN_FAR_CHUNK
    tail_start = pl.multiple_of(n_far * far_rows, tq)
    last_even = (n_far % 2) == 1

    def finish(n, start, last_ref):
        for hh in range(2):
            s = scores(hh, start, n)
            for w in range(n):
                sw = s[w * tq:(w + 1) * tq, :]
                if w == n - 1:
                    sw = sw + bias0_ref[hh]
                elif w == n - 2:
                    sw = sw + bias1_ref[hh]
                st_ref[hh, w * tq:(w + 1) * tq, :] = sw
        if last_ref is not None:
            far_softmax(last_ref, n_far - 1)
        for hh in range(2):
            softmax_pv(hh, st_ref, start, n)

    @pl.when(own == 0)
    def _():
        finish(1, 0, None)

    for n_left in range(ATTN_FAR_CHUNK):
        right_size = (own >= 1) & (left == n_left)

        @pl.when(right_size & (n_far == 0))
        def _():
            finish(n_left + 2, tail_start, None)

        @pl.when(right_size & (n_far > 0) & last_even)
        def _():
            finish(n_left + 2, tail_start, sa_ref)

        @pl.when(right_size & (n_far > 0) & jnp.logical_not(last_even))
        def _():
            finish(n_left + 2, tail_start, sb_ref)

    outs = [acc_ref[hh] / l_ref[hh] for hh in range(2)]
    o_t = jnp.where(sub < ATTN_HEAD_DIM, outs[0], outs[1])
    o_ref[...] = jnp.transpose(o_t).astype(BF16)


def _attention(qp, kp, vt, bias0, bias1):
    b, nh, s, _ = qp.shape
    tq = MOBA_BLOCK
    nq = s // tq
    return pl.pallas_call(
        _attn_kernel,
        grid=(b, nh // 2, nq),
        in_specs=[
            pl.BlockSpec((None, 2, tq, LANES), lambda bi, pi, qi: (bi, pi, qi, 0)),
            pl.BlockSpec((None, 2, s, LANES), lambda bi, pi, qi: (bi, pi, 0, 0)),
            pl.BlockSpec((None, LANES, s), lambda bi, pi, qi: (bi, pi, 0)),
            pl.BlockSpec((2, tq, tq), lambda bi, pi, qi: (pi, 0, 0)),
            pl.BlockSpec((2, tq, tq), lambda bi, pi, qi: (pi, 0, 0)),
        ],
        out_specs=pl.BlockSpec((None, tq, LANES), lambda bi, pi, qi: (bi, qi, pi)),
        out_shape=jax.ShapeDtypeStruct((b, s, ATTN_DIM), BF16),
        scratch_shapes=[
            pltpu.VMEM((2, 1, tq), F32),
            pltpu.VMEM((2, 1, tq), F32),
            pltpu.VMEM((2, LANES, tq), F32),
            pltpu.VMEM((2, ATTN_FAR_CHUNK * tq, tq), F32),
            pltpu.VMEM((2, ATTN_FAR_CHUNK * tq, tq), F32),
            pltpu.VMEM((2, ATTN_TAIL * tq, tq), F32),
        ],
        compiler_params=_cparams(3),
        name="moba_attention",
    )(qp, kp, vt, bias0, bias1)


def _t5_bucket(rel):
    n = jnp.maximum(rel, 0)
    max_exact = REL_BUCKETS // 2
    scaled = (jnp.log(jnp.maximum(n, max_exact).astype(F32) / max_exact)
              / math.log(REL_MAX_DISTANCE / max_exact))
    large = jnp.minimum(max_exact + (scaled * (REL_BUCKETS - max_exact)).astype(jnp.int32), REL_BUCKETS - 1)
    return jnp.where(n < max_exact, n, large)


def _attn_biases(rel_bias_table):
    table = rel_bias_table.astype(F32).T
    pos = jnp.arange(MOBA_BLOCK)
    rel0 = pos[None, :] - pos[:, None]
    far = table[:, REL_BUCKETS - 1][:, None, None]

    rel = jnp.arange(1 - 2 * MOBA_BLOCK, 2 * MOBA_BLOCK)
    hit = _t5_bucket(rel)[None, :, None] == jnp.arange(REL_BUCKETS)
    by_rel = jnp.sum(jnp.where(hit, table[:, None, :], 0.0), axis=-1)
    zero = 2 * MOBA_BLOCK - 1

    def tile(offset):
        n = MOBA_BLOCK
        w = by_rel[:, zero + offset - (n - 1):zero + offset + n]
        wp = jnp.pad(w, ((0, 0), (0, 1)))
        skew = jnp.broadcast_to(wp[:, None, :], (wp.shape[0], n, 2 * n)).reshape(wp.shape[0], -1)
        skew = skew[:, :n * (2 * n - 1)].reshape(wp.shape[0], n, 2 * n - 1)
        return skew[:, :, n - 1:]

    bias0 = jnp.where(rel0[None] >= 0, tile(0) - far, MASK_VALUE)
    bias1 = tile(MOBA_BLOCK) - far
    return bias0, bias1


def _gla_kernel(gq_ref, gk_ref, gv_ref, gr_ref, gz_ref, wa_ref, ba_ref, on_ref, y_ref, state_ref):
    s_idx = pl.program_id(1)
    tg = gq_ref.shape[0]
    dk, dv, ck = GLA_KEY_DIM, GLA_VALUE_DIM, GLA_CHUNK

    @pl.when(s_idx == 0)
    def _():
        state_ref[...] = jnp.zeros_like(state_ref)

    z = gz_ref[...].astype(BF16)
    log_a = jax.nn.log_sigmoid(_dot(z, wa_ref[...]) + ba_ref[...]) / GLA_GATE_TEMP
    row = lax.broadcasted_iota(jnp.int32, log_a.shape, 0) % ck
    bcum = log_a
    shift = 1
    while shift < ck:
        bcum = bcum + jnp.where(row >= shift, pltpu.roll(bcum, shift, 0), 0.0)
        shift *= 2
    q = gq_ref[...] * (dk ** -0.5)
    k = gk_ref[...]
    tri = (lax.broadcasted_iota(jnp.int32, (ck, ck), 0) >= lax.broadcasted_iota(jnp.int32, (ck, ck), 1))
    states = [state_ref[h] for h in range(GLA_HEADS)]
    for c in range(tg // ck):
        rows = slice(c * ck, (c + 1) * ck)
        for h in range(GLA_HEADS):
            kcols = slice(h * dk, (h + 1) * dk)
            vcols = slice(h * dv, (h + 1) * dv)
            bc = bcum[rows, kcols]
            btot = bc[ck - 1:ck, :]
            qe = (q[rows, kcols] * jnp.exp(bc)).astype(BF16)
            ke = (k[rows, kcols] * jnp.exp(-bc)).astype(BF16)
            kd = (k[rows, kcols] * jnp.exp(btot - bc)).astype(BF16)
            vc = gv_ref[rows, vcols].astype(BF16)
            att = jnp.where(tri, _dot_nt(qe, ke), 0.0).astype(BF16)
            st = states[h]
            o = _dot(att, vc) + _dot_nt(qe, st.astype(BF16))
            states[h] = st * jnp.exp(btot) + _dot_tn(vc, kd)
            o = _rms(o, on_ref[...])
            r = gr_ref[rows, vcols]
            y_ref[rows, vcols] = (o * (r * jax.nn.sigmoid(r))).astype(BF16)
    for h in range(GLA_HEADS):
        state_ref[h] = states[h]


def _gla(gq, gk, gv, gr, gz, wa, ba, on):
    b, s, _ = gq.shape
    tg = 256
    tok = lambda w: pl.BlockSpec((None, tg, w), lambda bi, si: (bi, si, 0))
    full = lambda arr: pl.BlockSpec(arr.shape, lambda bi, si: (0,) * arr.ndim)
    return pl.pallas_call(
        _gla_kernel,
        grid=(b, s // tg),
        in_specs=[tok(GLA_QK_DIM), tok(GLA_QK_DIM), tok(GLA_V_DIM), tok(GLA_V_DIM), tok(GZ_PAD),
                  full(wa), full(ba), full(on)],
        out_specs=tok(GLA_V_DIM),
        out_shape=jax.ShapeDtypeStruct((b, s, GLA_V_DIM), BF16),
        scratch_shapes=[pltpu.VMEM((GLA_HEADS, GLA_VALUE_DIM, GLA_KEY_DIM), F32)],
        compiler_params=_cparams(2),
        name="gla",
    )(gq, gk, gv, gr, gz, wa, ba, on)


def _merge_route_kernel(x_ref, yc_ref, ya_ref, yg_ref, gmix_ref, wg_ref, wb_ref, wo_ref, gffn_ref,
                        wr_hi_ref, wr_lo_ref, br_ref, xo_ref, h2_ref, route_ref, counts_ref, run_ref):
    x = x_ref[...]
    h = _rms(x, gmix_ref[...]).astype(BF16)
    merged = None
    for n, y_ref in enumerate((yc_ref, ya_ref, yg_ref)):
        term = jax.nn.sigmoid(_dot(h, wg_ref[n])) * _dot(y_ref[...], wb_ref[n])
        merged = term if merged is None else merged + term
    xo = x + _dot(merged.astype(BF16), wo_ref[...])
    xo_ref[...] = xo
    h2 = _rms(xo, gffn_ref[...])
    _rows_to_tiles(h2_ref, h2)

    h_hi, h_lo = _split_bf16(h2)
    logits = (_dot(h_hi, wr_hi_ref[...]) + _dot(h_lo, wr_hi_ref[...]) + _dot(h_hi, wr_lo_ref[...])
              + br_ref[...])
    lane = lax.broadcasted_iota(jnp.int32, logits.shape, 1).astype(F32)
    big = 4.0 * ROUTE_LANES
    lg = jnp.where(lane < N_GROUPS, logits, -jnp.inf)
    gmax = jnp.max(lg, axis=1, keepdims=True)
    gidx = jnp.min(jnp.where(lg == gmax, lane, big), axis=1, keepdims=True)
    p_group_top = 1.0 / jnp.sum(jnp.exp(lg - gmax), axis=1, keepdims=True)
    lo_lane = N_GROUPS + gidx * EXPERTS_PER_GROUP
    le = jnp.where((lane >= lo_lane) & (lane < lo_lane + EXPERTS_PER_GROUP), logits, -jnp.inf)
    emax = jnp.max(le, axis=1, keepdims=True)
    i1 = jnp.min(jnp.where(le == emax, lane, big), axis=1, keepdims=True)
    esum = jnp.sum(jnp.exp(le - emax), axis=1, keepdims=True)
    le2 = jnp.where(lane == i1, -jnp.inf, le)
    emax2 = jnp.max(le2, axis=1, keepdims=True)
    i2 = jnp.min(jnp.where(le2 == emax2, lane, big), axis=1, keepdims=True)
    p1 = 1.0 / esum
    p2 = jnp.exp(emax2 - emax) / esum
    psum = p1 + p2
    w1 = p_group_top * p1 / psum
    w2 = p_group_top * p2 / psum
    e1 = i1 - N_GROUPS
    e2 = i2 - N_GROUPS

    @pl.when(pl.program_id(0) == 0)
    def _():
        run_ref[...] = jnp.zeros_like(run_ref)

    tm = x.shape[0]
    oh1 = jnp.where(lane == i1, 1.0, 0.0)
    oh2 = jnp.where(lane == i2, 1.0, 0.0)
    ohs = oh1 + oh2
    lower = (lax.broadcasted_iota(jnp.int32, (tm, tm), 0) > lax.broadcasted_iota(jnp.int32, (tm, tm), 1))
    before = _dot(jnp.where(lower, 1.0, 0.0).astype(BF16), ohs.astype(BF16)) + run_ref[...]
    rank1 = jnp.sum(oh1 * before, axis=1, keepdims=True)
    rank2 = jnp.sum(oh2 * before, axis=1, keepdims=True)
    run_ref[...] = run_ref[...] + jnp.sum(ohs, axis=0, keepdims=True)
    counts_ref[...] = run_ref[...]

    rl = lax.broadcasted_iota(jnp.int32, (tm, ROUTE_OUT), 1)
    rec = jnp.zeros((tm, ROUTE_OUT), F32)
    for slot, val in enumerate((e1, e2, w1, w2, rank1, rank2)):
        rec = jnp.where(rl == slot, val, rec)
    route_ref[...] = rec


def _merge_route(x2d, yc, ya, yg, gmix, wg, wb, wo, gffn, wr_hi, wr_lo, br):
    t, d = x2d.shape
    tm = 512
    tok = lambda w: pl.BlockSpec((tm, w), lambda i: (i, 0))
    full = lambda arr: pl.BlockSpec(arr.shape, lambda i: (0,) * arr.ndim)
    ins = (x2d, yc, ya, yg, gmix, wg, wb, wo, gffn, wr_hi, wr_lo, br)
    return pl.pallas_call(
        _merge_route_kernel,
        grid=(t // tm,),
        in_specs=[tok(d), tok(CONV_DIM), tok(ATTN_DIM), tok(GLA_V_DIM)] + [full(a) for a in ins[4:]],
        out_specs=(tok(d), pl.BlockSpec((tm * SUBLANES, LANES), lambda i: (i, 0)), tok(ROUTE_OUT),
                   pl.BlockSpec((1, ROUTE_LANES), lambda i: (0, 0))),
        out_shape=(jax.ShapeDtypeStruct((t, d), F32), jax.ShapeDtypeStruct((t * SUBLANES, LANES), F32),
                   jax.ShapeDtypeStruct((t, ROUTE_OUT), F32), jax.ShapeDtypeStruct((1, ROUTE_LANES), F32)),
        scratch_shapes=[pltpu.VMEM((1, ROUTE_LANES), F32)],
        compiler_params=_cparams(1),
        name="merge_route",
    )(*ins)


def _tile_slots(dest, tc):
    nt = dest.shape[0] // tc
    return dest.reshape(nt, tc, TOP_K).transpose(0, 2, 1).reshape(nt, 1, TOP_K * tc)


def _dispatch_kernel(last_blk_ref, dest_ref, h2_ref, xs_hbm, zeros_ref, rows_buf, sems):
    tc = h2_ref.shape[0] // SUBLANES
    step = pl.program_id(0)
    slot = step % 2
    sem = sems.at[0]

    @pl.when(pl.program_id(0) == 0)
    def _():
        zeros_ref[...] = jnp.zeros_like(zeros_ref)
        blk_rows = EXPERT_BLOCK * SUBLANES

        def zero_fill(first_row):
            return pltpu.make_async_copy(
                zeros_ref, xs_hbm.at[pl.ds(pl.multiple_of(first_row * SUBLANES, SUBLANES), blk_rows)], sem)

        for go in (lambda c: c.start(), lambda c: c.wait()):
            for e in range(N_EXPERTS):
                @pl.when(last_blk_ref[e] >= 0)
                def _():
                    go(zero_fill(last_blk_ref[e]))

            def unused(blk, carry):
                go(zero_fill(blk * EXPERT_BLOCK))
                return carry

            lax.fori_loop(last_blk_ref[N_EXPERTS], xs_hbm.shape[0] // blk_rows, unused, 0)

    stage = rows_buf.at[slot]
    stage[...] = h2_ref[...]

    def row_copy(r, kk):
        src = pl.multiple_of(r * SUBLANES, SUBLANES)
        dst = pl.multiple_of(dest_ref[0, kk * tc + r] * SUBLANES, SUBLANES)
        return pltpu.make_async_copy(stage.at[pl.ds(src, SUBLANES)], xs_hbm.at[pl.ds(dst, SUBLANES)],
                                     sems.at[slot])

    def issue(r, carry):
        row_copy(r, 0).start(priority=0)
        row_copy(r, 1).start(priority=1)
        return carry

    lax.fori_loop(0, tc, issue, 0, unroll=ROW_DMA_UNROLL)

    def drain(which):
        for _ in range(TOP_K):
            pltpu.make_async_copy(rows_buf.at[which], xs_hbm.at[pl.ds(0, tc * SUBLANES)], sems.at[which]).wait()

    @pl.when(step > 0)
    def _():
        drain(1 - slot)

    @pl.when(step == pl.num_programs(0) - 1)
    def _():
        drain(slot)


def _dispatch(dest, h2_tiles, n_rows, last_blk):
    t = h2_tiles.shape[0] // SUBLANES
    tc = 256
    grid_spec = pltpu.PrefetchScalarGridSpec(
        num_scalar_prefetch=1,
        grid=(t // tc,),
        in_specs=[
            pl.BlockSpec((None, 1, TOP_K * tc), lambda i, lb: (i, 0, 0), memory_space=pltpu.SMEM),
            pl.BlockSpec((tc * SUBLANES, LANES), lambda i, lb: (i, 0)),
        ],
        out_specs=pl.BlockSpec(memory_space=pl.ANY),
        scratch_shapes=[pltpu.VMEM((EXPERT_BLOCK * SUBLANES, LANES), F32),
                        pltpu.VMEM((2, tc * SUBLANES, LANES), F32), pltpu.SemaphoreType.DMA((2,))],
    )
    return pl.pallas_call(
        _dispatch_kernel,
        grid_spec=grid_spec,
        out_shape=jax.ShapeDtypeStruct((n_rows * SUBLANES, LANES), F32),
        compiler_params=_cparams(1),
        name="moe_dispatch",
    )(last_blk, _tile_slots(dest, tc), h2_tiles)


def _expert_kernel(blk_expert_ref, n_used_ref, xs_ref, wg_ref, wu_ref, wd_ref, ys_ref,
                   wg_bf, wu_bf, wd_bf):
    i = pl.program_id(0)
    rb = xs_ref.shape[0] // SUBLANES
    new_expert = (i == 0) | (blk_expert_ref[i] != blk_expert_ref[jnp.maximum(i - 1, 0)])

    @pl.when(new_expert)
    def _():
        wg_bf[...] = wg_ref[...].astype(BF16)
        wu_bf[...] = wu_ref[...].astype(BF16)
        wd_bf[...] = wd_ref[...].astype(BF16)

    @pl.when(i < n_used_ref[0])
    def _():
        xb = jnp.concatenate(_tiles_to_rows(xs_ref, rb), axis=1).astype(BF16)
        gate = _dot(xb, wg_bf[...])
        up = _dot(xb, wu_bf[...])
        act = (gate * jax.nn.sigmoid(gate) * up).astype(BF16)
        _rows_to_tiles(ys_ref, _dot(act, wd_bf[...]))

    @pl.when(i >= n_used_ref[0])
    def _():
        ys_ref[...] = jnp.zeros_like(ys_ref)


def _experts(blk_expert, n_used, xs_tiles, wg, wu, wd, layer):
    d = SUBLANES * LANES
    n_blocks = blk_expert.shape[0]
    rb = EXPERT_BLOCK
    row_block = (rb * SUBLANES, LANES)
    grid_spec = pltpu.PrefetchScalarGridSpec(
        num_scalar_prefetch=2,
        grid=(n_blocks,),
        in_specs=[
            pl.BlockSpec(row_block, lambda i, be, nu: (jnp.minimum(i, nu[0] - 1), 0)),
            pl.BlockSpec((None, None, d, EXPERT_FF), lambda i, be, nu: (layer, be[i], 0, 0)),
            pl.BlockSpec((None, None, d, EXPERT_FF), lambda i, be, nu: (layer, be[i], 0, 0)),
            pl.BlockSpec((None, None, EXPERT_FF, d), lambda i, be, nu: (layer, be[i], 0, 0)),
        ],
        out_specs=pl.BlockSpec(row_block, lambda i, be, nu: (i, 0)),
        scratch_shapes=[pltpu.VMEM((d, EXPERT_FF), BF16), pltpu.VMEM((d, EXPERT_FF), BF16),
                        pltpu.VMEM((EXPERT_FF, d), BF16)],
    )
    return pl.pallas_call(
        _expert_kernel,
        grid_spec=grid_spec,
        out_shape=jax.ShapeDtypeStruct(xs_tiles.shape, F32),
        compiler_params=_cparams(1),
        name="moe_experts",
    )(blk_expert, n_used, xs_tiles, wg, wu, wd)


def _combine_kernel(slots_ref, next_slots_ref, x_ref, route_ref, ys_hbm, o_ref, rows_buf, sem):
    tc = x_ref.shape[0]
    rows = _pipelined_expert_rows(slots_ref, next_slots_ref, ys_hbm, rows_buf, sem,
                                  pl.program_id(0), pl.num_programs(0), tc)
    for j, chunk in enumerate(_moe_residual_chunks(x_ref, route_ref, rows, tc)):
        o_ref[:, j * LANES:(j + 1) * LANES] = chunk


def _combine(dest, x2d, route, ys_tiles):
    t, d = x2d.shape
    tc = 256
    nt = t // tc
    slots = _tile_slots(dest, tc)
    slot_block = lambda idx: pl.BlockSpec((None, 1, TOP_K * tc), idx, memory_space=pltpu.SMEM)
    return pl.pallas_call(
        _combine_kernel,
        grid=(nt,),
        in_specs=[
            slot_block(lambda i: (i, 0, 0)),
            slot_block(lambda i: (jnp.minimum(i + 1, nt - 1), 0, 0)),
            pl.BlockSpec((tc, d), lambda i: (i, 0)),
            pl.BlockSpec((tc, ROUTE_OUT), lambda i: (i, 0)),
            pl.BlockSpec(memory_space=pl.ANY),
        ],
        out_specs=pl.BlockSpec((tc, d), lambda i: (i, 0)),
        out_shape=jax.ShapeDtypeStruct((t, d), F32),
        scratch_shapes=[pltpu.VMEM((2, TOP_K, tc * SUBLANES, LANES), F32), pltpu.SemaphoreType.DMA((2,))],
        compiler_params=_cparams(1),
        name="moe_combine",
    )(slots, slots, x2d, route, ys_tiles)


def _dispatch_plan(route, counts, t):
    e_ids = route[:, :TOP_K].astype(jnp.int32)
    rank = route[:, 2 * TOP_K:3 * TOP_K].astype(jnp.int32)
    counts = counts[0, N_GROUPS:N_GROUPS + N_EXPERTS].astype(jnp.int32)
    padded = ((counts + EXPERT_BLOCK - 1) // EXPERT_BLOCK) * EXPERT_BLOCK
    pad_end = jnp.cumsum(padded)
    pad_start = pad_end - padded
    onehot = e_ids[:, :, None] == jnp.arange(N_EXPERTS, dtype=jnp.int32)
    dest = rank + jnp.sum(jnp.where(onehot, pad_start, 0), axis=-1)
    n_blocks = -(-(t * TOP_K) // EXPERT_BLOCK) + N_EXPERTS
    blk_start = jnp.arange(n_blocks, dtype=jnp.int32) * EXPERT_BLOCK
    blk_expert = jnp.minimum(jnp.sum(blk_start[:, None] >= pad_end[None, :], axis=1), N_EXPERTS - 1)
    n_used = jnp.maximum(pad_end[-1:] // EXPERT_BLOCK, 1).astype(jnp.int32)
    last_blk = jnp.concatenate([jnp.where(padded > 0, pad_end - EXPERT_BLOCK, -1), n_used]).astype(jnp.int32)
    return blk_expert.astype(jnp.int32), n_used, dest, n_blocks * EXPERT_BLOCK, last_blk


def kernel(x, rel_bias_table, norm_mix, w_in, conv_w, conv_b, q_norm, k_norm, w_gla_alpha, b_gla_alpha,
           gla_out_norm, w_merge_gate, w_branch, w_out, norm_ffn, w_router_group, b_router_group,
           w_router_expert, b_router_expert, w_expert_gate, w_expert_up, w_expert_down):
    b, s, d = x.shape
    t = b * s
    depth = w_in.shape[0]
    assert s % MOBA_BLOCK == 0 and t % EXPERT_BLOCK == 0
    assert d == SUBLANES * LANES, "row-granular DMAs store each activation row as one (8, 128) tile"
    bias0, bias1 = _attn_biases(rel_bias_table)
    head_id = jnp.arange(ATTN_DIM) // ATTN_HEAD_DIM
    hsum = (head_id[:, None] == head_id[None, :]).astype(BF16)
    for l in range(depth):
        w_l = jnp.pad(w_in[l].astype(BF16), ((0, 0), (0, GZ_PAD - GLA_GATE_RANK)))
        w_alpha = jnp.pad(w_gla_alpha[l].astype(BF16), ((0, GZ_PAD - GLA_GATE_RANK), (0, 0)))
        yconv, q, k, v, kmean, gq, gk, gv, gr, gz = _inproj(
            x, norm_mix[l][None], w_l, conv_w[l], conv_b[l][None],
            jnp.tile(q_norm[l], ATTN_HEADS)[None], jnp.tile(k_norm[l], ATTN_HEADS)[None], hsum)
        qp, kp = _select(q, k, kmean.reshape(b, s // MOBA_BLOCK, ATTN_DIM))
        yattn = _attention(qp, kp, v, bias0, bias1)
        ygla = _gla(gq, gk, gv, gr, gz, w_alpha, b_gla_alpha[l][None],
                    gla_out_norm[l][None])
        w_r = jnp.concatenate([w_router_group[l], w_router_expert[l]], axis=1)
        w_r = jnp.pad(w_r, ((0, 0), (0, ROUTE_LANES - w_r.shape[1])))
        b_r = jnp.pad(jnp.concatenate([b_router_group[l], b_router_expert[l]]),
                      (0, ROUTE_LANES - N_GROUPS - N_EXPERTS))[None]
        wr_hi, wr_lo = _split_bf16(w_r)
        xo, h2, route, counts = _merge_route(
            x.reshape(t, d), yconv.reshape(t, -1), yattn.reshape(t, -1), ygla.reshape(t, -1),
            norm_mix[l][None], w_merge_gate[l].astype(BF16), w_branch[l].astype(BF16),
            w_out[l].astype(BF16), norm_ffn[l][None], wr_hi, wr_lo, b_r)
        blk_expert, n_used, dest, n_rows, last_blk = _dispatch_plan(route, counts, t)
        xs = _dispatch(dest, h2, n_rows, last_blk)
        ys = _experts(blk_expert, n_used, xs, w_expert_gate, w_expert_up, w_expert_down, l)
        x = _combine(dest, xo, route, ys).reshape(b, s, d)
    return x
```

```python
import functools
import math

import jax
import jax.numpy as jnp
import numpy as np
from jax import lax
from jax.experimental import pallas as pl
from jax.experimental.pallas import tpu as pltpu

CONV_DIM = 512
CONV_WIDTH = 3
ATTN_HEADS = 8
ATTN_HEAD_DIM = 64
ATTN_DIM = ATTN_HEADS * ATTN_HEAD_DIM
MOBA_BLOCK = 256
MOBA_TOPK = 3
REL_BUCKETS = 32
REL_MAX_DISTANCE = 128
GLA_HEADS = 4
GLA_KEY_DIM = 64
GLA_VALUE_DIM = 128
GLA_QK_DIM = GLA_HEADS * GLA_KEY_DIM
GLA_V_DIM = GLA_HEADS * GLA_VALUE_DIM
GLA_GATE_RANK = 16
GLA_GATE_TEMP = 16.0
GLA_CHUNK = 64
N_GROUPS = 4
EXPERTS_PER_GROUP = 8
N_EXPERTS = N_GROUPS * EXPERTS_PER_GROUP
TOP_K = 2
EXPERT_FF = 512
EXPERT_BLOCK = 256
RMS_EPS = 1e-6

LANES = 128
VMEM_LIMIT_BYTES = 56 * 1024 * 1024

MASK_VALUE = -1e30
ROUTE_LANES = 128
ROUTE_OUT = 8
ATTN_FAR_CHUNK = 4
ATTN_TAIL = ATTN_FAR_CHUNK + 1
XS_RING = 3
ROW_DMA_UNROLL = 8
GZ_PAD = LANES

F32 = jnp.float32
BF16 = jnp.bfloat16


def _cparams(n_axes):
    return pltpu.CompilerParams(
        dimension_semantics=("arbitrary",) * n_axes,
        vmem_limit_bytes=VMEM_LIMIT_BYTES,
    )


def _rms(x, gain):
    return x * lax.rsqrt(jnp.mean(x * x, axis=-1, keepdims=True) + RMS_EPS) * gain


def _split_bf16(x):
    hi = x.astype(BF16)
    lo = (x - hi.astype(F32)).astype(BF16)
    return hi, lo


def _dot(a, b):
    return jnp.dot(a, b, preferred_element_type=F32)


def _dot_nt(a, b):
    return lax.dot_general(a, b, (((1,), (1,)), ((), ())), preferred_element_type=F32)


SUBLANES = 8


def _rows_to_tiles(dst_ref, x):
    n = x.shape[0]
    for j in range(SUBLANES):
        dst_ref[pl.ds(j, n, stride=SUBLANES), :] = x[:, j * LANES:(j + 1) * LANES]


def _tiles_to_rows(src_ref, n):
    return [src_ref[pl.ds(j, n, stride=SUBLANES), :] for j in range(SUBLANES)]


def _dot_tn(a, b):
    return lax.dot_general(a, b, (((0,), (0,)), ((), ())), preferred_element_type=F32)


def _start_expert_row_gather(slots_ref, ys_hbm, buf, sem, n):
    def row_copy(r, kk):
        src = pl.multiple_of(slots_ref[0, kk * n + r] * SUBLANES, SUBLANES)
        dst = pl.multiple_of(r * SUBLANES, SUBLANES)
        return pltpu.make_async_copy(ys_hbm.at[pl.ds(src, SUBLANES)], buf.at[kk, pl.ds(dst, SUBLANES)], sem)

    def issue(r, carry):
        row_copy(r, 0).start(priority=0)
        row_copy(r, 1).start(priority=1)
        return carry

    lax.fori_loop(0, n, issue, 0, unroll=ROW_DMA_UNROLL)


def _finish_expert_row_gather(ys_hbm, buf, sem, n):
    for kk in range(TOP_K):
        pltpu.make_async_copy(ys_hbm.at[pl.ds(0, n * SUBLANES)], buf.at[kk], sem).wait()


def _pipelined_expert_rows(slots_ref, next_slots_ref, ys_hbm, rows_buf, sem, tile, n_tiles, n):
    slot = tile % 2

    @pl.when(tile == 0)
    def _():
        _start_expert_row_gather(slots_ref, ys_hbm, rows_buf.at[0], sem.at[0], n)

    @pl.when(tile + 1 < n_tiles)
    def _():
        _start_expert_row_gather(next_slots_ref, ys_hbm, rows_buf.at[1 - slot], sem.at[1 - slot], n)

    _finish_expert_row_gather(ys_hbm, rows_buf.at[slot], sem.at[slot], n)
    return rows_buf.at[slot]


def _moe_residual_chunks(x_ref, route_ref, buf, n):
    route = route_ref[...]
    w1 = route[:, TOP_K:TOP_K + 1]
    w2 = route[:, TOP_K + 1:TOP_K + 2]
    y1 = _tiles_to_rows(buf.at[0], n)
    y2 = _tiles_to_rows(buf.at[1], n)
    return [x_ref[:, j * LANES:(j + 1) * LANES] + (y1[j] * w1 + y2[j] * w2) for j in range(SUBLANES)]


def _inproj_kernel(x_ref, gmix_ref, w_ref, convw_ref, convb_ref,
                   qn_ref, kn_ref, hsum_ref,
                   yconv_ref, q_ref, k_ref, v_ref, kmean_ref, gq_ref, gk_ref, gv_ref, gr_ref, gz_ref,
                   carry_ref):
    s_idx = pl.program_id(1)
    ts = x_ref.shape[0]
    h = _rms(x_ref[...], gmix_ref[...]).astype(BF16)

    c3 = 3 * CONV_DIM
    a3 = c3 + 3 * ATTN_DIM
    g3 = a3 + 2 * GLA_QK_DIM + 2 * GLA_V_DIM
    c = _dot(h, w_ref[:, :c3])
    cb = c[:, :CONV_DIM]
    u = c[:, CONV_DIM:2 * CONV_DIM] * c[:, 2 * CONV_DIM:]

    @pl.when(s_idx == 0)
    def _():
        carry_ref[...] = jnp.zeros_like(carry_ref)

    prev = carry_ref[...]
    row = lax.broadcasted_iota(jnp.int32, u.shape, 0)
    u1 = pltpu.roll(u, 1, 0)
    u1 = jnp.where(row == 0, prev[7:8, :], u1)
    u2 = pltpu.roll(u, 2, 0)
    u2 = jnp.where(row == 0, prev[6:7, :], jnp.where(row == 1, prev[7:8, :], u2))
    carry_ref[...] = u[ts - 8:, :]
    y = convb_ref[...] + convw_ref[0:1, :] * u2
    y = y + convw_ref[1:2, :] * u1
    y = y + convw_ref[2:3, :] * u
    yconv_ref[...] = (cb * y).astype(BF16)

    a = _dot(h, w_ref[:, c3:a3])
    hsum = hsum_ref[...]

    def head_norm(t, gain):
        hi, lo = _split_bf16(t * t)
        ss = _dot(hi, hsum) + _dot(lo, hsum)
        return t * lax.rsqrt(ss * (1.0 / ATTN_HEAD_DIM) + RMS_EPS) * gain

    qn = head_norm(a[:, :ATTN_DIM], qn_ref[...])
    kn = head_norm(a[:, ATTN_DIM:2 * ATTN_DIM], kn_ref[...])
    q_ref[...] = (qn * (ATTN_HEAD_DIM ** -0.5)).astype(BF16)
    k_ref[...] = kn.astype(BF16)
    v_ref[...] = jnp.transpose(a[:, 2 * ATTN_DIM:]).astype(BF16)
    for j in range(ts // MOBA_BLOCK):
        kmean_ref[j] = jnp.mean(kn[j * MOBA_BLOCK:(j + 1) * MOBA_BLOCK], axis=0, keepdims=True)

    g = _dot(h, w_ref[:, a3:g3])
    gq_ref[...] = g[:, :GLA_QK_DIM]
    gk_ref[...] = g[:, GLA_QK_DIM:2 * GLA_QK_DIM]
    gv_ref[...] = g[:, 2 * GLA_QK_DIM:2 * GLA_QK_DIM + GLA_V_DIM]
    gr_ref[...] = g[:, 2 * GLA_QK_DIM + GLA_V_DIM:]
    gz_ref[...] = _dot(h, w_ref[:, g3:])


def _inproj(x, gmix, w, convw, convb, qn, kn, hsum):
    b, s, d = x.shape
    blocks_per_tile = 2
    ts = blocks_per_tile * MOBA_BLOCK
    assert s % ts == 0
    nb = s // MOBA_BLOCK
    tok = lambda w: pl.BlockSpec((None, ts, w), lambda bi, si: (bi, si, 0))
    full = lambda arr: pl.BlockSpec(arr.shape, lambda bi, si: (0,) * arr.ndim)
    out_shapes = (
        jax.ShapeDtypeStruct((b, s, CONV_DIM), BF16),
        jax.ShapeDtypeStruct((b, s, ATTN_DIM), BF16),
        jax.ShapeDtypeStruct((b, s, ATTN_DIM), BF16),
        jax.ShapeDtypeStruct((b, ATTN_DIM, s), BF16),
        jax.ShapeDtypeStruct((b, nb, 1, ATTN_DIM), F32),
        jax.ShapeDtypeStruct((b, s, GLA_QK_DIM), F32),
        jax.ShapeDtypeStruct((b, s, GLA_QK_DIM), F32),
        jax.ShapeDtypeStruct((b, s, GLA_V_DIM), F32),
        jax.ShapeDtypeStruct((b, s, GLA_V_DIM), F32),
        jax.ShapeDtypeStruct((b, s, GZ_PAD), F32),
    )
    out_specs = (
        tok(CONV_DIM), tok(ATTN_DIM), tok(ATTN_DIM),
        pl.BlockSpec((None, ATTN_DIM, ts), lambda bi, si: (bi, 0, si)),
        pl.BlockSpec((None, blocks_per_tile, 1, ATTN_DIM), lambda bi, si: (bi, si, 0, 0)),
        tok(GLA_QK_DIM), tok(GLA_QK_DIM), tok(GLA_V_DIM), tok(GLA_V_DIM), tok(GZ_PAD),
    )
    ins = (x, gmix, w, convw, convb, qn, kn, hsum)
    in_specs = [tok(d)] + [full(a) for a in ins[1:]]
    return pl.pallas_call(
        _inproj_kernel,
        grid=(b, s // ts),
        in_specs=in_specs,
        out_specs=out_specs,
        out_shape=out_shapes,
        scratch_shapes=[pltpu.VMEM((8, CONV_DIM), F32)],
        compiler_params=_cparams(2),
        name="inproj",
    )(*ins)


def _select_kernel(q_ref, k_ref, kmean_ref, qp_ref, kp_ref):
    own = pl.program_id(1)
    tq = q_ref.shape[0]
    nb = kmean_ref.shape[0]
    half = ATTN_HEAD_DIM
    lane = lax.broadcasted_iota(jnp.int32, (tq, LANES), 1)
    blk = lax.broadcasted_iota(jnp.int32, (nb, tq), 0).astype(F32)
    own_f = own.astype(F32)
    onehot = jnp.where(lane - half == own, 1.0, 0.0).astype(F32)
    kmean = kmean_ref[...].astype(BF16)
    for p in range(ATTN_HEADS // 2):
        qpair = q_ref[:, p * LANES:(p + 1) * LANES]
        kpair = k_ref[:, p * LANES:(p + 1) * LANES].astype(F32)
        kmpair = kmean[:, p * LANES:(p + 1) * LANES]
        qpair_f = qpair.astype(F32)
        for sub in range(2):
            h = 2 * p + sub
            lane_sel = (lane >= sub * half) & (lane < (sub + 1) * half)
            qh = jnp.where(lane_sel, qpair_f, 0.0).astype(BF16)
            gate_t = _dot_nt(kmpair, qh)
            g = jnp.where(blk < own_f, gate_t, -jnp.inf)
            alive = jnp.ones((nb, tq), F32)
            sel = jnp.where(blk == own_f, 1.0, 0.0)
            for r in range(MOBA_TOPK):
                ga = jnp.where(alive > 0.0, g, -jnp.inf)
                mx = jnp.max(ga, axis=0, keepdims=True)
                cand = jnp.where((alive > 0.0) & (g == mx), blk, 2.0 * LANES)
                first = jnp.min(cand, axis=0, keepdims=True)
                hit = blk == first
                sel = jnp.where(hit, jnp.maximum(sel, jnp.where(own_f > r, 1.0, 0.0)), sel)
                alive = jnp.where(hit, 0.0, alive)
            m_t = jnp.where(sel > 0.0, 0.0, MASK_VALUE).astype(F32)
            m_t = jnp.concatenate([m_t, jnp.zeros((LANES - nb, tq), F32)], axis=0)
            m = jnp.transpose(m_t)
            m = pltpu.roll(m, half, 1)
            m = jnp.where((lane >= half) & (lane < half + nb), m, 0.0)
            qs = qpair_f if sub == 0 else pltpu.roll(qpair_f, half, 1)
            ks = kpair if sub == 0 else pltpu.roll(kpair, half, 1)
            qp_ref[h] = jnp.where(lane < half, qs, m).astype(BF16)
            kp_ref[h] = jnp.where(lane < half, ks, onehot).astype(BF16)


def _select(q, k, kmean):
    b, s, _ = q.shape
    tq = MOBA_BLOCK
    nb = s // tq
    assert nb <= LANES - ATTN_HEAD_DIM, "block one-hot must fit beside the head dim in one lane tile"
    tok = pl.BlockSpec((None, tq, ATTN_DIM), lambda bi, si: (bi, si, 0))
    slab = pl.BlockSpec((None, ATTN_HEADS, tq, LANES), lambda bi, si: (bi, 0, si, 0))
    shp = jax.ShapeDtypeStruct((b, ATTN_HEADS, s, LANES), BF16)
    return pl.pallas_call(
        _select_kernel,
        grid=(b, nb),
        in_specs=[tok, tok, pl.BlockSpec((None, nb, ATTN_DIM), lambda bi, si: (bi, 0, 0))],
        out_specs=(slab, slab),
        out_shape=(shp, shp),
        compiler_params=_cparams(2),
        name="moba_select",
    )(q, k, kmean)


def _attn_kernel(qp_ref, kp_ref, vt_ref, bias0_ref, bias1_ref, o_ref, m_ref, l_ref, acc_ref,
                 sa_ref, sb_ref, st_ref):
    own = pl.program_id(2)
    tq = qp_ref.shape[1]
    far_rows = ATTN_FAR_CHUNK * tq
    sub = lax.broadcasted_iota(jnp.int32, (LANES, tq), 0)
    n_far = jnp.maximum(own - 1, 0) // ATTN_FAR_CHUNK

    def scores(hh, start, n):
        return _dot_nt(kp_ref[hh, pl.ds(start, n * tq), :], qp_ref[hh])

    def far_scores(dst_ref, chunk):
        start = pl.multiple_of(chunk * far_rows, far_rows)
        for hh in range(2):
            dst_ref[hh] = scores(hh, start, ATTN_FAR_CHUNK)

    def softmax_pv(hh, s_ref, start, n):
        blocks = [s_ref[hh, w * tq:(w + 1) * tq, :] for w in range(n)]
        m_prev = m_ref[hh]
        m_new = jnp.maximum(m_prev, jnp.max(functools.reduce(jnp.maximum, blocks), axis=0, keepdims=True))
        alpha = jnp.exp(m_prev - m_new)
        ps = [jnp.exp(blk - m_new) for blk in blocks]
        l_ref[hh] = alpha * l_ref[hh] + jnp.sum(functools.reduce(jnp.add, ps), axis=0, keepdims=True)
        pt = jnp.concatenate([x.astype(BF16) for x in ps], axis=0)
        acc_ref[hh] = alpha * acc_ref[hh] + _dot(vt_ref[:, pl.ds(start, n * tq)], pt)
        m_ref[hh] = m_new

    for hh in range(2):
        m_ref[hh] = jnp.full((1, tq), MASK_VALUE, F32)
        l_ref[hh] = jnp.zeros((1, tq), F32)
        acc_ref[hh] = jnp.zeros((LANES, tq), F32)

    @pl.when(n_far > 0)
    def _():
        far_scores(sa_ref, 0)

    def far_softmax(s_ref, chunk):
        for hh in range(2):
            softmax_pv(hh, s_ref, pl.multiple_of(chunk * far_rows, far_rows), ATTN_FAR_CHUNK)

    def far_pair(j, carry):
        c0 = 2 * j
        far_scores(sb_ref, c0 + 1)
        far_softmax(sa_ref, c0)

        @pl.when(c0 + 2 < n_far)
        def _():
            far_scores(sa_ref, c0 + 2)
            far_softmax(sb_ref, c0 + 1)

        return carry

    lax.fori_loop(0, n_far // 2, far_pair, 0)

    left = jnp.maximum(own - 1, 0) % ATTN_FAR_CHUNK
    tail_start = pl.multiple_of(n_far * far_rows, tq)
    last_even = (n_far % 2) == 1

    def finish(n, start, last_ref):
        for hh in range(2):
            s = scores(hh, start, n)
            for w in range(n):
                sw = s[w * tq:(w + 1) * tq, :]
                if w == n - 1:
                    sw = sw + bias0_ref[hh]
                elif w == n - 2:
                    sw = sw + bias1_ref[hh]
                st_ref[hh, w * tq:(w + 1) * tq, :] = sw
        if last_ref is not None:
            far_softmax(last_ref, n_far - 1)
        for hh in range(2):
            softmax_pv(hh, st_ref, start, n)

    @pl.when(own == 0)
    def _():
        finish(1, 0, None)

    for n_left in range(ATTN_FAR_CHUNK):
        right_size = (own >= 1) & (left == n_left)

        @pl.when(right_size & (n_far == 0))
        def _():
            finish(n_left + 2, tail_start, None)

        @pl.when(right_size & (n_far > 0) & last_even)
        def _():
            finish(n_left + 2, tail_start, sa_ref)

        @pl.when(right_size & (n_far > 0) & jnp.logical_not(last_even))
        def _():
            finish(n_left + 2, tail_start, sb_ref)

    outs = [acc_ref[hh] / l_ref[hh] for hh in range(2)]
    o_t = jnp.where(sub < ATTN_HEAD_DIM, outs[0], outs[1])
    o_ref[...] = jnp.transpose(o_t).astype(BF16)


def _attention(qp, kp, vt, bias0, bias1):
    b, nh, s, _ = qp.shape
    tq = MOBA_BLOCK
    nq = s // tq
    return pl.pallas_call(
        _attn_kernel,
        grid=(b, nh // 2, nq),
        in_specs=[
            pl.BlockSpec((None, 2, tq, LANES), lambda bi, pi, qi: (bi, pi, qi, 0)),
            pl.BlockSpec((None, 2, s, LANES), lambda bi, pi, qi: (bi, pi, 0, 0)),
            pl.BlockSpec((None, LANES, s), lambda bi, pi, qi: (bi, pi, 0)),
            pl.BlockSpec((2, tq, tq), lambda bi, pi, qi: (pi, 0, 0)),
            pl.BlockSpec((2, tq, tq), lambda bi, pi, qi: (pi, 0, 0)),
        ],
        out_specs=pl.BlockSpec((None, tq, LANES), lambda bi, pi, qi: (bi, qi, pi)),
        out_shape=jax.ShapeDtypeStruct((b, s, ATTN_DIM), BF16),
        scratch_shapes=[
            pltpu.VMEM((2, 1, tq), F32),
            pltpu.VMEM((2, 1, tq), F32),
            pltpu.VMEM((2, LANES, tq), F32),
            pltpu.VMEM((2, ATTN_FAR_CHUNK * tq, tq), F32),
            pltpu.VMEM((2, ATTN_FAR_CHUNK * tq, tq), F32),
            pltpu.VMEM((2, ATTN_TAIL * tq, tq), F32),
        ],
        compiler_params=_cparams(3),
        name="moba_attention",
    )(qp, kp, vt, bias0, bias1)


def _t5_bucket(rel):
    n = jnp.maximum(rel, 0)
    max_exact = REL_BUCKETS // 2
    scaled = (jnp.log(jnp.maximum(n, max_exact).astype(F32) / max_exact)
              / math.log(REL_MAX_DISTANCE / max_exact))
    large = jnp.minimum(max_exact + (scaled * (REL_BUCKETS - max_exact)).astype(jnp.int32), REL_BUCKETS - 1)
    return jnp.where(n < max_exact, n, large)


def _attn_biases(rel_bias_table):
    table = rel_bias_table.astype(F32).T
    pos = jnp.arange(MOBA_BLOCK)
    rel0 = pos[None, :] - pos[:, None]
    far = table[:, REL_BUCKETS - 1][:, None, None]

    rel = jnp.arange(1 - 2 * MOBA_BLOCK, 2 * MOBA_BLOCK)
    hit = _t5_bucket(rel)[None, :, None] == jnp.arange(REL_BUCKETS)
    by_rel = jnp.sum(jnp.where(hit, table[:, None, :], 0.0), axis=-1)
    zero = 2 * MOBA_BLOCK - 1

    def tile(offset):
        n = MOBA_BLOCK
        w = by_rel[:, zero + offset - (n - 1):zero + offset + n]
        wp = jnp.pad(w, ((0, 0), (0, 1)))
        skew = jnp.broadcast_to(wp[:, None, :], (wp.shape[0], n, 2 * n)).reshape(wp.shape[0], -1)
        skew = skew[:, :n * (2 * n - 1)].reshape(wp.shape[0], n, 2 * n - 1)
        return skew[:, :, n - 1:]

    bias0 = jnp.where(rel0[None] >= 0, tile(0) - far, MASK_VALUE)
    bias1 = tile(MOBA_BLOCK) - far
    return bias0, bias1


def _gla_kernel(gq_ref, gk_ref, gv_ref, gr_ref, gz_ref, wa_ref, ba_ref, on_ref, y_ref, state_ref):
    s_idx = pl.program_id(1)
    tg = gq_ref.shape[0]
    dk, dv, ck = GLA_KEY_DIM, GLA_VALUE_DIM, GLA_CHUNK

    @pl.when(s_idx == 0)
    def _():
        state_ref[...] = jnp.zeros_like(state_ref)

    z = gz_ref[...].astype(BF16)
    log_a = jax.nn.log_sigmoid(_dot(z, wa_ref[...]) + ba_ref[...]) / GLA_GATE_TEMP
    row = lax.broadcasted_iota(jnp.int32, log_a.shape, 0) % ck
    bcum = log_a
    shift = 1
    while shift < ck:
        bcum = bcum + jnp.where(row >= shift, pltpu.roll(bcum, shift, 0), 0.0)
        shift *= 2
    q = gq_ref[...] * (dk ** -0.5)
    k = gk_ref[...]
    tri = (lax.broadcasted_iota(jnp.int32, (ck, ck), 0) >= lax.broadcasted_iota(jnp.int32, (ck, ck), 1))
    states = [state_ref[h] for h in range(GLA_HEADS)]
    for c in range(tg // ck):
        rows = slice(c * ck, (c + 1) * ck)
        for h in range(GLA_HEADS):
            kcols = slice(h * dk, (h + 1) * dk)
            vcols = slice(h * dv, (h + 1) * dv)
            bc = bcum[rows, kcols]
            btot = bc[ck - 1:ck, :]
            qe = (q[rows, kcols] * jnp.exp(bc)).astype(BF16)
            ke = (k[rows, kcols] * jnp.exp(-bc)).astype(BF16)
            kd = (k[rows, kcols] * jnp.exp(btot - bc)).astype(BF16)
            vc = gv_ref[rows, vcols].astype(BF16)
            att = jnp.where(tri, _dot_nt(qe, ke), 0.0).astype(BF16)
            st = states[h]
            o = _dot(att, vc) + _dot_nt(qe, st.astype(BF16))
            states[h] = st * jnp.exp(btot) + _dot_tn(vc, kd)
            o = _rms(o, on_ref[...])
            r = gr_ref[rows, vcols]
            y_ref[rows, vcols] = (o * (r * jax.nn.sigmoid(r))).astype(BF16)
    for h in range(GLA_HEADS):
        state_ref[h] = states[h]


def _gla(gq, gk, gv, gr, gz, wa, ba, on):
    b, s, _ = gq.shape
    tg = 256
    tok = lambda w: pl.BlockSpec((None, tg, w), lambda bi, si: (bi, si, 0))
    full = lambda arr: pl.BlockSpec(arr.shape, lambda bi, si: (0,) * arr.ndim)
    return pl.pallas_call(
        _gla_kernel,
        grid=(b, s // tg),
        in_specs=[tok(GLA_QK_DIM), tok(GLA_QK_DIM), tok(GLA_V_DIM), tok(GLA_V_DIM), tok(GZ_PAD),
                  full(wa), full(ba), full(on)],
        out_specs=tok(GLA_V_DIM),
        out_shape=jax.ShapeDtypeStruct((b, s, GLA_V_DIM), BF16),
        scratch_shapes=[pltpu.VMEM((GLA_HEADS, GLA_VALUE_DIM, GLA_KEY_DIM), F32)],
        compiler_params=_cparams(2),
        name="gla",
    )(gq, gk, gv, gr, gz, wa, ba, on)


def _merge_route_kernel(x_ref, yc_ref, ya_ref, yg_ref, gmix_ref, wg_ref, wb_ref, wo_ref, gffn_ref,
                        wr_hi_ref, wr_lo_ref, br_ref, xo_ref, h2_ref, route_ref, counts_ref, run_ref):
    x = x_ref[...]
    h = _rms(x, gmix_ref[...]).astype(BF16)
    merged = None
    for n, y_ref in enumerate((yc_ref, ya_ref, yg_ref)):
        term = jax.nn.sigmoid(_dot(h, wg_ref[n])) * _dot(y_ref[...], wb_ref[n])
        merged = term if merged is None else merged + term
    xo = x + _dot(merged.astype(BF16), wo_ref[...])
    xo_ref[...] = xo
    h2 = _rms(xo, gffn_ref[...])
    _rows_to_tiles(h2_ref, h2)

    h_hi, h_lo = _split_bf16(h2)
    logits = (_dot(h_hi, wr_hi_ref[...]) + _dot(h_lo, wr_hi_ref[...]) + _dot(h_hi, wr_lo_ref[...])
              + br_ref[...])
    lane = lax.broadcasted_iota(jnp.int32, logits.shape, 1).astype(F32)
    big = 4.0 * ROUTE_LANES
    lg = jnp.where(lane < N_GROUPS, logits, -jnp.inf)
    gmax = jnp.max(lg, axis=1, keepdims=True)
    gidx = jnp.min(jnp.where(lg == gmax, lane, big), axis=1, keepdims=True)
    p_group_top = 1.0 / jnp.sum(jnp.exp(lg - gmax), axis=1, keepdims=True)
    lo_lane = N_GROUPS + gidx * EXPERTS_PER_GROUP
    le = jnp.where((lane >= lo_lane) & (lane < lo_lane + EXPERTS_PER_GROUP), logits, -jnp.inf)
    emax = jnp.max(le, axis=1, keepdims=True)
    i1 = jnp.min(jnp.where(le == emax, lane, big), axis=1, keepdims=True)
    esum = jnp.sum(jnp.exp(le - emax), axis=1, keepdims=True)
    le2 = jnp.where(lane == i1, -jnp.inf, le)
    emax2 = jnp.max(le2, axis=1, keepdims=True)
    i2 = jnp.min(jnp.where(le2 == emax2, lane, big), axis=1, keepdims=True)
    p1 = 1.0 / esum
    p2 = jnp.exp(emax2 - emax) / esum
    psum = p1 + p2
    w1 = p_group_top * p1 / psum
    w2 = p_group_top * p2 / psum
    e1 = i1 - N_GROUPS
    e2 = i2 - N_GROUPS

    @pl.when(pl.program_id(0) == 0)
    def _():
        run_ref[...] = jnp.zeros_like(run_ref)

    tm = x.shape[0]
    oh1 = jnp.where(lane == i1, 1.0, 0.0)
    oh2 = jnp.where(lane == i2, 1.0, 0.0)
    ohs = oh1 + oh2
    lower = (lax.broadcasted_iota(jnp.int32, (tm, tm), 0) > lax.broadcasted_iota(jnp.int32, (tm, tm), 1))
    before = _dot(jnp.where(lower, 1.0, 0.0).astype(BF16), ohs.astype(BF16)) + run_ref[...]
    rank1 = jnp.sum(oh1 * before, axis=1, keepdims=True)
    rank2 = jnp.sum(oh2 * before, axis=1, keepdims=True)
    run_ref[...] = run_ref[...] + jnp.sum(ohs, axis=0, keepdims=True)
    counts_ref[...] = run_ref[...]

    rl = lax.broadcasted_iota(jnp.int32, (tm, ROUTE_OUT), 1)
    rec = jnp.zeros((tm, ROUTE_OUT), F32)
    for slot, val in enumerate((e1, e2, w1, w2, rank1, rank2)):
        rec = jnp.where(rl == slot, val, rec)
    route_ref[...] = rec


def _merge_route(x2d, yc, ya, yg, gmix, wg, wb, wo, gffn, wr_hi, wr_lo, br):
    t, d = x2d.shape
    tm = 512
    tok = lambda w: pl.BlockSpec((tm, w), lambda i: (i, 0))
    full = lambda arr: pl.BlockSpec(arr.shape, lambda i: (0,) * arr.ndim)
    ins = (x2d, yc, ya, yg, gmix, wg, wb, wo, gffn, wr_hi, wr_lo, br)
    return pl.pallas_call(
        _merge_route_kernel,
        grid=(t // tm,),
        in_specs=[tok(d), tok(CONV_DIM), tok(ATTN_DIM), tok(GLA_V_DIM)] + [full(a) for a in ins[4:]],
        out_specs=(tok(d), pl.BlockSpec((tm * SUBLANES, LANES), lambda i: (i, 0)), tok(ROUTE_OUT),
                   pl.BlockSpec((1, ROUTE_LANES), lambda i: (0, 0))),
        out_shape=(jax.ShapeDtypeStruct((t, d), F32), jax.ShapeDtypeStruct((t * SUBLANES, LANES), F32),
                   jax.ShapeDtypeStruct((t, ROUTE_OUT), F32), jax.ShapeDtypeStruct((1, ROUTE_LANES), F32)),
        scratch_shapes=[pltpu.VMEM((1, ROUTE_LANES), F32)],
        compiler_params=_cparams(1),
        name="merge_route",
    )(*ins)


def _tile_slots(dest, tc):
    nt = dest.shape[0] // tc
    return dest.reshape(nt, tc, TOP_K).transpose(0, 2, 1).reshape(nt, 1, TOP_K * tc)


def _dispatch_kernel(last_blk_ref, dest_ref, h2_ref, xs_hbm, zeros_ref, rows_buf, sems):
    tc = h2_ref.shape[0] // SUBLANES
    step = pl.program_id(0)
    slot = step % 2
    sem = sems.at[0]

    @pl.when(pl.program_id(0) == 0)
    def _():
        zeros_ref[...] = jnp.zeros_like(zeros_ref)
        blk_rows = EXPERT_BLOCK * SUBLANES

        def zero_fill(first_row):
            return pltpu.make_async_copy(
                zeros_ref, xs_hbm.at[pl.ds(pl.multiple_of(first_row * SUBLANES, SUBLANES), blk_rows)], sem)

        for go in (lambda c: c.start(), lambda c: c.wait()):
            for e in range(N_EXPERTS):
                @pl.when(last_blk_ref[e] >= 0)
                def _():
                    go(zero_fill(last_blk_ref[e]))

            def unused(blk, carry):
                go(zero_fill(blk * EXPERT_BLOCK))
                return carry

            lax.fori_loop(last_blk_ref[N_EXPERTS], xs_hbm.shape[0] // blk_rows, unused, 0)

    stage = rows_buf.at[slot]
    stage[...] = h2_ref[...]

    def row_copy(r, kk):
        src = pl.multiple_of(r * SUBLANES, SUBLANES)
        dst = pl.multiple_of(dest_ref[0, kk * tc + r] * SUBLANES, SUBLANES)
        return pltpu.make_async_copy(stage.at[pl.ds(src, SUBLANES)], xs_hbm.at[pl.ds(dst, SUBLANES)],
                                     sems.at[slot])

    def issue(r, carry):
        row_copy(r, 0).start(priority=0)
        row_copy(r, 1).start(priority=1)
        return carry

    lax.fori_loop(0, tc, issue, 0, unroll=ROW_DMA_UNROLL)

    def drain(which):
        for _ in range(TOP_K):
            pltpu.make_async_copy(rows_buf.at[which], xs_hbm.at[pl.ds(0, tc * SUBLANES)], sems.at[which]).wait()

    @pl.when(step > 0)
    def _():
        drain(1 - slot)

    @pl.when(step == pl.num_programs(0) - 1)
    def _():
        drain(slot)


def _dispatch(dest, h2_tiles, n_rows, last_blk):
    t = h2_tiles.shape[0] // SUBLANES
    tc = 256
    grid_spec = pltpu.PrefetchScalarGridSpec(
        num_scalar_prefetch=1,
        grid=(t // tc,),
        in_specs=[
            pl.BlockSpec((None, 1, TOP_K * tc), lambda i, lb: (i, 0, 0), memory_space=pltpu.SMEM),
            pl.BlockSpec((tc * SUBLANES, LANES), lambda i, lb: (i, 0)),
        ],
        out_specs=pl.BlockSpec(memory_space=pl.ANY),
        scratch_shapes=[pltpu.VMEM((EXPERT_BLOCK * SUBLANES, LANES), F32),
                        pltpu.VMEM((2, tc * SUBLANES, LANES), F32), pltpu.SemaphoreType.DMA((2,))],
    )
    return pl.pallas_call(
        _dispatch_kernel,
        grid_spec=grid_spec,
        out_shape=jax.ShapeDtypeStruct((n_rows * SUBLANES, LANES), F32),
        compiler_params=_cparams(1),
        name="moe_dispatch",
    )(last_blk, _tile_slots(dest, tc), h2_tiles)


def _expert_kernel(blk_expert_ref, n_used_ref, xs_hbm, wg_ref, wu_ref, wd_ref, ys_ref,
                   wg_bf, wu_bf, wd_bf, ring, sems):
    i = pl.program_id(0)
    n_steps = pl.num_programs(0)
    blk_rows = ring.shape[1]
    rb = blk_rows // SUBLANES

    def fetch(step):
        src = pl.multiple_of(jnp.minimum(step, n_used_ref[0] - 1) * blk_rows, blk_rows)
        slot = step % XS_RING
        return pltpu.make_async_copy(xs_hbm.at[pl.ds(src, blk_rows)], ring.at[slot], sems.at[slot])

    @pl.when(i == 0)
    def _():
        for step in range(XS_RING - 1):
            fetch(step).start()

    @pl.when(i + (XS_RING - 1) < n_steps)
    def _():
        fetch(i + (XS_RING - 1)).start()

    fetch(i).wait()
    xs_ref = ring.at[i % XS_RING]
    new_expert = (i == 0) | (blk_expert_ref[i] != blk_expert_ref[jnp.maximum(i - 1, 0)])

    @pl.when(new_expert)
    def _():
        wg_bf[...] = wg_ref[...].astype(BF16)
        wu_bf[...] = wu_ref[...].astype(BF16)
        wd_bf[...] = wd_ref[...].astype(BF16)

    @pl.when(i < n_used_ref[0])
    def _():
        xb = jnp.concatenate(_tiles_to_rows(xs_ref, rb), axis=1).astype(BF16)
        gate = _dot(xb, wg_bf[...])
        up = _dot(xb, wu_bf[...])
        act = (gate * jax.nn.sigmoid(gate) * up).astype(BF16)
        _rows_to_tiles(ys_ref, _dot(act, wd_bf[...]))

    @pl.when(i >= n_used_ref[0])
    def _():
        ys_ref[...] = jnp.zeros_like(ys_ref)


def _experts(blk_expert, n_used, xs_tiles, wg, wu, wd, layer):
    d = SUBLANES * LANES
    n_blocks = blk_expert.shape[0]
    rb = EXPERT_BLOCK
    row_block = (rb * SUBLANES, LANES)
    grid_spec = pltpu.PrefetchScalarGridSpec(
        num_scalar_prefetch=2,
        grid=(n_blocks,),
        in_specs=[
            pl.BlockSpec(memory_space=pl.ANY),
            pl.BlockSpec((None, None, d, EXPERT_FF), lambda i, be, nu: (layer, be[i], 0, 0)),
            pl.BlockSpec((None, None, d, EXPERT_FF), lambda i, be, nu: (layer, be[i], 0, 0)),
            pl.BlockSpec((None, None, EXPERT_FF, d), lambda i, be, nu: (layer, be[i], 0, 0)),
        ],
        out_specs=pl.BlockSpec(row_block, lambda i, be, nu: (i, 0)),
        scratch_shapes=[pltpu.VMEM((d, EXPERT_FF), BF16), pltpu.VMEM((d, EXPERT_FF), BF16),
                        pltpu.VMEM((EXPERT_FF, d), BF16),
                        pltpu.VMEM((XS_RING,) + row_block, F32), pltpu.SemaphoreType.DMA((XS_RING,))],
    )
    assert n_blocks >= XS_RING
    return pl.pallas_call(
        _expert_kernel,
        grid_spec=grid_spec,
        out_shape=jax.ShapeDtypeStruct(xs_tiles.shape, F32),
        compiler_params=_cparams(1),
        name="moe_experts",
    )(blk_expert, n_used, xs_tiles, wg, wu, wd)


def _combine_kernel(slots_ref, next_slots_ref, x_ref, route_ref, ys_hbm, o_ref, rows_buf, sem):
    tc = x_ref.shape[0]
    rows = _pipelined_expert_rows(slots_ref, next_slots_ref, ys_hbm, rows_buf, sem,
                                  pl.program_id(0), pl.num_programs(0), tc)
    for j, chunk in enumerate(_moe_residual_chunks(x_ref, route_ref, rows, tc)):
        o_ref[:, j * LANES:(j + 1) * LANES] = chunk


def _combine(dest, x2d, route, ys_tiles):
    t, d = x2d.shape
    tc = 256
    nt = t // tc
    slots = _tile_slots(dest, tc)
    slot_block = lambda idx: pl.BlockSpec((None, 1, TOP_K * tc), idx, memory_space=pltpu.SMEM)
    return pl.pallas_call(
        _combine_kernel,
        grid=(nt,),
        in_specs=[
            slot_block(lambda i: (i, 0, 0)),
            slot_block(lambda i: (jnp.minimum(i + 1, nt - 1), 0, 0)),
            pl.BlockSpec((tc, d), lambda i: (i, 0)),
            pl.BlockSpec((tc, ROUTE_OUT), lambda i: (i, 0)),
            pl.BlockSpec(memory_space=pl.ANY),
        ],
        out_specs=pl.BlockSpec((tc, d), lambda i: (i, 0)),
        out_shape=jax.ShapeDtypeStruct((t, d), F32),
        scratch_shapes=[pltpu.VMEM((2, TOP_K, tc * SUBLANES, LANES), F32), pltpu.SemaphoreType.DMA((2,))],
        compiler_params=_cparams(1),
        name="moe_combine",
    )(slots, slots, x2d, route, ys_tiles)


def _dispatch_plan(route, counts, t):
    e_ids = route[:, :TOP_K].astype(jnp.int32)
    rank = route[:, 2 * TOP_K:3 * TOP_K].astype(jnp.int32)
    counts = counts[0, N_GROUPS:N_GROUPS + N_EXPERTS].astype(jnp.int32)
    padded = ((counts + EXPERT_BLOCK - 1) // EXPERT_BLOCK) * EXPERT_BLOCK
    pad_end = jnp.cumsum(padded)
    pad_start = pad_end - padded
    onehot = e_ids[:, :, None] == jnp.arange(N_EXPERTS, dtype=jnp.int32)
    dest = rank + jnp.sum(jnp.where(onehot, pad_start, 0), axis=-1)
    n_blocks = -(-(t * TOP_K) // EXPERT_BLOCK) + N_EXPERTS
    blk_start = jnp.arange(n_blocks, dtype=jnp.int32) * EXPERT_BLOCK
    blk_expert = jnp.minimum(jnp.sum(blk_start[:, None] >= pad_end[None, :], axis=1), N_EXPERTS - 1)
    n_used = jnp.maximum(pad_end[-1:] // EXPERT_BLOCK, 1).astype(jnp.int32)
    last_blk = jnp.concatenate([jnp.where(padded > 0, pad_end - EXPERT_BLOCK, -1), n_used]).astype(jnp.int32)
    return blk_expert.astype(jnp.int32), n_used, dest, n_blocks * EXPERT_BLOCK, last_blk


def kernel(x, rel_bias_table, norm_mix, w_in, conv_w, conv_b, q_norm, k_norm, w_gla_alpha, b_gla_alpha,
           gla_out_norm, w_merge_gate, w_branch, w_out, norm_ffn, w_router_group, b_router_group,
           w_router_expert, b_router_expert, w_expert_gate, w_expert_up, w_expert_down):
    b, s, d = x.shape
    t = b * s
    depth = w_in.shape[0]
    assert s % MOBA_BLOCK == 0 and t % EXPERT_BLOCK == 0
    assert d == SUBLANES * LANES, "row-granular DMAs store each activation row as one (8, 128) tile"
    bias0, bias1 = _attn_biases(rel_bias_table)
    head_id = jnp.arange(ATTN_DIM) // ATTN_HEAD_DIM
    hsum = (head_id[:, None] == head_id[None, :]).astype(BF16)
    for l in range(depth):
        w_l = jnp.pad(w_in[l].astype(BF16), ((0, 0), (0, GZ_PAD - GLA_GATE_RANK)))
        w_alpha = jnp.pad(w_gla_alpha[l].astype(BF16), ((0, GZ_PAD - GLA_GATE_RANK), (0, 0)))
        yconv, q, k, v, kmean, gq, gk, gv, gr, gz = _inproj(
            x, norm_mix[l][None], w_l, conv_w[l], conv_b[l][None],
            jnp.tile(q_norm[l], ATTN_HEADS)[None], jnp.tile(k_norm[l], ATTN_HEADS)[None], hsum)
        qp, kp = _select(q, k, kmean.reshape(b, s // MOBA_BLOCK, ATTN_DIM))
        yattn = _attention(qp, kp, v, bias0, bias1)
        ygla = _gla(gq, gk, gv, gr, gz, w_alpha, b_gla_alpha[l][None],
                    gla_out_norm[l][None])
        w_r = jnp.concatenate([w_router_group[l], w_router_expert[l]], axis=1)
        w_r = jnp.pad(w_r, ((0, 0), (0, ROUTE_LANES - w_r.shape[1])))
        b_r = jnp.pad(jnp.concatenate([b_router_group[l], b_router_expert[l]]),
                      (0, ROUTE_LANES - N_GROUPS - N_EXPERTS))[None]
        wr_hi, wr_lo = _split_bf16(w_r)
        xo, h2, route, counts = _merge_route(
            x.reshape(t, d), yconv.reshape(t, -1), yattn.reshape(t, -1), ygla.reshape(t, -1),
            norm_mix[l][None], w_merge_gate[l].astype(BF16), w_branch[l].astype(BF16),
            w_out[l].astype(BF16), norm_ffn[l][None], wr_hi, wr_lo, b_r)
        blk_expert, n_used, dest, n_rows, last_blk = _dispatch_plan(route, counts, t)
        xs = _dispatch(dest, h2, n_rows, last_blk)
        ys = _experts(blk_expert, n_used, xs, w_expert_gate, w_expert_up, w_expert_down, l)
        x = _combine(dest, xo, route, ys).reshape(b, s, d)
    return x
```
